```python
import jax, jax.numpy as jnp
from jax import lax
import numpy as np

D_MODEL = 1024
BATCH = 2
SEQ = 8192
DEPTH = 4

CTX_LEN = 256
GRID_W = 64
N_HEADS = 8
N_KV_HEADS = 2
HEAD_DIM = 128
N_GROUPS = N_HEADS // N_KV_HEADS
WINDOW = 128
ATTN_BLOCK = 128
ROPE_THETA = 10000.0
ROPE_HALF = HEAD_DIM // 2
ROPE_FREQS = ROPE_HALF // 2
D_RNN = 1024
N_RNN_BLOCKS = 8
RNN_BLOCK = D_RNN // N_RNN_BLOCKS
CONV_W = 4
CONV_LEFT = 2
LRU_C = 8.0
D_FF = 2816
N_EXPERTS = 8
TOP_K = 2
D_FF_EXPERT = 3584
N_MOE_LAYERS = DEPTH // 2
N_DENSE_LAYERS = DEPTH - N_MOE_LAYERS
EPS = 1e-6
NEG_INF = -1e30
Q_COLS = N_HEADS * HEAD_DIM
KV_COLS = N_KV_HEADS * HEAD_DIM
IN_COLS = Q_COLS + 2 * KV_COLS + 2 * D_RNN + 2 * D_MODEL
SPLIT_IDX = (Q_COLS, Q_COLS + KV_COLS, Q_COLS + 2 * KV_COLS, Q_COLS + 2 * KV_COLS + D_RNN,
             Q_COLS + 2 * KV_COLS + 2 * D_RNN, Q_COLS + 2 * KV_COLS + 2 * D_RNN + D_MODEL)

kernel_name = "hybrid_dit_swa_rglru_moe"


def _rmsnorm(x, g):
    xf = x.astype(jnp.float32)
    y = xf * lax.rsqrt(jnp.mean(xf * xf, axis=-1, keepdims=True) + EPS)
    return (y * g.astype(jnp.float32)).astype(x.dtype)


def _modulate(h, shift, scale):
    return h * (1 + scale) + shift


def _axial_angles(row, col):
    inv = ROPE_THETA ** (-jnp.arange(ROPE_FREQS, dtype=jnp.float32) / ROPE_FREQS)
    return row.astype(jnp.float32)[:, None] * inv, col.astype(jnp.float32)[:, None] * inv


def _rot_half(xh, ang):
    x1, x2 = xh[..., :ROPE_FREQS], xh[..., ROPE_FREQS:]
    cos = jnp.cos(ang)[None, :, None, :]
    sin = jnp.sin(ang)[None, :, None, :]
    return jnp.concatenate([x1 * cos - x2 * sin, x2 * cos + x1 * sin], axis=-1)


def _rope_2d(x, ang_row, ang_col):
    xf = x.astype(jnp.float32)
    y = jnp.concatenate([_rot_half(xf[..., :ROPE_HALF], ang_row), _rot_half(xf[..., ROPE_HALF:], ang_col)], axis=-1)
    return y.astype(x.dtype)


def _windowed_attention(q, k, v, kc, vc, sink):
    B, S = q.shape[0], q.shape[1]
    nb = S // ATTN_BLOCK
    scale = HEAD_DIM ** -0.5
    qb = q.reshape(B, nb, ATTN_BLOCK, N_KV_HEADS, N_GROUPS, HEAD_DIM)
    pad = ((0, 0), (ATTN_BLOCK, ATTN_BLOCK), (0, 0), (0, 0))
    kp = jnp.pad(k, pad).reshape(B, nb + 2, ATTN_BLOCK, N_KV_HEADS, HEAD_DIM)
    vp = jnp.pad(v, pad).reshape(B, nb + 2, ATTN_BLOCK, N_KV_HEADS, HEAD_DIM)
    kband = jnp.concatenate([kp[:, :-2], kp[:, 1:-1], kp[:, 2:]], axis=2)
    vband = jnp.concatenate([vp[:, :-2], vp[:, 1:-1], vp[:, 2:]], axis=2)
    s_lat = jnp.einsum('bnqkgd,bnjkd->bnkgqj', qb, kband).astype(jnp.float32) * scale
    s_ctx = jnp.einsum('bnqkgd,bckd->bnkgqc', qb, kc).astype(jnp.float32) * scale
    qi = jnp.arange(ATTN_BLOCK)[:, None]
    kj = jnp.arange(3 * ATTN_BLOCK)[None, :]
    rel = kj - ATTN_BLOCK - qi
    kpos = jnp.arange(nb)[:, None, None] * ATTN_BLOCK + kj[None] - ATTN_BLOCK
    valid = (jnp.abs(rel) <= WINDOW)[None] & (kpos >= 0) & (kpos < S)
    s_lat = jnp.where(valid[None, :, None, None], s_lat, NEG_INF)
    sink_l = jnp.broadcast_to(sink.astype(jnp.float32).reshape(1, 1, N_KV_HEADS, N_GROUPS, 1, 1),
                              s_ctx.shape[:-1] + (1,))
    p = jax.nn.softmax(jnp.concatenate([s_lat, s_ctx, sink_l], axis=-1), axis=-1)
    n_lat = 3 * ATTN_BLOCK
    n_ctx = kc.shape[1]
    p_lat = p[..., :n_lat].astype(v.dtype)
    p_ctx = p[..., n_lat:n_lat + n_ctx].astype(v.dtype)
    o = jnp.einsum('bnkgqj,bnjkd->bnqkgd', p_lat, vband) + jnp.einsum('bnkgqc,bckd->bnqkgd', p_ctx, vc)
    return o.reshape(B, S, Q_COLS)


def _context_attention(qc, kc, vc, sink):
    B, C = qc.shape[0], qc.shape[1]
    qg = qc.reshape(B, C, N_KV_HEADS, N_GROUPS, HEAD_DIM)
    s = jnp.einsum('bqkgd,bckd->bkgqc', qg, kc).astype(jnp.float32) * (HEAD_DIM ** -0.5)
    sink_c = jnp.broadcast_to(sink.astype(jnp.float32).reshape(1, N_KV_HEADS, N_GROUPS, 1, 1), s.shape[:-1] + (1,))
    p = jax.nn.softmax(jnp.concatenate([s, sink_c], axis=-1), axis=-1)
    o = jnp.einsum('bkgqc,bckd->bqkgd', p[..., :C].astype(vc.dtype), vc)
    return o.reshape(B, C, Q_COLS)


def _conv_centred(x, w, b):
    L = x.shape[1]
    xp = jnp.pad(x, ((0, 0), (CONV_LEFT, CONV_W - 1 - CONV_LEFT), (0, 0)))
    y = xp[:, 0:L] * w[0]
    for j in range(1, CONV_W):
        y = y + xp[:, j:j + L] * w[j]
    return y + b


def _rglru_coeffs(x, wa, ba, wx, bx, lam):
    B, L = x.shape[0], x.shape[1]
    xb = x.reshape(B, L, N_RNN_BLOCKS, RNN_BLOCK)
    r = jax.nn.sigmoid(jnp.einsum('blhi,hij->blhj', xb, wa).reshape(B, L, D_RNN) + ba)
    i = jax.nn.sigmoid(jnp.einsum('blhi,hij->blhj', xb, wx).reshape(B, L, D_RNN) + bx)
    log_a = -LRU_C * r.astype(jnp.float32) * jax.nn.softplus(-lam.astype(jnp.float32))
    a = jnp.exp(log_a)
    bcoef = jnp.sqrt(-jnp.expm1(2.0 * log_a)) * (i * x).astype(jnp.float32)
    return a, bcoef


def _linear_scan(a, b, h0):
    b = b.at[:, 0].add(a[:, 0] * h0)

    def comb(left, right):
        return left[0] * right[0], right[0] * left[1] + right[1]

    _, h = lax.associative_scan(comb, (a, b), axis=1)
    return h


def _bidir_rglru(xl, xc, ga_w, ga_b, gx_w, gx_b, lam):
    B = xl.shape[0]
    h0 = jnp.zeros((B, D_RNN), jnp.float32)
    outs_l, outs_c = [], []
    for d in range(2):
        seq_c = xc if d == 0 else jnp.flip(xc, axis=1)
        seq_l = xl if d == 0 else jnp.flip(xl, axis=1)
        a, b = _rglru_coeffs(seq_c, ga_w[d], ga_b[d], gx_w[d], gx_b[d], lam[d])
        hc = _linear_scan(a, b, h0)
        a, b = _rglru_coeffs(seq_l, ga_w[d], ga_b[d], gx_w[d], gx_b[d], lam[d])
        hl = _linear_scan(a, b, hc[:, -1])
        if d == 1:
            hc = jnp.flip(hc, axis=1)
            hl = jnp.flip(hl, axis=1)
        outs_l.append(hl)
        outs_c.append(hc)
    return (outs_l[0] + outs_l[1]).astype(xl.dtype), (outs_c[0] + outs_c[1]).astype(xc.dtype)


def _merge(attn, rec, g_attn, g_rec, w_o_attn, w_o_lru, w_out):
    m = jax.nn.sigmoid(g_attn) * (attn @ w_o_attn) + jax.nn.sigmoid(g_rec) * (rec @ w_o_lru)
    return m @ w_out


def _token_mixer(hl, hc, ang_row, ang_col, w_in, sink, conv_w, conv_b, ga_w, ga_b, gx_w, gx_b, lam,
                 w_o_attn, w_o_lru, w_out, need_ctx):
    B, S = hl.shape[0], hl.shape[1]
    C = hc.shape[1]
    ql, kl, vl, xl, yl, gal, grl = jnp.split(hl @ w_in, SPLIT_IDX, axis=-1)
    qc, kc, vc, xc, yc, gac, grc = jnp.split(hc @ w_in, SPLIT_IDX, axis=-1)
    ql = _rope_2d(ql.reshape(B, S, N_HEADS, HEAD_DIM), ang_row, ang_col)
    kl = _rope_2d(kl.reshape(B, S, N_KV_HEADS, HEAD_DIM), ang_row, ang_col)
    vl = vl.reshape(B, S, N_KV_HEADS, HEAD_DIM)
    kc = kc.reshape(B, C, N_KV_HEADS, HEAD_DIM)
    vc = vc.reshape(B, C, N_KV_HEADS, HEAD_DIM)
    attn_l = _windowed_attention(ql, kl, vl, kc, vc, sink)
    xl = _conv_centred(xl, conv_w, conv_b)
    xc = _conv_centred(xc, conv_w, conv_b)
    rl, rc = _bidir_rglru(xl, xc, ga_w, ga_b, gx_w, gx_b, lam)
    out_l = _merge(attn_l, rl * jax.nn.gelu(yl), gal, grl, w_o_attn, w_o_lru, w_out)
    if not need_ctx:
        return out_l, None
    attn_c = _context_attention(qc.reshape(B, C, N_HEADS, HEAD_DIM), kc, vc, sink)
    out_c = _merge(attn_c, rc * jax.nn.gelu(yc), gac, grc, w_o_attn, w_o_lru, w_out)
    return out_l, out_c


def _swiglu(h, wg, wu, wd):
    return (jax.nn.silu(h @ wg) * (h @ wu)) @ wd


def _moe(h, w_r, wg, wu, wd):
    logits = (h @ w_r).astype(jnp.float32)
    top_v, top_i = lax.top_k(logits, TOP_K)
    top_w = jax.nn.softmax(top_v, axis=-1)
    gates = jnp.sum(jax.nn.one_hot(top_i, N_EXPERTS, dtype=jnp.float32) * top_w[..., None], axis=-2)
    gates = gates.astype(h.dtype)
    out = gates[..., 0:1] * _swiglu(h, wg[0], wu[0], wd[0])
    for e in range(1, N_EXPERTS):
        out = out + gates[..., e:e + 1] * _swiglu(h, wg[e], wu[e], wd[e])
    return out


def setup_inputs(seed: int = 0) -> dict:
    key = jax.random.key(seed)
    ks = jax.random.split(key, 32)
    f32 = jnp.float32

    def nrm(k, shape, s):
        return jax.random.normal(k, shape, f32) * s

    u = jax.random.uniform(ks[15], (DEPTH, 2, D_RNN), f32, minval=0.9, maxval=0.999)
    a = u ** (1.0 / LRU_C)
    lru_lambda = jnp.log(a) - jnp.log1p(-a)
    return {
        "x": nrm(ks[0], (BATCH, SEQ, D_MODEL), 1.0),
        "c": nrm(ks[1], (BATCH, D_MODEL), 1.0),
        "ctx": nrm(ks[2], (BATCH, CTX_LEN, D_MODEL), 1.0),
        "c_ctx": nrm(ks[3], (D_MODEL,), 1.0),
        "w_mod": nrm(ks[4], (DEPTH, D_MODEL, 6 * D_MODEL), 0.5 * D_MODEL ** -0.5),
        "b_mod": nrm(ks[5], (DEPTH, 6 * D_MODEL), 0.02),
        "norm1_g": 1.0 + nrm(ks[6], (DEPTH, D_MODEL), 0.05),
        "norm2_g": 1.0 + nrm(ks[7], (DEPTH, D_MODEL), 0.05),
        "w_in": nrm(ks[8], (DEPTH, D_MODEL, IN_COLS), D_MODEL ** -0.5),
        "attn_sink": nrm(ks[9], (DEPTH, N_HEADS), 0.5),
        "conv_w": nrm(ks[10], (DEPTH, CONV_W, D_RNN), CONV_W ** -0.5),
        "conv_b": nrm(ks[11], (DEPTH, D_RNN), 0.02),
        "gate_a_w": nrm(ks[12], (DEPTH, 2, N_RNN_BLOCKS, RNN_BLOCK, RNN_BLOCK), RNN_BLOCK ** -0.5),
        "gate_a_b": nrm(ks[13], (DEPTH, 2, D_RNN), 0.02),
        "gate_x_w": nrm(ks[14], (DEPTH, 2, N_RNN_BLOCKS, RNN_BLOCK, RNN_BLOCK), RNN_BLOCK ** -0.5),
        "gate_x_b": nrm(ks[16], (DEPTH, 2, D_RNN), 0.02),
        "lru_lambda": lru_lambda,
        "w_o_attn": nrm(ks[17], (DEPTH, Q_COLS, D_MODEL), Q_COLS ** -0.5),
        "w_o_lru": nrm(ks[18], (DEPTH, D_RNN, D_MODEL), D_RNN ** -0.5),
        "w_out": nrm(ks[19], (DEPTH, D_MODEL, D_MODEL), D_MODEL ** -0.5),
        "ff_w_gate": nrm(ks[20], (N_DENSE_LAYERS, D_MODEL, D_FF), D_MODEL ** -0.5),
        "ff_w_up": nrm(ks[21], (N_DENSE_LAYERS, D_MODEL, D_FF), D_MODEL ** -0.5),
        "ff_w_down": nrm(ks[22], (N_DENSE_LAYERS, D_FF, D_MODEL), D_FF ** -0.5),
        "router_w": nrm(ks[23], (N_MOE_LAYERS, D_MODEL, N_EXPERTS), D_MODEL ** -0.5),
        "exp_w_gate": nrm(ks[24], (N_MOE_LAYERS, N_EXPERTS, D_MODEL, D_FF_EXPERT), D_MODEL ** -0.5),
        "exp_w_up": nrm(ks[25], (N_MOE_LAYERS, N_EXPERTS, D_MODEL, D_FF_EXPERT), D_MODEL ** -0.5),
        "exp_w_down": nrm(ks[26], (N_MOE_LAYERS, N_EXPERTS, D_FF_EXPERT, D_MODEL), D_FF_EXPERT ** -0.5),
        "final_g": 1.0 + nrm(ks[27], (D_MODEL,), 0.05),
    }


def reference(x, c, ctx, c_ctx, w_mod, b_mod, norm1_g, norm2_g, w_in, attn_sink, conv_w, conv_b,
              gate_a_w, gate_a_b, gate_x_w, gate_x_b, lru_lambda, w_o_attn, w_o_lru, w_out,
              ff_w_gate, ff_w_up, ff_w_down, router_w, exp_w_gate, exp_w_up, exp_w_down, final_g):
    S = x.shape[1]
    C = ctx.shape[1]
    rows = S // GRID_W
    row = jnp.repeat(jnp.arange(rows), GRID_W)
    col = jnp.tile(jnp.arange(GRID_W), rows)
    ang_row, ang_col = _axial_angles(row, col)
    s_c = jax.nn.silu(c)
    s_cc = jax.nn.silu(c_ctx)
    h_lat = x
    h_ctx = ctx
    for l in range(DEPTH):
        last = l == DEPTH - 1
        mod_l = (s_c @ w_mod[l] + b_mod[l])[:, None, :]
        mod_c = (s_cc @ w_mod[l] + b_mod[l])[None, None, :]
        sh1, sc1, g1, sh2, sc2, g2 = jnp.split(mod_l, 6, axis=-1)
        csh1, csc1, cg1, csh2, csc2, cg2 = jnp.split(mod_c, 6, axis=-1)
        nl = _modulate(_rmsnorm(h_lat, norm1_g[l]), sh1, sc1)
        nc = _modulate(_rmsnorm(h_ctx, norm1_g[l]), csh1, csc1)
        yl, yc = _token_mixer(nl, nc, ang_row, ang_col, w_in[l], attn_sink[l], conv_w[l], conv_b[l],
                              gate_a_w[l], gate_a_b[l], gate_x_w[l], gate_x_b[l], lru_lambda[l],
                              w_o_attn[l], w_o_lru[l], w_out[l], not last)
        h_lat = h_lat + g1 * yl
        nl2 = _modulate(_rmsnorm(h_lat, norm2_g[l]), sh2, sc2)
        if last:
            n2 = nl2
        else:
            h_ctx = h_ctx + cg1 * yc
            nc2 = _modulate(_rmsnorm(h_ctx, norm2_g[l]), csh2, csc2)
            n2 = jnp.concatenate([nc2, nl2], axis=1)
        i = l // 2
        if l % 2 == 0:
            f = _swiglu(n2, ff_w_gate[i], ff_w_up[i], ff_w_down[i])
        else:
            f = _moe(n2, router_w[i], exp_w_gate[i], exp_w_up[i], exp_w_down[i])
        if last:
            h_lat = h_lat + g2 * f
        else:
            h_ctx = h_ctx + cg2 * f[:, :C]
            h_lat = h_lat + g2 * f[:, C:]
    return _rmsnorm(h_lat, final_g)
```

```python
import functools

import jax
import jax.numpy as jnp
from jax import lax
from jax.experimental import pallas as pl
from jax.experimental.pallas import tpu as pltpu

F32 = jnp.float32
BF16 = jnp.bfloat16

D_MODEL = 1024
DEPTH = 4
GRID_W = 64
N_HEADS = 8
N_KV_HEADS = 2
HEAD_DIM = 128
N_GROUPS = N_HEADS // N_KV_HEADS
ATTN_BLOCK = 128
ROPE_THETA = 10000.0
ROPE_FREQS = HEAD_DIM // 4
D_RNN = 1024
N_RNN_BLOCKS = 8
RNN_BLOCK = D_RNN // N_RNN_BLOCKS
LRU_C = 8.0
D_FF = 2816
N_EXPERTS = 8
D_FF_EXPERT = 3584
EPS = 1e-6
NEG_INF = -1e30
Q_COLS = N_HEADS * HEAD_DIM
KV_COLS = N_KV_HEADS * HEAD_DIM
IN_COLS = Q_COLS + 2 * KV_COLS + 2 * D_RNN + 2 * D_MODEL
ATTN_SCALE = HEAD_DIM ** -0.5

LANES = 128
SUBLANES = 8
TM = 512
TN_IN = 512
TQ = 512
TT = 256
MOD_ROWS = 8
VMEM_LIMIT = 56 * 1024 * 1024


def _cparams(n_axes):
    return pltpu.CompilerParams(dimension_semantics=("arbitrary",) * n_axes,
                                vmem_limit_bytes=VMEM_LIMIT)


def _sigmoid(z):
    return 1.0 / (1.0 + jnp.exp(-z))


def _norm_mod(h, g, shift, scale):
    ms = jnp.mean(h * h, axis=-1, keepdims=True)
    y = h * lax.rsqrt(ms + EPS) * g
    return y * (1.0 + scale) + shift


def _mod_kernel(c_ref, w_ref, b_ref, o_ref):
    cv = c_ref[...]
    s = cv * _sigmoid(cv)
    o_ref[...] = jnp.dot(s, w_ref[...], preferred_element_type=F32,
                         precision=lax.Precision.HIGHEST) + b_ref[...]


def _modulation(cpad, w_mod, b_mod):
    nchunk = 6
    return pl.pallas_call(
        _mod_kernel,
        grid=(DEPTH, nchunk),
        in_specs=[
            pl.BlockSpec((MOD_ROWS, D_MODEL), lambda l, n: (0, 0)),
            pl.BlockSpec((None, D_MODEL, D_MODEL), lambda l, n: (l, 0, n)),
            pl.BlockSpec((None, 1, D_MODEL), lambda l, n: (l, 0, n)),
        ],
        out_specs=pl.BlockSpec((None, MOD_ROWS, D_MODEL), lambda l, n: (l, 0, n)),
        out_shape=jax.ShapeDtypeStruct((DEPTH, MOD_ROWS, 6 * D_MODEL), F32),
        compiler_params=_cparams(2),
        name="modulation",
    )(cpad, w_mod, b_mod.reshape(DEPTH, 1, 6 * D_MODEL))


def _rope(xh, cos, sin_signed, first_half):
    sw = jnp.where(first_half, pltpu.roll(xh, 96, 1), pltpu.roll(xh, 32, 1))
    return xh * cos + sw * sin_signed


def _in_proj_kernel(h_ref, g_ref, sh_ref, sc_ref, w_ref, cos_ref, sin_ref,
                    q_ref, kv_ref, x_ref, gy_ref, sga_ref, sgr_ref, n_scr):
    j = pl.program_id(1)

    @pl.when(j == 0)
    def _():
        n_scr[...] = _norm_mod(h_ref[...], g_ref[...], sh_ref[...], sc_ref[...]).astype(BF16)

    acc = jnp.dot(n_scr[...], w_ref[...], preferred_element_type=F32)

    def rope_heads(n_heads, scale):
        cos = cos_ref[...]
        sin = sin_ref[...]
        lane = lax.broadcasted_iota(jnp.int32, (TM, LANES), 1)
        first_half = (lane & 32) == 0
        outs = []
        for hh in range(n_heads):
            y = _rope(acc[:, hh * HEAD_DIM:(hh + 1) * HEAD_DIM], cos, sin, first_half)
            outs.append(y * scale if scale != 1.0 else y)
        return outs

    @pl.when(j < 2)
    def _():
        for hh, y in enumerate(rope_heads(4, ATTN_SCALE)):
            q_ref[:, hh * HEAD_DIM:(hh + 1) * HEAD_DIM] = y.astype(BF16)

    @pl.when(j == 2)
    def _():
        for hh, y in enumerate(rope_heads(2, 1.0)):
            kv_ref[:, hh * HEAD_DIM:(hh + 1) * HEAD_DIM] = y.astype(BF16)
        kv_ref[:, KV_COLS:] = acc[:, KV_COLS:].astype(BF16)

    @pl.when((j == 3) | (j == 4))
    def _():
        x_ref[...] = acc

    @pl.when((j == 5) | (j == 6))
    def _():
        gy_ref[...] = jax.nn.gelu(acc).astype(BF16)

    @pl.when((j == 7) | (j == 8))
    def _():
        sga_ref[...] = _sigmoid(acc).astype(BF16)

    @pl.when(j >= 9)
    def _():
        sgr_ref[...] = _sigmoid(acc).astype(BF16)


def _in_proj(h, g, mod3, w_in, cos_t, sin_t, *, n_lat_tiles, tiles_per_batch, n_batch):
    T = h.shape[0]
    n_m = T // TM
    n_n = IN_COLS // TN_IN

    def grp(i):
        return jnp.minimum(i // tiles_per_batch, n_batch)

    def pos_tile(i):
        return jnp.where(i < n_lat_tiles, i % tiles_per_batch, tiles_per_batch)

    def two(first):
        return lambda i, j: (i, jnp.clip(j - first, 0, 1))

    out_shape = (
        jax.ShapeDtypeStruct((T, Q_COLS), BF16),
        jax.ShapeDtypeStruct((T, 2 * KV_COLS), BF16),
        jax.ShapeDtypeStruct((T, D_RNN), F32),
        jax.ShapeDtypeStruct((T, D_RNN), BF16),
        jax.ShapeDtypeStruct((T, D_MODEL), BF16),
        jax.ShapeDtypeStruct((T, D_MODEL), BF16),
    )
    return pl.pallas_call(
        _in_proj_kernel,
        grid=(n_m, n_n),
        in_specs=[
            pl.BlockSpec((TM, D_MODEL), lambda i, j: (i, 0)),
            pl.BlockSpec((1, D_MODEL), lambda i, j: (0, 0)),
            pl.BlockSpec((None, 1, D_MODEL), lambda i, j: (grp(i), 0, 0)),
            pl.BlockSpec((None, 1, D_MODEL), lambda i, j: (grp(i), 0, 1)),
            pl.BlockSpec((D_MODEL, TN_IN), lambda i, j: (0, j)),
            pl.BlockSpec((TM, HEAD_DIM), lambda i, j: (pos_tile(i), 0)),
            pl.BlockSpec((TM, HEAD_DIM), lambda i, j: (pos_tile(i), 0)),
        ],
        out_specs=(
            pl.BlockSpec((TM, TN_IN), two(0)),
            pl.BlockSpec((TM, TN_IN), lambda i, j: (i, 0)),
            pl.BlockSpec((TM, TN_IN), two(3)),
            pl.BlockSpec((TM, TN_IN), two(5)),
            pl.BlockSpec((TM, TN_IN), two(7)),
            pl.BlockSpec((TM, TN_IN), two(9)),
        ),
        out_shape=out_shape,
        scratch_shapes=[pltpu.VMEM((TM, D_MODEL), BF16)],
        compiler_params=_cparams(2),
        name="in_proj",
    )(h, g, mod3, mod3, w_in, cos_t, sin_t)


def _stack_heads(qt):
    return jnp.concatenate([qt[:, g * HEAD_DIM:(g + 1) * HEAD_DIM] for g in range(N_GROUPS)], axis=0)


def _sink_col(sink_ref, kvh, rows):
    return jnp.concatenate(
        [jnp.full((rows, 1), sink_ref[kvh * N_GROUPS + g], F32) for g in range(N_GROUPS)], axis=0)


def _softmax_pv(s, sink, vall):
    m = jnp.maximum(jnp.max(s, axis=1, keepdims=True), sink)
    p = jnp.exp(s - m)
    denom = jnp.sum(p, axis=1, keepdims=True) + jnp.exp(sink - m)
    o = jnp.dot(p.astype(BF16), vall, preferred_element_type=F32)
    return o / denom


def _attn_kernel(sink_ref, q_ref, k_ref, v_ref, kc_ref, vc_ref, o_ref, *, n_blocks):
    kvh = pl.program_id(1)
    i = pl.program_id(2)
    rows = ATTN_BLOCK * N_GROUPS
    qi = lax.broadcasted_iota(jnp.int32, (rows, ATTN_BLOCK), 0) & (ATTN_BLOCK - 1)
    kj = lax.broadcasted_iota(jnp.int32, (rows, ATTN_BLOCK), 1)
    tri_prev = kj >= qi
    tri_next = kj <= qi
    sink = _sink_col(sink_ref, kvh, ATTN_BLOCK)
    kc = kc_ref[...]
    vc = vc_ref[...]

    def body(qb, carry):
        n = i * (TQ // ATTN_BLOCK) + qb
        r0 = pl.multiple_of(qb * ATTN_BLOCK, ATTN_BLOCK)
        p0 = pl.multiple_of(jnp.maximum(n - 1, 0) * ATTN_BLOCK, ATTN_BLOCK)
        c0 = pl.multiple_of(n * ATTN_BLOCK, ATTN_BLOCK)
        n0 = pl.multiple_of(jnp.minimum(n + 1, n_blocks - 1) * ATTN_BLOCK, ATTN_BLOCK)
        qs = _stack_heads(q_ref[pl.ds(r0, ATTN_BLOCK), :])
        kall = jnp.concatenate([k_ref[pl.ds(p0, ATTN_BLOCK), :], k_ref[pl.ds(c0, ATTN_BLOCK), :],
                                k_ref[pl.ds(n0, ATTN_BLOCK), :], kc], axis=0)
        vall = jnp.concatenate([v_ref[pl.ds(p0, ATTN_BLOCK), :], v_ref[pl.ds(c0, ATTN_BLOCK), :],
                                v_ref[pl.ds(n0, ATTN_BLOCK), :], vc], axis=0)
        s = lax.dot_general(qs, kall, (((1,), (1,)), ((), ())), preferred_element_type=F32)
        pen_prev = jnp.where(n > 0, 0.0, NEG_INF)
        pen_next = jnp.where(n < n_blocks - 1, 0.0, NEG_INF)
        sp = jnp.where(tri_prev, s[:, :ATTN_BLOCK] + pen_prev, NEG_INF)
        sn = jnp.where(tri_next, s[:, 2 * ATTN_BLOCK:3 * ATTN_BLOCK] + pen_next, NEG_INF)
        s = jnp.concatenate([sp, s[:, ATTN_BLOCK:2 * ATTN_BLOCK], sn, s[:, 3 * ATTN_BLOCK:]], axis=1)
        o = _softmax_pv(s, sink, vall)
        for g in range(N_GROUPS):
            o_ref[pl.ds(r0, ATTN_BLOCK), g * HEAD_DIM:(g + 1) * HEAD_DIM] = (
                o[g * ATTN_BLOCK:(g + 1) * ATTN_BLOCK].astype(BF16))
        return carry

    lax.fori_loop(0, TQ // ATTN_BLOCK, body, 0)


def _latent_attention(sink, q, kv, *, n_batch, seq, ctx_len):
    T = q.shape[0]
    gw = N_GROUPS * HEAD_DIM
    tiles_per_batch = seq // TQ
    ctx_blk0 = n_batch * seq // ctx_len
    grid_spec = pltpu.PrefetchScalarGridSpec(
        num_scalar_prefetch=1,
        grid=(n_batch, N_KV_HEADS, tiles_per_batch),
        in_specs=[
            pl.BlockSpec((TQ, gw), lambda b, k, i, s: (b * tiles_per_batch + i, k)),
            pl.BlockSpec((seq, HEAD_DIM), lambda b, k, i, s: (b, k)),
            pl.BlockSpec((seq, HEAD_DIM), lambda b, k, i, s: (b, N_KV_HEADS + k)),
            pl.BlockSpec((ctx_len, HEAD_DIM), lambda b, k, i, s: (ctx_blk0 + b, k)),
            pl.BlockSpec((ctx_len, HEAD_DIM), lambda b, k, i, s: (ctx_blk0 + b, N_KV_HEADS + k)),
        ],
        out_specs=pl.BlockSpec((TQ, gw), lambda b, k, i, s: (b * tiles_per_batch + i, k)),
    )
    return pl.pallas_call(
        functools.partial(_attn_kernel, n_blocks=seq // ATTN_BLOCK),
        grid_spec=grid_spec,
        out_shape=jax.ShapeDtypeStruct((T, Q_COLS), BF16),
        compiler_params=_cparams(3),
        name="latent_attention",
    )(sink, q, kv, kv, kv, kv)


def _ctx_attn_kernel(sink_ref, q_ref, kc_ref, vc_ref, a_in_ref, o_ref):
    del a_in_ref
    kvh = pl.program_id(1)
    rows = q_ref.shape[0]
    qs = _stack_heads(q_ref[...])
    s = lax.dot_general(qs, kc_ref[...], (((1,), (1,)), ((), ())), preferred_element_type=F32)
    o = _softmax_pv(s, _sink_col(sink_ref, kvh, rows), vc_ref[...])
    for g in range(N_GROUPS):
        o_ref[:, g * HEAD_DIM:(g + 1) * HEAD_DIM] = o[g * rows:(g + 1) * rows].astype(BF16)


def _context_attention(sink, q, kv, attn, *, n_batch, seq, ctx_len):
    gw = N_GROUPS * HEAD_DIM
    ctx_blk0 = n_batch * seq // ctx_len
    grid_spec = pltpu.PrefetchScalarGridSpec(
        num_scalar_prefetch=1,
        grid=(n_batch, N_KV_HEADS),
        in_specs=[
            pl.BlockSpec((ctx_len, gw), lambda b, k, s: (ctx_blk0 + b, k)),
            pl.BlockSpec((ctx_len, HEAD_DIM), lambda b, k, s: (ctx_blk0 + b, k)),
            pl.BlockSpec((ctx_len, HEAD_DIM), lambda b, k, s: (ctx_blk0 + b, N_KV_HEADS + k)),
            pl.BlockSpec(memory_space=pl.ANY),
        ],
        out_specs=pl.BlockSpec((ctx_len, gw), lambda b, k, s: (ctx_blk0 + b, k)),
    )
    return pl.pallas_call(
        _ctx_attn_kernel,
        grid_spec=grid_spec,
        out_shape=jax.ShapeDtypeStruct(attn.shape, attn.dtype),
        input_output_aliases={4: 0},
        compiler_params=_cparams(2),
        name="context_attention",
    )(sink, q, kv, kv, attn)


def _softplus(z):
    return jnp.maximum(z, 0.0) + jnp.log1p(jnp.exp(-jnp.abs(z)))


def _rglru_kernel(xf_ref, xfp_ref, xfn_ref, xb_ref, xbp_ref, xbn_ref, cw_ref, cb_ref, wcat_ref,
                  gb_ref, lam_ref, hf_ref, hb_ref, ext, a_s, b_s, carry):
    j = pl.program_id(1)
    last_j = pl.num_programs(1) - 1

    @pl.when(j == 0)
    def _():
        carry[...] = jnp.zeros_like(carry)

    seg_start = (j <= 1, (j == 0) | (j == last_j))
    seg_end = ((j == 0) | (j == last_j), j <= 1)
    mains = (xf_ref, xb_ref)
    prevs = (xfp_ref, xbp_ref)
    nexts = (xfn_ref, xbn_ref)

    for d in range(2):
        ext[d, 0:SUBLANES, :] = jnp.where(seg_start[d], 0.0, prevs[d][...])
        ext[d, SUBLANES:SUBLANES + TT, :] = mains[d][...]
        ext[d, SUBLANES + TT:, :] = jnp.where(seg_end[d], 0.0, nexts[d][...])
        xc = cb_ref[...] + ext[d, SUBLANES - 2:SUBLANES - 2 + TT, :] * cw_ref[0:1, :]
        xc = xc + ext[d, SUBLANES - 1:SUBLANES - 1 + TT, :] * cw_ref[1:2, :]
        xc = xc + ext[d, SUBLANES:SUBLANES + TT, :] * cw_ref[2:3, :]
        xc = xc + ext[d, SUBLANES + 1:SUBLANES + 1 + TT, :] * cw_ref[3:4, :]
        c_d = -LRU_C * _softplus(-lam_ref[d:d + 1, :])
        for blk in range(N_RNN_BLOCKS):
            sl = slice(blk * RNN_BLOCK, (blk + 1) * RNN_BLOCK)
            xcb = xc[:, sl]
            z = jnp.dot(xcb.astype(BF16), wcat_ref[d, blk], preferred_element_type=F32)
            r = _sigmoid(z[:, :RNN_BLOCK] + gb_ref[d, 0:1, sl])
            ig = _sigmoid(z[:, RNN_BLOCK:] + gb_ref[d, 1:2, sl])
            log_a = r * c_d[:, sl]
            a = jnp.exp(log_a)
            a_s[d, :, sl] = a
            b_s[d, :, sl] = jnp.sqrt(1.0 - a * a) * (ig * xcb)

    row = lax.broadcasted_iota(jnp.int32, (SUBLANES, D_RNN), 0)
    n_sub = TT // SUBLANES

    def scan8(a, b, reverse):
        for s in (1, 2, 4):
            if reverse:
                keep = row < SUBLANES - s
                shift = SUBLANES - s
            else:
                keep = row >= s
                shift = s
            a_sh = jnp.where(keep, pltpu.roll(a, shift, 0), 1.0)
            b_sh = jnp.where(keep, pltpu.roll(b, shift, 0), 0.0)
            b = a * b_sh + b
            a = a * a_sh
        return a, b

    def body(k, hc):
        hcf, hcb = hc
        rf = pl.multiple_of(k * SUBLANES, SUBLANES)
        rb = pl.multiple_of((n_sub - 1 - k) * SUBLANES, SUBLANES)
        af, bf = scan8(a_s[0, pl.ds(rf, SUBLANES), :], b_s[0, pl.ds(rf, SUBLANES), :], False)
        ab, bb = scan8(a_s[1, pl.ds(rb, SUBLANES), :], b_s[1, pl.ds(rb, SUBLANES), :], True)
        hf = af * hcf + bf
        hb = ab * hcb + bb
        hf_ref[pl.ds(rf, SUBLANES), :] = hf
        hb_ref[pl.ds(rb, SUBLANES), :] = hb
        return (jnp.broadcast_to(hf[SUBLANES - 1:SUBLANES, :], (SUBLANES, D_RNN)),
                jnp.broadcast_to(hb[0:1, :], (SUBLANES, D_RNN)))

    hcf, hcb = lax.fori_loop(0, n_sub, body, (carry[0], carry[1]))
    carry[0] = hcf
    carry[1] = hcb


def _rglru(x, conv_w, conv_b, wcat, gate_b, lam, *, n_batch, seq, ctx_len):
    T = x.shape[0]
    assert ctx_len == TT and seq % TT == 0
    tps = seq // TT
    n_lat_t = n_batch * tps
    per8 = TT // SUBLANES
    last8 = T // SUBLANES - 1

    def ftile(b, j):
        return jnp.where(j == 0, n_lat_t + b, b * tps + j - 1)

    def btile(b, j):
        return jnp.where(j == 0, n_lat_t + b, b * tps + tps - j)

    def main(tile):
        return pl.BlockSpec((TT, D_RNN), lambda b, j: (tile(b, j), 0))

    def prev(tile):
        return pl.BlockSpec((SUBLANES, D_RNN), lambda b, j: (jnp.maximum(tile(b, j) * per8 - 1, 0), 0))

    def nxt(tile):
        return pl.BlockSpec((SUBLANES, D_RNN),
                            lambda b, j: (jnp.minimum((tile(b, j) + 1) * per8, last8), 0))

    def const(shape):
        return pl.BlockSpec(shape, lambda b, j: (0,) * len(shape))

    return pl.pallas_call(
        _rglru_kernel,
        grid=(n_batch, tps + 1),
        in_specs=[main(ftile), prev(ftile), nxt(ftile), main(btile), prev(btile), nxt(btile),
                  const(conv_w.shape), const(conv_b.shape), const(wcat.shape), const(gate_b.shape),
                  const(lam.shape)],
        out_specs=(main(ftile), main(btile)),
        out_shape=(jax.ShapeDtypeStruct((T, D_RNN), F32), jax.ShapeDtypeStruct((T, D_RNN), F32)),
        scratch_shapes=[pltpu.VMEM((2, TT + 2 * SUBLANES, D_RNN), F32),
                        pltpu.VMEM((2, TT, D_RNN), F32),
                        pltpu.VMEM((2, TT, D_RNN), F32),
                        pltpu.VMEM((2, SUBLANES, D_RNN), F32)],
        compiler_params=_cparams(2),
        name="rglru",
    )(x, x, x, x, x, x, conv_w, conv_b, wcat, gate_b, lam)


def _merge_kernel(h_ref, attn_ref, hf_ref, hb_ref, gy_ref, sga_ref, sgr_ref, g1_ref,
                  woa_ref, wol_ref, wout_ref, o_ref):
    rec = ((hf_ref[...] + hb_ref[...]) * gy_ref[...].astype(F32)).astype(BF16)
    ta = jnp.dot(attn_ref[...], woa_ref[...], preferred_element_type=F32)
    tl = jnp.dot(rec, wol_ref[...], preferred_element_type=F32)
    m = sga_ref[...].astype(F32) * ta + sgr_ref[...].astype(F32) * tl
    y = jnp.dot(m.astype(BF16), wout_ref[...], preferred_element_type=F32)
    o_ref[...] = h_ref[...] + g1_ref[...] * y


def _merge(h, attn, hf, hb, gy, sga, sgr, mod3, woa, wol, wout, *, tiles_per_batch, n_batch):
    T = h.shape[0]

    def grp(i):
        return jnp.minimum(i // tiles_per_batch, n_batch)

    tok = pl.BlockSpec((TM, D_MODEL), lambda i: (i, 0))
    wsp = pl.BlockSpec((D_MODEL, D_MODEL), lambda i: (0, 0))
    return pl.pallas_call(
        _merge_kernel,
        grid=(T // TM,),
        in_specs=[tok, tok, tok, tok, tok, tok, tok,
                  pl.BlockSpec((None, 1, D_MODEL), lambda i: (grp(i), 0, 2)),
                  wsp, wsp, wsp],
        out_specs=tok,
        out_shape=jax.ShapeDtypeStruct((T, D_MODEL), F32),
        compiler_params=_cparams(1),
        name="merge",
    )(h, attn, hf, hb, gy, sga, sgr, mod3, woa, wol, wout)


def _swiglu_partial(n, wg, wu, wd):
    gt = jnp.dot(n, wg, preferred_element_type=F32)
    ut = jnp.dot(n, wu, preferred_element_type=F32)
    act = (gt * _sigmoid(gt) * ut).astype(BF16)
    return jnp.dot(act, wd, preferred_element_type=F32)


def _ffn_kernel(h_ref, g_ref, sh_ref, sc_ref, g2_ref, wg_ref, wu_ref, wd_ref, o_ref, n_scr, acc):
    f = pl.program_id(1)

    @pl.when(f == 0)
    def _():
        n_scr[...] = _norm_mod(h_ref[...], g_ref[...], sh_ref[...], sc_ref[...]).astype(BF16)
        acc[...] = jnp.zeros_like(acc)

    acc[...] += _swiglu_partial(n_scr[...], wg_ref[...], wu_ref[...], wd_ref[...])

    @pl.when(f == pl.num_programs(1) - 1)
    def _():
        o_ref[...] = h_ref[...] + g2_ref[...] * acc[...]


def _ffn(h, g, mod3, wg, wu, wd, *, tiles_per_batch, n_batch):
    T = h.shape[0]
    n_f = 2
    tf = D_FF // n_f
    assert tf % LANES == 0

    def grp(i):
        return jnp.minimum(i // tiles_per_batch, n_batch)

    def modspec(k):
        return pl.BlockSpec((None, 1, D_MODEL), lambda i, f: (grp(i), 0, k))

    tok = pl.BlockSpec((TM, D_MODEL), lambda i, f: (i, 0))
    return pl.pallas_call(
        _ffn_kernel,
        grid=(T // TM, n_f),
        in_specs=[tok, pl.BlockSpec((1, D_MODEL), lambda i, f: (0, 0)),
                  modspec(3), modspec(4), modspec(5),
                  pl.BlockSpec((D_MODEL, tf), lambda i, f: (0, f)),
                  pl.BlockSpec((D_MODEL, tf), lambda i, f: (0, f)),
                  pl.BlockSpec((tf, D_MODEL), lambda i, f: (f, 0))],
        out_specs=tok,
        out_shape=jax.ShapeDtypeStruct((T, D_MODEL), F32),
        scratch_shapes=[pltpu.VMEM((TM, D_MODEL), BF16), pltpu.VMEM((TM, D_MODEL), F32)],
        compiler_params=_cparams(2),
        name="dense_ffn",
    )(h, g, mod3, mod3, mod3, wg, wu, wd)


def _moe_kernel(h_ref, g_ref, sh_ref, sc_ref, g2_ref, wr_ref, wg_ref, wu_ref, wd_ref, o_ref,
                n_scr, gates, acc):
    e = pl.program_id(1)
    f = pl.program_id(2)

    @pl.when((e == 0) & (f == 0))
    def _():
        n = _norm_mod(h_ref[...], g_ref[...], sh_ref[...], sc_ref[...])
        n_scr[...] = n.astype(BF16)
        acc[...] = jnp.zeros_like(acc)
        logits = jnp.dot(n, wr_ref[...], preferred_element_type=F32, precision=lax.Precision.HIGHEST)
        lane = lax.broadcasted_iota(jnp.int32, logits.shape, 1)
        logits = jnp.where(lane < N_EXPERTS, logits, -jnp.inf)
        m1 = jnp.max(logits, axis=1, keepdims=True)
        i1 = jnp.min(jnp.where(logits == m1, lane, LANES), axis=1, keepdims=True)
        rest = jnp.where(lane == i1, -jnp.inf, logits)
        m2 = jnp.max(rest, axis=1, keepdims=True)
        i2 = jnp.min(jnp.where(rest == m2, lane, LANES), axis=1, keepdims=True)
        e2 = jnp.exp(m2 - m1)
        w1 = 1.0 / (1.0 + e2)
        w2 = e2 / (1.0 + e2)
        gates[...] = jnp.where(lane == i1, w1, jnp.where(lane == i2, w2, 0.0))

    lane = lax.broadcasted_iota(jnp.int32, gates.shape, 1)
    gate_e = jnp.sum(jnp.where(lane == e, gates[...], 0.0), axis=1, keepdims=True)
    acc[...] += gate_e * _swiglu_partial(n_scr[...], wg_ref[...], wu_ref[...], wd_ref[...])

    @pl.when((e == pl.num_programs(1) - 1) & (f == pl.num_programs(2) - 1))
    def _():
        o_ref[...] = h_ref[...] + g2_ref[...] * acc[...]


def _moe(h, g, mod3, w_r, wg, wu, wd, *, tiles_per_batch, n_batch):
    T = h.shape[0]
    n_f = 2
    tf = D_FF_EXPERT // n_f
    assert tf % LANES == 0

    def grp(i):
        return jnp.minimum(i // tiles_per_batch, n_batch)

    def modspec(k):
        return pl.BlockSpec((None, 1, D_MODEL), lambda i, e, f: (grp(i), 0, k))

    tok = pl.BlockSpec((TM, D_MODEL), lambda i, e, f: (i, 0))
    return pl.pallas_call(
        _moe_kernel,
        grid=(T // TM, N_EXPERTS, n_f),
        in_specs=[tok, pl.BlockSpec((1, D_MODEL), lambda i, e, f: (0, 0)),
                  modspec(3), modspec(4), modspec(5),
                  pl.BlockSpec((D_MODEL, LANES), lambda i, e, f: (0, 0)),
                  pl.BlockSpec((None, D_MODEL, tf), lambda i, e, f: (e, 0, f)),
                  pl.BlockSpec((None, D_MODEL, tf), lambda i, e, f: (e, 0, f)),
                  pl.BlockSpec((None, tf, D_MODEL), lambda i, e, f: (e, f, 0))],
        out_specs=tok,
        out_shape=jax.ShapeDtypeStruct((T, D_MODEL), F32),
        scratch_shapes=[pltpu.VMEM((TM, D_MODEL), BF16), pltpu.VMEM((TM, LANES), F32),
                        pltpu.VMEM((TM, D_MODEL), F32)],
        compiler_params=_cparams(3),
        name="moe_ffn",
    )(h, g, mod3, mod3, mod3, w_r, wg, wu, wd)


def _final_norm_kernel(h_ref, g_ref, o_ref):
    h = h_ref[...]
    ms = jnp.mean(h * h, axis=-1, keepdims=True)
    o_ref[...] = h * lax.rsqrt(ms + EPS) * g_ref[...]


def _final_norm(h, g, n_rows):
    tok = pl.BlockSpec((TM, D_MODEL), lambda i: (i, 0))
    return pl.pallas_call(
        _final_norm_kernel,
        grid=(n_rows // TM,),
        in_specs=[tok, pl.BlockSpec((1, D_MODEL), lambda i: (0, 0))],
        out_specs=tok,
        out_shape=jax.ShapeDtypeStruct((n_rows, D_MODEL), F32),
        compiler_params=_cparams(1),
        name="final_norm",
    )(h, g)


def _rope_tables(seq):
    t = jnp.arange(seq)
    inv = ROPE_THETA ** (-jnp.arange(ROPE_FREQS, dtype=F32) / ROPE_FREQS)
    ang_r = (t // GRID_W).astype(F32)[:, None] * inv
    ang_c = (t % GRID_W).astype(F32)[:, None] * inv
    cos = jnp.concatenate([jnp.cos(ang_r)] * 2 + [jnp.cos(ang_c)] * 2, axis=1)
    sin = jnp.concatenate([-jnp.sin(ang_r), jnp.sin(ang_r), -jnp.sin(ang_c), jnp.sin(ang_c)], axis=1)
    cos = jnp.concatenate([cos, jnp.ones((TM, HEAD_DIM), F32)], axis=0)
    sin = jnp.concatenate([sin, jnp.zeros((TM, HEAD_DIM), F32)], axis=0)
    return cos, sin


def kernel(x, c, ctx, c_ctx, w_mod, b_mod, norm1_g, norm2_g, w_in, attn_sink, conv_w, conv_b, gate_a_w, gate_a_b, gate_x_w, gate_x_b, lru_lambda, w_o_attn, w_o_lru, w_out, ff_w_gate, ff_w_up, ff_w_down, router_w, exp_w_gate, exp_w_up, exp_w_down, final_g):
    n_batch, seq, _ = x.shape
    ctx_len = ctx.shape[1]
    assert n_batch * ctx_len == TM and seq % TM == 0 and n_batch + 1 <= MOD_ROWS
    n_lat = n_batch * seq
    tiles_per_batch = seq // TM
    geo = dict(tiles_per_batch=tiles_per_batch, n_batch=n_batch)
    shp = dict(n_batch=n_batch, seq=seq, ctx_len=ctx_len)

    cpad = jnp.zeros((MOD_ROWS, D_MODEL), F32).at[:n_batch].set(c).at[n_batch].set(c_ctx)
    mod = _modulation(cpad, w_mod, b_mod)
    cos_t, sin_t = _rope_tables(seq)
    h = jnp.concatenate([x.reshape(n_lat, D_MODEL), ctx.reshape(n_batch * ctx_len, D_MODEL)], axis=0)

    for l in range(DEPTH):
        mod3 = mod[l].reshape(MOD_ROWS, 1, 6 * D_MODEL)
        q, kv, xr, gy, sga, sgr = _in_proj(
            h, norm1_g[l].reshape(1, D_MODEL), mod3, w_in[l].astype(BF16), cos_t, sin_t,
            n_lat_tiles=n_lat // TM, **geo)
        attn = _latent_attention(attn_sink[l], q, kv, **shp)
        attn = _context_attention(attn_sink[l], q, kv, attn, **shp)
        wcat = jnp.concatenate([gate_a_w[l], gate_x_w[l]], axis=-1).astype(BF16)
        gate_b = jnp.stack([gate_a_b[l], gate_x_b[l]], axis=1)
        hf, hb = _rglru(xr, conv_w[l], conv_b[l].reshape(1, D_RNN), wcat, gate_b, lru_lambda[l], **shp)
        h = _merge(h, attn, hf, hb, gy, sga, sgr, mod3, w_o_attn[l].astype(BF16),
                   w_o_lru[l].astype(BF16), w_out[l].astype(BF16), **geo)
        g2 = norm2_g[l].reshape(1, D_MODEL)
        i = l // 2
        if l % 2 == 0:
            h = _ffn(h, g2, mod3, ff_w_gate[i].astype(BF16), ff_w_up[i].astype(BF16),
                     ff_w_down[i].astype(BF16), **geo)
        else:
            w_r = jnp.zeros((D_MODEL, LANES), F32).at[:, :N_EXPERTS].set(router_w[i])
            h = _moe(h, g2, mod3, w_r, exp_w_gate[i].astype(BF16), exp_w_up[i].astype(BF16),
                     exp_w_down[i].astype(BF16), **geo)

    out = _final_norm(h, final_g.reshape(1, D_MODEL), n_lat)
    return out.reshape(n_batch, seq, D_MODEL)
```

```python
import functools

import jax
import jax.numpy as jnp
from jax import lax
from jax.experimental import pallas as pl
from jax.experimental.pallas import tpu as pltpu

F32 = jnp.float32
BF16 = jnp.bfloat16

D_MODEL = 1024
DEPTH = 4
GRID_W = 64
N_HEADS = 8
N_KV_HEADS = 2
HEAD_DIM = 128
N_GROUPS = N_HEADS // N_KV_HEADS
ATTN_BLOCK = 128
ROPE_THETA = 10000.0
ROPE_FREQS = HEAD_DIM // 4
D_RNN = 1024
N_RNN_BLOCKS = 8
RNN_BLOCK = D_RNN // N_RNN_BLOCKS
LRU_C = 8.0
D_FF = 2816
N_EXPERTS = 8
D_FF_EXPERT = 3584
EPS = 1e-6
NEG_INF = -1e30
Q_COLS = N_HEADS * HEAD_DIM
KV_COLS = N_KV_HEADS * HEAD_DIM
IN_COLS = Q_COLS + 2 * KV_COLS + 2 * D_RNN + 2 * D_MODEL
ATTN_SCALE = HEAD_DIM ** -0.5

LANES = 128
SUBLANES = 8
TM = 512
TN_IN = 512
TQ = 512
TT = 256
TG = 512
MOE_F_CHUNKS = 4
MOD_ROWS = 8
VMEM_LIMIT = 56 * 1024 * 1024


def _cparams(n_axes):
    return pltpu.CompilerParams(dimension_semantics=("arbitrary",) * n_axes,
                                vmem_limit_bytes=VMEM_LIMIT)


def _sigmoid(z):
    return 1.0 / (1.0 + jnp.exp(-z))


def _norm_mod(h, g, shift, scale):
    ms = jnp.mean(h * h, axis=-1, keepdims=True)
    y = h * lax.rsqrt(ms + EPS) * g
    return y * (1.0 + scale) + shift


def _mod_kernel(c_ref, w_ref, b_ref, o_ref):
    cv = c_ref[...]
    s = cv * _sigmoid(cv)
    o_ref[...] = jnp.dot(s, w_ref[...], preferred_element_type=F32,
                         precision=lax.Precision.HIGHEST) + b_ref[...]


def _modulation(cpad, w_mod, b_mod):
    nchunk = 6
    return pl.pallas_call(
        _mod_kernel,
        grid=(DEPTH, nchunk),
        in_specs=[
            pl.BlockSpec((MOD_ROWS, D_MODEL), lambda l, n: (0, 0)),
            pl.BlockSpec((None, D_MODEL, D_MODEL), lambda l, n: (l, 0, n)),
            pl.BlockSpec((None, 1, D_MODEL), lambda l, n: (l, 0, n)),
        ],
        out_specs=pl.BlockSpec((None, MOD_ROWS, D_MODEL), lambda l, n: (l, 0, n)),
        out_shape=jax.ShapeDtypeStruct((DEPTH, MOD_ROWS, 6 * D_MODEL), F32),
        compiler_params=_cparams(2),
        name="modulation",
    )(cpad, w_mod, b_mod.reshape(DEPTH, 1, 6 * D_MODEL))


def _rope(xh, cos, sin_signed, first_half):
    sw = jnp.where(first_half, pltpu.roll(xh, 96, 1), pltpu.roll(xh, 32, 1))
    return xh * cos + sw * sin_signed


def _in_proj_kernel(h_ref, g_ref, sh_ref, sc_ref, w_ref, cos_ref, sin_ref,
                    q_ref, kv_ref, x_ref, gy_ref, sga_ref, sgr_ref):
    n = _norm_mod(h_ref[...], g_ref[...], sh_ref[...], sc_ref[...]).astype(BF16)
    cos = cos_ref[...]
    sin = sin_ref[...]
    lane = lax.broadcasted_iota(jnp.int32, (TM, LANES), 1)
    first_half = (lane & 32) == 0

    def proj(chunk):
        return jnp.dot(n, w_ref[:, chunk * TN_IN:(chunk + 1) * TN_IN], preferred_element_type=F32)

    def rope_store(acc, n_heads, scale, ref, col0):
        for hh in range(n_heads):
            y = _rope(acc[:, hh * HEAD_DIM:(hh + 1) * HEAD_DIM], cos, sin, first_half)
            if scale != 1.0:
                y = y * scale
            ref[:, col0 + hh * HEAD_DIM:col0 + (hh + 1) * HEAD_DIM] = y.astype(BF16)

    for c in range(2):
        rope_store(proj(c), 4, ATTN_SCALE, q_ref, c * TN_IN)
    acc = proj(2)
    rope_store(acc, 2, 1.0, kv_ref, 0)
    kv_ref[:, KV_COLS:] = acc[:, KV_COLS:].astype(BF16)
    for c in range(2):
        cols = slice(c * TN_IN, (c + 1) * TN_IN)
        x_ref[:, cols] = proj(3 + c)
        gy_ref[:, cols] = jax.nn.gelu(proj(5 + c)).astype(BF16)
        sga_ref[:, cols] = _sigmoid(proj(7 + c)).astype(BF16)
        sgr_ref[:, cols] = _sigmoid(proj(9 + c)).astype(BF16)


def _in_proj(h, g, mod3, w_in, layer, cos_t, sin_t, *, n_lat_tiles, tiles_per_batch, n_batch):
    T = h.shape[0]

    def grp(i):
        return jnp.minimum(i // tiles_per_batch, n_batch)

    def pos_tile(i):
        return jnp.where(i < n_lat_tiles, i % tiles_per_batch, tiles_per_batch)

    def tok(width):
        return pl.BlockSpec((TM, width), lambda i: (i, 0))

    out_shape = (
        jax.ShapeDtypeStruct((T, Q_COLS), BF16),
        jax.ShapeDtypeStruct((T, 2 * KV_COLS), BF16),
        jax.ShapeDtypeStruct((T, D_RNN), F32),
        jax.ShapeDtypeStruct((T, D_RNN), BF16),
        jax.ShapeDtypeStruct((T, D_MODEL), BF16),
        jax.ShapeDtypeStruct((T, D_MODEL), BF16),
    )
    return pl.pallas_call(
        _in_proj_kernel,
        grid=(T // TM,),
        in_specs=[
            tok(D_MODEL),
            pl.BlockSpec((None, 1, D_MODEL), lambda i: (layer, 0, 0)),
            pl.BlockSpec((None, 1, D_MODEL), lambda i: (grp(i), 0, 0)),
            pl.BlockSpec((None, 1, D_MODEL), lambda i: (grp(i), 0, 1)),
            pl.BlockSpec((None, D_MODEL, IN_COLS), lambda i: (layer, 0, 0)),
            pl.BlockSpec((TM, HEAD_DIM), lambda i: (pos_tile(i), 0)),
            pl.BlockSpec((TM, HEAD_DIM), lambda i: (pos_tile(i), 0)),
        ],
        out_specs=(tok(Q_COLS), tok(2 * KV_COLS), tok(D_RNN), tok(D_RNN), tok(D_MODEL), tok(D_MODEL)),
        out_shape=out_shape,
        compiler_params=_cparams(1),
        name="in_proj",
    )(h, g, mod3, mod3, w_in, cos_t, sin_t)


def _stack_heads(qt):
    return jnp.concatenate([qt[:, g * HEAD_DIM:(g + 1) * HEAD_DIM] for g in range(N_GROUPS)], axis=0)


def _sink_col(sink_ref, kvh, rows):
    return jnp.concatenate(
        [jnp.full((rows, 1), sink_ref[kvh * N_GROUPS + g], F32) for g in range(N_GROUPS)], axis=0)


def _softmax_pv(s, sink, vall):
    m = jnp.maximum(jnp.max(s, axis=1, keepdims=True), sink)
    p = jnp.exp(s - m)
    denom = jnp.sum(p, axis=1, keepdims=True) + jnp.exp(sink - m)
    o = jnp.dot(p.astype(BF16), vall, preferred_element_type=F32)
    return o / denom


def _attn_kernel(sink_ref, q_ref, k_ref, v_ref, kc_ref, vc_ref, o_ref, *, n_blocks):
    kvh = pl.program_id(1)
    i = pl.program_id(2)
    rows = ATTN_BLOCK * N_GROUPS
    qi = lax.broadcasted_iota(jnp.int32, (rows, ATTN_BLOCK), 0) & (ATTN_BLOCK - 1)
    kj = lax.broadcasted_iota(jnp.int32, (rows, ATTN_BLOCK), 1)
    tri_prev = kj >= qi
    tri_next = kj <= qi
    sink = _sink_col(sink_ref, kvh, ATTN_BLOCK)
    kc = kc_ref[...]
    vc = vc_ref[...]

    def body(qb, carry):
        n = i * (TQ // ATTN_BLOCK) + qb
        r0 = pl.multiple_of(qb * ATTN_BLOCK, ATTN_BLOCK)
        p0 = pl.multiple_of(jnp.maximum(n - 1, 0) * ATTN_BLOCK, ATTN_BLOCK)
        c0 = pl.multiple_of(n * ATTN_BLOCK, ATTN_BLOCK)
        n0 = pl.multiple_of(jnp.minimum(n + 1, n_blocks - 1) * ATTN_BLOCK, ATTN_BLOCK)
        qs = _stack_heads(q_ref[pl.ds(r0, ATTN_BLOCK), :])
        kall = jnp.concatenate([k_ref[pl.ds(p0, ATTN_BLOCK), :], k_ref[pl.ds(c0, ATTN_BLOCK), :],
                                k_ref[pl.ds(n0, ATTN_BLOCK), :], kc], axis=0)
        vall = jnp.concatenate([v_ref[pl.ds(p0, ATTN_BLOCK), :], v_ref[pl.ds(c0, ATTN_BLOCK), :],
                                v_ref[pl.ds(n0, ATTN_BLOCK), :], vc], axis=0)
        s = lax.dot_general(qs, kall, (((1,), (1,)), ((), ())), preferred_element_type=F32)
        pen_prev = jnp.where(n > 0, 0.0, NEG_INF)
        pen_next = jnp.where(n < n_blocks - 1, 0.0, NEG_INF)
        sp = jnp.where(tri_prev, s[:, :ATTN_BLOCK] + pen_prev, NEG_INF)
        sn = jnp.where(tri_next, s[:, 2 * ATTN_BLOCK:3 * ATTN_BLOCK] + pen_next, NEG_INF)
        s = jnp.concatenate([sp, s[:, ATTN_BLOCK:2 * ATTN_BLOCK], sn, s[:, 3 * ATTN_BLOCK:]], axis=1)
        o = _softmax_pv(s, sink, vall)
        for g in range(N_GROUPS):
            o_ref[pl.ds(r0, ATTN_BLOCK), g * HEAD_DIM:(g + 1) * HEAD_DIM] = (
                o[g * ATTN_BLOCK:(g + 1) * ATTN_BLOCK].astype(BF16))
        return carry

    lax.fori_loop(0, TQ // ATTN_BLOCK, body, 0)


def _latent_attention(sink, q, kv, *, n_batch, seq, ctx_len):
    gw = N_GROUPS * HEAD_DIM
    tiles_per_batch = seq // TQ
    ctx_blk0 = n_batch * seq // ctx_len
    grid_spec = pltpu.PrefetchScalarGridSpec(
        num_scalar_prefetch=1,
        grid=(n_batch, N_KV_HEADS, tiles_per_batch),
        in_specs=[
            pl.BlockSpec((TQ, gw), lambda b, k, i, s: (b * tiles_per_batch + i, k)),
            pl.BlockSpec((seq, HEAD_DIM), lambda b, k, i, s: (b, k)),
            pl.BlockSpec((seq, HEAD_DIM), lambda b, k, i, s: (b, N_KV_HEADS + k)),
            pl.BlockSpec((ctx_len, HEAD_DIM), lambda b, k, i, s: (ctx_blk0 + b, k)),
            pl.BlockSpec((ctx_len, HEAD_DIM), lambda b, k, i, s: (ctx_blk0 + b, N_KV_HEADS + k)),
        ],
        out_specs=pl.BlockSpec((TQ, gw), lambda b, k, i, s: (b * tiles_per_batch + i, k)),
    )
    return pl.pallas_call(
        functools.partial(_attn_kernel, n_blocks=seq // ATTN_BLOCK),
        grid_spec=grid_spec,
        out_shape=jax.ShapeDtypeStruct((n_batch * seq, Q_COLS), BF16),
        compiler_params=_cparams(3),
        name="latent_attention",
    )(sink, q, kv, kv, kv, kv)


def _ctx_attn_kernel(sink_ref, q_ref, kc_ref, vc_ref, o_ref):
    kvh = pl.program_id(1)
    rows = q_ref.shape[0]
    qs = _stack_heads(q_ref[...])
    s = lax.dot_general(qs, kc_ref[...], (((1,), (1,)), ((), ())), preferred_element_type=F32)
    o = _softmax_pv(s, _sink_col(sink_ref, kvh, rows), vc_ref[...])
    for g in range(N_GROUPS):
        o_ref[:, g * HEAD_DIM:(g + 1) * HEAD_DIM] = o[g * rows:(g + 1) * rows].astype(BF16)


def _context_attention(sink, q, kv, *, n_batch, seq, ctx_len):
    gw = N_GROUPS * HEAD_DIM
    ctx_blk0 = n_batch * seq // ctx_len
    grid_spec = pltpu.PrefetchScalarGridSpec(
        num_scalar_prefetch=1,
        grid=(n_batch, N_KV_HEADS),
        in_specs=[
            pl.BlockSpec((ctx_len, gw), lambda b, k, s: (ctx_blk0 + b, k)),
            pl.BlockSpec((ctx_len, HEAD_DIM), lambda b, k, s: (ctx_blk0 + b, k)),
            pl.BlockSpec((ctx_len, HEAD_DIM), lambda b, k, s: (ctx_blk0 + b, N_KV_HEADS + k)),
        ],
        out_specs=pl.BlockSpec((ctx_len, gw), lambda b, k, s: (b, k)),
    )
    return pl.pallas_call(
        _ctx_attn_kernel,
        grid_spec=grid_spec,
        out_shape=jax.ShapeDtypeStruct((n_batch * ctx_len, Q_COLS), BF16),
        compiler_params=_cparams(2),
        name="context_attention",
    )(sink, q, kv, kv)


def _softplus(z):
    return jnp.maximum(z, 0.0) + jnp.log1p(jnp.exp(-jnp.abs(z)))


def _rglru_kernel(xf_ref, xfp_ref, xfn_ref, xb_ref, xbp_ref, xbn_ref, cw_ref, cb_ref, wcat_ref,
                  gb_ref, lam_ref, hf_ref, hb_ref, ext, a_s, b_s, carry):
    j = pl.program_id(1)
    last_j = pl.num_programs(1) - 1

    @pl.when(j == 0)
    def _():
        carry[...] = jnp.zeros_like(carry)

    seg_start = (j <= 1, (j == 0) | (j == last_j))
    seg_end = ((j == 0) | (j == last_j), j <= 1)
    mains = (xf_ref, xb_ref)
    prevs = (xfp_ref, xbp_ref)
    nexts = (xfn_ref, xbn_ref)

    for d in range(2):
        ext[d, 0:SUBLANES, :] = jnp.where(seg_start[d], 0.0, prevs[d][...])
        ext[d, SUBLANES:SUBLANES + TT, :] = mains[d][...]
        ext[d, SUBLANES + TT:, :] = jnp.where(seg_end[d], 0.0, nexts[d][...])
        xc = cb_ref[...] + ext[d, SUBLANES - 2:SUBLANES - 2 + TT, :] * cw_ref[0:1, :]
        xc = xc + ext[d, SUBLANES - 1:SUBLANES - 1 + TT, :] * cw_ref[1:2, :]
        xc = xc + ext[d, SUBLANES:SUBLANES + TT, :] * cw_ref[2:3, :]
        xc = xc + ext[d, SUBLANES + 1:SUBLANES + 1 + TT, :] * cw_ref[3:4, :]
        c_d = -LRU_C * _softplus(-lam_ref[d:d + 1, :])
        for blk in range(N_RNN_BLOCKS):
            sl = slice(blk * RNN_BLOCK, (blk + 1) * RNN_BLOCK)
            xcb = xc[:, sl]
            z = jnp.dot(xcb.astype(BF16), wcat_ref[d, blk], preferred_element_type=F32)
            r = _sigmoid(z[:, :RNN_BLOCK] + gb_ref[d, 0:1, sl])
            ig = _sigmoid(z[:, RNN_BLOCK:] + gb_ref[d, 1:2, sl])
            log_a = r * c_d[:, sl]
            a = jnp.exp(log_a)
            a_s[d, :, sl] = a
            b_s[d, :, sl] = jnp.sqrt(1.0 - a * a) * (ig * xcb)

    row = lax.broadcasted_iota(jnp.int32, (SUBLANES, D_RNN), 0)
    n_sub = TT // SUBLANES

    def scan8(a, b, reverse):
        for s in (1, 2, 4):
            if reverse:
                keep = row < SUBLANES - s
                shift = SUBLANES - s
            else:
                keep = row >= s
                shift = s
            a_sh = jnp.where(keep, pltpu.roll(a, shift, 0), 1.0)
            b_sh = jnp.where(keep, pltpu.roll(b, shift, 0), 0.0)
            b = a * b_sh + b
            a = a * a_sh
        return a, b

    def body(k, hc):
        hcf, hcb = hc
        rf = pl.multiple_of(k * SUBLANES, SUBLANES)
        rb = pl.multiple_of((n_sub - 1 - k) * SUBLANES, SUBLANES)
        af, bf = scan8(a_s[0, pl.ds(rf, SUBLANES), :], b_s[0, pl.ds(rf, SUBLANES), :], False)
        ab, bb = scan8(a_s[1, pl.ds(rb, SUBLANES), :], b_s[1, pl.ds(rb, SUBLANES), :], True)
        hf = af * hcf + bf
        hb = ab * hcb + bb
        hf_ref[pl.ds(rf, SUBLANES), :] = hf
        hb_ref[pl.ds(rb, SUBLANES), :] = hb
        return (jnp.broadcast_to(hf[SUBLANES - 1:SUBLANES, :], (SUBLANES, D_RNN)),
                jnp.broadcast_to(hb[0:1, :], (SUBLANES, D_RNN)))

    hcf, hcb = lax.fori_loop(0, n_sub, body, (carry[0], carry[1]))
    carry[0] = hcf
    carry[1] = hcb


def _rglru(x, conv_w, conv_b, wcat, gate_b, lam, *, n_batch, seq, ctx_len):
    T = x.shape[0]
    assert ctx_len == TT and seq % TT == 0
    tps = seq // TT
    n_lat_t = n_batch * tps
    per8 = TT // SUBLANES
    last8 = T // SUBLANES - 1

    def ftile(b, j):
        return jnp.where(j == 0, n_lat_t + b, b * tps + j - 1)

    def btile(b, j):
        return jnp.where(j == 0, n_lat_t + b, b * tps + tps - j)

    def main(tile):
        return pl.BlockSpec((TT, D_RNN), lambda b, j: (tile(b, j), 0))

    def prev(tile):
        return pl.BlockSpec((SUBLANES, D_RNN), lambda b, j: (jnp.maximum(tile(b, j) * per8 - 1, 0), 0))

    def nxt(tile):
        return pl.BlockSpec((SUBLANES, D_RNN),
                            lambda b, j: (jnp.minimum((tile(b, j) + 1) * per8, last8), 0))

    def const(shape):
        return pl.BlockSpec(shape, lambda b, j: (0,) * len(shape))

    return pl.pallas_call(
        _rglru_kernel,
        grid=(n_batch, tps + 1),
        in_specs=[main(ftile), prev(ftile), nxt(ftile), main(btile), prev(btile), nxt(btile),
                  const(conv_w.shape), const(conv_b.shape), const(wcat.shape), const(gate_b.shape),
                  const(lam.shape)],
        out_specs=(main(ftile), main(btile)),
        out_shape=(jax.ShapeDtypeStruct((T, D_RNN), F32), jax.ShapeDtypeStruct((T, D_RNN), F32)),
        scratch_shapes=[pltpu.VMEM((2, TT + 2 * SUBLANES, D_RNN), F32),
                        pltpu.VMEM((2, TT, D_RNN), F32),
                        pltpu.VMEM((2, TT, D_RNN), F32),
                        pltpu.VMEM((2, SUBLANES, D_RNN), F32)],
        compiler_params=_cparams(2),
        name="rglru",
    )(x, x, x, x, x, x, conv_w, conv_b, wcat, gate_b, lam)


def _merge_kernel(h_ref, attn_ref, attn_ctx_ref, hf_ref, hb_ref, gy_ref, sga_ref, sgr_ref, g1_ref,
                  woa_ref, wol_ref, wout_ref, o_ref):
    rec = ((hf_ref[...] + hb_ref[...]) * gy_ref[...].astype(F32)).astype(BF16)
    is_ctx_tile = pl.program_id(0) == pl.num_programs(0) - 1
    attn = jnp.where(is_ctx_tile, attn_ctx_ref[...], attn_ref[...])
    ta = jnp.dot(attn, woa_ref[...], preferred_element_type=F32)
    tl = jnp.dot(rec, wol_ref[...], preferred_element_type=F32)
    m = sga_ref[...].astype(F32) * ta + sgr_ref[...].astype(F32) * tl
    y = jnp.dot(m.astype(BF16), wout_ref[...], preferred_element_type=F32)
    o_ref[...] = h_ref[...] + g1_ref[...] * y


def _merge(h, attn, attn_ctx, hf, hb, gy, sga, sgr, mod3, woa, wol, wout, layer, *, tiles_per_batch, n_batch):
    T = h.shape[0]
    n_lat_tiles = attn.shape[0] // TM
    assert attn_ctx.shape[0] == TM and n_lat_tiles == T // TM - 1

    def grp(i):
        return jnp.minimum(i // tiles_per_batch, n_batch)

    tok = pl.BlockSpec((TM, D_MODEL), lambda i: (i, 0))
    wsp = pl.BlockSpec((None, D_MODEL, D_MODEL), lambda i: (layer, 0, 0))
    return pl.pallas_call(
        _merge_kernel,
        grid=(T // TM,),
        in_specs=[tok,
                  pl.BlockSpec((TM, Q_COLS), lambda i: (jnp.minimum(i, n_lat_tiles - 1), 0)),
                  pl.BlockSpec((TM, Q_COLS), lambda i: (0, 0)),
                  tok, tok, tok, tok, tok,
                  pl.BlockSpec((None, 1, D_MODEL), lambda i: (grp(i), 0, 2)),
                  wsp, wsp, wsp],
        out_specs=tok,
        out_shape=jax.ShapeDtypeStruct((T, D_MODEL), F32),
        compiler_params=_cparams(1),
        name="merge",
    )(h, attn, attn_ctx, hf, hb, gy, sga, sgr, mod3, woa, wol, wout)


def _swiglu_partial(n, wg, wu, wd):
    gt = jnp.dot(n, wg, preferred_element_type=F32)
    ut = jnp.dot(n, wu, preferred_element_type=F32)
    act = (gt * _sigmoid(gt) * ut).astype(BF16)
    return jnp.dot(act, wd, preferred_element_type=F32)


def _ffn_kernel(h_ref, g_ref, sh_ref, sc_ref, g2_ref, wg_ref, wu_ref, wd_ref, o_ref, n_scr, acc):
    f = pl.program_id(1)

    @pl.when(f == 0)
    def _():
        n_scr[...] = _norm_mod(h_ref[...], g_ref[...], sh_ref[...], sc_ref[...]).astype(BF16)
        acc[...] = jnp.zeros_like(acc)

    acc[...] += _swiglu_partial(n_scr[...], wg_ref[...], wu_ref[...], wd_ref[...])

    @pl.when(f == pl.num_programs(1) - 1)
    def _():
        o_ref[...] = h_ref[...] + g2_ref[...] * acc[...]


def _ffn(h, g, mod3, wg, wu, wd, layer, ff_layer, *, tiles_per_batch, n_batch):
    T = h.shape[0]
    n_f = 2
    tf = D_FF // n_f
    assert tf % LANES == 0

    def grp(i):
        return jnp.minimum(i // tiles_per_batch, n_batch)

    def modspec(k):
        return pl.BlockSpec((None, 1, D_MODEL), lambda i, f: (grp(i), 0, k))

    tok = pl.BlockSpec((TM, D_MODEL), lambda i, f: (i, 0))
    return pl.pallas_call(
        _ffn_kernel,
        grid=(T // TM, n_f),
        in_specs=[tok, pl.BlockSpec((None, 1, D_MODEL), lambda i, f: (layer, 0, 0)),
                  modspec(3), modspec(4), modspec(5),
                  pl.BlockSpec((None, D_MODEL, tf), lambda i, f: (ff_layer, 0, f)),
                  pl.BlockSpec((None, D_MODEL, tf), lambda i, f: (ff_layer, 0, f)),
                  pl.BlockSpec((None, tf, D_MODEL), lambda i, f: (ff_layer, f, 0))],
        out_specs=tok,
        out_shape=jax.ShapeDtypeStruct((T, D_MODEL), F32),
        scratch_shapes=[pltpu.VMEM((TM, D_MODEL), BF16), pltpu.VMEM((TM, D_MODEL), F32)],
        compiler_params=_cparams(2),
        name="dense_ffn",
    )(h, g, mod3, mod3, mod3, wg, wu, wd)


ROUTE_E1, ROUTE_E2, ROUTE_W1, ROUTE_W2, ROUTE_R1, ROUTE_R2 = range(6)


def _router_kernel(h_ref, g_ref, sh_ref, sc_ref, wr_ref, route_ref, cnt_ref, run):
    @pl.when(pl.program_id(0) == 0)
    def _():
        run[...] = jnp.zeros_like(run)

    n = _norm_mod(h_ref[...], g_ref[...], sh_ref[...], sc_ref[...])
    logits = jnp.dot(n, wr_ref[...], preferred_element_type=F32, precision=lax.Precision.HIGHEST)
    lane = lax.broadcasted_iota(jnp.int32, logits.shape, 1)
    logits = jnp.where(lane < N_EXPERTS, logits, -jnp.inf)
    m1 = jnp.max(logits, axis=1, keepdims=True)
    i1 = jnp.min(jnp.where(logits == m1, lane, LANES), axis=1, keepdims=True)
    rest = jnp.where(lane == i1, -jnp.inf, logits)
    m2 = jnp.max(rest, axis=1, keepdims=True)
    i2 = jnp.min(jnp.where(rest == m2, lane, LANES), axis=1, keepdims=True)
    e2 = jnp.exp(m2 - m1)
    w1 = 1.0 / (1.0 + e2)
    w2 = e2 / (1.0 + e2)

    hit1 = lane == i1
    hit2 = lane == i2
    onehot = jnp.where(hit1 | hit2, 1.0, 0.0)
    r_i = lax.broadcasted_iota(jnp.int32, (TM, TM), 0)
    c_i = lax.broadcasted_iota(jnp.int32, (TM, TM), 1)
    lower = jnp.where(c_i < r_i, 1.0, 0.0).astype(BF16)
    prefix = jnp.dot(lower, onehot.astype(BF16), preferred_element_type=F32) + run[0:1, :]
    rank1 = jnp.sum(jnp.where(hit1, prefix, 0.0), axis=1, keepdims=True)
    rank2 = jnp.sum(jnp.where(hit2, prefix, 0.0), axis=1, keepdims=True)
    run[...] = run[...] + jnp.sum(onehot, axis=0, keepdims=True)
    cnt_ref[...] = run[...]

    rec = jnp.zeros(logits.shape, F32)
    for k, v in ((ROUTE_E1, i1.astype(F32)), (ROUTE_E2, i2.astype(F32)), (ROUTE_W1, w1), (ROUTE_W2, w2),
                 (ROUTE_R1, rank1), (ROUTE_R2, rank2)):
        rec = jnp.where(lane == k, v, rec)
    route_ref[...] = rec


def _router(h, g2, mod3, w_r, layer, *, tiles_per_batch, n_batch):
    T = h.shape[0]

    def grp(i):
        return jnp.minimum(i // tiles_per_batch, n_batch)

    def modspec(k):
        return pl.BlockSpec((None, 1, D_MODEL), lambda i: (grp(i), 0, k))

    return pl.pallas_call(
        _router_kernel,
        grid=(T // TM,),
        in_specs=[pl.BlockSpec((TM, D_MODEL), lambda i: (i, 0)),
                  pl.BlockSpec((None, 1, D_MODEL), lambda i: (layer, 0, 0)),
                  modspec(3), modspec(4),
                  pl.BlockSpec((D_MODEL, LANES), lambda i: (0, 0))],
        out_specs=(pl.BlockSpec((TM, LANES), lambda i: (i, 0)),
                   pl.BlockSpec((SUBLANES, LANES), lambda i: (0, 0))),
        out_shape=(jax.ShapeDtypeStruct((T, LANES), F32), jax.ShapeDtypeStruct((SUBLANES, LANES), F32)),
        scratch_shapes=[pltpu.VMEM((SUBLANES, LANES), F32)],
        compiler_params=_cparams(1),
        name="moe_router",
    )(h, g2, mod3, mod3, w_r)


def _row_copy(src_ref, src_row, dst_ref, dst_row, sem):
    return pltpu.make_async_copy(src_ref.at[pl.ds(src_row, 1)], dst_ref.at[pl.ds(dst_row, 1)], sem)


def _dispatch_kernel(dest_ref, h_ref, g_ref, sh_ref, sc_ref, xs_in_ref, xs_ref, n_scr, sem):
    del xs_in_ref
    base = pl.program_id(0) * (2 * TM)
    n_scr[...] = _norm_mod(h_ref[...], g_ref[...], sh_ref[...], sc_ref[...])

    def start(r, carry):
        for s in range(2):
            _row_copy(n_scr, r, xs_ref, dest_ref[base + 2 * r + s], sem).start()
        return carry

    def wait(r, carry):
        for s in range(2):
            _row_copy(n_scr, 0, xs_ref, 0, sem).wait()
        return carry

    lax.fori_loop(0, TM, start, 0)
    lax.fori_loop(0, TM, wait, 0)


def _dispatch(dest, h, g2, mod3, xs_zero, layer, *, tiles_per_batch, n_batch):
    T = h.shape[0]

    def grp(i):
        return jnp.minimum(i // tiles_per_batch, n_batch)

    def modspec(k):
        return pl.BlockSpec((None, 1, D_MODEL), lambda i, d: (grp(i), 0, k))

    grid_spec = pltpu.PrefetchScalarGridSpec(
        num_scalar_prefetch=1,
        grid=(T // TM,),
        in_specs=[pl.BlockSpec((TM, D_MODEL), lambda i, d: (i, 0)),
                  pl.BlockSpec((None, 1, D_MODEL), lambda i, d: (layer, 0, 0)),
                  modspec(3), modspec(4),
                  pl.BlockSpec(memory_space=pl.ANY)],
        out_specs=pl.BlockSpec(memory_space=pl.ANY),
        scratch_shapes=[pltpu.VMEM((TM, D_MODEL), F32), pltpu.SemaphoreType.DMA(())],
    )
    return pl.pallas_call(
        _dispatch_kernel,
        grid_spec=grid_spec,
        out_shape=jax.ShapeDtypeStruct(xs_zero.shape, xs_zero.dtype),
        input_output_aliases={5: 0},
        compiler_params=_cparams(1),
        name="moe_dispatch",
    )(dest, h, g2, mod3, mod3, xs_zero)


def _expert_kernel(te_ref, nu_ref, x_ref, wg_ref, wu_ref, wd_ref, o_ref, xb, acc):
    k = pl.program_id(0)
    f = pl.program_id(1)

    @pl.when(f == 0)
    def _():
        xb[...] = x_ref[...].astype(BF16)
        acc[...] = jnp.zeros_like(acc)

    @pl.when(k < nu_ref[0])
    def _():
        acc[...] += _swiglu_partial(xb[...], wg_ref[...].astype(BF16), wu_ref[...].astype(BF16),
                                    wd_ref[...].astype(BF16))

    @pl.when(f == pl.num_programs(1) - 1)
    def _():
        o_ref[...] = acc[...]


def _experts(tile_expert, n_used, xs, wg, wu, wd, layer):
    P = xs.shape[0]
    n_f = MOE_F_CHUNKS
    tf = D_FF_EXPERT // n_f
    assert tf % LANES == 0

    def fsel(k, f, nu):
        return jnp.where(k < nu[0], f, n_f - 1)

    grid_spec = pltpu.PrefetchScalarGridSpec(
        num_scalar_prefetch=2,
        grid=(P // TG, n_f),
        in_specs=[pl.BlockSpec((TG, D_MODEL), lambda k, f, te, nu: (k, 0)),
                  pl.BlockSpec((None, None, D_MODEL, tf), lambda k, f, te, nu: (layer, te[k], 0, fsel(k, f, nu))),
                  pl.BlockSpec((None, None, D_MODEL, tf), lambda k, f, te, nu: (layer, te[k], 0, fsel(k, f, nu))),
                  pl.BlockSpec((None, None, tf, D_MODEL), lambda k, f, te, nu: (layer, te[k], fsel(k, f, nu), 0))],
        out_specs=pl.BlockSpec((TG, D_MODEL), lambda k, f, te, nu: (k, 0)),
        scratch_shapes=[pltpu.VMEM((TG, D_MODEL), BF16), pltpu.VMEM((TG, D_MODEL), F32)],
    )
    return pl.pallas_call(
        _expert_kernel,
        grid_spec=grid_spec,
        out_shape=jax.ShapeDtypeStruct((P, D_MODEL), F32),
        compiler_params=_cparams(2),
        name="moe_experts",
    )(tile_expert, n_used, xs, wg, wu, wd)


def _combine_kernel(dest_ref, h_ref, g2_ref, route_ref, y_ref, o_ref, ybuf, sem):
    base = pl.program_id(0) * (2 * TM)

    def start(r, carry):
        for s in range(2):
            _row_copy(y_ref, dest_ref[base + 2 * r + s], ybuf, s * TM + r, sem).start()
        return carry

    def wait(r, carry):
        for s in range(2):
            _row_copy(y_ref, 0, ybuf, 0, sem).wait()
        return carry

    lax.fori_loop(0, TM, start, 0)
    lax.fori_loop(0, TM, wait, 0)
    route = route_ref[...]
    f = route[:, ROUTE_W1:ROUTE_W1 + 1] * ybuf[0:TM, :] + route[:, ROUTE_W2:ROUTE_W2 + 1] * ybuf[TM:, :]
    o_ref[...] = h_ref[...] + g2_ref[...] * f


def _combine(dest, h, mod3, route, y, *, tiles_per_batch, n_batch):
    T = h.shape[0]

    def grp(i):
        return jnp.minimum(i // tiles_per_batch, n_batch)

    grid_spec = pltpu.PrefetchScalarGridSpec(
        num_scalar_prefetch=1,
        grid=(T // TM,),
        in_specs=[pl.BlockSpec((TM, D_MODEL), lambda i, d: (i, 0)),
                  pl.BlockSpec((None, 1, D_MODEL), lambda i, d: (grp(i), 0, 5)),
                  pl.BlockSpec((TM, LANES), lambda i, d: (i, 0)),
                  pl.BlockSpec(memory_space=pl.ANY)],
        out_specs=pl.BlockSpec((TM, D_MODEL), lambda i, d: (i, 0)),
        scratch_shapes=[pltpu.VMEM((2 * TM, D_MODEL), F32), pltpu.SemaphoreType.DMA(())],
    )
    return pl.pallas_call(
        _combine_kernel,
        grid_spec=grid_spec,
        out_shape=jax.ShapeDtypeStruct((T, D_MODEL), F32),
        compiler_params=_cparams(1),
        name="moe_combine",
    )(dest, h, mod3, route, y)


def _moe_layer(h, g2, mod3, w_r, wg, wu, wd, layer, moe_layer, **geo):
    T = h.shape[0]
    n_tiles = (2 * T) // TG + N_EXPERTS
    route, cnt = _router(h, g2, mod3, w_r, layer, **geo)

    counts = cnt[0, :N_EXPERTS].astype(jnp.int32)
    padded = ((counts + TG - 1) // TG) * TG
    ends = jnp.cumsum(padded)
    offs = ends - padded
    e12 = route[:, ROUTE_E1:ROUTE_E2 + 1].astype(jnp.int32)
    r12 = route[:, ROUTE_R1:ROUTE_R2 + 1].astype(jnp.int32)
    onehot = e12[:, :, None] == jnp.arange(N_EXPERTS)[None, None, :]
    dest = (jnp.sum(jnp.where(onehot, offs[None, None, :], 0), axis=-1) + r12).reshape(2 * T)
    n_used = (ends[-1] // TG).reshape(1)
    tile_expert = jnp.sum(jnp.arange(n_tiles)[:, None] >= (ends // TG)[None, :], axis=1)
    tile_expert = jnp.minimum(tile_expert, N_EXPERTS - 1).astype(jnp.int32)

    xs = _dispatch(dest, h, g2, mod3, jnp.zeros((n_tiles * TG, D_MODEL), F32), layer, **geo)
    y = _experts(tile_expert, n_used, xs, wg, wu, wd, moe_layer)
    return _combine(dest, h, mod3, route, y, **geo)


def _final_norm_kernel(h_ref, g_ref, o_ref):
    h = h_ref[...]
    ms = jnp.mean(h * h, axis=-1, keepdims=True)
    o_ref[...] = h * lax.rsqrt(ms + EPS) * g_ref[...]


def _final_norm(h, g, n_rows):
    tok = pl.BlockSpec((TM, D_MODEL), lambda i: (i, 0))
    return pl.pallas_call(
        _final_norm_kernel,
        grid=(n_rows // TM,),
        in_specs=[tok, pl.BlockSpec((1, D_MODEL), lambda i: (0, 0))],
        out_specs=tok,
        out_shape=jax.ShapeDtypeStruct((n_rows, D_MODEL), F32),
        compiler_params=_cparams(1),
        name="final_norm",
    )(h, g)


def _rope_tables(seq):
    t = jnp.arange(seq)
    inv = ROPE_THETA ** (-jnp.arange(ROPE_FREQS, dtype=F32) / ROPE_FREQS)
    ang_r = (t // GRID_W).astype(F32)[:, None] * inv
    ang_c = (t % GRID_W).astype(F32)[:, None] * inv
    cos = jnp.concatenate([jnp.cos(ang_r)] * 2 + [jnp.cos(ang_c)] * 2, axis=1)
    sin = jnp.concatenate([-jnp.sin(ang_r), jnp.sin(ang_r), -jnp.sin(ang_c), jnp.sin(ang_c)], axis=1)
    cos = jnp.concatenate([cos, jnp.ones((TM, HEAD_DIM), F32)], axis=0)
    sin = jnp.concatenate([sin, jnp.zeros((TM, HEAD_DIM), F32)], axis=0)
    return cos, sin


def kernel(x, c, ctx, c_ctx, w_mod, b_mod, norm1_g, norm2_g, w_in, attn_sink, conv_w, conv_b, gate_a_w, gate_a_b, gate_x_w, gate_x_b, lru_lambda, w_o_attn, w_o_lru, w_out, ff_w_gate, ff_w_up, ff_w_down, router_w, exp_w_gate, exp_w_up, exp_w_down, final_g):
    n_batch, seq, _ = x.shape
    ctx_len = ctx.shape[1]
    assert n_batch * ctx_len == TM and seq % TM == 0 and n_batch + 1 <= MOD_ROWS
    n_lat = n_batch * seq
    tiles_per_batch = seq // TM
    geo = dict(tiles_per_batch=tiles_per_batch, n_batch=n_batch)
    shp = dict(n_batch=n_batch, seq=seq, ctx_len=ctx_len)

    cpad = jnp.zeros((MOD_ROWS, D_MODEL), F32).at[:n_batch].set(c).at[n_batch].set(c_ctx)
    mod = _modulation(cpad, w_mod, b_mod)
    cos_t, sin_t = _rope_tables(seq)
    h = jnp.concatenate([x.reshape(n_lat, D_MODEL), ctx.reshape(n_batch * ctx_len, D_MODEL)], axis=0)

    g1 = norm1_g.reshape(DEPTH, 1, D_MODEL)
    g2 = norm2_g.reshape(DEPTH, 1, D_MODEL)
    w_in_b = w_in.astype(BF16)
    woa_b, wol_b, wout_b = w_o_attn.astype(BF16), w_o_lru.astype(BF16), w_out.astype(BF16)
    ffg_b, ffu_b, ffd_b = ff_w_gate.astype(BF16), ff_w_up.astype(BF16), ff_w_down.astype(BF16)

    for l in range(DEPTH):
        mod3 = mod[l].reshape(MOD_ROWS, 1, 6 * D_MODEL)
        q, kv, xr, gy, sga, sgr = _in_proj(h, g1, mod3, w_in_b, l, cos_t, sin_t,
                                           n_lat_tiles=n_lat // TM, **geo)
        attn = _latent_attention(attn_sink[l], q, kv, **shp)
        attn_ctx = _context_attention(attn_sink[l], q, kv, **shp)
        wcat = jnp.concatenate([gate_a_w[l], gate_x_w[l]], axis=-1).astype(BF16)
        gate_b = jnp.stack([gate_a_b[l], gate_x_b[l]], axis=1)
        hf, hb = _rglru(xr, conv_w[l], conv_b[l].reshape(1, D_RNN), wcat, gate_b, lru_lambda[l], **shp)
        h = _merge(h, attn, attn_ctx, hf, hb, gy, sga, sgr, mod3, woa_b, wol_b, wout_b, l, **geo)
        i = l // 2
        if l % 2 == 0:
            h = _ffn(h, g2, mod3, ffg_b, ffu_b, ffd_b, l, i, **geo)
        else:
            w_r = jnp.zeros((D_MODEL, LANES), F32).at[:, :N_EXPERTS].set(router_w[i])
            h = _moe_layer(h, g2, mod3, w_r, exp_w_gate, exp_w_up, exp_w_down, l, i, **geo)

    out = _final_norm(h, final_g.reshape(1, D_MODEL), n_lat)
    return out.reshape(n_batch, seq, D_MODEL)
```

```python
import functools

import jax
import jax.numpy as jnp
from jax import lax
from jax.experimental import pallas as pl
from jax.experimental.pallas import tpu as pltpu

F32 = jnp.float32
BF16 = jnp.bfloat16

D_MODEL = 1024
DEPTH = 4
GRID_W = 64
N_HEADS = 8
N_KV_HEADS = 2
HEAD_DIM = 128
N_GROUPS = N_HEADS // N_KV_HEADS
ATTN_BLOCK = 128
ROPE_THETA = 10000.0
ROPE_FREQS = HEAD_DIM // 4
D_RNN = 1024
N_RNN_BLOCKS = 8
RNN_BLOCK = D_RNN // N_RNN_BLOCKS
LRU_C = 8.0
D_FF = 2816
N_EXPERTS = 8
D_FF_EXPERT = 3584
EPS = 1e-6
NEG_INF = -1e30
Q_COLS = N_HEADS * HEAD_DIM
KV_COLS = N_KV_HEADS * HEAD_DIM
IN_COLS = Q_COLS + 2 * KV_COLS + 2 * D_RNN + 2 * D_MODEL
ATTN_SCALE = HEAD_DIM ** -0.5

LANES = 128
SUBLANES = 8
TM = 512
TN_IN = 512
TQ = 512
TT = 256
TG = 1024
TG_SUB = 512
MOE_F_CHUNKS = 7
ROW_DMA_UNROLL = 8
MOD_ROWS = 8
VMEM_LIMIT = 56 * 1024 * 1024


def _cparams(n_axes):
    return pltpu.CompilerParams(dimension_semantics=("arbitrary",) * n_axes,
                                vmem_limit_bytes=VMEM_LIMIT)


def _sigmoid(z):
    return 1.0 / (1.0 + jnp.exp(-z))


def _norm_mod(h, g, shift, scale):
    ms = jnp.mean(h * h, axis=-1, keepdims=True)
    y = h * lax.rsqrt(ms + EPS) * g
    return y * (1.0 + scale) + shift


def _mod_kernel(c_ref, w_ref, b_ref, o_ref):
    cv = c_ref[...]
    s = cv * _sigmoid(cv)
    o_ref[...] = jnp.dot(s, w_ref[...], preferred_element_type=F32,
                         precision=lax.Precision.HIGHEST) + b_ref[...]


def _modulation(cpad, w_mod, b_mod):
    nchunk = 6
    return pl.pallas_call(
        _mod_kernel,
        grid=(DEPTH, nchunk),
        in_specs=[
            pl.BlockSpec((MOD_ROWS, D_MODEL), lambda l, n: (0, 0)),
            pl.BlockSpec((None, D_MODEL, D_MODEL), lambda l, n: (l, 0, n)),
            pl.BlockSpec((None, 1, D_MODEL), lambda l, n: (l, 0, n)),
        ],
        out_specs=pl.BlockSpec((None, MOD_ROWS, D_MODEL), lambda l, n: (l, 0, n)),
        out_shape=jax.ShapeDtypeStruct((DEPTH, MOD_ROWS, 6 * D_MODEL), F32),
        compiler_params=_cparams(2),
        name="modulation",
    )(cpad, w_mod, b_mod.reshape(DEPTH, 1, 6 * D_MODEL))


def _rope(xh, cos, sin_signed, first_half):
    sw = jnp.where(first_half, pltpu.roll(xh, 96, 1), pltpu.roll(xh, 32, 1))
    return xh * cos + sw * sin_signed


def _in_proj_kernel(h_ref, g_ref, sh_ref, sc_ref, w_ref, cos_ref, sin_ref,
                    q_ref, kv_ref, x_ref, gy_ref, sga_ref, sgr_ref):
    n = _norm_mod(h_ref[...], g_ref[...], sh_ref[...], sc_ref[...]).astype(BF16)
    cos = cos_ref[...]
    sin = sin_ref[...]
    lane = lax.broadcasted_iota(jnp.int32, (TM, LANES), 1)
    first_half = (lane & 32) == 0

    def proj(chunk):
        return jnp.dot(n, w_ref[:, chunk * TN_IN:(chunk + 1) * TN_IN], preferred_element_type=F32)

    def rope_store(acc, n_heads, scale, ref, col0):
        for hh in range(n_heads):
            y = _rope(acc[:, hh * HEAD_DIM:(hh + 1) * HEAD_DIM], cos, sin, first_half)
            if scale != 1.0:
                y = y * scale
            ref[:, col0 + hh * HEAD_DIM:col0 + (hh + 1) * HEAD_DIM] = y.astype(BF16)

    for c in range(2):
        rope_store(proj(c), 4, ATTN_SCALE, q_ref, c * TN_IN)
    acc = proj(2)
    rope_store(acc, 2, 1.0, kv_ref, 0)
    kv_ref[:, KV_COLS:] = acc[:, KV_COLS:].astype(BF16)
    for c in range(2):
        cols = slice(c * TN_IN, (c + 1) * TN_IN)
        x_ref[:, cols] = proj(3 + c)
        gy_ref[:, cols] = jax.nn.gelu(proj(5 + c)).astype(BF16)
        sga_ref[:, cols] = _sigmoid(proj(7 + c)).astype(BF16)
        sgr_ref[:, cols] = _sigmoid(proj(9 + c)).astype(BF16)


def _in_proj(h, g, mod3, w_in, layer, cos_t, sin_t, *, n_lat_tiles, tiles_per_batch, n_batch):
    T = h.shape[0]

    def grp(i):
        return jnp.minimum(i // tiles_per_batch, n_batch)

    def pos_tile(i):
        return jnp.where(i < n_lat_tiles, i % tiles_per_batch, tiles_per_batch)

    def tok(width):
        return pl.BlockSpec((TM, width), lambda i: (i, 0))

    out_shape = (
        jax.ShapeDtypeStruct((T, Q_COLS), BF16),
        jax.ShapeDtypeStruct((T, 2 * KV_COLS), BF16),
        jax.ShapeDtypeStruct((T, D_RNN), F32),
        jax.ShapeDtypeStruct((T, D_RNN), BF16),
        jax.ShapeDtypeStruct((T, D_MODEL), BF16),
        jax.ShapeDtypeStruct((T, D_MODEL), BF16),
    )
    return pl.pallas_call(
        _in_proj_kernel,
        grid=(T // TM,),
        in_specs=[
            tok(D_MODEL),
            pl.BlockSpec((None, 1, D_MODEL), lambda i: (layer, 0, 0)),
            pl.BlockSpec((None, 1, D_MODEL), lambda i: (grp(i), 0, 0)),
            pl.BlockSpec((None, 1, D_MODEL), lambda i: (grp(i), 0, 1)),
            pl.BlockSpec((None, D_MODEL, IN_COLS), lambda i: (layer, 0, 0)),
            pl.BlockSpec((TM, HEAD_DIM), lambda i: (pos_tile(i), 0)),
            pl.BlockSpec((TM, HEAD_DIM), lambda i: (pos_tile(i), 0)),
        ],
        out_specs=(tok(Q_COLS), tok(2 * KV_COLS), tok(D_RNN), tok(D_RNN), tok(D_MODEL), tok(D_MODEL)),
        out_shape=out_shape,
        compiler_params=_cparams(1),
        name="in_proj",
    )(h, g, mod3, mod3, w_in, cos_t, sin_t)


def _stack_heads(qt):
    return jnp.concatenate([qt[:, g * HEAD_DIM:(g + 1) * HEAD_DIM] for g in range(N_GROUPS)], axis=0)


def _sink_col(sink_ref, kvh, rows):
    return jnp.concatenate(
        [jnp.full((rows, 1), sink_ref[kvh * N_GROUPS + g], F32) for g in range(N_GROUPS)], axis=0)


def _softmax_pv(s, sink, vall):
    m = jnp.maximum(jnp.max(s, axis=1, keepdims=True), sink)
    p = jnp.exp(s - m)
    denom = jnp.sum(p, axis=1, keepdims=True) + jnp.exp(sink - m)
    o = jnp.dot(p.astype(BF16), vall, preferred_element_type=F32)
    return o / denom


def _attn_kernel(sink_ref, q_ref, k_ref, v_ref, kc_ref, vc_ref, o_ref, *, n_blocks):
    kvh = pl.program_id(1)
    i = pl.program_id(2)
    rows = ATTN_BLOCK * N_GROUPS
    qi = lax.broadcasted_iota(jnp.int32, (rows, ATTN_BLOCK), 0) & (ATTN_BLOCK - 1)
    kj = lax.broadcasted_iota(jnp.int32, (rows, ATTN_BLOCK), 1)
    tri_prev = kj >= qi
    tri_next = kj <= qi
    sink = _sink_col(sink_ref, kvh, ATTN_BLOCK)
    kc = kc_ref[...]
    vc = vc_ref[...]

    def body(qb, carry):
        n = i * (TQ // ATTN_BLOCK) + qb
        r0 = pl.multiple_of(qb * ATTN_BLOCK, ATTN_BLOCK)
        p0 = pl.multiple_of(jnp.maximum(n - 1, 0) * ATTN_BLOCK, ATTN_BLOCK)
        c0 = pl.multiple_of(n * ATTN_BLOCK, ATTN_BLOCK)
        n0 = pl.multiple_of(jnp.minimum(n + 1, n_blocks - 1) * ATTN_BLOCK, ATTN_BLOCK)
        qs = _stack_heads(q_ref[pl.ds(r0, ATTN_BLOCK), :])
        kall = jnp.concatenate([k_ref[pl.ds(p0, ATTN_BLOCK), :], k_ref[pl.ds(c0, ATTN_BLOCK), :],
                                k_ref[pl.ds(n0, ATTN_BLOCK), :], kc], axis=0)
        vall = jnp.concatenate([v_ref[pl.ds(p0, ATTN_BLOCK), :], v_ref[pl.ds(c0, ATTN_BLOCK), :],
                                v_ref[pl.ds(n0, ATTN_BLOCK), :], vc], axis=0)
        s = lax.dot_general(qs, kall, (((1,), (1,)), ((), ())), preferred_element_type=F32)
        pen_prev = jnp.where(n > 0, 0.0, NEG_INF)
        pen_next = jnp.where(n < n_blocks - 1, 0.0, NEG_INF)
        sp = jnp.where(tri_prev, s[:, :ATTN_BLOCK] + pen_prev, NEG_INF)
        sn = jnp.where(tri_next, s[:, 2 * ATTN_BLOCK:3 * ATTN_BLOCK] + pen_next, NEG_INF)
        s = jnp.concatenate([sp, s[:, ATTN_BLOCK:2 * ATTN_BLOCK], sn, s[:, 3 * ATTN_BLOCK:]], axis=1)
        o = _softmax_pv(s, sink, vall)
        for g in range(N_GROUPS):
            o_ref[pl.ds(r0, ATTN_BLOCK), g * HEAD_DIM:(g + 1) * HEAD_DIM] = (
                o[g * ATTN_BLOCK:(g + 1) * ATTN_BLOCK].astype(BF16))
        return carry

    lax.fori_loop(0, TQ // ATTN_BLOCK, body, 0, unroll=True)


def _latent_attention(sink, q, kv, *, n_batch, seq, ctx_len):
    gw = N_GROUPS * HEAD_DIM
    tiles_per_batch = seq // TQ
    ctx_blk0 = n_batch * seq // ctx_len
    grid_spec = pltpu.PrefetchScalarGridSpec(
        num_scalar_prefetch=1,
        grid=(n_batch, N_KV_HEADS, tiles_per_batch),
        in_specs=[
            pl.BlockSpec((TQ, gw), lambda b, k, i, s: (b * tiles_per_batch + i, k)),
            pl.BlockSpec((seq, HEAD_DIM), lambda b, k, i, s: (b, k)),
            pl.BlockSpec((seq, HEAD_DIM), lambda b, k, i, s: (b, N_KV_HEADS + k)),
            pl.BlockSpec((ctx_len, HEAD_DIM), lambda b, k, i, s: (ctx_blk0 + b, k)),
            pl.BlockSpec((ctx_len, HEAD_DIM), lambda b, k, i, s: (ctx_blk0 + b, N_KV_HEADS + k)),
        ],
        out_specs=pl.BlockSpec((TQ, gw), lambda b, k, i, s: (b * tiles_per_batch + i, k)),
    )
    return pl.pallas_call(
        functools.partial(_attn_kernel, n_blocks=seq // ATTN_BLOCK),
        grid_spec=grid_spec,
        out_shape=jax.ShapeDtypeStruct((n_batch * seq, Q_COLS), BF16),
        compiler_params=_cparams(3),
        name="latent_attention",
    )(sink, q, kv, kv, kv, kv)


def _ctx_attn_kernel(sink_ref, q_ref, kc_ref, vc_ref, o_ref):
    kvh = pl.program_id(1)
    rows = q_ref.shape[0]
    qs = _stack_heads(q_ref[...])
    s = lax.dot_general(qs, kc_ref[...], (((1,), (1,)), ((), ())), preferred_element_type=F32)
    o = _softmax_pv(s, _sink_col(sink_ref, kvh, rows), vc_ref[...])
    for g in range(N_GROUPS):
        o_ref[:, g * HEAD_DIM:(g + 1) * HEAD_DIM] = o[g * rows:(g + 1) * rows].astype(BF16)


def _context_attention(sink, q, kv, *, n_batch, seq, ctx_len):
    gw = N_GROUPS * HEAD_DIM
    ctx_blk0 = n_batch * seq // ctx_len
    grid_spec = pltpu.PrefetchScalarGridSpec(
        num_scalar_prefetch=1,
        grid=(n_batch, N_KV_HEADS),
        in_specs=[
            pl.BlockSpec((ctx_len, gw), lambda b, k, s: (ctx_blk0 + b, k)),
            pl.BlockSpec((ctx_len, HEAD_DIM), lambda b, k, s: (ctx_blk0 + b, k)),
            pl.BlockSpec((ctx_len, HEAD_DIM), lambda b, k, s: (ctx_blk0 + b, N_KV_HEADS + k)),
        ],
        out_specs=pl.BlockSpec((ctx_len, gw), lambda b, k, s: (b, k)),
    )
    return pl.pallas_call(
        _ctx_attn_kernel,
        grid_spec=grid_spec,
        out_shape=jax.ShapeDtypeStruct((n_batch * ctx_len, Q_COLS), BF16),
        compiler_params=_cparams(2),
        name="context_attention",
    )(sink, q, kv, kv)


def _softplus(z):
    return jnp.maximum(z, 0.0) + jnp.log1p(jnp.exp(-jnp.abs(z)))


def _rglru_kernel(xf_ref, xfp_ref, xfn_ref, xb_ref, xbp_ref, xbn_ref, cw_ref, cb_ref, wcat_ref,
                  gb_ref, lam_ref, hf_ref, hb_ref, a_s, b_s, carry):
    j = pl.program_id(1)
    last_j = pl.num_programs(1) - 1

    @pl.when(j == 0)
    def _():
        carry[...] = jnp.zeros_like(carry)

    seg_start = (j <= 1, (j == 0) | (j == last_j))
    seg_end = ((j == 0) | (j == last_j), j <= 1)
    mains = (xf_ref, xb_ref)
    prevs = (xfp_ref, xbp_ref)
    nexts = (xfn_ref, xbn_ref)

    n_ext = TT + 2 * SUBLANES
    body_rows = slice(SUBLANES, SUBLANES + TT)
    for d in range(2):
        x0 = mains[d][...]
        ext = jnp.concatenate([jnp.where(seg_start[d], 0.0, prevs[d][...]), x0,
                               jnp.where(seg_end[d], 0.0, nexts[d][...])], axis=0)
        xc = cb_ref[...] + pltpu.roll(ext, 2, 0)[body_rows] * cw_ref[0:1, :]
        xc = xc + pltpu.roll(ext, 1, 0)[body_rows] * cw_ref[1:2, :]
        xc = xc + x0 * cw_ref[2:3, :]
        xc = xc + pltpu.roll(ext, n_ext - 1, 0)[body_rows] * cw_ref[3:4, :]
        c_d = -LRU_C * _softplus(-lam_ref[d:d + 1, :])
        for blk in range(N_RNN_BLOCKS):
            sl = slice(blk * RNN_BLOCK, (blk + 1) * RNN_BLOCK)
            xcb = xc[:, sl]
            z = jnp.dot(xcb.astype(BF16), wcat_ref[d, blk], preferred_element_type=F32)
            r = _sigmoid(z[:, :RNN_BLOCK] + gb_ref[d, 0:1, sl])
            ig = _sigmoid(z[:, RNN_BLOCK:] + gb_ref[d, 1:2, sl])
            log_a = r * c_d[:, sl]
            a = jnp.exp(log_a)
            a_s[d, :, sl] = a
            b_s[d, :, sl] = jnp.sqrt(1.0 - a * a) * (ig * xcb)

    row = lax.broadcasted_iota(jnp.int32, (SUBLANES, D_RNN), 0)
    n_sub = TT // SUBLANES

    def scan8(a, b, reverse):
        for s in (1, 2, 4):
            if reverse:
                keep = row < SUBLANES - s
                shift = SUBLANES - s
            else:
                keep = row >= s
                shift = s
            a_sh = jnp.where(keep, pltpu.roll(a, shift, 0), 1.0)
            b_sh = jnp.where(keep, pltpu.roll(b, shift, 0), 0.0)
            b = a * b_sh + b
            a = a * a_sh
        return a, b

    def body(k, hc):
        hcf, hcb = hc
        rf = pl.multiple_of(k * SUBLANES, SUBLANES)
        rb = pl.multiple_of((n_sub - 1 - k) * SUBLANES, SUBLANES)
        af, bf = scan8(a_s[0, pl.ds(rf, SUBLANES), :], b_s[0, pl.ds(rf, SUBLANES), :], False)
        ab, bb = scan8(a_s[1, pl.ds(rb, SUBLANES), :], b_s[1, pl.ds(rb, SUBLANES), :], True)
        hf = af * hcf + bf
        hb = ab * hcb + bb
        b_s[0, pl.ds(rf, SUBLANES), :] = hf
        b_s[1, pl.ds(rb, SUBLANES), :] = hb
        return (jnp.broadcast_to(hf[SUBLANES - 1:SUBLANES, :], (SUBLANES, D_RNN)),
                jnp.broadcast_to(hb[0:1, :], (SUBLANES, D_RNN)))

    hcf, hcb = lax.fori_loop(0, n_sub, body, (carry[0], carry[1]))
    carry[0] = hcf
    carry[1] = hcb
    hf_ref[...] = b_s[0].astype(BF16)
    hb_ref[...] = b_s[1].astype(BF16)


def _rglru(x, conv_w, conv_b, wcat, gate_b, lam, *, n_batch, seq, ctx_len):
    T = x.shape[0]
    assert ctx_len == TT and seq % TT == 0
    tps = seq // TT
    n_lat_t = n_batch * tps
    per8 = TT // SUBLANES
    last8 = T // SUBLANES - 1

    def ftile(b, j):
        return jnp.where(j == 0, n_lat_t + b, b * tps + j - 1)

    def btile(b, j):
        return jnp.where(j == 0, n_lat_t + b, b * tps + tps - j)

    def main(tile):
        return pl.BlockSpec((TT, D_RNN), lambda b, j: (tile(b, j), 0))

    def prev(tile):
        return pl.BlockSpec((SUBLANES, D_RNN), lambda b, j: (jnp.maximum(tile(b, j) * per8 - 1, 0), 0))

    def nxt(tile):
        return pl.BlockSpec((SUBLANES, D_RNN),
                            lambda b, j: (jnp.minimum((tile(b, j) + 1) * per8, last8), 0))

    def const(shape):
        return pl.BlockSpec(shape, lambda b, j: (0,) * len(shape))

    return pl.pallas_call(
        _rglru_kernel,
        grid=(n_batch, tps + 1),
        in_specs=[main(ftile), prev(ftile), nxt(ftile), main(btile), prev(btile), nxt(btile),
                  const(conv_w.shape), const(conv_b.shape), const(wcat.shape), const(gate_b.shape),
                  const(lam.shape)],
        out_specs=(main(ftile), main(btile)),
        out_shape=(jax.ShapeDtypeStruct((T, D_RNN), BF16), jax.ShapeDtypeStruct((T, D_RNN), BF16)),
        scratch_shapes=[
                        pltpu.VMEM((2, TT, D_RNN), F32),
                        pltpu.VMEM((2, TT, D_RNN), F32),
                        pltpu.VMEM((2, SUBLANES, D_RNN), F32)],
        compiler_params=_cparams(2),
        name="rglru",
    )(x, x, x, x, x, x, conv_w, conv_b, wcat, gate_b, lam)


def _merge_kernel(h_ref, attn_ref, attn_ctx_ref, hf_ref, hb_ref, gy_ref, sga_ref, sgr_ref, g1_ref,
                  woa_ref, wol_ref, wout_ref, o_ref):
    rec = ((hf_ref[...].astype(F32) + hb_ref[...].astype(F32)) * gy_ref[...].astype(F32)).astype(BF16)
    is_ctx_tile = pl.program_id(0) == pl.num_programs(0) - 1
    attn = jnp.where(is_ctx_tile, attn_ctx_ref[...], attn_ref[...])
    ta = jnp.dot(attn, woa_ref[...], preferred_element_type=F32)
    tl = jnp.dot(rec, wol_ref[...], preferred_element_type=F32)
    m = sga_ref[...].astype(F32) * ta + sgr_ref[...].astype(F32) * tl
    y = jnp.dot(m.astype(BF16), wout_ref[...], preferred_element_type=F32)
    o_ref[...] = h_ref[...] + g1_ref[...] * y


def _merge(h, attn, attn_ctx, hf, hb, gy, sga, sgr, mod3, woa, wol, wout, layer, *, tiles_per_batch, n_batch):
    T = h.shape[0]
    n_lat_tiles = attn.shape[0] // TM
    assert attn_ctx.shape[0] == TM and n_lat_tiles == T // TM - 1

    def grp(i):
        return jnp.minimum(i // tiles_per_batch, n_batch)

    tok = pl.BlockSpec((TM, D_MODEL), lambda i: (i, 0))
    wsp = pl.BlockSpec((None, D_MODEL, D_MODEL), lambda i: (layer, 0, 0))
    return pl.pallas_call(
        _merge_kernel,
        grid=(T // TM,),
        in_specs=[tok,
                  pl.BlockSpec((TM, Q_COLS), lambda i: (jnp.minimum(i, n_lat_tiles - 1), 0)),
                  pl.BlockSpec((TM, Q_COLS), lambda i: (0, 0)),
                  tok, tok, tok, tok, tok,
                  pl.BlockSpec((None, 1, D_MODEL), lambda i: (grp(i), 0, 2)),
                  wsp, wsp, wsp],
        out_specs=tok,
        out_shape=jax.ShapeDtypeStruct((T, D_MODEL), F32),
        compiler_params=_cparams(1),
        name="merge",
    )(h, attn, attn_ctx, hf, hb, gy, sga, sgr, mod3, woa, wol, wout)


def _swiglu_partial(n, wg, wu, wd):
    gt = jnp.dot(n, wg, preferred_element_type=F32)
    ut = jnp.dot(n, wu, preferred_element_type=F32)
    act = (gt * _sigmoid(gt) * ut).astype(BF16)
    return jnp.dot(act, wd, preferred_element_type=F32)


def _ffn_kernel(h_ref, g_ref, sh_ref, sc_ref, g2_ref, wg_ref, wu_ref, wd_ref, o_ref, n_scr, acc):
    f = pl.program_id(1)

    @pl.when(f == 0)
    def _():
        n_scr[...] = _norm_mod(h_ref[...], g_ref[...], sh_ref[...], sc_ref[...]).astype(BF16)
        acc[...] = jnp.zeros_like(acc)

    acc[...] += _swiglu_partial(n_scr[...], wg_ref[...], wu_ref[...], wd_ref[...])

    @pl.when(f == pl.num_programs(1) - 1)
    def _():
        o_ref[...] = h_ref[...] + g2_ref[...] * acc[...]


def _ffn(h, g, mod3, wg, wu, wd, layer, ff_layer, *, tiles_per_batch, n_batch):
    T = h.shape[0]
    n_f = 2
    tf = D_FF // n_f
    assert tf % LANES == 0

    def grp(i):
        return jnp.minimum(i // tiles_per_batch, n_batch)

    def modspec(k):
        return pl.BlockSpec((None, 1, D_MODEL), lambda i, f: (grp(i), 0, k))

    tok = pl.BlockSpec((TM, D_MODEL), lambda i, f: (i, 0))
    return pl.pallas_call(
        _ffn_kernel,
        grid=(T // TM, n_f),
        in_specs=[tok, pl.BlockSpec((None, 1, D_MODEL), lambda i, f: (layer, 0, 0)),
                  modspec(3), modspec(4), modspec(5),
                  pl.BlockSpec((None, D_MODEL, tf), lambda i, f: (ff_layer, 0, f)),
                  pl.BlockSpec((None, D_MODEL, tf), lambda i, f: (ff_layer, 0, f)),
                  pl.BlockSpec((None, tf, D_MODEL), lambda i, f: (ff_layer, f, 0))],
        out_specs=tok,
        out_shape=jax.ShapeDtypeStruct((T, D_MODEL), F32),
        scratch_shapes=[pltpu.VMEM((TM, D_MODEL), BF16), pltpu.VMEM((TM, D_MODEL), F32)],
        compiler_params=_cparams(2),
        name="dense_ffn",
    )(h, g, mod3, mod3, mod3, wg, wu, wd)


ROUTE_E1, ROUTE_E2, ROUTE_W1, ROUTE_W2, ROUTE_R1, ROUTE_R2 = range(6)


def _router_kernel(h_ref, g_ref, sh_ref, sc_ref, wr_ref, route_ref, cnt_ref, run):
    @pl.when(pl.program_id(0) == 0)
    def _():
        run[...] = jnp.zeros_like(run)

    n = _norm_mod(h_ref[...], g_ref[...], sh_ref[...], sc_ref[...])
    logits = jnp.dot(n, wr_ref[...], preferred_element_type=F32, precision=lax.Precision.HIGHEST)
    lane = lax.broadcasted_iota(jnp.int32, logits.shape, 1)
    logits = jnp.where(lane < N_EXPERTS, logits, -jnp.inf)
    m1 = jnp.max(logits, axis=1, keepdims=True)
    i1 = jnp.min(jnp.where(logits == m1, lane, LANES), axis=1, keepdims=True)
    rest = jnp.where(lane == i1, -jnp.inf, logits)
    m2 = jnp.max(rest, axis=1, keepdims=True)
    i2 = jnp.min(jnp.where(rest == m2, lane, LANES), axis=1, keepdims=True)
    e2 = jnp.exp(m2 - m1)
    w1 = 1.0 / (1.0 + e2)
    w2 = e2 / (1.0 + e2)

    hit1 = lane == i1
    hit2 = lane == i2
    onehot = jnp.where(hit1 | hit2, 1.0, 0.0)
    r_i = lax.broadcasted_iota(jnp.int32, (TM, TM), 0)
    c_i = lax.broadcasted_iota(jnp.int32, (TM, TM), 1)
    lower = jnp.where(c_i < r_i, 1.0, 0.0).astype(BF16)
    prefix = jnp.dot(lower, onehot.astype(BF16), preferred_element_type=F32) + run[0:1, :]
    rank1 = jnp.sum(jnp.where(hit1, prefix, 0.0), axis=1, keepdims=True)
    rank2 = jnp.sum(jnp.where(hit2, prefix, 0.0), axis=1, keepdims=True)
    run[...] = run[...] + jnp.sum(onehot, axis=0, keepdims=True)
    cnt_ref[...] = run[...]

    rec = jnp.zeros(logits.shape, F32)
    for k, v in ((ROUTE_E1, i1.astype(F32)), (ROUTE_E2, i2.astype(F32)), (ROUTE_W1, w1), (ROUTE_W2, w2),
                 (ROUTE_R1, rank1), (ROUTE_R2, rank2)):
        rec = jnp.where(lane == k, v, rec)
    route_ref[...] = rec


def _router(h, g2, mod3, w_r, layer, *, tiles_per_batch, n_batch):
    T = h.shape[0]

    def grp(i):
        return jnp.minimum(i // tiles_per_batch, n_batch)

    def modspec(k):
        return pl.BlockSpec((None, 1, D_MODEL), lambda i: (grp(i), 0, k))

    return pl.pallas_call(
        _router_kernel,
        grid=(T // TM,),
        in_specs=[pl.BlockSpec((TM, D_MODEL), lambda i: (i, 0)),
                  pl.BlockSpec((None, 1, D_MODEL), lambda i: (layer, 0, 0)),
                  modspec(3), modspec(4),
                  pl.BlockSpec((D_MODEL, LANES), lambda i: (0, 0))],
        out_specs=(pl.BlockSpec((TM, LANES), lambda i: (i, 0)),
                   pl.BlockSpec((SUBLANES, LANES), lambda i: (0, 0))),
        out_shape=(jax.ShapeDtypeStruct((T, LANES), F32), jax.ShapeDtypeStruct((SUBLANES, LANES), F32)),
        scratch_shapes=[pltpu.VMEM((SUBLANES, LANES), F32)],
        compiler_params=_cparams(1),
        name="moe_router",
    )(h, g2, mod3, mod3, w_r)


def _row_copy(src_ref, src_row, dst_ref, dst_row, sem):
    return pltpu.make_async_copy(src_ref.at[pl.ds(src_row, 1)], dst_ref.at[pl.ds(dst_row, 1)], sem)


def _dispatch_kernel(dest_ref, h_ref, g_ref, sh_ref, sc_ref, xs_in_ref, xs_ref, n_scr, sem):
    del xs_in_ref
    base = pl.program_id(0) * (2 * TM)
    n_scr[...] = _norm_mod(h_ref[...], g_ref[...], sh_ref[...], sc_ref[...])

    def start(r, carry):
        for s in range(2):
            _row_copy(n_scr, r, xs_ref, dest_ref[base + 2 * r + s], sem).start()
        return carry

    lax.fori_loop(0, TM, start, 0, unroll=ROW_DMA_UNROLL)
    for s in range(2):
        pltpu.make_async_copy(n_scr, xs_ref.at[pl.ds(0, TM)], sem).wait()


def _dispatch(dest, h, g2, mod3, xs_zero, layer, *, tiles_per_batch, n_batch):
    T = h.shape[0]

    def grp(i):
        return jnp.minimum(i // tiles_per_batch, n_batch)

    def modspec(k):
        return pl.BlockSpec((None, 1, D_MODEL), lambda i, d: (grp(i), 0, k))

    grid_spec = pltpu.PrefetchScalarGridSpec(
        num_scalar_prefetch=1,
        grid=(T // TM,),
        in_specs=[pl.BlockSpec((TM, D_MODEL), lambda i, d: (i, 0)),
                  pl.BlockSpec((None, 1, D_MODEL), lambda i, d: (layer, 0, 0)),
                  modspec(3), modspec(4),
                  pl.BlockSpec(memory_space=pl.ANY)],
        out_specs=pl.BlockSpec(memory_space=pl.ANY),
        scratch_shapes=[pltpu.VMEM((TM, D_MODEL), F32), pltpu.SemaphoreType.DMA(())],
    )
    return pl.pallas_call(
        _dispatch_kernel,
        grid_spec=grid_spec,
        out_shape=jax.ShapeDtypeStruct(xs_zero.shape, xs_zero.dtype),
        input_output_aliases={5: 0},
        compiler_params=_cparams(1),
        name="moe_dispatch",
    )(dest, h, g2, mod3, mod3, xs_zero)


def _expert_kernel(te_ref, nu_ref, rows_ref, x_ref, wg_ref, wu_ref, wd_ref, o_ref, xb, acc):
    del te_ref, nu_ref
    k = pl.program_id(0)
    f = pl.program_id(1)

    @pl.when(f == 0)
    def _():
        xb[...] = x_ref[...].astype(BF16)
        acc[...] = jnp.zeros_like(acc)

    wg = wg_ref[...].astype(BF16)
    wu = wu_ref[...].astype(BF16)
    wd = wd_ref[...].astype(BF16)
    for sub in range(TG // TG_SUB):
        rows = slice(sub * TG_SUB, (sub + 1) * TG_SUB)

        @pl.when(rows_ref[k] > sub * TG_SUB)
        def _():
            acc[rows, :] += _swiglu_partial(xb[rows, :], wg, wu, wd)

    @pl.when(f == pl.num_programs(1) - 1)
    def _():
        o_ref[...] = acc[...]


def _experts(tile_expert, n_used, tile_rows, xs, wg, wu, wd, layer):
    P = xs.shape[0]
    n_f = MOE_F_CHUNKS
    tf = D_FF_EXPERT // n_f
    assert tf % LANES == 0

    def fsel(k, f, nu):
        return jnp.where(k < nu[0], f, n_f - 1)

    grid_spec = pltpu.PrefetchScalarGridSpec(
        num_scalar_prefetch=3,
        grid=(P // TG, n_f),
        in_specs=[pl.BlockSpec((TG, D_MODEL), lambda k, f, te, nu, tr: (k, 0)),
                  pl.BlockSpec((None, None, D_MODEL, tf), lambda k, f, te, nu, tr: (layer, te[k], 0, fsel(k, f, nu))),
                  pl.BlockSpec((None, None, D_MODEL, tf), lambda k, f, te, nu, tr: (layer, te[k], 0, fsel(k, f, nu))),
                  pl.BlockSpec((None, None, tf, D_MODEL), lambda k, f, te, nu, tr: (layer, te[k], fsel(k, f, nu), 0))],
        out_specs=pl.BlockSpec((TG, D_MODEL), lambda k, f, te, nu, tr: (k, 0)),
        scratch_shapes=[pltpu.VMEM((TG, D_MODEL), BF16), pltpu.VMEM((TG, D_MODEL), F32)],
    )
    return pl.pallas_call(
        _expert_kernel,
        grid_spec=grid_spec,
        out_shape=jax.ShapeDtypeStruct((P, D_MODEL), F32),
        compiler_params=_cparams(2),
        name="moe_experts",
    )(tile_expert, n_used, tile_rows, xs, wg, wu, wd)


def _combine_kernel(dest_ref, h_ref, g2_ref, route_ref, y_ref, o_ref, ybuf, sem):
    base = pl.program_id(0) * (2 * TM)

    def start(r, carry):
        for s in range(2):
            _row_copy(y_ref, dest_ref[base + 2 * r + s], ybuf, s * TM + r, sem).start()
        return carry

    lax.fori_loop(0, TM, start, 0, unroll=ROW_DMA_UNROLL)
    for s in range(2):
        pltpu.make_async_copy(y_ref.at[pl.ds(0, TM)], ybuf.at[pl.ds(s * TM, TM)], sem).wait()
    route = route_ref[...]
    f = route[:, ROUTE_W1:ROUTE_W1 + 1] * ybuf[0:TM, :] + route[:, ROUTE_W2:ROUTE_W2 + 1] * ybuf[TM:, :]
    o_ref[...] = h_ref[...] + g2_ref[...] * f


def _combine(dest, h, mod3, route, y, *, tiles_per_batch, n_batch):
    T = h.shape[0]

    def grp(i):
        return jnp.minimum(i // tiles_per_batch, n_batch)

    grid_spec = pltpu.PrefetchScalarGridSpec(
        num_scalar_prefetch=1,
        grid=(T // TM,),
        in_specs=[pl.BlockSpec((TM, D_MODEL), lambda i, d: (i, 0)),
                  pl.BlockSpec((None, 1, D_MODEL), lambda i, d: (grp(i), 0, 5)),
                  pl.BlockSpec((TM, LANES), lambda i, d: (i, 0)),
                  pl.BlockSpec(memory_space=pl.ANY)],
        out_specs=pl.BlockSpec((TM, D_MODEL), lambda i, d: (i, 0)),
        scratch_shapes=[pltpu.VMEM((2 * TM, D_MODEL), F32), pltpu.SemaphoreType.DMA(())],
    )
    return pl.pallas_call(
        _combine_kernel,
        grid_spec=grid_spec,
        out_shape=jax.ShapeDtypeStruct((T, D_MODEL), F32),
        compiler_params=_cparams(1),
        name="moe_combine",
    )(dest, h, mod3, route, y)


def _moe_layer(h, g2, mod3, w_r, wg, wu, wd, layer, moe_layer, **geo):
    T = h.shape[0]
    n_tiles = (2 * T) // TG + N_EXPERTS
    route, cnt = _router(h, g2, mod3, w_r, layer, **geo)

    counts = cnt[0, :N_EXPERTS].astype(jnp.int32)
    padded = ((counts + TG - 1) // TG) * TG
    ends = jnp.cumsum(padded)
    offs = ends - padded
    e12 = route[:, ROUTE_E1:ROUTE_E2 + 1].astype(jnp.int32)
    r12 = route[:, ROUTE_R1:ROUTE_R2 + 1].astype(jnp.int32)
    onehot = e12[:, :, None] == jnp.arange(N_EXPERTS)[None, None, :]
    dest = (jnp.sum(jnp.where(onehot, offs[None, None, :], 0), axis=-1) + r12).reshape(2 * T)
    n_used = (ends[-1] // TG).reshape(1)
    tiles = jnp.arange(n_tiles)
    te_raw = jnp.sum(tiles[:, None] >= (ends // TG)[None, :], axis=1)
    tile_expert = jnp.minimum(te_raw, N_EXPERTS - 1).astype(jnp.int32)
    sel = tile_expert[:, None] == jnp.arange(N_EXPERTS)[None, :]
    cnt_k = jnp.sum(jnp.where(sel, counts[None, :], 0), axis=1)
    off_k = jnp.sum(jnp.where(sel, offs[None, :], 0), axis=1)
    tile_rows = jnp.where(te_raw < N_EXPERTS, jnp.clip(cnt_k - (tiles * TG - off_k), 0, TG), 0)

    xs = _dispatch(dest, h, g2, mod3, jnp.zeros((n_tiles * TG, D_MODEL), F32), layer, **geo)
    y = _experts(tile_expert, n_used, tile_rows.astype(jnp.int32), xs, wg, wu, wd, moe_layer)
    return _combine(dest, h, mod3, route, y, **geo)


def _final_norm_kernel(h_ref, g_ref, o_ref):
    h = h_ref[...]
    ms = jnp.mean(h * h, axis=-1, keepdims=True)
    o_ref[...] = h * lax.rsqrt(ms + EPS) * g_ref[...]


def _final_norm(h, g, n_rows):
    tok = pl.BlockSpec((TM, D_MODEL), lambda i: (i, 0))
    return pl.pallas_call(
        _final_norm_kernel,
        grid=(n_rows // TM,),
        in_specs=[tok, pl.BlockSpec((1, D_MODEL), lambda i: (0, 0))],
        out_specs=tok,
        out_shape=jax.ShapeDtypeStruct((n_rows, D_MODEL), F32),
        compiler_params=_cparams(1),
        name="final_norm",
    )(h, g)


def _rope_tables(seq):
    t = jnp.arange(seq)
    inv = ROPE_THETA ** (-jnp.arange(ROPE_FREQS, dtype=F32) / ROPE_FREQS)
    ang_r = (t // GRID_W).astype(F32)[:, None] * inv
    ang_c = (t % GRID_W).astype(F32)[:, None] * inv
    cos = jnp.concatenate([jnp.cos(ang_r)] * 2 + [jnp.cos(ang_c)] * 2, axis=1)
    sin = jnp.concatenate([-jnp.sin(ang_r), jnp.sin(ang_r), -jnp.sin(ang_c), jnp.sin(ang_c)], axis=1)
    cos = jnp.concatenate([cos, jnp.ones((TM, HEAD_DIM), F32)], axis=0)
    sin = jnp.concatenate([sin, jnp.zeros((TM, HEAD_DIM), F32)], axis=0)
    return cos, sin


def kernel(x, c, ctx, c_ctx, w_mod, b_mod, norm1_g, norm2_g, w_in, attn_sink, conv_w, conv_b, gate_a_w, gate_a_b, gate_x_w, gate_x_b, lru_lambda, w_o_attn, w_o_lru, w_out, ff_w_gate, ff_w_up, ff_w_down, router_w, exp_w_gate, exp_w_up, exp_w_down, final_g):
    n_batch, seq, _ = x.shape
    ctx_len = ctx.shape[1]
    assert n_batch * ctx_len == TM and seq % TM == 0 and n_batch + 1 <= MOD_ROWS
    n_lat = n_batch * seq
    tiles_per_batch = seq // TM
    geo = dict(tiles_per_batch=tiles_per_batch, n_batch=n_batch)
    shp = dict(n_batch=n_batch, seq=seq, ctx_len=ctx_len)

    cpad = jnp.zeros((MOD_ROWS, D_MODEL), F32).at[:n_batch].set(c).at[n_batch].set(c_ctx)
    mod = _modulation(cpad, w_mod, b_mod)
    cos_t, sin_t = _rope_tables(seq)
    h = jnp.concatenate([x.reshape(n_lat, D_MODEL), ctx.reshape(n_batch * ctx_len, D_MODEL)], axis=0)

    g1 = norm1_g.reshape(DEPTH, 1, D_MODEL)
    g2 = norm2_g.reshape(DEPTH, 1, D_MODEL)
    w_in_b = w_in.astype(BF16)
    woa_b, wol_b, wout_b = w_o_attn.astype(BF16), w_o_lru.astype(BF16), w_out.astype(BF16)
    ffg_b, ffu_b, ffd_b = ff_w_gate.astype(BF16), ff_w_up.astype(BF16), ff_w_down.astype(BF16)

    for l in range(DEPTH):
        mod3 = mod[l].reshape(MOD_ROWS, 1, 6 * D_MODEL)
        q, kv, xr, gy, sga, sgr = _in_proj(h, g1, mod3, w_in_b, l, cos_t, sin_t,
                                           n_lat_tiles=n_lat // TM, **geo)
        attn = _latent_attention(attn_sink[l], q, kv, **shp)
        attn_ctx = _context_attention(attn_sink[l], q, kv, **shp)
        wcat = jnp.concatenate([gate_a_w[l], gate_x_w[l]], axis=-1).astype(BF16)
        gate_b = jnp.stack([gate_a_b[l], gate_x_b[l]], axis=1)
        hf, hb = _rglru(xr, conv_w[l], conv_b[l].reshape(1, D_RNN), wcat, gate_b, lru_lambda[l], **shp)
        h = _merge(h, attn, attn_ctx, hf, hb, gy, sga, sgr, mod3, woa_b, wol_b, wout_b, l, **geo)
        i = l // 2
        if l % 2 == 0:
            h = _ffn(h, g2, mod3, ffg_b, ffu_b, ffd_b, l, i, **geo)
        else:
            w_r = jnp.zeros((D_MODEL, LANES), F32).at[:, :N_EXPERTS].set(router_w[i])
            h = _moe_layer(h, g2, mod3, w_r, exp_w_gate, exp_w_up, exp_w_down, l, i, **geo)

    out = _final_norm(h, final_g.reshape(1, D_MODEL), n_lat)
    return out.reshape(n_batch, seq, D_MODEL)
```

```python
import functools

import jax
import jax.numpy as jnp
from jax import lax
from jax.experimental import pallas as pl
from jax.experimental.pallas import tpu as pltpu

F32 = jnp.float32
BF16 = jnp.bfloat16

D_MODEL = 1024
DEPTH = 4
GRID_W = 64
N_HEADS = 8
N_KV_HEADS = 2
HEAD_DIM = 128
N_GROUPS = N_HEADS // N_KV_HEADS
ATTN_BLOCK = 128
ROPE_THETA = 10000.0
ROPE_FREQS = HEAD_DIM // 4
D_RNN = 1024
N_RNN_BLOCKS = 8
RNN_BLOCK = D_RNN // N_RNN_BLOCKS
LRU_C = 8.0
D_FF = 2816
N_EXPERTS = 8
D_FF_EXPERT = 3584
EPS = 1e-6
NEG_INF = -1e30
Q_COLS = N_HEADS * HEAD_DIM
KV_COLS = N_KV_HEADS * HEAD_DIM
IN_COLS = Q_COLS + 2 * KV_COLS + 2 * D_RNN + 2 * D_MODEL
LOG2E = 1.4426950408889634
ATTN_SCALE = HEAD_DIM ** -0.5 * LOG2E

LANES = 128
SUBLANES = 8
TM = 512
TN_IN = 512
TQ = 512
TT = 256
TG = 1024
TG_SUB = 512
MOE_F_CHUNKS = 7
ROW_DMA_UNROLL = 8
MOD_ROWS = 8
VMEM_LIMIT = 56 * 1024 * 1024


def _cparams(n_axes):
    return pltpu.CompilerParams(dimension_semantics=("arbitrary",) * n_axes,
                                vmem_limit_bytes=VMEM_LIMIT)


def _sigmoid(z):
    return 1.0 / (1.0 + jnp.exp(-z))


def _norm_mod(h, g, shift, scale):
    ms = jnp.mean(h * h, axis=-1, keepdims=True)
    y = h * lax.rsqrt(ms + EPS) * g
    return y * (1.0 + scale) + shift


def _mod_kernel(c_ref, w_ref, b_ref, o_ref):
    cv = c_ref[...]
    s = cv * _sigmoid(cv)
    o_ref[...] = jnp.dot(s, w_ref[...], preferred_element_type=F32,
                         precision=lax.Precision.HIGHEST) + b_ref[...]


def _modulation(cpad, w_mod, b_mod):
    nchunk = 6
    return pl.pallas_call(
        _mod_kernel,
        grid=(DEPTH, nchunk),
        in_specs=[
            pl.BlockSpec((MOD_ROWS, D_MODEL), lambda l, n: (0, 0)),
            pl.BlockSpec((None, D_MODEL, D_MODEL), lambda l, n: (l, 0, n)),
            pl.BlockSpec((None, 1, D_MODEL), lambda l, n: (l, 0, n)),
        ],
        out_specs=pl.BlockSpec((None, MOD_ROWS, D_MODEL), lambda l, n: (l, 0, n)),
        out_shape=jax.ShapeDtypeStruct((DEPTH, MOD_ROWS, 6 * D_MODEL), F32),
        compiler_params=_cparams(2),
        name="modulation",
    )(cpad, w_mod, b_mod.reshape(DEPTH, 1, 6 * D_MODEL))


def _rope(xh, cos, sin_signed, first_half):
    sw = jnp.where(first_half, pltpu.roll(xh, 96, 1), pltpu.roll(xh, 32, 1))
    return xh * cos + sw * sin_signed


def _in_proj_kernel(h_ref, g_ref, sh_ref, sc_ref, w_ref, cos_ref, sin_ref,
                    q_ref, kv_ref, x_ref, gy_ref, sga_ref, sgr_ref):
    n = _norm_mod(h_ref[...], g_ref[...], sh_ref[...], sc_ref[...]).astype(BF16)
    cos = cos_ref[...]
    sin = sin_ref[...]
    lane = lax.broadcasted_iota(jnp.int32, (TM, LANES), 1)
    first_half = (lane & 32) == 0

    def proj(chunk):
        return jnp.dot(n, w_ref[:, chunk * TN_IN:(chunk + 1) * TN_IN], preferred_element_type=F32)

    def rope_store(acc, n_heads, scale, ref, col0):
        for hh in range(n_heads):
            y = _rope(acc[:, hh * HEAD_DIM:(hh + 1) * HEAD_DIM], cos, sin, first_half)
            if scale != 1.0:
                y = y * scale
            ref[:, col0 + hh * HEAD_DIM:col0 + (hh + 1) * HEAD_DIM] = y.astype(BF16)

    for c in range(2):
        rope_store(proj(c), 4, ATTN_SCALE, q_ref, c * TN_IN)
    acc = proj(2)
    rope_store(acc, 2, 1.0, kv_ref, 0)
    kv_ref[:, KV_COLS:] = acc[:, KV_COLS:].astype(BF16)
    for c in range(2):
        cols = slice(c * TN_IN, (c + 1) * TN_IN)
        x_ref[:, cols] = proj(3 + c)
        gy_ref[:, cols] = jax.nn.gelu(proj(5 + c)).astype(BF16)
        sga_ref[:, cols] = _sigmoid(proj(7 + c)).astype(BF16)
        sgr_ref[:, cols] = _sigmoid(proj(9 + c)).astype(BF16)


def _in_proj(h, g, mod3, w_in, layer, cos_t, sin_t, *, n_lat_tiles, tiles_per_batch, n_batch):
    T = h.shape[0]

    def grp(i):
        return jnp.minimum(i // tiles_per_batch, n_batch)

    def pos_tile(i):
        return jnp.where(i < n_lat_tiles, i % tiles_per_batch, tiles_per_batch)

    def tok(width):
        return pl.BlockSpec((TM, width), lambda i: (i, 0))

    out_shape = (
        jax.ShapeDtypeStruct((T, Q_COLS), BF16),
        jax.ShapeDtypeStruct((T, 2 * KV_COLS), BF16),
        jax.ShapeDtypeStruct((T, D_RNN), F32),
        jax.ShapeDtypeStruct((T, D_RNN), BF16),
        jax.ShapeDtypeStruct((T, D_MODEL), BF16),
        jax.ShapeDtypeStruct((T, D_MODEL), BF16),
    )
    return pl.pallas_call(
        _in_proj_kernel,
        grid=(T // TM,),
        in_specs=[
            tok(D_MODEL),
            pl.BlockSpec((None, 1, D_MODEL), lambda i: (layer, 0, 0)),
            pl.BlockSpec((None, 1, D_MODEL), lambda i: (grp(i), 0, 0)),
            pl.BlockSpec((None, 1, D_MODEL), lambda i: (grp(i), 0, 1)),
            pl.BlockSpec((None, D_MODEL, IN_COLS), lambda i: (layer, 0, 0)),
            pl.BlockSpec((TM, HEAD_DIM), lambda i: (pos_tile(i), 0)),
            pl.BlockSpec((TM, HEAD_DIM), lambda i: (pos_tile(i), 0)),
        ],
        out_specs=(tok(Q_COLS), tok(2 * KV_COLS), tok(D_RNN), tok(D_RNN), tok(D_MODEL), tok(D_MODEL)),
        out_shape=out_shape,
        compiler_params=_cparams(1),
        name="in_proj",
    )(h, g, mod3, mod3, w_in, cos_t, sin_t)


def _stack_heads(qt):
    return jnp.concatenate([qt[:, g * HEAD_DIM:(g + 1) * HEAD_DIM] for g in range(N_GROUPS)], axis=0)


def _sink_col(sink_ref, kvh, rows):
    return jnp.concatenate(
        [jnp.full((rows, 1), sink_ref[kvh * N_GROUPS + g] * LOG2E, F32) for g in range(N_GROUPS)], axis=0)


def _ones_column(n_keys):
    lane = lax.broadcasted_iota(jnp.int32, (n_keys, HEAD_DIM), 1)
    return jnp.where(lane == 0, 1.0, 0.0).astype(BF16)


def _softmax_pv(s, sink, vall):
    m = jnp.maximum(jnp.max(s, axis=1, keepdims=True), sink)
    p = jnp.exp2(s - m).astype(BF16)
    v_aug = jnp.concatenate([vall, _ones_column(vall.shape[0])], axis=1)
    oa = jnp.dot(p, v_aug, preferred_element_type=F32)
    denom = oa[:, HEAD_DIM:HEAD_DIM + 1] + jnp.exp2(sink - m)
    return oa[:, :HEAD_DIM] / denom


def _attn_kernel(sink_ref, q_ref, k_ref, v_ref, kc_ref, vc_ref, o_ref, *, n_blocks):
    kvh = pl.program_id(1)
    i = pl.program_id(2)
    rows = ATTN_BLOCK * N_GROUPS
    qi = lax.broadcasted_iota(jnp.int32, (rows, ATTN_BLOCK), 0) & (ATTN_BLOCK - 1)
    kj = lax.broadcasted_iota(jnp.int32, (rows, ATTN_BLOCK), 1)
    tri_prev = kj >= qi
    tri_next = kj <= qi
    sink = _sink_col(sink_ref, kvh, ATTN_BLOCK)
    kc = kc_ref[...]
    vc = vc_ref[...]

    def body(qb, carry):
        n = i * (TQ // ATTN_BLOCK) + qb
        r0 = pl.multiple_of(qb * ATTN_BLOCK, ATTN_BLOCK)
        p0 = pl.multiple_of(jnp.maximum(n - 1, 0) * ATTN_BLOCK, ATTN_BLOCK)
        c0 = pl.multiple_of(n * ATTN_BLOCK, ATTN_BLOCK)
        n0 = pl.multiple_of(jnp.minimum(n + 1, n_blocks - 1) * ATTN_BLOCK, ATTN_BLOCK)
        qs = _stack_heads(q_ref[pl.ds(r0, ATTN_BLOCK), :])
        kall = jnp.concatenate([k_ref[pl.ds(p0, ATTN_BLOCK), :], k_ref[pl.ds(c0, ATTN_BLOCK), :],
                                k_ref[pl.ds(n0, ATTN_BLOCK), :], kc], axis=0)
        vall = jnp.concatenate([v_ref[pl.ds(p0, ATTN_BLOCK), :], v_ref[pl.ds(c0, ATTN_BLOCK), :],
                                v_ref[pl.ds(n0, ATTN_BLOCK), :], vc], axis=0)
        s = lax.dot_general(qs, kall, (((1,), (1,)), ((), ())), preferred_element_type=F32)
        pen_prev = jnp.where(n > 0, 0.0, NEG_INF)
        pen_next = jnp.where(n < n_blocks - 1, 0.0, NEG_INF)
        sp = jnp.where(tri_prev, s[:, :ATTN_BLOCK] + pen_prev, NEG_INF)
        sn = jnp.where(tri_next, s[:, 2 * ATTN_BLOCK:3 * ATTN_BLOCK] + pen_next, NEG_INF)
        s = jnp.concatenate([sp, s[:, ATTN_BLOCK:2 * ATTN_BLOCK], sn, s[:, 3 * ATTN_BLOCK:]], axis=1)
        o = _softmax_pv(s, sink, vall)
        for g in range(N_GROUPS):
            o_ref[pl.ds(r0, ATTN_BLOCK), g * HEAD_DIM:(g + 1) * HEAD_DIM] = (
                o[g * ATTN_BLOCK:(g + 1) * ATTN_BLOCK].astype(BF16))
        return carry

    lax.fori_loop(0, TQ // ATTN_BLOCK, body, 0, unroll=True)


def _latent_attention(sink, q, kv, *, n_batch, seq, ctx_len):
    gw = N_GROUPS * HEAD_DIM
    tiles_per_batch = seq // TQ
    ctx_blk0 = n_batch * seq // ctx_len
    grid_spec = pltpu.PrefetchScalarGridSpec(
        num_scalar_prefetch=1,
        grid=(n_batch, N_KV_HEADS, tiles_per_batch),
        in_specs=[
            pl.BlockSpec((TQ, gw), lambda b, k, i, s: (b * tiles_per_batch + i, k)),
            pl.BlockSpec((seq, HEAD_DIM), lambda b, k, i, s: (b, k)),
            pl.BlockSpec((seq, HEAD_DIM), lambda b, k, i, s: (b, N_KV_HEADS + k)),
            pl.BlockSpec((ctx_len, HEAD_DIM), lambda b, k, i, s: (ctx_blk0 + b, k)),
            pl.BlockSpec((ctx_len, HEAD_DIM), lambda b, k, i, s: (ctx_blk0 + b, N_KV_HEADS + k)),
        ],
        out_specs=pl.BlockSpec((TQ, gw), lambda b, k, i, s: (b * tiles_per_batch + i, k)),
    )
    return pl.pallas_call(
        functools.partial(_attn_kernel, n_blocks=seq // ATTN_BLOCK),
        grid_spec=grid_spec,
        out_shape=jax.ShapeDtypeStruct((n_batch * seq, Q_COLS), BF16),
        compiler_params=_cparams(3),
        name="latent_attention",
    )(sink, q, kv, kv, kv, kv)


def _ctx_attn_kernel(sink_ref, q_ref, kc_ref, vc_ref, o_ref):
    kvh = pl.program_id(1)
    rows = q_ref.shape[0]
    qs = _stack_heads(q_ref[...])
    s = lax.dot_general(qs, kc_ref[...], (((1,), (1,)), ((), ())), preferred_element_type=F32)
    o = _softmax_pv(s, _sink_col(sink_ref, kvh, rows), vc_ref[...])
    for g in range(N_GROUPS):
        o_ref[:, g * HEAD_DIM:(g + 1) * HEAD_DIM] = o[g * rows:(g + 1) * rows].astype(BF16)


def _context_attention(sink, q, kv, *, n_batch, seq, ctx_len):
    gw = N_GROUPS * HEAD_DIM
    ctx_blk0 = n_batch * seq // ctx_len
    grid_spec = pltpu.PrefetchScalarGridSpec(
        num_scalar_prefetch=1,
        grid=(n_batch, N_KV_HEADS),
        in_specs=[
            pl.BlockSpec((ctx_len, gw), lambda b, k, s: (ctx_blk0 + b, k)),
            pl.BlockSpec((ctx_len, HEAD_DIM), lambda b, k, s: (ctx_blk0 + b, k)),
            pl.BlockSpec((ctx_len, HEAD_DIM), lambda b, k, s: (ctx_blk0 + b, N_KV_HEADS + k)),
        ],
        out_specs=pl.BlockSpec((ctx_len, gw), lambda b, k, s: (b, k)),
    )
    return pl.pallas_call(
        _ctx_attn_kernel,
        grid_spec=grid_spec,
        out_shape=jax.ShapeDtypeStruct((n_batch * ctx_len, Q_COLS), BF16),
        compiler_params=_cparams(2),
        name="context_attention",
    )(sink, q, kv, kv)


def _softplus(z):
    return jnp.maximum(z, 0.0) + jnp.log1p(jnp.exp(-jnp.abs(z)))


def _rglru_kernel(xf_ref, xfp_ref, xfn_ref, xb_ref, xbp_ref, xbn_ref, cw_ref, cb_ref, wcat_ref,
                  gb_ref, lam_ref, hf_ref, hb_ref, a_s, b_s, carry):
    j = pl.program_id(1)
    last_j = pl.num_programs(1) - 1

    @pl.when(j == 0)
    def _():
        carry[...] = jnp.zeros_like(carry)

    seg_start = (j <= 1, (j == 0) | (j == last_j))
    seg_end = ((j == 0) | (j == last_j), j <= 1)
    mains = (xf_ref, xb_ref)
    prevs = (xfp_ref, xbp_ref)
    nexts = (xfn_ref, xbn_ref)

    n_ext = TT + 2 * SUBLANES
    body_rows = slice(SUBLANES, SUBLANES + TT)
    for d in range(2):
        x0 = mains[d][...]
        ext = jnp.concatenate([jnp.where(seg_start[d], 0.0, prevs[d][...]), x0,
                               jnp.where(seg_end[d], 0.0, nexts[d][...])], axis=0)
        xc = cb_ref[...] + pltpu.roll(ext, 2, 0)[body_rows] * cw_ref[0:1, :]
        xc = xc + pltpu.roll(ext, 1, 0)[body_rows] * cw_ref[1:2, :]
        xc = xc + x0 * cw_ref[2:3, :]
        xc = xc + pltpu.roll(ext, n_ext - 1, 0)[body_rows] * cw_ref[3:4, :]
        c_d = -LRU_C * _softplus(-lam_ref[d:d + 1, :])
        for blk in range(N_RNN_BLOCKS):
            sl = slice(blk * RNN_BLOCK, (blk + 1) * RNN_BLOCK)
            xcb = xc[:, sl]
            z = jnp.dot(xcb.astype(BF16), wcat_ref[d, blk], preferred_element_type=F32)
            r = _sigmoid(z[:, :RNN_BLOCK] + gb_ref[d, 0:1, sl])
            ig = _sigmoid(z[:, RNN_BLOCK:] + gb_ref[d, 1:2, sl])
            log_a = r * c_d[:, sl]
            a = jnp.exp(log_a)
            a_s[d, :, sl] = a
            b_s[d, :, sl] = jnp.sqrt(1.0 - a * a) * (ig * xcb)

    row = lax.broadcasted_iota(jnp.int32, (SUBLANES, D_RNN), 0)
    n_sub = TT // SUBLANES

    def scan8(a, b, reverse):
        for s in (1, 2, 4):
            if reverse:
                keep = row < SUBLANES - s
                shift = SUBLANES - s
            else:
                keep = row >= s
                shift = s
            a_sh = jnp.where(keep, pltpu.roll(a, shift, 0), 1.0)
            b_sh = jnp.where(keep, pltpu.roll(b, shift, 0), 0.0)
            b = a * b_sh + b
            a = a * a_sh
        return a, b

    def body(k, hc):
        hcf, hcb = hc
        rf = pl.multiple_of(k * SUBLANES, SUBLANES)
        rb = pl.multiple_of((n_sub - 1 - k) * SUBLANES, SUBLANES)
        af, bf = scan8(a_s[0, pl.ds(rf, SUBLANES), :], b_s[0, pl.ds(rf, SUBLANES), :], False)
        ab, bb = scan8(a_s[1, pl.ds(rb, SUBLANES), :], b_s[1, pl.ds(rb, SUBLANES), :], True)
        hf = af * hcf + bf
        hb = ab * hcb + bb
        b_s[0, pl.ds(rf, SUBLANES), :] = hf
        b_s[1, pl.ds(rb, SUBLANES), :] = hb
        return (jnp.broadcast_to(hf[SUBLANES - 1:SUBLANES, :], (SUBLANES, D_RNN)),
                jnp.broadcast_to(hb[0:1, :], (SUBLANES, D_RNN)))

    hcf, hcb = lax.fori_loop(0, n_sub, body, (carry[0], carry[1]))
    carry[0] = hcf
    carry[1] = hcb
    hf_ref[...] = b_s[0].astype(BF16)
    hb_ref[...] = b_s[1].astype(BF16)


def _rglru(x, conv_w, conv_b, wcat, gate_b, lam, *, n_batch, seq, ctx_len):
    T = x.shape[0]
    assert ctx_len == TT and seq % TT == 0
    tps = seq // TT
    n_lat_t = n_batch * tps
    per8 = TT // SUBLANES
    last8 = T // SUBLANES - 1

    def ftile(b, j):
        return jnp.where(j == 0, n_lat_t + b, b * tps + j - 1)

    def btile(b, j):
        return jnp.where(j == 0, n_lat_t + b, b * tps + tps - j)

    def main(tile):
        return pl.BlockSpec((TT, D_RNN), lambda b, j: (tile(b, j), 0))

    def prev(tile):
        return pl.BlockSpec((SUBLANES, D_RNN), lambda b, j: (jnp.maximum(tile(b, j) * per8 - 1, 0), 0))

    def nxt(tile):
        return pl.BlockSpec((SUBLANES, D_RNN),
                            lambda b, j: (jnp.minimum((tile(b, j) + 1) * per8, last8), 0))

    def const(shape):
        return pl.BlockSpec(shape, lambda b, j: (0,) * len(shape))

    return pl.pallas_call(
        _rglru_kernel,
        grid=(n_batch, tps + 1),
        in_specs=[main(ftile), prev(ftile), nxt(ftile), main(btile), prev(btile), nxt(btile),
                  const(conv_w.shape), const(conv_b.shape), const(wcat.shape), const(gate_b.shape),
                  const(lam.shape)],
        out_specs=(main(ftile), main(btile)),
        out_shape=(jax.ShapeDtypeStruct((T, D_RNN), BF16), jax.ShapeDtypeStruct((T, D_RNN), BF16)),
        scratch_shapes=[
                        pltpu.VMEM((2, TT, D_RNN), F32),
                        pltpu.VMEM((2, TT, D_RNN), F32),
                        pltpu.VMEM((2, SUBLANES, D_RNN), F32)],
        compiler_params=_cparams(2),
        name="rglru",
    )(x, x, x, x, x, x, conv_w, conv_b, wcat, gate_b, lam)


def _merge_kernel(h_ref, attn_ref, attn_ctx_ref, hf_ref, hb_ref, gy_ref, sga_ref, sgr_ref, g1_ref,
                  woa_ref, wol_ref, wout_ref, o_ref):
    rec = ((hf_ref[...].astype(F32) + hb_ref[...].astype(F32)) * gy_ref[...].astype(F32)).astype(BF16)
    is_ctx_tile = pl.program_id(0) == pl.num_programs(0) - 1
    attn = jnp.where(is_ctx_tile, attn_ctx_ref[...], attn_ref[...])
    ta = jnp.dot(attn, woa_ref[...], preferred_element_type=F32)
    tl = jnp.dot(rec, wol_ref[...], preferred_element_type=F32)
    m = sga_ref[...].astype(F32) * ta + sgr_ref[...].astype(F32) * tl
    y = jnp.dot(m.astype(BF16), wout_ref[...], preferred_element_type=F32)
    o_ref[...] = h_ref[...] + g1_ref[...] * y


def _merge(h, attn, attn_ctx, hf, hb, gy, sga, sgr, mod3, woa, wol, wout, layer, *, tiles_per_batch, n_batch):
    T = h.shape[0]
    n_lat_tiles = attn.shape[0] // TM
    assert attn_ctx.shape[0] == TM and n_lat_tiles == T // TM - 1

    def grp(i):
        return jnp.minimum(i // tiles_per_batch, n_batch)

    tok = pl.BlockSpec((TM, D_MODEL), lambda i: (i, 0))
    wsp = pl.BlockSpec((None, D_MODEL, D_MODEL), lambda i: (layer, 0, 0))
    return pl.pallas_call(
        _merge_kernel,
        grid=(T // TM,),
        in_specs=[tok,
                  pl.BlockSpec((TM, Q_COLS), lambda i: (jnp.minimum(i, n_lat_tiles - 1), 0)),
                  pl.BlockSpec((TM, Q_COLS), lambda i: (0, 0)),
                  tok, tok, tok, tok, tok,
                  pl.BlockSpec((None, 1, D_MODEL), lambda i: (grp(i), 0, 2)),
                  wsp, wsp, wsp],
        out_specs=tok,
        out_shape=jax.ShapeDtypeStruct((T, D_MODEL), F32),
        compiler_params=_cparams(1),
        name="merge",
    )(h, attn, attn_ctx, hf, hb, gy, sga, sgr, mod3, woa, wol, wout)


def _swiglu_partial(n, wg, wu, wd):
    gt = jnp.dot(n, wg, preferred_element_type=F32)
    ut = jnp.dot(n, wu, preferred_element_type=F32)
    act = (gt * _sigmoid(gt) * ut).astype(BF16)
    return jnp.dot(act, wd, preferred_element_type=F32)


def _ffn_kernel(h_ref, g_ref, sh_ref, sc_ref, g2_ref, wg_ref, wu_ref, wd_ref, o_ref, n_scr, acc):
    f = pl.program_id(1)

    @pl.when(f == 0)
    def _():
        n_scr[...] = _norm_mod(h_ref[...], g_ref[...], sh_ref[...], sc_ref[...]).astype(BF16)
        acc[...] = jnp.zeros_like(acc)

    acc[...] += _swiglu_partial(n_scr[...], wg_ref[...], wu_ref[...], wd_ref[...])

    @pl.when(f == pl.num_programs(1) - 1)
    def _():
        o_ref[...] = h_ref[...] + g2_ref[...] * acc[...]


def _ffn(h, g, mod3, wg, wu, wd, layer, ff_layer, *, tiles_per_batch, n_batch):
    T = h.shape[0]
    n_f = 2
    tf = D_FF // n_f
    assert tf % LANES == 0

    def grp(i):
        return jnp.minimum(i // tiles_per_batch, n_batch)

    def modspec(k):
        return pl.BlockSpec((None, 1, D_MODEL), lambda i, f: (grp(i), 0, k))

    tok = pl.BlockSpec((TM, D_MODEL), lambda i, f: (i, 0))
    return pl.pallas_call(
        _ffn_kernel,
        grid=(T // TM, n_f),
        in_specs=[tok, pl.BlockSpec((None, 1, D_MODEL), lambda i, f: (layer, 0, 0)),
                  modspec(3), modspec(4), modspec(5),
                  pl.BlockSpec((None, D_MODEL, tf), lambda i, f: (ff_layer, 0, f)),
                  pl.BlockSpec((None, D_MODEL, tf), lambda i, f: (ff_layer, 0, f)),
                  pl.BlockSpec((None, tf, D_MODEL), lambda i, f: (ff_layer, f, 0))],
        out_specs=tok,
        out_shape=jax.ShapeDtypeStruct((T, D_MODEL), F32),
        scratch_shapes=[pltpu.VMEM((TM, D_MODEL), BF16), pltpu.VMEM((TM, D_MODEL), F32)],
        compiler_params=_cparams(2),
        name="dense_ffn",
    )(h, g, mod3, mod3, mod3, wg, wu, wd)


ROUTE_E1, ROUTE_E2, ROUTE_W1, ROUTE_W2, ROUTE_R1, ROUTE_R2 = range(6)


def _dot_split(a, b):
    a_hi = a.astype(BF16)
    a_lo = (a - a_hi.astype(F32)).astype(BF16)
    b_hi = b.astype(BF16)
    b_lo = (b - b_hi.astype(F32)).astype(BF16)

    def mm(x, y):
        return jnp.dot(x, y, preferred_element_type=F32)

    return mm(a_hi, b_hi) + (mm(a_hi, b_lo) + mm(a_lo, b_hi))


def _router_kernel(h_ref, g_ref, sh_ref, sc_ref, wr_ref, route_ref, cnt_ref, run):
    @pl.when(pl.program_id(0) == 0)
    def _():
        run[...] = jnp.zeros_like(run)

    n = _norm_mod(h_ref[...], g_ref[...], sh_ref[...], sc_ref[...])
    logits = _dot_split(n, wr_ref[...])
    lane = lax.broadcasted_iota(jnp.int32, logits.shape, 1)
    logits = jnp.where(lane < N_EXPERTS, logits, -jnp.inf)
    m1 = jnp.max(logits, axis=1, keepdims=True)
    i1 = jnp.min(jnp.where(logits == m1, lane, LANES), axis=1, keepdims=True)
    rest = jnp.where(lane == i1, -jnp.inf, logits)
    m2 = jnp.max(rest, axis=1, keepdims=True)
    i2 = jnp.min(jnp.where(rest == m2, lane, LANES), axis=1, keepdims=True)
    e2 = jnp.exp(m2 - m1)
    w1 = 1.0 / (1.0 + e2)
    w2 = e2 / (1.0 + e2)

    hit1 = lane == i1
    hit2 = lane == i2
    onehot = jnp.where(hit1 | hit2, 1.0, 0.0)
    r_i = lax.broadcasted_iota(jnp.int32, (TM, TM), 0)
    c_i = lax.broadcasted_iota(jnp.int32, (TM, TM), 1)
    lower = jnp.where(c_i < r_i, 1.0, 0.0).astype(BF16)
    prefix = jnp.dot(lower, onehot.astype(BF16), preferred_element_type=F32) + run[0:1, :]
    rank1 = jnp.sum(jnp.where(hit1, prefix, 0.0), axis=1, keepdims=True)
    rank2 = jnp.sum(jnp.where(hit2, prefix, 0.0), axis=1, keepdims=True)
    run[...] = run[...] + jnp.sum(onehot, axis=0, keepdims=True)
    cnt_ref[...] = run[...]

    rec = jnp.zeros(logits.shape, F32)
    for k, v in ((ROUTE_E1, i1.astype(F32)), (ROUTE_E2, i2.astype(F32)), (ROUTE_W1, w1), (ROUTE_W2, w2),
                 (ROUTE_R1, rank1), (ROUTE_R2, rank2)):
        rec = jnp.where(lane == k, v, rec)
    route_ref[...] = rec


def _router(h, g2, mod3, w_r, layer, *, tiles_per_batch, n_batch):
    T = h.shape[0]

    def grp(i):
        return jnp.minimum(i // tiles_per_batch, n_batch)

    def modspec(k):
        return pl.BlockSpec((None, 1, D_MODEL), lambda i: (grp(i), 0, k))

    return pl.pallas_call(
        _router_kernel,
        grid=(T // TM,),
        in_specs=[pl.BlockSpec((TM, D_MODEL), lambda i: (i, 0)),
                  pl.BlockSpec((None, 1, D_MODEL), lambda i: (layer, 0, 0)),
                  modspec(3), modspec(4),
                  pl.BlockSpec((D_MODEL, LANES), lambda i: (0, 0))],
        out_specs=(pl.BlockSpec((TM, LANES), lambda i: (i, 0)),
                   pl.BlockSpec((SUBLANES, LANES), lambda i: (0, 0))),
        out_shape=(jax.ShapeDtypeStruct((T, LANES), F32), jax.ShapeDtypeStruct((SUBLANES, LANES), F32)),
        scratch_shapes=[pltpu.VMEM((SUBLANES, LANES), F32)],
        compiler_params=_cparams(1),
        name="moe_router",
    )(h, g2, mod3, mod3, w_r)


def _row_copy(src_ref, src_row, dst_ref, dst_row, sem):
    return pltpu.make_async_copy(src_ref.at[pl.ds(src_row, 1)], dst_ref.at[pl.ds(dst_row, 1)], sem)


def _dispatch_kernel(dest_ref, h_ref, g_ref, sh_ref, sc_ref, xs_in_ref, xs_ref, n_scr, sem):
    del xs_in_ref
    base = pl.program_id(0) * (2 * TM)
    n_scr[...] = _norm_mod(h_ref[...], g_ref[...], sh_ref[...], sc_ref[...])

    def start(r, carry):
        for s in range(2):
            _row_copy(n_scr, r, xs_ref, dest_ref[base + 2 * r + s], sem).start()
        return carry

    lax.fori_loop(0, TM, start, 0, unroll=ROW_DMA_UNROLL)
    for s in range(2):
        pltpu.make_async_copy(n_scr, xs_ref.at[pl.ds(0, TM)], sem).wait()


def _dispatch(dest, h, g2, mod3, xs_zero, layer, *, tiles_per_batch, n_batch):
    T = h.shape[0]

    def grp(i):
        return jnp.minimum(i // tiles_per_batch, n_batch)

    def modspec(k):
        return pl.BlockSpec((None, 1, D_MODEL), lambda i, d: (grp(i), 0, k))

    grid_spec = pltpu.PrefetchScalarGridSpec(
        num_scalar_prefetch=1,
        grid=(T // TM,),
        in_specs=[pl.BlockSpec((TM, D_MODEL), lambda i, d: (i, 0)),
                  pl.BlockSpec((None, 1, D_MODEL), lambda i, d: (layer, 0, 0)),
                  modspec(3), modspec(4),
                  pl.BlockSpec(memory_space=pl.ANY)],
        out_specs=pl.BlockSpec(memory_space=pl.ANY),
        scratch_shapes=[pltpu.VMEM((TM, D_MODEL), F32), pltpu.SemaphoreType.DMA(())],
    )
    return pl.pallas_call(
        _dispatch_kernel,
        grid_spec=grid_spec,
        out_shape=jax.ShapeDtypeStruct(xs_zero.shape, xs_zero.dtype),
        input_output_aliases={5: 0},
        compiler_params=_cparams(1),
        name="moe_dispatch",
    )(dest, h, g2, mod3, mod3, xs_zero)


def _expert_kernel(te_ref, nu_ref, rows_ref, x_ref, wg_ref, wu_ref, wd_ref, o_ref, xb, acc):
    del te_ref, nu_ref
    k = pl.program_id(0)
    f = pl.program_id(1)

    @pl.when(f == 0)
    def _():
        xb[...] = x_ref[...].astype(BF16)
        acc[...] = jnp.zeros_like(acc)

    wg = wg_ref[...].astype(BF16)
    wu = wu_ref[...].astype(BF16)
    wd = wd_ref[...].astype(BF16)
    for sub in range(TG // TG_SUB):
        rows = slice(sub * TG_SUB, (sub + 1) * TG_SUB)

        @pl.when(rows_ref[k] > sub * TG_SUB)
        def _():
            acc[rows, :] += _swiglu_partial(xb[rows, :], wg, wu, wd)

    @pl.when(f == pl.num_programs(1) - 1)
    def _():
        o_ref[...] = acc[...]


def _experts(tile_expert, n_used, tile_rows, xs, wg, wu, wd, layer):
    P = xs.shape[0]
    n_f = MOE_F_CHUNKS
    tf = D_FF_EXPERT // n_f
    assert tf % LANES == 0

    def fsel(k, f, nu):
        return jnp.where(k < nu[0], f, n_f - 1)

    grid_spec = pltpu.PrefetchScalarGridSpec(
        num_scalar_prefetch=3,
        grid=(P // TG, n_f),
        in_specs=[pl.BlockSpec((TG, D_MODEL), lambda k, f, te, nu, tr: (k, 0)),
                  pl.BlockSpec((None, None, D_MODEL, tf), lambda k, f, te, nu, tr: (layer, te[k], 0, fsel(k, f, nu))),
                  pl.BlockSpec((None, None, D_MODEL, tf), lambda k, f, te, nu, tr: (layer, te[k], 0, fsel(k, f, nu))),
                  pl.BlockSpec((None, None, tf, D_MODEL), lambda k, f, te, nu, tr: (layer, te[k], fsel(k, f, nu), 0))],
        out_specs=pl.BlockSpec((TG, D_MODEL), lambda k, f, te, nu, tr: (k, 0)),
        scratch_shapes=[pltpu.VMEM((TG, D_MODEL), BF16), pltpu.VMEM((TG, D_MODEL), F32)],
    )
    return pl.pallas_call(
        _expert_kernel,
        grid_spec=grid_spec,
        out_shape=jax.ShapeDtypeStruct((P, D_MODEL), F32),
        compiler_params=_cparams(2),
        name="moe_experts",
    )(tile_expert, n_used, tile_rows, xs, wg, wu, wd)


def _combine_kernel(dest_ref, h_ref, g2_ref, route_ref, fg_ref, y_ref, o_ref, ybuf, sem, *, final):
    base = pl.program_id(0) * (2 * TM)

    def start(r, carry):
        for s in range(2):
            _row_copy(y_ref, dest_ref[base + 2 * r + s], ybuf, s * TM + r, sem).start()
        return carry

    lax.fori_loop(0, TM, start, 0, unroll=ROW_DMA_UNROLL)
    for s in range(2):
        pltpu.make_async_copy(y_ref.at[pl.ds(0, TM)], ybuf.at[pl.ds(s * TM, TM)], sem).wait()
    route = route_ref[...]
    f = route[:, ROUTE_W1:ROUTE_W1 + 1] * ybuf[0:TM, :] + route[:, ROUTE_W2:ROUTE_W2 + 1] * ybuf[TM:, :]
    out = h_ref[...] + g2_ref[...] * f
    if final:
        ms = jnp.mean(out * out, axis=-1, keepdims=True)
        out = out * lax.rsqrt(ms + EPS) * fg_ref[...]
    o_ref[...] = out


def _combine(dest, h, mod3, route, final_g, y, *, final, tiles_per_batch, n_batch):
    n_tok_tiles = h.shape[0] // TM - (1 if final else 0)

    def grp(i):
        return jnp.minimum(i // tiles_per_batch, n_batch)

    grid_spec = pltpu.PrefetchScalarGridSpec(
        num_scalar_prefetch=1,
        grid=(n_tok_tiles,),
        in_specs=[pl.BlockSpec((TM, D_MODEL), lambda i, d: (i, 0)),
                  pl.BlockSpec((None, 1, D_MODEL), lambda i, d: (grp(i), 0, 5)),
                  pl.BlockSpec((TM, LANES), lambda i, d: (i, 0)),
                  pl.BlockSpec((1, D_MODEL), lambda i, d: (0, 0)),
                  pl.BlockSpec(memory_space=pl.ANY)],
        out_specs=pl.BlockSpec((TM, D_MODEL), lambda i, d: (i, 0)),
        scratch_shapes=[pltpu.VMEM((2 * TM, D_MODEL), F32), pltpu.SemaphoreType.DMA(())],
    )
    return pl.pallas_call(
        functools.partial(_combine_kernel, final=final),
        grid_spec=grid_spec,
        out_shape=jax.ShapeDtypeStruct((n_tok_tiles * TM, D_MODEL), F32),
        compiler_params=_cparams(1),
        name="moe_combine",
    )(dest, h, mod3, route, final_g, y)


def _moe_layer(h, g2, mod3, w_r, wg, wu, wd, final_g, layer, moe_layer, *, final, **geo):
    T = h.shape[0]
    n_tiles = (2 * T) // TG + N_EXPERTS
    route, cnt = _router(h, g2, mod3, w_r, layer, **geo)

    counts = cnt[0, :N_EXPERTS].astype(jnp.int32)
    padded = ((counts + TG - 1) // TG) * TG
    ends = jnp.cumsum(padded)
    offs = ends - padded
    e12 = route[:, ROUTE_E1:ROUTE_E2 + 1].astype(jnp.int32)
    r12 = route[:, ROUTE_R1:ROUTE_R2 + 1].astype(jnp.int32)
    onehot = e12[:, :, None] == jnp.arange(N_EXPERTS)[None, None, :]
    dest = (jnp.sum(jnp.where(onehot, offs[None, None, :], 0), axis=-1) + r12).reshape(2 * T)
    n_used = (ends[-1] // TG).reshape(1)
    tiles = jnp.arange(n_tiles)
    te_raw = jnp.sum(tiles[:, None] >= (ends // TG)[None, :], axis=1)
    tile_expert = jnp.minimum(te_raw, N_EXPERTS - 1).astype(jnp.int32)
    sel = tile_expert[:, None] == jnp.arange(N_EXPERTS)[None, :]
    cnt_k = jnp.sum(jnp.where(sel, counts[None, :], 0), axis=1)
    off_k = jnp.sum(jnp.where(sel, offs[None, :], 0), axis=1)
    tile_rows = jnp.where(te_raw < N_EXPERTS, jnp.clip(cnt_k - (tiles * TG - off_k), 0, TG), 0)

    xs = _dispatch(dest, h, g2, mod3, jnp.zeros((n_tiles * TG, D_MODEL), F32), layer, **geo)
    y = _experts(tile_expert, n_used, tile_rows.astype(jnp.int32), xs, wg, wu, wd, moe_layer)
    return _combine(dest, h, mod3, route, final_g, y, final=final, **geo)


def _rope_tables(seq):
    t = jnp.arange(seq)
    inv = ROPE_THETA ** (-jnp.arange(ROPE_FREQS, dtype=F32) / ROPE_FREQS)
    ang_r = (t // GRID_W).astype(F32)[:, None] * inv
    ang_c = (t % GRID_W).astype(F32)[:, None] * inv
    cos = jnp.concatenate([jnp.cos(ang_r)] * 2 + [jnp.cos(ang_c)] * 2, axis=1)
    sin = jnp.concatenate([-jnp.sin(ang_r), jnp.sin(ang_r), -jnp.sin(ang_c), jnp.sin(ang_c)], axis=1)
    cos = jnp.concatenate([cos, jnp.ones((TM, HEAD_DIM), F32)], axis=0)
    sin = jnp.concatenate([sin, jnp.zeros((TM, HEAD_DIM), F32)], axis=0)
    return cos, sin


def kernel(x, c, ctx, c_ctx, w_mod, b_mod, norm1_g, norm2_g, w_in, attn_sink, conv_w, conv_b, gate_a_w, gate_a_b, gate_x_w, gate_x_b, lru_lambda, w_o_attn, w_o_lru, w_out, ff_w_gate, ff_w_up, ff_w_down, router_w, exp_w_gate, exp_w_up, exp_w_down, final_g):
    n_batch, seq, _ = x.shape
    ctx_len = ctx.shape[1]
    assert n_batch * ctx_len == TM and seq % TM == 0 and n_batch + 1 <= MOD_ROWS
    n_lat = n_batch * seq
    tiles_per_batch = seq // TM
    geo = dict(tiles_per_batch=tiles_per_batch, n_batch=n_batch)
    shp = dict(n_batch=n_batch, seq=seq, ctx_len=ctx_len)

    cpad = jnp.zeros((MOD_ROWS, D_MODEL), F32).at[:n_batch].set(c).at[n_batch].set(c_ctx)
    mod = _modulation(cpad, w_mod, b_mod)
    cos_t, sin_t = _rope_tables(seq)
    h = jnp.concatenate([x.reshape(n_lat, D_MODEL), ctx.reshape(n_batch * ctx_len, D_MODEL)], axis=0)

    g1 = norm1_g.reshape(DEPTH, 1, D_MODEL)
    g2 = norm2_g.reshape(DEPTH, 1, D_MODEL)
    w_in_b = w_in.astype(BF16)
    woa_b, wol_b, wout_b = w_o_attn.astype(BF16), w_o_lru.astype(BF16), w_out.astype(BF16)
    ffg_b, ffu_b, ffd_b = ff_w_gate.astype(BF16), ff_w_up.astype(BF16), ff_w_down.astype(BF16)

    for l in range(DEPTH):
        mod3 = mod[l].reshape(MOD_ROWS, 1, 6 * D_MODEL)
        q, kv, xr, gy, sga, sgr = _in_proj(h, g1, mod3, w_in_b, l, cos_t, sin_t,
                                           n_lat_tiles=n_lat // TM, **geo)
        attn = _latent_attention(attn_sink[l], q, kv, **shp)
        attn_ctx = _context_attention(attn_sink[l], q, kv, **shp)
        wcat = jnp.concatenate([gate_a_w[l], gate_x_w[l]], axis=-1).astype(BF16)
        gate_b = jnp.stack([gate_a_b[l], gate_x_b[l]], axis=1)
        hf, hb = _rglru(xr, conv_w[l], conv_b[l].reshape(1, D_RNN), wcat, gate_b, lru_lambda[l], **shp)
        h = _merge(h, attn, attn_ctx, hf, hb, gy, sga, sgr, mod3, woa_b, wol_b, wout_b, l, **geo)
        i = l // 2
        if l % 2 == 0:
            h = _ffn(h, g2, mod3, ffg_b, ffu_b, ffd_b, l, i, **geo)
        else:
            w_r = jnp.zeros((D_MODEL, LANES), F32).at[:, :N_EXPERTS].set(router_w[i])
            h = _moe_layer(h, g2, mod3, w_r, exp_w_gate, exp_w_up, exp_w_down,
                           final_g.reshape(1, D_MODEL), l, i, final=(l == DEPTH - 1), **geo)

    assert DEPTH % 2 == 0 and h.shape[0] == n_lat
    return h.reshape(n_batch, seq, D_MODEL)
```

```python
import functools

import jax
import jax.numpy as jnp
from jax import lax
from jax.experimental import pallas as pl
from jax.experimental.pallas import tpu as pltpu

F32 = jnp.float32
BF16 = jnp.bfloat16

D_MODEL = 1024
DEPTH = 4
GRID_W = 64
N_HEADS = 8
N_KV_HEADS = 2
HEAD_DIM = 128
N_GROUPS = N_HEADS // N_KV_HEADS
ATTN_BLOCK = 128
ROPE_THETA = 10000.0
ROPE_FREQS = HEAD_DIM // 4
D_RNN = 1024
N_RNN_BLOCKS = 8
RNN_BLOCK = D_RNN // N_RNN_BLOCKS
LRU_C = 8.0
D_FF = 2816
N_EXPERTS = 8
D_FF_EXPERT = 3584
EPS = 1e-6
NEG_INF = -1e30
Q_COLS = N_HEADS * HEAD_DIM
KV_COLS = N_KV_HEADS * HEAD_DIM
IN_COLS = Q_COLS + 2 * KV_COLS + 2 * D_RNN + 2 * D_MODEL
LOG2E = 1.4426950408889634
ATTN_SCALE = HEAD_DIM ** -0.5 * LOG2E

LANES = 128
SUBLANES = 8
TM = 512
TN_IN = 512
TQ = 512
TT = 256
SUBSEQ = TT // SUBLANES
TG = 1024
TG_SUB = 512
MOE_F_CHUNKS = 7
ROW_DMA_UNROLL = 8
MOD_ROWS = 8
VMEM_LIMIT = 56 * 1024 * 1024


def _cparams(n_axes):
    return pltpu.CompilerParams(dimension_semantics=("arbitrary",) * n_axes,
                                vmem_limit_bytes=VMEM_LIMIT)


def _sigmoid(z):
    return 0.5 * jnp.tanh(0.5 * z) + 0.5


def _norm_mod(h, g, shift, scale):
    ms = jnp.mean(h * h, axis=-1, keepdims=True)
    y = h * lax.rsqrt(ms + EPS) * g
    return y * (1.0 + scale) + shift


def _mod_kernel(c_ref, w_ref, b_ref, o_ref):
    cv = c_ref[...]
    s = cv * _sigmoid(cv)
    o_ref[...] = jnp.dot(s, w_ref[...], preferred_element_type=F32,
                         precision=lax.Precision.HIGHEST) + b_ref[...]


def _modulation(cpad, w_mod, b_mod):
    nchunk = 6
    return pl.pallas_call(
        _mod_kernel,
        grid=(DEPTH, nchunk),
        in_specs=[
            pl.BlockSpec((MOD_ROWS, D_MODEL), lambda l, n: (0, 0)),
            pl.BlockSpec((None, D_MODEL, D_MODEL), lambda l, n: (l, 0, n)),
            pl.BlockSpec((None, 1, D_MODEL), lambda l, n: (l, 0, n)),
        ],
        out_specs=pl.BlockSpec((None, MOD_ROWS, D_MODEL), lambda l, n: (l, 0, n)),
        out_shape=jax.ShapeDtypeStruct((DEPTH, MOD_ROWS, 6 * D_MODEL), F32),
        compiler_params=_cparams(2),
        name="modulation",
    )(cpad, w_mod, b_mod.reshape(DEPTH, 1, 6 * D_MODEL))


def _rope(xh, cos, sin_signed, first_half):
    sw = jnp.where(first_half, pltpu.roll(xh, 96, 1), pltpu.roll(xh, 32, 1))
    return xh * cos + sw * sin_signed


def _in_proj_kernel(h_ref, g_ref, sh_ref, sc_ref, w_ref, cos_ref, sin_ref, perm_ref,
                    q_ref, kv_ref, x_ref, gy_ref, sga_ref, sgr_ref):
    n = _norm_mod(h_ref[...], g_ref[...], sh_ref[...], sc_ref[...]).astype(BF16)
    cos = cos_ref[...]
    sin = sin_ref[...]
    lane = lax.broadcasted_iota(jnp.int32, (TM, LANES), 1)
    first_half = (lane & 32) == 0

    def proj(chunk):
        return jnp.dot(n, w_ref[:, chunk * TN_IN:(chunk + 1) * TN_IN], preferred_element_type=F32)

    def rope_store(acc, n_heads, scale, ref, col0):
        for hh in range(n_heads):
            y = _rope(acc[:, hh * HEAD_DIM:(hh + 1) * HEAD_DIM], cos, sin, first_half)
            if scale != 1.0:
                y = y * scale
            ref[:, col0 + hh * HEAD_DIM:col0 + (hh + 1) * HEAD_DIM] = y.astype(BF16)

    for c in range(2):
        rope_store(proj(c), 4, ATTN_SCALE, q_ref, c * TN_IN)
    acc = proj(2)
    rope_store(acc, 2, 1.0, kv_ref, 0)
    kv_ref[:, KV_COLS:] = acc[:, KV_COLS:].astype(BF16)
    n_perm = jnp.dot(perm_ref[...], n, preferred_element_type=F32).astype(BF16)
    for c in range(2):
        cols = slice(c * TN_IN, (c + 1) * TN_IN)
        x_ref[:, cols] = jnp.dot(n_perm, w_ref[:, (3 + c) * TN_IN:(4 + c) * TN_IN],
                                 preferred_element_type=F32)
        gy_ref[:, cols] = jax.nn.gelu(proj(5 + c)).astype(BF16)
        sga_ref[:, cols] = _sigmoid(proj(7 + c)).astype(BF16)
        sgr_ref[:, cols] = _sigmoid(proj(9 + c)).astype(BF16)


def _in_proj(h, g, mod3, w_in, layer, cos_t, sin_t, perm, *, n_lat_tiles, tiles_per_batch, n_batch):
    T = h.shape[0]

    def grp(i):
        return jnp.minimum(i // tiles_per_batch, n_batch)

    def pos_tile(i):
        return jnp.where(i < n_lat_tiles, i % tiles_per_batch, tiles_per_batch)

    def tok(width):
        return pl.BlockSpec((TM, width), lambda i: (i, 0))

    out_shape = (
        jax.ShapeDtypeStruct((T, Q_COLS), BF16),
        jax.ShapeDtypeStruct((T, 2 * KV_COLS), BF16),
        jax.ShapeDtypeStruct((T, D_RNN), F32),
        jax.ShapeDtypeStruct((T, D_RNN), BF16),
        jax.ShapeDtypeStruct((T, D_MODEL), BF16),
        jax.ShapeDtypeStruct((T, D_MODEL), BF16),
    )
    return pl.pallas_call(
        _in_proj_kernel,
        grid=(T // TM,),
        in_specs=[
            tok(D_MODEL),
            pl.BlockSpec((None, 1, D_MODEL), lambda i: (layer, 0, 0)),
            pl.BlockSpec((None, 1, D_MODEL), lambda i: (grp(i), 0, 0)),
            pl.BlockSpec((None, 1, D_MODEL), lambda i: (grp(i), 0, 1)),
            pl.BlockSpec((None, D_MODEL, IN_COLS), lambda i: (layer, 0, 0)),
            pl.BlockSpec((TM, HEAD_DIM), lambda i: (pos_tile(i), 0)),
            pl.BlockSpec((TM, HEAD_DIM), lambda i: (pos_tile(i), 0)),
            pl.BlockSpec((TM, TM), lambda i: (0, 0)),
        ],
        out_specs=(tok(Q_COLS), tok(2 * KV_COLS), tok(D_RNN), tok(D_RNN), tok(D_MODEL), tok(D_MODEL)),
        out_shape=out_shape,
        compiler_params=_cparams(1),
        name="in_proj",
    )(h, g, mod3, mod3, w_in, cos_t, sin_t, perm)


def _stack_heads(qt):
    return jnp.concatenate([qt[:, g * HEAD_DIM:(g + 1) * HEAD_DIM] for g in range(N_GROUPS)], axis=0)


def _sink_col(sink_ref, kvh, rows):
    return jnp.concatenate(
        [jnp.full((rows, 1), sink_ref[kvh * N_GROUPS + g] * LOG2E, F32) for g in range(N_GROUPS)], axis=0)


def _ones_column(n_keys):
    lane = lax.broadcasted_iota(jnp.int32, (n_keys, HEAD_DIM), 1)
    return jnp.where(lane == 0, 1.0, 0.0).astype(BF16)


def _softmax_pv(s, sink, vall):
    m = jnp.maximum(jnp.max(s, axis=1, keepdims=True), sink)
    p = jnp.exp2(s - m).astype(BF16)
    v_aug = jnp.concatenate([vall, _ones_column(vall.shape[0])], axis=1)
    oa = jnp.dot(p, v_aug, preferred_element_type=F32)
    denom = oa[:, HEAD_DIM:HEAD_DIM + 1] + jnp.exp2(sink - m)
    return oa[:, :HEAD_DIM] / denom


def _attn_kernel(sink_ref, q_ref, k_ref, v_ref, kc_ref, vc_ref, o_ref, *, n_blocks):
    kvh = pl.program_id(1)
    i = pl.program_id(2)
    rows = ATTN_BLOCK * N_GROUPS
    qi = lax.broadcasted_iota(jnp.int32, (rows, ATTN_BLOCK), 0) & (ATTN_BLOCK - 1)
    kj = lax.broadcasted_iota(jnp.int32, (rows, ATTN_BLOCK), 1)
    tri_prev = kj >= qi
    tri_next = kj <= qi
    sink = _sink_col(sink_ref, kvh, ATTN_BLOCK)
    kc = kc_ref[...]
    vc = vc_ref[...]

    def body(qb, carry):
        n = i * (TQ // ATTN_BLOCK) + qb
        r0 = pl.multiple_of(qb * ATTN_BLOCK, ATTN_BLOCK)
        p0 = pl.multiple_of(jnp.maximum(n - 1, 0) * ATTN_BLOCK, ATTN_BLOCK)
        c0 = pl.multiple_of(n * ATTN_BLOCK, ATTN_BLOCK)
        n0 = pl.multiple_of(jnp.minimum(n + 1, n_blocks - 1) * ATTN_BLOCK, ATTN_BLOCK)
        qs = _stack_heads(q_ref[pl.ds(r0, ATTN_BLOCK), :])
        kall = jnp.concatenate([k_ref[pl.ds(p0, ATTN_BLOCK), :], k_ref[pl.ds(c0, ATTN_BLOCK), :],
                                k_ref[pl.ds(n0, ATTN_BLOCK), :], kc], axis=0)
        vall = jnp.concatenate([v_ref[pl.ds(p0, ATTN_BLOCK), :], v_ref[pl.ds(c0, ATTN_BLOCK), :],
                                v_ref[pl.ds(n0, ATTN_BLOCK), :], vc], axis=0)
        s = lax.dot_general(qs, kall, (((1,), (1,)), ((), ())), preferred_element_type=F32)
        pen_prev = jnp.where(n > 0, 0.0, NEG_INF)
        pen_next = jnp.where(n < n_blocks - 1, 0.0, NEG_INF)
        sp = jnp.where(tri_prev, s[:, :ATTN_BLOCK] + pen_prev, NEG_INF)
        sn = jnp.where(tri_next, s[:, 2 * ATTN_BLOCK:3 * ATTN_BLOCK] + pen_next, NEG_INF)
        s = jnp.concatenate([sp, s[:, ATTN_BLOCK:2 * ATTN_BLOCK], sn, s[:, 3 * ATTN_BLOCK:]], axis=1)
        o = _softmax_pv(s, sink, vall)
        for g in range(N_GROUPS):
            o_ref[pl.ds(r0, ATTN_BLOCK), g * HEAD_DIM:(g + 1) * HEAD_DIM] = (
                o[g * ATTN_BLOCK:(g + 1) * ATTN_BLOCK].astype(BF16))
        return carry

    lax.fori_loop(0, TQ // ATTN_BLOCK, body, 0, unroll=True)


def _latent_attention(sink, q, kv, *, n_batch, seq, ctx_len):
    gw = N_GROUPS * HEAD_DIM
    tiles_per_batch = seq // TQ
    ctx_blk0 = n_batch * seq // ctx_len
    grid_spec = pltpu.PrefetchScalarGridSpec(
        num_scalar_prefetch=1,
        grid=(n_batch, N_KV_HEADS, tiles_per_batch),
        in_specs=[
            pl.BlockSpec((TQ, gw), lambda b, k, i, s: (b * tiles_per_batch + i, k)),
            pl.BlockSpec((seq, HEAD_DIM), lambda b, k, i, s: (b, k)),
            pl.BlockSpec((seq, HEAD_DIM), lambda b, k, i, s: (b, N_KV_HEADS + k)),
            pl.BlockSpec((ctx_len, HEAD_DIM), lambda b, k, i, s: (ctx_blk0 + b, k)),
            pl.BlockSpec((ctx_len, HEAD_DIM), lambda b, k, i, s: (ctx_blk0 + b, N_KV_HEADS + k)),
        ],
        out_specs=pl.BlockSpec((TQ, gw), lambda b, k, i, s: (b * tiles_per_batch + i, k)),
    )
    return pl.pallas_call(
        functools.partial(_attn_kernel, n_blocks=seq // ATTN_BLOCK),
        grid_spec=grid_spec,
        out_shape=jax.ShapeDtypeStruct((n_batch * seq, Q_COLS), BF16),
        compiler_params=_cparams(3),
        name="latent_attention",
    )(sink, q, kv, kv, kv, kv)


def _ctx_attn_kernel(sink_ref, q_ref, kc_ref, vc_ref, o_ref):
    kvh = pl.program_id(1)
    rows = q_ref.shape[0]
    qs = _stack_heads(q_ref[...])
    s = lax.dot_general(qs, kc_ref[...], (((1,), (1,)), ((), ())), preferred_element_type=F32)
    o = _softmax_pv(s, _sink_col(sink_ref, kvh, rows), vc_ref[...])
    for g in range(N_GROUPS):
        o_ref[:, g * HEAD_DIM:(g + 1) * HEAD_DIM] = o[g * rows:(g + 1) * rows].astype(BF16)


def _context_attention(sink, q, kv, *, n_batch, seq, ctx_len):
    gw = N_GROUPS * HEAD_DIM
    ctx_blk0 = n_batch * seq // ctx_len
    grid_spec = pltpu.PrefetchScalarGridSpec(
        num_scalar_prefetch=1,
        grid=(n_batch, N_KV_HEADS),
        in_specs=[
            pl.BlockSpec((ctx_len, gw), lambda b, k, s: (ctx_blk0 + b, k)),
            pl.BlockSpec((ctx_len, HEAD_DIM), lambda b, k, s: (ctx_blk0 + b, k)),
            pl.BlockSpec((ctx_len, HEAD_DIM), lambda b, k, s: (ctx_blk0 + b, N_KV_HEADS + k)),
        ],
        out_specs=pl.BlockSpec((ctx_len, gw), lambda b, k, s: (b, k)),
    )
    return pl.pallas_call(
        _ctx_attn_kernel,
        grid_spec=grid_spec,
        out_shape=jax.ShapeDtypeStruct((n_batch * ctx_len, Q_COLS), BF16),
        compiler_params=_cparams(2),
        name="context_attention",
    )(sink, q, kv, kv)


def _softplus(z):
    return jnp.maximum(z, 0.0) + jnp.log1p(jnp.exp(-jnp.abs(z)))


def _scan8(a, b, row, reverse):
    for s in (1, 2, 4):
        if reverse:
            keep = row < SUBLANES - s
            shift = SUBLANES - s
        else:
            keep = row >= s
            shift = s
        a_sh = jnp.where(keep, pltpu.roll(a, shift, 0), 1.0)
        b_sh = jnp.where(keep, pltpu.roll(b, shift, 0), 0.0)
        b = a * b_sh + b
        a = a * a_sh
    return a, b


def _rglru_kernel(xf_ref, xfp_ref, xfn_ref, xb_ref, xbp_ref, xbn_ref, cw_ref, cb_ref, wcat_ref,
                  gb_ref, lam_ref, unperm_ref, hf_ref, hb_ref, a_s, b_s, h_bf, carry):
    j = pl.program_id(1)
    last_j = pl.num_programs(1) - 1

    @pl.when(j == 0)
    def _():
        carry[...] = jnp.zeros_like(carry)

    seg_start = (j <= 1, (j == 0) | (j == last_j))
    seg_end = ((j == 0) | (j == last_j), j <= 1)
    mains = (xf_ref, xb_ref)
    prevs = (xfp_ref, xbp_ref)
    nexts = (xfn_ref, xbn_ref)
    sub = lax.broadcasted_iota(jnp.int32, (SUBLANES, D_RNN), 0)
    S8 = SUBLANES

    for d in range(2):
        x0 = mains[d][...]
        halo = prevs[d][...]
        t_m1 = jnp.where(seg_start[d], 0.0, halo[2 * S8 - 1:2 * S8, :])
        t_m2 = jnp.where(seg_start[d], 0.0, halo[S8 - 1:S8, :])
        t_p1 = jnp.where(seg_end[d], 0.0, nexts[d][0:1, :])
        g_m1 = jnp.where(sub == 0, t_m1, pltpu.roll(x0[TT - S8:, :], 1, 0))
        g_m2 = jnp.where(sub == 0, t_m2, pltpu.roll(x0[TT - 2 * S8:TT - S8, :], 1, 0))
        g_p1 = jnp.where(sub == S8 - 1, t_p1, pltpu.roll(x0[:S8, :], S8 - 1, 0))
        xc = cb_ref[...] + jnp.concatenate([g_m2, g_m1, x0[:TT - 2 * S8, :]], axis=0) * cw_ref[0:1, :]
        xc = xc + jnp.concatenate([g_m1, x0[:TT - S8, :]], axis=0) * cw_ref[1:2, :]
        xc = xc + x0 * cw_ref[2:3, :]
        xc = xc + jnp.concatenate([x0[S8:, :], g_p1], axis=0) * cw_ref[3:4, :]
        c_d = (-LRU_C * LOG2E) * _softplus(-lam_ref[d:d + 1, :])
        for blk in range(N_RNN_BLOCKS):
            sl = slice(blk * RNN_BLOCK, (blk + 1) * RNN_BLOCK)
            xcb = xc[:, sl]
            z = jnp.dot(xcb.astype(BF16), wcat_ref[d, blk], preferred_element_type=F32)
            r = _sigmoid(z[:, :RNN_BLOCK] + gb_ref[d, 0:1, sl])
            ig = _sigmoid(z[:, RNN_BLOCK:] + gb_ref[d, 1:2, sl])
            a = jnp.exp2(r * c_d[:, sl])
            v = 1.0 - a * a
            root = jnp.where(v > 0.0, v * lax.rsqrt(v), 0.0)
            a_s[d, :, sl] = a
            b_s[d, :, sl] = root * (ig * xcb)

    def local(g, c):
        hf, pf, hb, pb = c
        rf = pl.multiple_of(g * S8, S8)
        rb = pl.multiple_of((SUBSEQ - 1 - g) * S8, S8)
        af = a_s[0, pl.ds(rf, S8), :]
        ab = a_s[1, pl.ds(rb, S8), :]
        hf = af * hf + b_s[0, pl.ds(rf, S8), :]
        hb = ab * hb + b_s[1, pl.ds(rb, S8), :]
        pf = af * pf
        pb = ab * pb
        b_s[0, pl.ds(rf, S8), :] = hf
        b_s[1, pl.ds(rb, S8), :] = hb
        a_s[0, pl.ds(rf, S8), :] = pf
        a_s[1, pl.ds(rb, S8), :] = pb
        return hf, pf, hb, pb

    zero = jnp.zeros((S8, D_RNN), F32)
    one = jnp.ones((S8, D_RNN), F32)
    hf, pf, hb, pb = lax.fori_loop(0, SUBSEQ, local, (zero, one, zero, one), unroll=2)

    af, bf = _scan8(pf, hf, sub, False)
    endf = af * carry[0] + bf
    h_in = [jnp.where(sub == 0, carry[0], pltpu.roll(endf, 1, 0))]
    carry[0] = jnp.broadcast_to(endf[S8 - 1:S8, :], (S8, D_RNN))
    ab, bb = _scan8(pb, hb, sub, True)
    endb = ab * carry[1] + bb
    h_in.append(jnp.where(sub == S8 - 1, carry[1], pltpu.roll(endb, S8 - 1, 0)))
    carry[1] = jnp.broadcast_to(endb[0:1, :], (S8, D_RNN))

    outs = (hf_ref, hb_ref)
    for d in range(2):
        h_in2 = jnp.concatenate([h_in[d], h_in[d]], axis=0)

        def fix(k, c, d=d, h_in2=h_in2):
            rows = pl.ds(pl.multiple_of(k * 2 * S8, 2 * S8), 2 * S8)
            h_bf[d, rows, :] = (b_s[d, rows, :] + a_s[d, rows, :] * h_in2).astype(BF16)
            return c

        lax.fori_loop(0, SUBSEQ // 2, fix, 0, unroll=2)
        outs[d][...] = jnp.dot(unperm_ref[...], h_bf[d], preferred_element_type=F32).astype(BF16)


def _time_permutation(n_rows):
    p = jnp.arange(n_rows)
    src = (p // TT) * TT + (p % SUBLANES) * SUBSEQ + (p % TT) // SUBLANES
    return (src[:, None] == jnp.arange(n_rows)[None, :]).astype(BF16)


def _rglru(x, conv_w, conv_b, wcat, gate_b, lam, *, n_batch, seq, ctx_len):
    T = x.shape[0]
    assert ctx_len == TT and seq % TT == 0
    tps = seq // TT
    n_lat_t = n_batch * tps
    halo_rows = 2 * SUBLANES
    per_halo = TT // halo_rows
    per8 = TT // SUBLANES
    last8 = T // SUBLANES - 1
    unperm = _time_permutation(TT).T

    def ftile(b, j):
        return jnp.where(j == 0, n_lat_t + b, b * tps + j - 1)

    def btile(b, j):
        return jnp.where(j == 0, n_lat_t + b, b * tps + tps - j)

    def main(tile):
        return pl.BlockSpec((TT, D_RNN), lambda b, j: (tile(b, j), 0))

    def prev(tile):
        return pl.BlockSpec((halo_rows, D_RNN), lambda b, j: (jnp.maximum(tile(b, j) * per_halo - 1, 0), 0))

    def nxt(tile):
        return pl.BlockSpec((SUBLANES, D_RNN),
                            lambda b, j: (jnp.minimum((tile(b, j) + 1) * per8, last8), 0))

    def const(shape):
        return pl.BlockSpec(shape, lambda b, j: (0,) * len(shape))

    return pl.pallas_call(
        _rglru_kernel,
        grid=(n_batch, tps + 1),
        in_specs=[main(ftile), prev(ftile), nxt(ftile), main(btile), prev(btile), nxt(btile),
                  const(conv_w.shape), const(conv_b.shape), const(wcat.shape), const(gate_b.shape),
                  const(lam.shape), const(unperm.shape)],
        out_specs=(main(ftile), main(btile)),
        out_shape=(jax.ShapeDtypeStruct((T, D_RNN), BF16), jax.ShapeDtypeStruct((T, D_RNN), BF16)),
        scratch_shapes=[pltpu.VMEM((2, TT, D_RNN), F32),
                        pltpu.VMEM((2, TT, D_RNN), F32),
                        pltpu.VMEM((2, TT, D_RNN), BF16),
                        pltpu.VMEM((2, SUBLANES, D_RNN), F32)],
        compiler_params=_cparams(2),
        name="rglru",
    )(x, x, x, x, x, x, conv_w, conv_b, wcat, gate_b, lam, unperm)


def _merge_kernel(h_ref, attn_ref, attn_ctx_ref, hf_ref, hb_ref, gy_ref, sga_ref, sgr_ref, g1_ref,
                  woa_ref, wol_ref, wout_ref, o_ref):
    rec = ((hf_ref[...].astype(F32) + hb_ref[...].astype(F32)) * gy_ref[...].astype(F32)).astype(BF16)
    is_ctx_tile = pl.program_id(0) == pl.num_programs(0) - 1
    attn = jnp.where(is_ctx_tile, attn_ctx_ref[...], attn_ref[...])
    ta = jnp.dot(attn, woa_ref[...], preferred_element_type=F32)
    tl = jnp.dot(rec, wol_ref[...], preferred_element_type=F32)
    m = sga_ref[...].astype(F32) * ta + sgr_ref[...].astype(F32) * tl
    y = jnp.dot(m.astype(BF16), wout_ref[...], preferred_element_type=F32)
    o_ref[...] = h_ref[...] + g1_ref[...] * y


def _merge(h, attn, attn_ctx, hf, hb, gy, sga, sgr, mod3, woa, wol, wout, layer, *, tiles_per_batch, n_batch):
    T = h.shape[0]
    n_lat_tiles = attn.shape[0] // TM
    assert attn_ctx.shape[0] == TM and n_lat_tiles == T // TM - 1

    def grp(i):
        return jnp.minimum(i // tiles_per_batch, n_batch)

    tok = pl.BlockSpec((TM, D_MODEL), lambda i: (i, 0))
    wsp = pl.BlockSpec((None, D_MODEL, D_MODEL), lambda i: (layer, 0, 0))
    return pl.pallas_call(
        _merge_kernel,
        grid=(T // TM,),
        in_specs=[tok,
                  pl.BlockSpec((TM, Q_COLS), lambda i: (jnp.minimum(i, n_lat_tiles - 1), 0)),
                  pl.BlockSpec((TM, Q_COLS), lambda i: (0, 0)),
                  tok, tok, tok, tok, tok,
                  pl.BlockSpec((None, 1, D_MODEL), lambda i: (grp(i), 0, 2)),
                  wsp, wsp, wsp],
        out_specs=tok,
        out_shape=jax.ShapeDtypeStruct((T, D_MODEL), F32),
        compiler_params=_cparams(1),
        name="merge",
    )(h, attn, attn_ctx, hf, hb, gy, sga, sgr, mod3, woa, wol, wout)


def _swiglu_partial(n, wg, wu, wd):
    gt = jnp.dot(n, wg, preferred_element_type=F32)
    ut = jnp.dot(n, wu, preferred_element_type=F32)
    act = (gt * _sigmoid(gt) * ut).astype(BF16)
    return jnp.dot(act, wd, preferred_element_type=F32)


def _ffn_kernel(h_ref, g_ref, sh_ref, sc_ref, g2_ref, wg_ref, wu_ref, wd_ref, o_ref, n_scr, acc):
    f = pl.program_id(1)

    @pl.when(f == 0)
    def _():
        n_scr[...] = _norm_mod(h_ref[...], g_ref[...], sh_ref[...], sc_ref[...]).astype(BF16)
        acc[...] = jnp.zeros_like(acc)

    acc[...] += _swiglu_partial(n_scr[...], wg_ref[...], wu_ref[...], wd_ref[...])

    @pl.when(f == pl.num_programs(1) - 1)
    def _():
        o_ref[...] = h_ref[...] + g2_ref[...] * acc[...]


def _ffn(h, g, mod3, wg, wu, wd, layer, ff_layer, *, tiles_per_batch, n_batch):
    T = h.shape[0]
    n_f = 2
    tf = D_FF // n_f
    assert tf % LANES == 0

    def grp(i):
        return jnp.minimum(i // tiles_per_batch, n_batch)

    def modspec(k):
        return pl.BlockSpec((None, 1, D_MODEL), lambda i, f: (grp(i), 0, k))

    tok = pl.BlockSpec((TM, D_MODEL), lambda i, f: (i, 0))
    return pl.pallas_call(
        _ffn_kernel,
        grid=(T // TM, n_f),
        in_specs=[tok, pl.BlockSpec((None, 1, D_MODEL), lambda i, f: (layer, 0, 0)),
                  modspec(3), modspec(4), modspec(5),
                  pl.BlockSpec((None, D_MODEL, tf), lambda i, f: (ff_layer, 0, f)),
                  pl.BlockSpec((None, D_MODEL, tf), lambda i, f: (ff_layer, 0, f)),
                  pl.BlockSpec((None, tf, D_MODEL), lambda i, f: (ff_layer, f, 0))],
        out_specs=tok,
        out_shape=jax.ShapeDtypeStruct((T, D_MODEL), F32),
        scratch_shapes=[pltpu.VMEM((TM, D_MODEL), BF16), pltpu.VMEM((TM, D_MODEL), F32)],
        compiler_params=_cparams(2),
        name="dense_ffn",
    )(h, g, mod3, mod3, mod3, wg, wu, wd)


ROUTE_E1, ROUTE_E2, ROUTE_W1, ROUTE_W2, ROUTE_R1, ROUTE_R2 = range(6)


def _dot_split(a, b):
    a_hi = a.astype(BF16)
    a_lo = (a - a_hi.astype(F32)).astype(BF16)
    b_hi = b.astype(BF16)
    b_lo = (b - b_hi.astype(F32)).astype(BF16)

    def mm(x, y):
        return jnp.dot(x, y, preferred_element_type=F32)

    return mm(a_hi, b_hi) + (mm(a_hi, b_lo) + mm(a_lo, b_hi))


def _router_kernel(h_ref, g_ref, sh_ref, sc_ref, wr_ref, route_ref, cnt_ref, run):
    @pl.when(pl.program_id(0) == 0)
    def _():
        run[...] = jnp.zeros_like(run)

    n = _norm_mod(h_ref[...], g_ref[...], sh_ref[...], sc_ref[...])
    logits = _dot_split(n, wr_ref[...])
    lane = lax.broadcasted_iota(jnp.int32, logits.shape, 1)
    logits = jnp.where(lane < N_EXPERTS, logits, -jnp.inf)
    m1 = jnp.max(logits, axis=1, keepdims=True)
    i1 = jnp.min(jnp.where(logits == m1, lane, LANES), axis=1, keepdims=True)
    rest = jnp.where(lane == i1, -jnp.inf, logits)
    m2 = jnp.max(rest, axis=1, keepdims=True)
    i2 = jnp.min(jnp.where(rest == m2, lane, LANES), axis=1, keepdims=True)
    e2 = jnp.exp(m2 - m1)
    w1 = 1.0 / (1.0 + e2)
    w2 = e2 / (1.0 + e2)

    hit1 = lane == i1
    hit2 = lane == i2
    onehot = jnp.where(hit1 | hit2, 1.0, 0.0)
    r_i = lax.broadcasted_iota(jnp.int32, (TM, TM), 0)
    c_i = lax.broadcasted_iota(jnp.int32, (TM, TM), 1)
    lower = jnp.where(c_i < r_i, 1.0, 0.0).astype(BF16)
    prefix = jnp.dot(lower, onehot.astype(BF16), preferred_element_type=F32) + run[0:1, :]
    rank1 = jnp.sum(jnp.where(hit1, prefix, 0.0), axis=1, keepdims=True)
    rank2 = jnp.sum(jnp.where(hit2, prefix, 0.0), axis=1, keepdims=True)
    run[...] = run[...] + jnp.sum(onehot, axis=0, keepdims=True)
    cnt_ref[...] = run[...]

    rec = jnp.zeros(logits.shape, F32)
    for k, v in ((ROUTE_E1, i1.astype(F32)), (ROUTE_E2, i2.astype(F32)), (ROUTE_W1, w1), (ROUTE_W2, w2),
                 (ROUTE_R1, rank1), (ROUTE_R2, rank2)):
        rec = jnp.where(lane == k, v, rec)
    route_ref[...] = rec


def _router(h, g2, mod3, w_r, layer, *, tiles_per_batch, n_batch):
    T = h.shape[0]

    def grp(i):
        return jnp.minimum(i // tiles_per_batch, n_batch)

    def modspec(k):
        return pl.BlockSpec((None, 1, D_MODEL), lambda i: (grp(i), 0, k))

    return pl.pallas_call(
        _router_kernel,
        grid=(T // TM,),
        in_specs=[pl.BlockSpec((TM, D_MODEL), lambda i: (i, 0)),
                  pl.BlockSpec((None, 1, D_MODEL), lambda i: (layer, 0, 0)),
                  modspec(3), modspec(4),
                  pl.BlockSpec((D_MODEL, LANES), lambda i: (0, 0))],
        out_specs=(pl.BlockSpec((TM, LANES), lambda i: (i, 0)),
                   pl.BlockSpec((SUBLANES, LANES), lambda i: (0, 0))),
        out_shape=(jax.ShapeDtypeStruct((T, LANES), F32), jax.ShapeDtypeStruct((SUBLANES, LANES), F32)),
        scratch_shapes=[pltpu.VMEM((SUBLANES, LANES), F32)],
        compiler_params=_cparams(1),
        name="moe_router",
    )(h, g2, mod3, mod3, w_r)


def _row_copy(src_ref, src_row, dst_ref, dst_row, sem):
    return pltpu.make_async_copy(src_ref.at[pl.ds(src_row, 1)], dst_ref.at[pl.ds(dst_row, 1)], sem)


def _dispatch_kernel(dest_ref, h_ref, g_ref, sh_ref, sc_ref, xs_in_ref, xs_ref, n_scr, sem):
    del xs_in_ref
    base = pl.program_id(0) * (2 * TM)
    n_scr[...] = _norm_mod(h_ref[...], g_ref[...], sh_ref[...], sc_ref[...])

    def start(r, carry):
        for s in range(2):
            _row_copy(n_scr, r, xs_ref, dest_ref[base + 2 * r + s], sem).start()
        return carry

    lax.fori_loop(0, TM, start, 0, unroll=ROW_DMA_UNROLL)
    for s in range(2):
        pltpu.make_async_copy(n_scr, xs_ref.at[pl.ds(0, TM)], sem).wait()


def _dispatch(dest, h, g2, mod3, xs_zero, layer, *, tiles_per_batch, n_batch):
    T = h.shape[0]

    def grp(i):
        return jnp.minimum(i // tiles_per_batch, n_batch)

    def modspec(k):
        return pl.BlockSpec((None, 1, D_MODEL), lambda i, d: (grp(i), 0, k))

    grid_spec = pltpu.PrefetchScalarGridSpec(
        num_scalar_prefetch=1,
        grid=(T // TM,),
        in_specs=[pl.BlockSpec((TM, D_MODEL), lambda i, d: (i, 0)),
                  pl.BlockSpec((None, 1, D_MODEL), lambda i, d: (layer, 0, 0)),
                  modspec(3), modspec(4),
                  pl.BlockSpec(memory_space=pl.ANY)],
        out_specs=pl.BlockSpec(memory_space=pl.ANY),
        scratch_shapes=[pltpu.VMEM((TM, D_MODEL), F32), pltpu.SemaphoreType.DMA(())],
    )
    return pl.pallas_call(
        _dispatch_kernel,
        grid_spec=grid_spec,
        out_shape=jax.ShapeDtypeStruct(xs_zero.shape, xs_zero.dtype),
        input_output_aliases={5: 0},
        compiler_params=_cparams(1),
        name="moe_dispatch",
    )(dest, h, g2, mod3, mod3, xs_zero)


def _expert_kernel(te_ref, nu_ref, rows_ref, x_ref, wg_ref, wu_ref, wd_ref, o_ref, xb, acc):
    del te_ref, nu_ref
    k = pl.program_id(0)
    f = pl.program_id(1)

    @pl.when(f == 0)
    def _():
        xb[...] = x_ref[...].astype(BF16)
        acc[...] = jnp.zeros_like(acc)

    wg = wg_ref[...].astype(BF16)
    wu = wu_ref[...].astype(BF16)
    wd = wd_ref[...].astype(BF16)
    for sub in range(TG // TG_SUB):
        rows = slice(sub * TG_SUB, (sub + 1) * TG_SUB)

        @pl.when(rows_ref[k] > sub * TG_SUB)
        def _():
            acc[rows, :] += _swiglu_partial(xb[rows, :], wg, wu, wd)

    @pl.when(f == pl.num_programs(1) - 1)
    def _():
        o_ref[...] = acc[...]


def _experts(tile_expert, n_used, tile_rows, xs, wg, wu, wd, layer):
    P = xs.shape[0]
    n_f = MOE_F_CHUNKS
    tf = D_FF_EXPERT // n_f
    assert tf % LANES == 0

    def fsel(k, f, nu):
        return jnp.where(k < nu[0], f, n_f - 1)

    grid_spec = pltpu.PrefetchScalarGridSpec(
        num_scalar_prefetch=3,
        grid=(P // TG, n_f),
        in_specs=[pl.BlockSpec((TG, D_MODEL), lambda k, f, te, nu, tr: (k, 0)),
                  pl.BlockSpec((None, None, D_MODEL, tf), lambda k, f, te, nu, tr: (layer, te[k], 0, fsel(k, f, nu))),
                  pl.BlockSpec((None, None, D_MODEL, tf), lambda k, f, te, nu, tr: (layer, te[k], 0, fsel(k, f, nu))),
                  pl.BlockSpec((None, None, tf, D_MODEL), lambda k, f, te, nu, tr: (layer, te[k], fsel(k, f, nu), 0))],
        out_specs=pl.BlockSpec((TG, D_MODEL), lambda k, f, te, nu, tr: (k, 0)),
        scratch_shapes=[pltpu.VMEM((TG, D_MODEL), BF16), pltpu.VMEM((TG, D_MODEL), F32)],
    )
    return pl.pallas_call(
        _expert_kernel,
        grid_spec=grid_spec,
        out_shape=jax.ShapeDtypeStruct((P, D_MODEL), F32),
        compiler_params=_cparams(2),
        name="moe_experts",
    )(tile_expert, n_used, tile_rows, xs, wg, wu, wd)


def _combine_kernel(dest_ref, h_ref, g2_ref, route_ref, fg_ref, y_ref, o_ref, ybuf, sem, *, final):
    base = pl.program_id(0) * (2 * TM)

    def start(r, carry):
        for s in range(2):
            _row_copy(y_ref, dest_ref[base + 2 * r + s], ybuf, s * TM + r, sem).start()
        return carry

    lax.fori_loop(0, TM, start, 0, unroll=ROW_DMA_UNROLL)
    for s in range(2):
        pltpu.make_async_copy(y_ref.at[pl.ds(0, TM)], ybuf.at[pl.ds(s * TM, TM)], sem).wait()
    route = route_ref[...]
    f = route[:, ROUTE_W1:ROUTE_W1 + 1] * ybuf[0:TM, :] + route[:, ROUTE_W2:ROUTE_W2 + 1] * ybuf[TM:, :]
    out = h_ref[...] + g2_ref[...] * f
    if final:
        ms = jnp.mean(out * out, axis=-1, keepdims=True)
        out = out * lax.rsqrt(ms + EPS) * fg_ref[...]
    o_ref[...] = out


def _combine(dest, h, mod3, route, final_g, y, *, final, tiles_per_batch, n_batch):
    n_tok_tiles = h.shape[0] // TM - (1 if final else 0)

    def grp(i):
        return jnp.minimum(i // tiles_per_batch, n_batch)

    grid_spec = pltpu.PrefetchScalarGridSpec(
        num_scalar_prefetch=1,
        grid=(n_tok_tiles,),
        in_specs=[pl.BlockSpec((TM, D_MODEL), lambda i, d: (i, 0)),
                  pl.BlockSpec((None, 1, D_MODEL), lambda i, d: (grp(i), 0, 5)),
                  pl.BlockSpec((TM, LANES), lambda i, d: (i, 0)),
                  pl.BlockSpec((1, D_MODEL), lambda i, d: (0, 0)),
                  pl.BlockSpec(memory_space=pl.ANY)],
        out_specs=pl.BlockSpec((TM, D_MODEL), lambda i, d: (i, 0)),
        scratch_shapes=[pltpu.VMEM((2 * TM, D_MODEL), F32), pltpu.SemaphoreType.DMA(())],
    )
    return pl.pallas_call(
        functools.partial(_combine_kernel, final=final),
        grid_spec=grid_spec,
        out_shape=jax.ShapeDtypeStruct((n_tok_tiles * TM, D_MODEL), F32),
        compiler_params=_cparams(1),
        name="moe_combine",
    )(dest, h, mod3, route, final_g, y)


def _moe_layer(h, g2, mod3, w_r, wg, wu, wd, final_g, layer, moe_layer, *, final, **geo):
    T = h.shape[0]
    n_tiles = (2 * T) // TG + N_EXPERTS
    route, cnt = _router(h, g2, mod3, w_r, layer, **geo)

    counts = cnt[0, :N_EXPERTS].astype(jnp.int32)
    padded = ((counts + TG - 1) // TG) * TG
    ends = jnp.cumsum(padded)
    offs = ends - padded
    e12 = route[:, ROUTE_E1:ROUTE_E2 + 1].astype(jnp.int32)
    r12 = route[:, ROUTE_R1:ROUTE_R2 + 1].astype(jnp.int32)
    onehot = e12[:, :, None] == jnp.arange(N_EXPERTS)[None, None, :]
    dest = (jnp.sum(jnp.where(onehot, offs[None, None, :], 0), axis=-1) + r12).reshape(2 * T)
    n_used = (ends[-1] // TG).reshape(1)
    tiles = jnp.arange(n_tiles)
    te_raw = jnp.sum(tiles[:, None] >= (ends // TG)[None, :], axis=1)
    tile_expert = jnp.minimum(te_raw, N_EXPERTS - 1).astype(jnp.int32)
    sel = tile_expert[:, None] == jnp.arange(N_EXPERTS)[None, :]
    cnt_k = jnp.sum(jnp.where(sel, counts[None, :], 0), axis=1)
    off_k = jnp.sum(jnp.where(sel, offs[None, :], 0), axis=1)
    tile_rows = jnp.where(te_raw < N_EXPERTS, jnp.clip(cnt_k - (tiles * TG - off_k), 0, TG), 0)

    xs = _dispatch(dest, h, g2, mod3, jnp.zeros((n_tiles * TG, D_MODEL), F32), layer, **geo)
    y = _experts(tile_expert, n_used, tile_rows.astype(jnp.int32), xs, wg, wu, wd, moe_layer)
    return _combine(dest, h, mod3, route, final_g, y, final=final, **geo)


def _rope_tables(seq):
    t = jnp.arange(seq)
    inv = ROPE_THETA ** (-jnp.arange(ROPE_FREQS, dtype=F32) / ROPE_FREQS)
    ang_r = (t // GRID_W).astype(F32)[:, None] * inv
    ang_c = (t % GRID_W).astype(F32)[:, None] * inv
    cos = jnp.concatenate([jnp.cos(ang_r)] * 2 + [jnp.cos(ang_c)] * 2, axis=1)
    sin = jnp.concatenate([-jnp.sin(ang_r), jnp.sin(ang_r), -jnp.sin(ang_c), jnp.sin(ang_c)], axis=1)
    cos = jnp.concatenate([cos, jnp.ones((TM, HEAD_DIM), F32)], axis=0)
    sin = jnp.concatenate([sin, jnp.zeros((TM, HEAD_DIM), F32)], axis=0)
    return cos, sin


def kernel(x, c, ctx, c_ctx, w_mod, b_mod, norm1_g, norm2_g, w_in, attn_sink, conv_w, conv_b, gate_a_w, gate_a_b, gate_x_w, gate_x_b, lru_lambda, w_o_attn, w_o_lru, w_out, ff_w_gate, ff_w_up, ff_w_down, router_w, exp_w_gate, exp_w_up, exp_w_down, final_g):
    n_batch, seq, _ = x.shape
    ctx_len = ctx.shape[1]
    assert n_batch * ctx_len == TM and seq % TM == 0 and n_batch + 1 <= MOD_ROWS
    n_lat = n_batch * seq
    tiles_per_batch = seq // TM
    geo = dict(tiles_per_batch=tiles_per_batch, n_batch=n_batch)
    shp = dict(n_batch=n_batch, seq=seq, ctx_len=ctx_len)

    cpad = jnp.zeros((MOD_ROWS, D_MODEL), F32).at[:n_batch].set(c).at[n_batch].set(c_ctx)
    mod = _modulation(cpad, w_mod, b_mod)
    cos_t, sin_t = _rope_tables(seq)
    perm = _time_permutation(TM)
    h = jnp.concatenate([x.reshape(n_lat, D_MODEL), ctx.reshape(n_batch * ctx_len, D_MODEL)], axis=0)

    g1 = norm1_g.reshape(DEPTH, 1, D_MODEL)
    g2 = norm2_g.reshape(DEPTH, 1, D_MODEL)
    w_in_b = w_in.astype(BF16)
    woa_b, wol_b, wout_b = w_o_attn.astype(BF16), w_o_lru.astype(BF16), w_out.astype(BF16)
    ffg_b, ffu_b, ffd_b = ff_w_gate.astype(BF16), ff_w_up.astype(BF16), ff_w_down.astype(BF16)

    for l in range(DEPTH):
        mod3 = mod[l].reshape(MOD_ROWS, 1, 6 * D_MODEL)
        q, kv, xr, gy, sga, sgr = _in_proj(h, g1, mod3, w_in_b, l, cos_t, sin_t, perm,
                                           n_lat_tiles=n_lat // TM, **geo)
        attn = _latent_attention(attn_sink[l], q, kv, **shp)
        attn_ctx = _context_attention(attn_sink[l], q, kv, **shp)
        wcat = jnp.concatenate([gate_a_w[l], gate_x_w[l]], axis=-1).astype(BF16)
        gate_b = jnp.stack([gate_a_b[l], gate_x_b[l]], axis=1)
        hf, hb = _rglru(xr, conv_w[l], conv_b[l].reshape(1, D_RNN), wcat, gate_b, lru_lambda[l], **shp)
        h = _merge(h, attn, attn_ctx, hf, hb, gy, sga, sgr, mod3, woa_b, wol_b, wout_b, l, **geo)
        i = l // 2
        if l % 2 == 0:
            h = _ffn(h, g2, mod3, ffg_b, ffu_b, ffd_b, l, i, **geo)
        else:
            w_r = jnp.zeros((D_MODEL, LANES), F32).at[:, :N_EXPERTS].set(router_w[i])
            h = _moe_layer(h, g2, mod3, w_r, exp_w_gate, exp_w_up, exp_w_down,
                           final_g.reshape(1, D_MODEL), l, i, final=(l == DEPTH - 1), **geo)

    assert DEPTH % 2 == 0 and h.shape[0] == n_lat
    return h.reshape(n_batch, seq, D_MODEL)
```

```python
import functools

import jax
import jax.numpy as jnp
from jax import lax
from jax.experimental import pallas as pl
from jax.experimental.pallas import tpu as pltpu

F32 = jnp.float32
BF16 = jnp.bfloat16

D_MODEL = 1024
DEPTH = 4
GRID_W = 64
N_HEADS = 8
N_KV_HEADS = 2
HEAD_DIM = 128
N_GROUPS = N_HEADS // N_KV_HEADS
ATTN_BLOCK = 128
ROPE_THETA = 10000.0
ROPE_FREQS = HEAD_DIM // 4
D_RNN = 1024
N_RNN_BLOCKS = 8
RNN_BLOCK = D_RNN // N_RNN_BLOCKS
LRU_C = 8.0
D_FF = 2816
N_EXPERTS = 8
D_FF_EXPERT = 3584
EPS = 1e-6
NEG_INF = -1e30
Q_COLS = N_HEADS * HEAD_DIM
KV_COLS = N_KV_HEADS * HEAD_DIM
IN_COLS = Q_COLS + 2 * KV_COLS + 2 * D_RNN + 2 * D_MODEL
LOG2E = 1.4426950408889634
ATTN_SCALE = HEAD_DIM ** -0.5 * LOG2E

LANES = 128
SUBLANES = 8
TM = 512
TN_IN = 512
TQ = 512
TT = 256
SUBSEQ = TT // SUBLANES
TG = 1024
TG_SUB = 512
MOE_F_CHUNKS = 7
ROW_DMA_UNROLL = 8
MOD_ROWS = 8
VMEM_LIMIT = 56 * 1024 * 1024


def _cparams(n_axes):
    return pltpu.CompilerParams(dimension_semantics=("arbitrary",) * n_axes,
                                vmem_limit_bytes=VMEM_LIMIT)


def _sigmoid(z):
    return 0.5 * jnp.tanh(0.5 * z) + 0.5


def _norm_mod(h, g, shift, scale):
    ms = jnp.mean(h * h, axis=-1, keepdims=True)
    y = h * lax.rsqrt(ms + EPS) * g
    return y * (1.0 + scale) + shift


def _mod_kernel(c_ref, w_ref, b_ref, o_ref):
    cv = c_ref[...]
    s = cv * _sigmoid(cv)
    o_ref[...] = jnp.dot(s, w_ref[...], preferred_element_type=F32,
                         precision=lax.Precision.HIGHEST) + b_ref[...]


def _modulation(cpad, w_mod, b_mod):
    nchunk = 6
    return pl.pallas_call(
        _mod_kernel,
        grid=(DEPTH, nchunk),
        in_specs=[
            pl.BlockSpec((MOD_ROWS, D_MODEL), lambda l, n: (0, 0)),
            pl.BlockSpec((None, D_MODEL, D_MODEL), lambda l, n: (l, 0, n)),
            pl.BlockSpec((None, 1, D_MODEL), lambda l, n: (l, 0, n)),
        ],
        out_specs=pl.BlockSpec((None, MOD_ROWS, D_MODEL), lambda l, n: (l, 0, n)),
        out_shape=jax.ShapeDtypeStruct((DEPTH, MOD_ROWS, 6 * D_MODEL), F32),
        compiler_params=_cparams(2),
        name="modulation",
    )(cpad, w_mod, b_mod.reshape(DEPTH, 1, 6 * D_MODEL))


def _rope(xh, cos, sin_signed, first_half):
    sw = jnp.where(first_half, pltpu.roll(xh, 96, 1), pltpu.roll(xh, 32, 1))
    return xh * cos + sw * sin_signed


def _stream_tile(h_lat_ref, h_ctx_ref):
    is_ctx_tile = pl.program_id(0) == pl.num_programs(0) - 1
    return jnp.where(is_ctx_tile, h_ctx_ref[...], h_lat_ref[...])


def _stream_specs(stream, n_lat_tiles):
    _, _, ctx_block = stream
    return [pl.BlockSpec((TM, D_MODEL), lambda i: (jnp.minimum(i, n_lat_tiles - 1), 0)),
            pl.BlockSpec((TM, D_MODEL), lambda i: (ctx_block, 0))]


def _in_proj_kernel(h_lat_ref, h_ctx_ref, g_ref, sh_ref, sc_ref, w_ref, cos_ref, sin_ref, perm_ref,
                    q_ref, kv_ref, x_ref, gy_ref, sga_ref, sgr_ref):
    n = _norm_mod(_stream_tile(h_lat_ref, h_ctx_ref), g_ref[...], sh_ref[...], sc_ref[...]).astype(BF16)
    cos = cos_ref[...]
    sin = sin_ref[...]
    lane = lax.broadcasted_iota(jnp.int32, (TM, LANES), 1)
    first_half = (lane & 32) == 0

    def proj(chunk):
        return jnp.dot(n, w_ref[:, chunk * TN_IN:(chunk + 1) * TN_IN], preferred_element_type=F32)

    def rope_store(acc, n_heads, scale, ref, col0):
        for hh in range(n_heads):
            y = _rope(acc[:, hh * HEAD_DIM:(hh + 1) * HEAD_DIM], cos, sin, first_half)
            if scale != 1.0:
                y = y * scale
            ref[:, col0 + hh * HEAD_DIM:col0 + (hh + 1) * HEAD_DIM] = y.astype(BF16)

    for c in range(2):
        rope_store(proj(c), 4, ATTN_SCALE, q_ref, c * TN_IN)
    acc = proj(2)
    rope_store(acc, 2, 1.0, kv_ref, 0)
    kv_ref[:, KV_COLS:] = acc[:, KV_COLS:].astype(BF16)
    n_perm = jnp.dot(perm_ref[...], n, preferred_element_type=F32).astype(BF16)
    for c in range(2):
        cols = slice(c * TN_IN, (c + 1) * TN_IN)
        x_ref[:, cols] = jnp.dot(n_perm, w_ref[:, (3 + c) * TN_IN:(4 + c) * TN_IN],
                                 preferred_element_type=F32)
        gy_ref[:, cols] = jax.nn.gelu(proj(5 + c)).astype(BF16)
        sga_ref[:, cols] = _sigmoid(proj(7 + c)).astype(BF16)
        sgr_ref[:, cols] = _sigmoid(proj(9 + c)).astype(BF16)


def _in_proj(stream, g, mod3, w_in, layer, cos_t, sin_t, perm, *, n_lat_tiles, tiles_per_batch, n_batch):
    T = (n_lat_tiles + 1) * TM

    def grp(i):
        return jnp.minimum(i // tiles_per_batch, n_batch)

    def pos_tile(i):
        return jnp.where(i < n_lat_tiles, i % tiles_per_batch, tiles_per_batch)

    def tok(width):
        return pl.BlockSpec((TM, width), lambda i: (i, 0))

    out_shape = (
        jax.ShapeDtypeStruct((T, Q_COLS), BF16),
        jax.ShapeDtypeStruct((T, 2 * KV_COLS), BF16),
        jax.ShapeDtypeStruct((T, D_RNN), F32),
        jax.ShapeDtypeStruct((T, D_RNN), BF16),
        jax.ShapeDtypeStruct((T, D_MODEL), BF16),
        jax.ShapeDtypeStruct((T, D_MODEL), BF16),
    )
    return pl.pallas_call(
        _in_proj_kernel,
        grid=(T // TM,),
        in_specs=_stream_specs(stream, n_lat_tiles) + [
            pl.BlockSpec((None, 1, D_MODEL), lambda i: (layer, 0, 0)),
            pl.BlockSpec((None, 1, D_MODEL), lambda i: (grp(i), 0, 0)),
            pl.BlockSpec((None, 1, D_MODEL), lambda i: (grp(i), 0, 1)),
            pl.BlockSpec((None, D_MODEL, IN_COLS), lambda i: (layer, 0, 0)),
            pl.BlockSpec((TM, HEAD_DIM), lambda i: (pos_tile(i), 0)),
            pl.BlockSpec((TM, HEAD_DIM), lambda i: (pos_tile(i), 0)),
            pl.BlockSpec((TM, TM), lambda i: (0, 0)),
        ],
        out_specs=(tok(Q_COLS), tok(2 * KV_COLS), tok(D_RNN), tok(D_RNN), tok(D_MODEL), tok(D_MODEL)),
        out_shape=out_shape,
        compiler_params=_cparams(1),
        name="in_proj",
    )(stream[0], stream[1], g, mod3, mod3, w_in, cos_t, sin_t, perm)


def _stack_heads(qt):
    return jnp.concatenate([qt[:, g * HEAD_DIM:(g + 1) * HEAD_DIM] for g in range(N_GROUPS)], axis=0)


def _sink_col(sink_ref, kvh, rows):
    return jnp.concatenate(
        [jnp.full((rows, 1), sink_ref[kvh * N_GROUPS + g] * LOG2E, F32) for g in range(N_GROUPS)], axis=0)


def _ones_column(n_keys):
    lane = lax.broadcasted_iota(jnp.int32, (n_keys, HEAD_DIM), 1)
    return jnp.where(lane == 0, 1.0, 0.0).astype(BF16)


def _softmax_pv(s, sink, vall):
    m = jnp.maximum(jnp.max(s, axis=1, keepdims=True), sink)
    p = jnp.exp2(s - m).astype(BF16)
    v_aug = jnp.concatenate([vall, _ones_column(vall.shape[0])], axis=1)
    oa = jnp.dot(p, v_aug, preferred_element_type=F32)
    denom = oa[:, HEAD_DIM:HEAD_DIM + 1] + jnp.exp2(sink - m)
    return oa[:, :HEAD_DIM] / denom


def _attn_kernel(sink_ref, q_ref, k_ref, v_ref, kc_ref, vc_ref, o_ref, *, n_blocks):
    kvh = pl.program_id(1)
    i = pl.program_id(2)
    rows = ATTN_BLOCK * N_GROUPS
    qi = lax.broadcasted_iota(jnp.int32, (rows, ATTN_BLOCK), 0) & (ATTN_BLOCK - 1)
    kj = lax.broadcasted_iota(jnp.int32, (rows, ATTN_BLOCK), 1)
    tri_prev = kj >= qi
    tri_next = kj <= qi
    sink = _sink_col(sink_ref, kvh, ATTN_BLOCK)
    kc = kc_ref[...]
    vc = vc_ref[...]

    def body(qb, carry):
        n = i * (TQ // ATTN_BLOCK) + qb
        r0 = pl.multiple_of(qb * ATTN_BLOCK, ATTN_BLOCK)
        p0 = pl.multiple_of(jnp.maximum(n - 1, 0) * ATTN_BLOCK, ATTN_BLOCK)
        c0 = pl.multiple_of(n * ATTN_BLOCK, ATTN_BLOCK)
        n0 = pl.multiple_of(jnp.minimum(n + 1, n_blocks - 1) * ATTN_BLOCK, ATTN_BLOCK)
        qs = _stack_heads(q_ref[pl.ds(r0, ATTN_BLOCK), :])
        kall = jnp.concatenate([k_ref[pl.ds(p0, ATTN_BLOCK), :], k_ref[pl.ds(c0, ATTN_BLOCK), :],
                                k_ref[pl.ds(n0, ATTN_BLOCK), :], kc], axis=0)
        vall = jnp.concatenate([v_ref[pl.ds(p0, ATTN_BLOCK), :], v_ref[pl.ds(c0, ATTN_BLOCK), :],
                                v_ref[pl.ds(n0, ATTN_BLOCK), :], vc], axis=0)
        s = lax.dot_general(qs, kall, (((1,), (1,)), ((), ())), preferred_element_type=F32)
        pen_prev = jnp.where(n > 0, 0.0, NEG_INF)
        pen_next = jnp.where(n < n_blocks - 1, 0.0, NEG_INF)
        sp = jnp.where(tri_prev, s[:, :ATTN_BLOCK] + pen_prev, NEG_INF)
        sn = jnp.where(tri_next, s[:, 2 * ATTN_BLOCK:3 * ATTN_BLOCK] + pen_next, NEG_INF)
        s = jnp.concatenate([sp, s[:, ATTN_BLOCK:2 * ATTN_BLOCK], sn, s[:, 3 * ATTN_BLOCK:]], axis=1)
        o = _softmax_pv(s, sink, vall)
        for g in range(N_GROUPS):
            o_ref[pl.ds(r0, ATTN_BLOCK), g * HEAD_DIM:(g + 1) * HEAD_DIM] = (
                o[g * ATTN_BLOCK:(g + 1) * ATTN_BLOCK].astype(BF16))
        return carry

    lax.fori_loop(0, TQ // ATTN_BLOCK, body, 0, unroll=True)


def _latent_attention(sink, q, kv, *, n_batch, seq, ctx_len):
    gw = N_GROUPS * HEAD_DIM
    tiles_per_batch = seq // TQ
    ctx_blk0 = n_batch * seq // ctx_len
    grid_spec = pltpu.PrefetchScalarGridSpec(
        num_scalar_prefetch=1,
        grid=(n_batch, N_KV_HEADS, tiles_per_batch),
        in_specs=[
            pl.BlockSpec((TQ, gw), lambda b, k, i, s: (b * tiles_per_batch + i, k)),
            pl.BlockSpec((seq, HEAD_DIM), lambda b, k, i, s: (b, k)),
            pl.BlockSpec((seq, HEAD_DIM), lambda b, k, i, s: (b, N_KV_HEADS + k)),
            pl.BlockSpec((ctx_len, HEAD_DIM), lambda b, k, i, s: (ctx_blk0 + b, k)),
            pl.BlockSpec((ctx_len, HEAD_DIM), lambda b, k, i, s: (ctx_blk0 + b, N_KV_HEADS + k)),
        ],
        out_specs=pl.BlockSpec((TQ, gw), lambda b, k, i, s: (b * tiles_per_batch + i, k)),
    )
    return pl.pallas_call(
        functools.partial(_attn_kernel, n_blocks=seq // ATTN_BLOCK),
        grid_spec=grid_spec,
        out_shape=jax.ShapeDtypeStruct((n_batch * seq, Q_COLS), BF16),
        compiler_params=_cparams(3),
        name="latent_attention",
    )(sink, q, kv, kv, kv, kv)


def _ctx_attn_kernel(sink_ref, q_ref, kc_ref, vc_ref, o_ref):
    kvh = pl.program_id(1)
    rows = q_ref.shape[0]
    qs = _stack_heads(q_ref[...])
    s = lax.dot_general(qs, kc_ref[...], (((1,), (1,)), ((), ())), preferred_element_type=F32)
    o = _softmax_pv(s, _sink_col(sink_ref, kvh, rows), vc_ref[...])
    for g in range(N_GROUPS):
        o_ref[:, g * HEAD_DIM:(g + 1) * HEAD_DIM] = o[g * rows:(g + 1) * rows].astype(BF16)


def _context_attention(sink, q, kv, *, n_batch, seq, ctx_len):
    gw = N_GROUPS * HEAD_DIM
    ctx_blk0 = n_batch * seq // ctx_len
    grid_spec = pltpu.PrefetchScalarGridSpec(
        num_scalar_prefetch=1,
        grid=(n_batch, N_KV_HEADS),
        in_specs=[
            pl.BlockSpec((ctx_len, gw), lambda b, k, s: (ctx_blk0 + b, k)),
            pl.BlockSpec((ctx_len, HEAD_DIM), lambda b, k, s: (ctx_blk0 + b, k)),
            pl.BlockSpec((ctx_len, HEAD_DIM), lambda b, k, s: (ctx_blk0 + b, N_KV_HEADS + k)),
        ],
        out_specs=pl.BlockSpec((ctx_len, gw), lambda b, k, s: (b, k)),
    )
    return pl.pallas_call(
        _ctx_attn_kernel,
        grid_spec=grid_spec,
        out_shape=jax.ShapeDtypeStruct((n_batch * ctx_len, Q_COLS), BF16),
        compiler_params=_cparams(2),
        name="context_attention",
    )(sink, q, kv, kv)


def _softplus(z):
    return jnp.maximum(z, 0.0) + jnp.log1p(jnp.exp(-jnp.abs(z)))


def _scan8(a, b, row, reverse):
    for s in (1, 2, 4):
        if reverse:
            keep = row < SUBLANES - s
            shift = SUBLANES - s
        else:
            keep = row >= s
            shift = s
        a_sh = jnp.where(keep, pltpu.roll(a, shift, 0), 1.0)
        b_sh = jnp.where(keep, pltpu.roll(b, shift, 0), 0.0)
        b = a * b_sh + b
        a = a * a_sh
    return a, b


def _rglru_kernel(xf_ref, xfp_ref, xfn_ref, xb_ref, xbp_ref, xbn_ref, cw_ref, cb_ref, wcat_ref,
                  gb_ref, lam_ref, unperm_ref, hf_ref, hb_ref, a_s, b_s, h_bf, carry):
    j = pl.program_id(1)
    last_j = pl.num_programs(1) - 1

    @pl.when(j == 0)
    def _():
        carry[...] = jnp.zeros_like(carry)

    seg_start = (j <= 1, (j == 0) | (j == last_j))
    seg_end = ((j == 0) | (j == last_j), j <= 1)
    mains = (xf_ref, xb_ref)
    prevs = (xfp_ref, xbp_ref)
    nexts = (xfn_ref, xbn_ref)
    sub = lax.broadcasted_iota(jnp.int32, (SUBLANES, D_RNN), 0)
    S8 = SUBLANES

    for d in range(2):
        x0 = mains[d][...]
        halo = prevs[d][...]
        t_m1 = jnp.where(seg_start[d], 0.0, halo[2 * S8 - 1:2 * S8, :])
        t_m2 = jnp.where(seg_start[d], 0.0, halo[S8 - 1:S8, :])
        t_p1 = jnp.where(seg_end[d], 0.0, nexts[d][0:1, :])
        g_m1 = jnp.where(sub == 0, t_m1, pltpu.roll(x0[TT - S8:, :], 1, 0))
        g_m2 = jnp.where(sub == 0, t_m2, pltpu.roll(x0[TT - 2 * S8:TT - S8, :], 1, 0))
        g_p1 = jnp.where(sub == S8 - 1, t_p1, pltpu.roll(x0[:S8, :], S8 - 1, 0))
        xc = cb_ref[...] + jnp.concatenate([g_m2, g_m1, x0[:TT - 2 * S8, :]], axis=0) * cw_ref[0:1, :]
        xc = xc + jnp.concatenate([g_m1, x0[:TT - S8, :]], axis=0) * cw_ref[1:2, :]
        xc = xc + x0 * cw_ref[2:3, :]
        xc = xc + jnp.concatenate([x0[S8:, :], g_p1], axis=0) * cw_ref[3:4, :]
        c_d = (-LRU_C * LOG2E) * _softplus(-lam_ref[d:d + 1, :])
        for blk in range(N_RNN_BLOCKS):
            sl = slice(blk * RNN_BLOCK, (blk + 1) * RNN_BLOCK)
            xcb = xc[:, sl]
            z = jnp.dot(xcb.astype(BF16), wcat_ref[d, blk], preferred_element_type=F32)
            r = _sigmoid(z[:, :RNN_BLOCK] + gb_ref[d, 0:1, sl])
            ig = _sigmoid(z[:, RNN_BLOCK:] + gb_ref[d, 1:2, sl])
            a = jnp.exp2(r * c_d[:, sl])
            v = 1.0 - a * a
            root = jnp.where(v > 0.0, v * lax.rsqrt(v), 0.0)
            a_s[d, :, sl] = a
            b_s[d, :, sl] = root * (ig * xcb)

    def local(g, c):
        hf, pf, hb, pb = c
        rf = pl.multiple_of(g * S8, S8)
        rb = pl.multiple_of((SUBSEQ - 1 - g) * S8, S8)
        af = a_s[0, pl.ds(rf, S8), :]
        ab = a_s[1, pl.ds(rb, S8), :]
        hf = af * hf + b_s[0, pl.ds(rf, S8), :]
        hb = ab * hb + b_s[1, pl.ds(rb, S8), :]
        pf = af * pf
        pb = ab * pb
        b_s[0, pl.ds(rf, S8), :] = hf
        b_s[1, pl.ds(rb, S8), :] = hb
        a_s[0, pl.ds(rf, S8), :] = pf
        a_s[1, pl.ds(rb, S8), :] = pb
        return hf, pf, hb, pb

    zero = jnp.zeros((S8, D_RNN), F32)
    one = jnp.ones((S8, D_RNN), F32)
    hf, pf, hb, pb = lax.fori_loop(0, SUBSEQ, local, (zero, one, zero, one), unroll=2)

    af, bf = _scan8(pf, hf, sub, False)
    endf = af * carry[0] + bf
    h_in = [jnp.where(sub == 0, carry[0], pltpu.roll(endf, 1, 0))]
    carry[0] = jnp.broadcast_to(endf[S8 - 1:S8, :], (S8, D_RNN))
    ab, bb = _scan8(pb, hb, sub, True)
    endb = ab * carry[1] + bb
    h_in.append(jnp.where(sub == S8 - 1, carry[1], pltpu.roll(endb, S8 - 1, 0)))
    carry[1] = jnp.broadcast_to(endb[0:1, :], (S8, D_RNN))

    outs = (hf_ref, hb_ref)
    for d in range(2):
        h_in2 = jnp.concatenate([h_in[d], h_in[d]], axis=0)

        def fix(k, c, d=d, h_in2=h_in2):
            rows = pl.ds(pl.multiple_of(k * 2 * S8, 2 * S8), 2 * S8)
            h_bf[d, rows, :] = (b_s[d, rows, :] + a_s[d, rows, :] * h_in2).astype(BF16)
            return c

        lax.fori_loop(0, SUBSEQ // 2, fix, 0, unroll=2)
        outs[d][...] = jnp.dot(unperm_ref[...], h_bf[d], preferred_element_type=F32).astype(BF16)


def _time_permutation(n_rows):
    p = jnp.arange(n_rows)
    src = (p // TT) * TT + (p % SUBLANES) * SUBSEQ + (p % TT) // SUBLANES
    return (src[:, None] == jnp.arange(n_rows)[None, :]).astype(BF16)


def _rglru(x, conv_w, conv_b, wcat, gate_b, lam, *, n_batch, seq, ctx_len):
    T = x.shape[0]
    assert ctx_len == TT and seq % TT == 0
    tps = seq // TT
    n_lat_t = n_batch * tps
    halo_rows = 2 * SUBLANES
    per_halo = TT // halo_rows
    per8 = TT // SUBLANES
    last8 = T // SUBLANES - 1
    unperm = _time_permutation(TT).T

    def ftile(b, j):
        return jnp.where(j == 0, n_lat_t + b, b * tps + j - 1)

    def btile(b, j):
        return jnp.where(j == 0, n_lat_t + b, b * tps + tps - j)

    def main(tile):
        return pl.BlockSpec((TT, D_RNN), lambda b, j: (tile(b, j), 0))

    def prev(tile):
        return pl.BlockSpec((halo_rows, D_RNN), lambda b, j: (jnp.maximum(tile(b, j) * per_halo - 1, 0), 0))

    def nxt(tile):
        return pl.BlockSpec((SUBLANES, D_RNN),
                            lambda b, j: (jnp.minimum((tile(b, j) + 1) * per8, last8), 0))

    def const(shape):
        return pl.BlockSpec(shape, lambda b, j: (0,) * len(shape))

    return pl.pallas_call(
        _rglru_kernel,
        grid=(n_batch, tps + 1),
        in_specs=[main(ftile), prev(ftile), nxt(ftile), main(btile), prev(btile), nxt(btile),
                  const(conv_w.shape), const(conv_b.shape), const(wcat.shape), const(gate_b.shape),
                  const(lam.shape), const(unperm.shape)],
        out_specs=(main(ftile), main(btile)),
        out_shape=(jax.ShapeDtypeStruct((T, D_RNN), BF16), jax.ShapeDtypeStruct((T, D_RNN), BF16)),
        scratch_shapes=[pltpu.VMEM((2, TT, D_RNN), F32),
                        pltpu.VMEM((2, TT, D_RNN), F32),
                        pltpu.VMEM((2, TT, D_RNN), BF16),
                        pltpu.VMEM((2, SUBLANES, D_RNN), F32)],
        compiler_params=_cparams(2),
        name="rglru",
    )(x, x, x, x, x, x, conv_w, conv_b, wcat, gate_b, lam, unperm)


def _merge_kernel(h_lat_ref, h_ctx_ref, attn_ref, attn_ctx_ref, hf_ref, hb_ref, gy_ref, sga_ref, sgr_ref,
                  g1_ref, woa_ref, wol_ref, wout_ref, o_ref):
    rec = ((hf_ref[...].astype(F32) + hb_ref[...].astype(F32)) * gy_ref[...].astype(F32)).astype(BF16)
    is_ctx_tile = pl.program_id(0) == pl.num_programs(0) - 1
    attn = jnp.where(is_ctx_tile, attn_ctx_ref[...], attn_ref[...])
    ta = jnp.dot(attn, woa_ref[...], preferred_element_type=F32)
    tl = jnp.dot(rec, wol_ref[...], preferred_element_type=F32)
    m = sga_ref[...].astype(F32) * ta + sgr_ref[...].astype(F32) * tl
    y = jnp.dot(m.astype(BF16), wout_ref[...], preferred_element_type=F32)
    o_ref[...] = _stream_tile(h_lat_ref, h_ctx_ref) + g1_ref[...] * y


def _merge(stream, attn, attn_ctx, hf, hb, gy, sga, sgr, mod3, woa, wol, wout, layer, *,
           tiles_per_batch, n_batch):
    n_lat_tiles = attn.shape[0] // TM
    T = (n_lat_tiles + 1) * TM
    assert attn_ctx.shape[0] == TM and hf.shape[0] == T

    def grp(i):
        return jnp.minimum(i // tiles_per_batch, n_batch)

    tok = pl.BlockSpec((TM, D_MODEL), lambda i: (i, 0))
    wsp = pl.BlockSpec((None, D_MODEL, D_MODEL), lambda i: (layer, 0, 0))
    return pl.pallas_call(
        _merge_kernel,
        grid=(T // TM,),
        in_specs=_stream_specs(stream, n_lat_tiles) + [
                  pl.BlockSpec((TM, Q_COLS), lambda i: (jnp.minimum(i, n_lat_tiles - 1), 0)),
                  pl.BlockSpec((TM, Q_COLS), lambda i: (0, 0)),
                  tok, tok, tok, tok, tok,
                  pl.BlockSpec((None, 1, D_MODEL), lambda i: (grp(i), 0, 2)),
                  wsp, wsp, wsp],
        out_specs=tok,
        out_shape=jax.ShapeDtypeStruct((T, D_MODEL), F32),
        compiler_params=_cparams(1),
        name="merge",
    )(stream[0], stream[1], attn, attn_ctx, hf, hb, gy, sga, sgr, mod3, woa, wol, wout)


def _swiglu_partial(n, wg, wu, wd):
    gt = jnp.dot(n, wg, preferred_element_type=F32)
    ut = jnp.dot(n, wu, preferred_element_type=F32)
    act = (gt * _sigmoid(gt) * ut).astype(BF16)
    return jnp.dot(act, wd, preferred_element_type=F32)


def _ffn_kernel(h_ref, g_ref, sh_ref, sc_ref, g2_ref, wg_ref, wu_ref, wd_ref, o_ref, n_scr, acc):
    f = pl.program_id(1)

    @pl.when(f == 0)
    def _():
        n_scr[...] = _norm_mod(h_ref[...], g_ref[...], sh_ref[...], sc_ref[...]).astype(BF16)
        acc[...] = jnp.zeros_like(acc)

    acc[...] += _swiglu_partial(n_scr[...], wg_ref[...], wu_ref[...], wd_ref[...])

    @pl.when(f == pl.num_programs(1) - 1)
    def _():
        o_ref[...] = h_ref[...] + g2_ref[...] * acc[...]


def _ffn(h, g, mod3, wg, wu, wd, layer, ff_layer, *, tiles_per_batch, n_batch):
    T = h.shape[0]
    n_f = 2
    tf = D_FF // n_f
    assert tf % LANES == 0

    def grp(i):
        return jnp.minimum(i // tiles_per_batch, n_batch)

    def modspec(k):
        return pl.BlockSpec((None, 1, D_MODEL), lambda i, f: (grp(i), 0, k))

    tok = pl.BlockSpec((TM, D_MODEL), lambda i, f: (i, 0))
    return pl.pallas_call(
        _ffn_kernel,
        grid=(T // TM, n_f),
        in_specs=[tok, pl.BlockSpec((None, 1, D_MODEL), lambda i, f: (layer, 0, 0)),
                  modspec(3), modspec(4), modspec(5),
                  pl.BlockSpec((None, D_MODEL, tf), lambda i, f: (ff_layer, 0, f)),
                  pl.BlockSpec((None, D_MODEL, tf), lambda i, f: (ff_layer, 0, f)),
                  pl.BlockSpec((None, tf, D_MODEL), lambda i, f: (ff_layer, f, 0))],
        out_specs=tok,
        out_shape=jax.ShapeDtypeStruct((T, D_MODEL), F32),
        scratch_shapes=[pltpu.VMEM((TM, D_MODEL), BF16), pltpu.VMEM((TM, D_MODEL), F32)],
        compiler_params=_cparams(2),
        name="dense_ffn",
    )(h, g, mod3, mod3, mod3, wg, wu, wd)


ROUTE_E1, ROUTE_E2, ROUTE_W1, ROUTE_W2, ROUTE_R1, ROUTE_R2 = range(6)


def _dot_split(a, b):
    a_hi = a.astype(BF16)
    a_lo = (a - a_hi.astype(F32)).astype(BF16)
    b_hi = b.astype(BF16)
    b_lo = (b - b_hi.astype(F32)).astype(BF16)

    def mm(x, y):
        return jnp.dot(x, y, preferred_element_type=F32)

    return mm(a_hi, b_hi) + (mm(a_hi, b_lo) + mm(a_lo, b_hi))


def _router_kernel(h_ref, g_ref, sh_ref, sc_ref, wr_ref, route_ref, cnt_ref, run):
    @pl.when(pl.program_id(0) == 0)
    def _():
        run[...] = jnp.zeros_like(run)

    n = _norm_mod(h_ref[...], g_ref[...], sh_ref[...], sc_ref[...])
    logits = _dot_split(n, wr_ref[...])
    lane = lax.broadcasted_iota(jnp.int32, logits.shape, 1)
    logits = jnp.where(lane < N_EXPERTS, logits, -jnp.inf)
    m1 = jnp.max(logits, axis=1, keepdims=True)
    i1 = jnp.min(jnp.where(logits == m1, lane, LANES), axis=1, keepdims=True)
    rest = jnp.where(lane == i1, -jnp.inf, logits)
    m2 = jnp.max(rest, axis=1, keepdims=True)
    i2 = jnp.min(jnp.where(rest == m2, lane, LANES), axis=1, keepdims=True)
    e2 = jnp.exp(m2 - m1)
    w1 = 1.0 / (1.0 + e2)
    w2 = e2 / (1.0 + e2)

    hit1 = lane == i1
    hit2 = lane == i2
    onehot = jnp.where(hit1 | hit2, 1.0, 0.0)
    r_i = lax.broadcasted_iota(jnp.int32, (TM, TM), 0)
    c_i = lax.broadcasted_iota(jnp.int32, (TM, TM), 1)
    lower = jnp.where(c_i < r_i, 1.0, 0.0).astype(BF16)
    prefix = jnp.dot(lower, onehot.astype(BF16), preferred_element_type=F32) + run[0:1, :]
    rank1 = jnp.sum(jnp.where(hit1, prefix, 0.0), axis=1, keepdims=True)
    rank2 = jnp.sum(jnp.where(hit2, prefix, 0.0), axis=1, keepdims=True)
    run[...] = run[...] + jnp.sum(onehot, axis=0, keepdims=True)
    cnt_ref[...] = run[...]

    rec = jnp.zeros(logits.shape, F32)
    for k, v in ((ROUTE_E1, i1.astype(F32)), (ROUTE_E2, i2.astype(F32)), (ROUTE_W1, w1), (ROUTE_W2, w2),
                 (ROUTE_R1, rank1), (ROUTE_R2, rank2)):
        rec = jnp.where(lane == k, v, rec)
    route_ref[...] = rec


def _router(h, g2, mod3, w_r, layer, *, tiles_per_batch, n_batch):
    T = h.shape[0]

    def grp(i):
        return jnp.minimum(i // tiles_per_batch, n_batch)

    def modspec(k):
        return pl.BlockSpec((None, 1, D_MODEL), lambda i: (grp(i), 0, k))

    return pl.pallas_call(
        _router_kernel,
        grid=(T // TM,),
        in_specs=[pl.BlockSpec((TM, D_MODEL), lambda i: (i, 0)),
                  pl.BlockSpec((None, 1, D_MODEL), lambda i: (layer, 0, 0)),
                  modspec(3), modspec(4),
                  pl.BlockSpec((D_MODEL, LANES), lambda i: (0, 0))],
        out_specs=(pl.BlockSpec((TM, LANES), lambda i: (i, 0)),
                   pl.BlockSpec((SUBLANES, LANES), lambda i: (0, 0))),
        out_shape=(jax.ShapeDtypeStruct((T, LANES), F32), jax.ShapeDtypeStruct((SUBLANES, LANES), F32)),
        scratch_shapes=[pltpu.VMEM((SUBLANES, LANES), F32)],
        compiler_params=_cparams(1),
        name="moe_router",
    )(h, g2, mod3, mod3, w_r)


def _row_copy(src_ref, src_row, dst_ref, dst_row, sem):
    return pltpu.make_async_copy(src_ref.at[pl.ds(src_row, 1)], dst_ref.at[pl.ds(dst_row, 1)], sem)


def _dispatch_kernel(dest_ref, h_ref, g_ref, sh_ref, sc_ref, xs_in_ref, xs_ref, n_scr, sem):
    del xs_in_ref
    i = pl.program_id(0)
    base = i * (2 * TM)
    cur = i % 2
    n_scr[cur] = _norm_mod(h_ref[...], g_ref[...], sh_ref[...], sc_ref[...])

    def start(r, carry):
        for s in range(2):
            _row_copy(n_scr.at[cur], r, xs_ref, dest_ref[base + 2 * r + s], sem.at[cur]).start(priority=s)
        return carry

    lax.fori_loop(0, TM, start, 0, unroll=ROW_DMA_UNROLL)

    def wait_all(buf):
        for s in range(2):
            pltpu.make_async_copy(n_scr.at[buf], xs_ref.at[pl.ds(0, TM)], sem.at[buf]).wait()

    @pl.when(i > 0)
    def _():
        wait_all(1 - cur)

    @pl.when(i == pl.num_programs(0) - 1)
    def _():
        wait_all(cur)


def _dispatch(dest, h, g2, mod3, xs_zero, layer, *, tiles_per_batch, n_batch):
    T = h.shape[0]

    def grp(i):
        return jnp.minimum(i // tiles_per_batch, n_batch)

    def modspec(k):
        return pl.BlockSpec((None, 1, D_MODEL), lambda i, d: (grp(i), 0, k))

    grid_spec = pltpu.PrefetchScalarGridSpec(
        num_scalar_prefetch=1,
        grid=(T // TM,),
        in_specs=[pl.BlockSpec((TM, D_MODEL), lambda i, d: (i, 0)),
                  pl.BlockSpec((None, 1, D_MODEL), lambda i, d: (layer, 0, 0)),
                  modspec(3), modspec(4),
                  pl.BlockSpec(memory_space=pl.ANY)],
        out_specs=pl.BlockSpec(memory_space=pl.ANY),
        scratch_shapes=[pltpu.VMEM((2, TM, D_MODEL), F32), pltpu.SemaphoreType.DMA((2,))],
    )
    return pl.pallas_call(
        _dispatch_kernel,
        grid_spec=grid_spec,
        out_shape=jax.ShapeDtypeStruct(xs_zero.shape, xs_zero.dtype),
        input_output_aliases={5: 0},
        compiler_params=_cparams(1),
        name="moe_dispatch",
    )(dest, h, g2, mod3, mod3, xs_zero)


def _expert_kernel(te_ref, nu_ref, rows_ref, x_ref, wg_ref, wu_ref, wd_ref, o_ref, xb, acc):
    del te_ref, nu_ref
    k = pl.program_id(0)
    f = pl.program_id(1)

    @pl.when(f == 0)
    def _():
        xb[...] = x_ref[...].astype(BF16)
        acc[...] = jnp.zeros_like(acc)

    wg = wg_ref[...].astype(BF16)
    wu = wu_ref[...].astype(BF16)
    wd = wd_ref[...].astype(BF16)
    acc[:TG_SUB, :] += _swiglu_partial(xb[:TG_SUB, :], wg, wu, wd)
    for sub in range(1, TG // TG_SUB):
        rows = slice(sub * TG_SUB, (sub + 1) * TG_SUB)

        @pl.when(rows_ref[k] > sub * TG_SUB)
        def _():
            acc[rows, :] += _swiglu_partial(xb[rows, :], wg, wu, wd)

    @pl.when(f == pl.num_programs(1) - 1)
    def _():
        o_ref[...] = acc[...]


def _experts(tile_expert, n_used, tile_rows, xs, wg, wu, wd, layer):
    P = xs.shape[0]
    n_f = MOE_F_CHUNKS
    tf = D_FF_EXPERT // n_f
    assert tf % LANES == 0

    def fsel(k, f, nu):
        return jnp.where(k < nu[0], f, n_f - 1)

    grid_spec = pltpu.PrefetchScalarGridSpec(
        num_scalar_prefetch=3,
        grid=(P // TG, n_f),
        in_specs=[pl.BlockSpec((TG, D_MODEL), lambda k, f, te, nu, tr: (k, 0)),
                  pl.BlockSpec((None, None, D_MODEL, tf), lambda k, f, te, nu, tr: (layer, te[k], 0, fsel(k, f, nu))),
                  pl.BlockSpec((None, None, D_MODEL, tf), lambda k, f, te, nu, tr: (layer, te[k], 0, fsel(k, f, nu))),
                  pl.BlockSpec((None, None, tf, D_MODEL), lambda k, f, te, nu, tr: (layer, te[k], fsel(k, f, nu), 0))],
        out_specs=pl.BlockSpec((TG, D_MODEL), lambda k, f, te, nu, tr: (k, 0)),
        scratch_shapes=[pltpu.VMEM((TG, D_MODEL), BF16), pltpu.VMEM((TG, D_MODEL), F32)],
    )
    return pl.pallas_call(
        _expert_kernel,
        grid_spec=grid_spec,
        out_shape=jax.ShapeDtypeStruct((P, D_MODEL), F32),
        compiler_params=_cparams(2),
        name="moe_experts",
    )(tile_expert, n_used, tile_rows, xs, wg, wu, wd)


def _combine_kernel(dest_ref, h_ref, g2_ref, route_ref, fg_ref, y_ref, o_ref, ybuf, sem, *, final):
    i = pl.program_id(0)
    n_steps = pl.num_programs(0)

    def gather(tile, buf):
        base = tile * (2 * TM)

        def start(r, carry):
            for s in range(2):
                _row_copy(y_ref, dest_ref[base + 2 * r + s], ybuf.at[buf], s * TM + r,
                          sem.at[buf]).start(priority=s)
            return carry

        lax.fori_loop(0, TM, start, 0, unroll=ROW_DMA_UNROLL)

    @pl.when(i == 0)
    def _():
        gather(0, 0)

    @pl.when(i + 1 < n_steps)
    def _():
        gather(i + 1, (i + 1) % 2)

    cur = i % 2
    for s in range(2):
        pltpu.make_async_copy(y_ref.at[pl.ds(0, TM)], ybuf.at[cur, pl.ds(s * TM, TM)], sem.at[cur]).wait()
    route = route_ref[...]
    f = (route[:, ROUTE_W1:ROUTE_W1 + 1] * ybuf[cur, 0:TM, :]
         + route[:, ROUTE_W2:ROUTE_W2 + 1] * ybuf[cur, TM:, :])
    out = h_ref[...] + g2_ref[...] * f
    if final:
        ms = jnp.mean(out * out, axis=-1, keepdims=True)
        out = out * lax.rsqrt(ms + EPS) * fg_ref[...]
    o_ref[...] = out


def _combine(dest, h, mod3, route, final_g, y, *, final, tiles_per_batch, n_batch):
    n_tok_tiles = h.shape[0] // TM - (1 if final else 0)

    def grp(i):
        return jnp.minimum(i // tiles_per_batch, n_batch)

    grid_spec = pltpu.PrefetchScalarGridSpec(
        num_scalar_prefetch=1,
        grid=(n_tok_tiles,),
        in_specs=[pl.BlockSpec((TM, D_MODEL), lambda i, d: (i, 0)),
                  pl.BlockSpec((None, 1, D_MODEL), lambda i, d: (grp(i), 0, 5)),
                  pl.BlockSpec((TM, LANES), lambda i, d: (i, 0)),
                  pl.BlockSpec((1, D_MODEL), lambda i, d: (0, 0)),
                  pl.BlockSpec(memory_space=pl.ANY)],
        out_specs=pl.BlockSpec((TM, D_MODEL), lambda i, d: (i, 0)),
        scratch_shapes=[pltpu.VMEM((2, 2 * TM, D_MODEL), F32), pltpu.SemaphoreType.DMA((2,))],
    )
    return pl.pallas_call(
        functools.partial(_combine_kernel, final=final),
        grid_spec=grid_spec,
        out_shape=jax.ShapeDtypeStruct((n_tok_tiles * TM, D_MODEL), F32),
        compiler_params=_cparams(1),
        name="moe_combine",
    )(dest, h, mod3, route, final_g, y)


def _moe_layer(h, g2, mod3, w_r, wg, wu, wd, final_g, layer, moe_layer, *, final, **geo):
    T = h.shape[0]
    n_tiles = (2 * T) // TG + N_EXPERTS
    route, cnt = _router(h, g2, mod3, w_r, layer, **geo)

    counts = cnt[0, :N_EXPERTS].astype(jnp.int32)
    padded = ((counts + TG - 1) // TG) * TG
    ends = jnp.cumsum(padded)
    offs = ends - padded
    e12 = route[:, ROUTE_E1:ROUTE_E2 + 1].astype(jnp.int32)
    r12 = route[:, ROUTE_R1:ROUTE_R2 + 1].astype(jnp.int32)
    onehot = e12[:, :, None] == jnp.arange(N_EXPERTS)[None, None, :]
    dest = (jnp.sum(jnp.where(onehot, offs[None, None, :], 0), axis=-1) + r12).reshape(2 * T)
    n_used = (ends[-1] // TG).reshape(1)
    tiles = jnp.arange(n_tiles)
    te_raw = jnp.sum(tiles[:, None] >= (ends // TG)[None, :], axis=1)
    tile_expert = jnp.minimum(te_raw, N_EXPERTS - 1).astype(jnp.int32)
    sel = tile_expert[:, None] == jnp.arange(N_EXPERTS)[None, :]
    cnt_k = jnp.sum(jnp.where(sel, counts[None, :], 0), axis=1)
    off_k = jnp.sum(jnp.where(sel, offs[None, :], 0), axis=1)
    tile_rows = jnp.where(te_raw < N_EXPERTS, jnp.clip(cnt_k - (tiles * TG - off_k), 0, TG), 0)

    xs = _dispatch(dest, h, g2, mod3, jnp.zeros((n_tiles * TG, D_MODEL), F32), layer, **geo)
    y = _experts(tile_expert, n_used, tile_rows.astype(jnp.int32), xs, wg, wu, wd, moe_layer)
    return _combine(dest, h, mod3, route, final_g, y, final=final, **geo)


def _rope_tables(seq):
    t = jnp.arange(seq)
    inv = ROPE_THETA ** (-jnp.arange(ROPE_FREQS, dtype=F32) / ROPE_FREQS)
    ang_r = (t // GRID_W).astype(F32)[:, None] * inv
    ang_c = (t % GRID_W).astype(F32)[:, None] * inv
    cos = jnp.concatenate([jnp.cos(ang_r)] * 2 + [jnp.cos(ang_c)] * 2, axis=1)
    sin = jnp.concatenate([-jnp.sin(ang_r), jnp.sin(ang_r), -jnp.sin(ang_c), jnp.sin(ang_c)], axis=1)
    cos = jnp.concatenate([cos, jnp.ones((TM, HEAD_DIM), F32)], axis=0)
    sin = jnp.concatenate([sin, jnp.zeros((TM, HEAD_DIM), F32)], axis=0)
    return cos, sin


def kernel(x, c, ctx, c_ctx, w_mod, b_mod, norm1_g, norm2_g, w_in, attn_sink, conv_w, conv_b, gate_a_w, gate_a_b, gate_x_w, gate_x_b, lru_lambda, w_o_attn, w_o_lru, w_out, ff_w_gate, ff_w_up, ff_w_down, router_w, exp_w_gate, exp_w_up, exp_w_down, final_g):
    n_batch, seq, _ = x.shape
    ctx_len = ctx.shape[1]
    assert n_batch * ctx_len == TM and seq % TM == 0 and n_batch + 1 <= MOD_ROWS
    n_lat = n_batch * seq
    tiles_per_batch = seq // TM
    geo = dict(tiles_per_batch=tiles_per_batch, n_batch=n_batch)
    shp = dict(n_batch=n_batch, seq=seq, ctx_len=ctx_len)

    cpad = jnp.zeros((MOD_ROWS, D_MODEL), F32).at[:n_batch].set(c).at[n_batch].set(c_ctx)
    mod = _modulation(cpad, w_mod, b_mod)
    cos_t, sin_t = _rope_tables(seq)
    perm = _time_permutation(TM)
    stream = (x.reshape(n_lat, D_MODEL), ctx.reshape(n_batch * ctx_len, D_MODEL), 0)

    g1 = norm1_g.reshape(DEPTH, 1, D_MODEL)
    g2 = norm2_g.reshape(DEPTH, 1, D_MODEL)
    w_in_b = w_in.astype(BF16)
    woa_b, wol_b, wout_b = w_o_attn.astype(BF16), w_o_lru.astype(BF16), w_out.astype(BF16)
    ffg_b, ffu_b, ffd_b = ff_w_gate.astype(BF16), ff_w_up.astype(BF16), ff_w_down.astype(BF16)

    for l in range(DEPTH):
        mod3 = mod[l].reshape(MOD_ROWS, 1, 6 * D_MODEL)
        if l > 0:
            stream = (h, h, n_lat // TM)
        q, kv, xr, gy, sga, sgr = _in_proj(stream, g1, mod3, w_in_b, l, cos_t, sin_t, perm,
                                           n_lat_tiles=n_lat // TM, **geo)
        attn = _latent_attention(attn_sink[l], q, kv, **shp)
        attn_ctx = _context_attention(attn_sink[l], q, kv, **shp)
        wcat = jnp.concatenate([gate_a_w[l], gate_x_w[l]], axis=-1).astype(BF16)
        gate_b = jnp.stack([gate_a_b[l], gate_x_b[l]], axis=1)
        hf, hb = _rglru(xr, conv_w[l], conv_b[l].reshape(1, D_RNN), wcat, gate_b, lru_lambda[l], **shp)
        h = _merge(stream, attn, attn_ctx, hf, hb, gy, sga, sgr, mod3, woa_b, wol_b, wout_b, l, **geo)
        i = l // 2
        if l % 2 == 0:
            h = _ffn(h, g2, mod3, ffg_b, ffu_b, ffd_b, l, i, **geo)
        else:
            w_r = jnp.zeros((D_MODEL, LANES), F32).at[:, :N_EXPERTS].set(router_w[i])
            h = _moe_layer(h, g2, mod3, w_r, exp_w_gate, exp_w_up, exp_w_down,
                           final_g.reshape(1, D_MODEL), l, i, final=(l == DEPTH - 1), **geo)

    assert DEPTH % 2 == 0 and h.shape[0] == n_lat
    return h.reshape(n_batch, seq, D_MODEL)
```

```python
import functools

import jax
import jax.numpy as jnp
from jax import lax
from jax.experimental import pallas as pl
from jax.experimental.pallas import tpu as pltpu

F32 = jnp.float32
BF16 = jnp.bfloat16

D_MODEL = 1024
DEPTH = 4
GRID_W = 64
N_HEADS = 8
N_KV_HEADS = 2
HEAD_DIM = 128
N_GROUPS = N_HEADS // N_KV_HEADS
ATTN_BLOCK = 128
ROPE_THETA = 10000.0
ROPE_FREQS = HEAD_DIM // 4
D_RNN = 1024
N_RNN_BLOCKS = 8
RNN_BLOCK = D_RNN // N_RNN_BLOCKS
LRU_C = 8.0
D_FF = 2816
N_EXPERTS = 8
D_FF_EXPERT = 3584
EPS = 1e-6
NEG_INF = -1e30
Q_COLS = N_HEADS * HEAD_DIM
KV_COLS = N_KV_HEADS * HEAD_DIM
IN_COLS = Q_COLS + 2 * KV_COLS + 2 * D_RNN + 2 * D_MODEL
LOG2E = 1.4426950408889634
ATTN_SCALE = HEAD_DIM ** -0.5 * LOG2E

LANES = 128
SUBLANES = 8
TM = 512
TN_IN = 512
TQ = 512
TT = 256
SUBSEQ = TT // SUBLANES
TG = 1024
TG_SUB = 512
MOE_F_CHUNKS = 7
ROW_DMA_UNROLL = 8
MOD_ROWS = 8
VMEM_LIMIT = 56 * 1024 * 1024


def _cparams(n_axes):
    return pltpu.CompilerParams(dimension_semantics=("arbitrary",) * n_axes,
                                vmem_limit_bytes=VMEM_LIMIT)


def _sigmoid(z):
    return 0.5 * jnp.tanh(0.5 * z) + 0.5


def _norm_mod(h, g, shift, scale):
    ms = jnp.mean(h * h, axis=-1, keepdims=True)
    y = h * lax.rsqrt(ms + EPS) * g
    return y * (1.0 + scale) + shift


def _mod_kernel(c_ref, w_ref, b_ref, o_ref):
    cv = c_ref[...]
    s = cv * _sigmoid(cv)
    o_ref[...] = jnp.dot(s, w_ref[...], preferred_element_type=F32,
                         precision=lax.Precision.HIGHEST) + b_ref[...]


def _modulation(cpad, w_mod, b_mod):
    nchunk = 6
    return pl.pallas_call(
        _mod_kernel,
        grid=(DEPTH, nchunk),
        in_specs=[
            pl.BlockSpec((MOD_ROWS, D_MODEL), lambda l, n: (0, 0)),
            pl.BlockSpec((None, D_MODEL, D_MODEL), lambda l, n: (l, 0, n)),
            pl.BlockSpec((None, 1, D_MODEL), lambda l, n: (l, 0, n)),
        ],
        out_specs=pl.BlockSpec((None, MOD_ROWS, D_MODEL), lambda l, n: (l, 0, n)),
        out_shape=jax.ShapeDtypeStruct((DEPTH, MOD_ROWS, 6 * D_MODEL), F32),
        compiler_params=_cparams(2),
        name="modulation",
    )(cpad, w_mod, b_mod.reshape(DEPTH, 1, 6 * D_MODEL))


def _rope(xh, cos, sin_signed, first_half):
    sw = jnp.where(first_half, pltpu.roll(xh, 96, 1), pltpu.roll(xh, 32, 1))
    return xh * cos + sw * sin_signed


def _stream_tile(h_lat_ref, h_ctx_ref):
    is_ctx_tile = pl.program_id(0) == pl.num_programs(0) - 1
    return jnp.where(is_ctx_tile, h_ctx_ref[...], h_lat_ref[...])


def _stream_specs(stream, n_lat_tiles):
    _, _, ctx_block = stream
    return [pl.BlockSpec((TM, D_MODEL), lambda i: (jnp.minimum(i, n_lat_tiles - 1), 0)),
            pl.BlockSpec((TM, D_MODEL), lambda i: (ctx_block, 0))]


def _in_proj_kernel(h_lat_ref, h_ctx_ref, g_ref, sh_ref, sc_ref, w_ref, cos_ref, sin_ref, perm_ref,
                    q_ref, kv_ref, x_ref, gy_ref, sga_ref, sgr_ref):
    n = _norm_mod(_stream_tile(h_lat_ref, h_ctx_ref), g_ref[...], sh_ref[...], sc_ref[...]).astype(BF16)
    cos = cos_ref[...]
    sin = sin_ref[...]
    lane = lax.broadcasted_iota(jnp.int32, (TM, LANES), 1)
    first_half = (lane & 32) == 0

    def proj(chunk):
        return jnp.dot(n, w_ref[:, chunk * TN_IN:(chunk + 1) * TN_IN], preferred_element_type=F32)

    def rope_store(acc, n_heads, scale, ref, col0):
        for hh in range(n_heads):
            y = _rope(acc[:, hh * HEAD_DIM:(hh + 1) * HEAD_DIM], cos, sin, first_half)
            if scale != 1.0:
                y = y * scale
            ref[:, col0 + hh * HEAD_DIM:col0 + (hh + 1) * HEAD_DIM] = y.astype(BF16)

    for c in range(2):
        rope_store(proj(c), 4, ATTN_SCALE, q_ref, c * TN_IN)
    acc = proj(2)
    rope_store(acc, 2, 1.0, kv_ref, 0)
    kv_ref[:, KV_COLS:] = acc[:, KV_COLS:].astype(BF16)
    n_perm = jnp.dot(perm_ref[...], n, preferred_element_type=F32).astype(BF16)
    for c in range(2):
        cols = slice(c * TN_IN, (c + 1) * TN_IN)
        x_ref[:, cols] = jnp.dot(n_perm, w_ref[:, (3 + c) * TN_IN:(4 + c) * TN_IN],
                                 preferred_element_type=F32)
        gy_ref[:, cols] = jax.nn.gelu(proj(5 + c)).astype(BF16)
        sga_ref[:, cols] = _sigmoid(proj(7 + c)).astype(BF16)
        sgr_ref[:, cols] = _sigmoid(proj(9 + c)).astype(BF16)


def _in_proj(stream, g, mod3, w_in, layer, cos_t, sin_t, perm, *, n_lat_tiles, tiles_per_batch, n_batch):
    T = (n_lat_tiles + 1) * TM

    def grp(i):
        return jnp.minimum(i // tiles_per_batch, n_batch)

    def pos_tile(i):
        return jnp.where(i < n_lat_tiles, i % tiles_per_batch, tiles_per_batch)

    def tok(width):
        return pl.BlockSpec((TM, width), lambda i: (i, 0))

    out_shape = (
        jax.ShapeDtypeStruct((T, Q_COLS), BF16),
        jax.ShapeDtypeStruct((T, 2 * KV_COLS), BF16),
        jax.ShapeDtypeStruct((T, D_RNN), F32),
        jax.ShapeDtypeStruct((T, D_RNN), BF16),
        jax.ShapeDtypeStruct((T, D_MODEL), BF16),
        jax.ShapeDtypeStruct((T, D_MODEL), BF16),
    )
    return pl.pallas_call(
        _in_proj_kernel,
        grid=(T // TM,),
        in_specs=_stream_specs(stream, n_lat_tiles) + [
            pl.BlockSpec((None, 1, D_MODEL), lambda i: (layer, 0, 0)),
            pl.BlockSpec((None, 1, D_MODEL), lambda i: (grp(i), 0, 0)),
            pl.BlockSpec((None, 1, D_MODEL), lambda i: (grp(i), 0, 1)),
            pl.BlockSpec((None, D_MODEL, IN_COLS), lambda i: (layer, 0, 0)),
            pl.BlockSpec((TM, HEAD_DIM), lambda i: (pos_tile(i), 0)),
            pl.BlockSpec((TM, HEAD_DIM), lambda i: (pos_tile(i), 0)),
            pl.BlockSpec((TM, TM), lambda i: (0, 0)),
        ],
        out_specs=(tok(Q_COLS), tok(2 * KV_COLS), tok(D_RNN), tok(D_RNN), tok(D_MODEL), tok(D_MODEL)),
        out_shape=out_shape,
        compiler_params=_cparams(1),
        name="in_proj",
    )(stream[0], stream[1], g, mod3, mod3, w_in, cos_t, sin_t, perm)


def _stack_heads(qt):
    return jnp.concatenate([qt[:, g * HEAD_DIM:(g + 1) * HEAD_DIM] for g in range(N_GROUPS)], axis=0)


def _sink_col(sink_ref, kvh, rows):
    return jnp.concatenate(
        [jnp.full((rows, 1), sink_ref[kvh * N_GROUPS + g] * LOG2E, F32) for g in range(N_GROUPS)], axis=0)


def _ones_column(n_keys):
    lane = lax.broadcasted_iota(jnp.int32, (n_keys, HEAD_DIM), 1)
    return jnp.where(lane == 0, 1.0, 0.0).astype(BF16)


def _softmax_pv(s, sink, vall):
    m = jnp.maximum(jnp.max(s, axis=1, keepdims=True), sink)
    p = jnp.exp2(s - m).astype(BF16)
    v_aug = jnp.concatenate([vall, _ones_column(vall.shape[0])], axis=1)
    oa = jnp.dot(p, v_aug, preferred_element_type=F32)
    denom = oa[:, HEAD_DIM:HEAD_DIM + 1] + jnp.exp2(sink - m)
    return oa[:, :HEAD_DIM] / denom


def _attn_kernel(sink_ref, q_ref, k_ref, v_ref, kc_ref, vc_ref, o_ref, *, n_blocks):
    kvh = pl.program_id(1)
    i = pl.program_id(2)
    rows = ATTN_BLOCK * N_GROUPS
    qi = lax.broadcasted_iota(jnp.int32, (rows, ATTN_BLOCK), 0) & (ATTN_BLOCK - 1)
    kj = lax.broadcasted_iota(jnp.int32, (rows, ATTN_BLOCK), 1)
    tri_prev = kj >= qi
    tri_next = kj <= qi
    sink = _sink_col(sink_ref, kvh, ATTN_BLOCK)
    kc = kc_ref[...]
    vc = vc_ref[...]

    def body(qb, carry):
        n = i * (TQ // ATTN_BLOCK) + qb
        r0 = pl.multiple_of(qb * ATTN_BLOCK, ATTN_BLOCK)
        p0 = pl.multiple_of(jnp.maximum(n - 1, 0) * ATTN_BLOCK, ATTN_BLOCK)
        c0 = pl.multiple_of(n * ATTN_BLOCK, ATTN_BLOCK)
        n0 = pl.multiple_of(jnp.minimum(n + 1, n_blocks - 1) * ATTN_BLOCK, ATTN_BLOCK)
        qs = _stack_heads(q_ref[pl.ds(r0, ATTN_BLOCK), :])
        kall = jnp.concatenate([k_ref[pl.ds(p0, ATTN_BLOCK), :], k_ref[pl.ds(c0, ATTN_BLOCK), :],
                                k_ref[pl.ds(n0, ATTN_BLOCK), :], kc], axis=0)
        vall = jnp.concatenate([v_ref[pl.ds(p0, ATTN_BLOCK), :], v_ref[pl.ds(c0, ATTN_BLOCK), :],
                                v_ref[pl.ds(n0, ATTN_BLOCK), :], vc], axis=0)
        s = lax.dot_general(qs, kall, (((1,), (1,)), ((), ())), preferred_element_type=F32)
        pen_prev = jnp.where(n > 0, 0.0, NEG_INF)
        pen_next = jnp.where(n < n_blocks - 1, 0.0, NEG_INF)
        sp = jnp.where(tri_prev, s[:, :ATTN_BLOCK] + pen_prev, NEG_INF)
        sn = jnp.where(tri_next, s[:, 2 * ATTN_BLOCK:3 * ATTN_BLOCK] + pen_next, NEG_INF)
        s = jnp.concatenate([sp, s[:, ATTN_BLOCK:2 * ATTN_BLOCK], sn, s[:, 3 * ATTN_BLOCK:]], axis=1)
        o = _softmax_pv(s, sink, vall)
        for g in range(N_GROUPS):
            o_ref[pl.ds(r0, ATTN_BLOCK), g * HEAD_DIM:(g + 1) * HEAD_DIM] = (
                o[g * ATTN_BLOCK:(g + 1) * ATTN_BLOCK].astype(BF16))
        return carry

    lax.fori_loop(0, TQ // ATTN_BLOCK, body, 0, unroll=True)


def _latent_attention(sink, q, kv, *, n_batch, seq, ctx_len):
    gw = N_GROUPS * HEAD_DIM
    tiles_per_batch = seq // TQ
    ctx_blk0 = n_batch * seq // ctx_len
    grid_spec = pltpu.PrefetchScalarGridSpec(
        num_scalar_prefetch=1,
        grid=(n_batch, N_KV_HEADS, tiles_per_batch),
        in_specs=[
            pl.BlockSpec((TQ, gw), lambda b, k, i, s: (b * tiles_per_batch + i, k)),
            pl.BlockSpec((seq, HEAD_DIM), lambda b, k, i, s: (b, k)),
            pl.BlockSpec((seq, HEAD_DIM), lambda b, k, i, s: (b, N_KV_HEADS + k)),
            pl.BlockSpec((ctx_len, HEAD_DIM), lambda b, k, i, s: (ctx_blk0 + b, k)),
            pl.BlockSpec((ctx_len, HEAD_DIM), lambda b, k, i, s: (ctx_blk0 + b, N_KV_HEADS + k)),
        ],
        out_specs=pl.BlockSpec((TQ, gw), lambda b, k, i, s: (b * tiles_per_batch + i, k)),
    )
    return pl.pallas_call(
        functools.partial(_attn_kernel, n_blocks=seq // ATTN_BLOCK),
        grid_spec=grid_spec,
        out_shape=jax.ShapeDtypeStruct((n_batch * seq, Q_COLS), BF16),
        compiler_params=_cparams(3),
        name="latent_attention",
    )(sink, q, kv, kv, kv, kv)


def _ctx_attn_kernel(sink_ref, q_ref, kc_ref, vc_ref, o_ref):
    kvh = pl.program_id(1)
    rows = q_ref.shape[0]
    qs = _stack_heads(q_ref[...])
    s = lax.dot_general(qs, kc_ref[...], (((1,), (1,)), ((), ())), preferred_element_type=F32)
    o = _softmax_pv(s, _sink_col(sink_ref, kvh, rows), vc_ref[...])
    for g in range(N_GROUPS):
        o_ref[:, g * HEAD_DIM:(g + 1) * HEAD_DIM] = o[g * rows:(g + 1) * rows].astype(BF16)


def _context_attention(sink, q, kv, *, n_batch, seq, ctx_len):
    gw = N_GROUPS * HEAD_DIM
    ctx_blk0 = n_batch * seq // ctx_len
    grid_spec = pltpu.PrefetchScalarGridSpec(
        num_scalar_prefetch=1,
        grid=(n_batch, N_KV_HEADS),
        in_specs=[
            pl.BlockSpec((ctx_len, gw), lambda b, k, s: (ctx_blk0 + b, k)),
            pl.BlockSpec((ctx_len, HEAD_DIM), lambda b, k, s: (ctx_blk0 + b, k)),
            pl.BlockSpec((ctx_len, HEAD_DIM), lambda b, k, s: (ctx_blk0 + b, N_KV_HEADS + k)),
        ],
        out_specs=pl.BlockSpec((ctx_len, gw), lambda b, k, s: (b, k)),
    )
    return pl.pallas_call(
        _ctx_attn_kernel,
        grid_spec=grid_spec,
        out_shape=jax.ShapeDtypeStruct((n_batch * ctx_len, Q_COLS), BF16),
        compiler_params=_cparams(2),
        name="context_attention",
    )(sink, q, kv, kv)


def _softplus(z):
    return jnp.maximum(z, 0.0) + jnp.log1p(jnp.exp(-jnp.abs(z)))


def _scan8(a, b, row, reverse):
    for s in (1, 2, 4):
        if reverse:
            keep = row < SUBLANES - s
            shift = SUBLANES - s
        else:
            keep = row >= s
            shift = s
        a_sh = jnp.where(keep, pltpu.roll(a, shift, 0), 1.0)
        b_sh = jnp.where(keep, pltpu.roll(b, shift, 0), 0.0)
        b = a * b_sh + b
        a = a * a_sh
    return a, b


def _rglru_kernel(xf_ref, xfp_ref, xfn_ref, xb_ref, xbp_ref, xbn_ref, cw_ref, cb_ref, wcat_ref,
                  gb_ref, lam_ref, unperm_ref, hf_ref, hb_ref, a_s, b_s, h_bf, carry):
    j = pl.program_id(1)
    last_j = pl.num_programs(1) - 1

    @pl.when(j == 0)
    def _():
        carry[...] = jnp.zeros_like(carry)

    seg_start = (j <= 1, (j == 0) | (j == last_j))
    seg_end = ((j == 0) | (j == last_j), j <= 1)
    mains = (xf_ref, xb_ref)
    prevs = (xfp_ref, xbp_ref)
    nexts = (xfn_ref, xbn_ref)
    sub = lax.broadcasted_iota(jnp.int32, (SUBLANES, D_RNN), 0)
    S8 = SUBLANES

    for d in range(2):
        x0 = mains[d][...]
        halo = prevs[d][...]
        t_m1 = jnp.where(seg_start[d], 0.0, halo[2 * S8 - 1:2 * S8, :])
        t_m2 = jnp.where(seg_start[d], 0.0, halo[S8 - 1:S8, :])
        t_p1 = jnp.where(seg_end[d], 0.0, nexts[d][0:1, :])
        g_m1 = jnp.where(sub == 0, t_m1, pltpu.roll(x0[TT - S8:, :], 1, 0))
        g_m2 = jnp.where(sub == 0, t_m2, pltpu.roll(x0[TT - 2 * S8:TT - S8, :], 1, 0))
        g_p1 = jnp.where(sub == S8 - 1, t_p1, pltpu.roll(x0[:S8, :], S8 - 1, 0))
        xc = cb_ref[...] + jnp.concatenate([g_m2, g_m1, x0[:TT - 2 * S8, :]], axis=0) * cw_ref[0:1, :]
        xc = xc + jnp.concatenate([g_m1, x0[:TT - S8, :]], axis=0) * cw_ref[1:2, :]
        xc = xc + x0 * cw_ref[2:3, :]
        xc = xc + jnp.concatenate([x0[S8:, :], g_p1], axis=0) * cw_ref[3:4, :]
        c_d = (-LRU_C * LOG2E) * _softplus(-lam_ref[d:d + 1, :])
        for blk in range(N_RNN_BLOCKS):
            sl = slice(blk * RNN_BLOCK, (blk + 1) * RNN_BLOCK)
            xcb = xc[:, sl]
            z = jnp.dot(xcb.astype(BF16), wcat_ref[d, blk], preferred_element_type=F32)
            r = _sigmoid(z[:, :RNN_BLOCK] + gb_ref[d, 0:1, sl])
            ig = _sigmoid(z[:, RNN_BLOCK:] + gb_ref[d, 1:2, sl])
            a = jnp.exp2(r * c_d[:, sl])
            v = 1.0 - a * a
            root = jnp.where(v > 0.0, v * lax.rsqrt(v), 0.0)
            a_s[d, :, sl] = a
            b_s[d, :, sl] = root * (ig * xcb)

    def local(g, c):
        hf, pf, hb, pb = c
        rf = pl.multiple_of(g * S8, S8)
        rb = pl.multiple_of((SUBSEQ - 1 - g) * S8, S8)
        af = a_s[0, pl.ds(rf, S8), :]
        ab = a_s[1, pl.ds(rb, S8), :]
        hf = af * hf + b_s[0, pl.ds(rf, S8), :]
        hb = ab * hb + b_s[1, pl.ds(rb, S8), :]
        pf = af * pf
        pb = ab * pb
        b_s[0, pl.ds(rf, S8), :] = hf
        b_s[1, pl.ds(rb, S8), :] = hb
        a_s[0, pl.ds(rf, S8), :] = pf
        a_s[1, pl.ds(rb, S8), :] = pb
        return hf, pf, hb, pb

    zero = jnp.zeros((S8, D_RNN), F32)
    one = jnp.ones((S8, D_RNN), F32)
    hf, pf, hb, pb = lax.fori_loop(0, SUBSEQ, local, (zero, one, zero, one), unroll=2)

    af, bf = _scan8(pf, hf, sub, False)
    endf = af * carry[0] + bf
    h_in = [jnp.where(sub == 0, carry[0], pltpu.roll(endf, 1, 0))]
    carry[0] = jnp.broadcast_to(endf[S8 - 1:S8, :], (S8, D_RNN))
    ab, bb = _scan8(pb, hb, sub, True)
    endb = ab * carry[1] + bb
    h_in.append(jnp.where(sub == S8 - 1, carry[1], pltpu.roll(endb, S8 - 1, 0)))
    carry[1] = jnp.broadcast_to(endb[0:1, :], (S8, D_RNN))

    outs = (hf_ref, hb_ref)
    for d in range(2):
        h_in2 = jnp.concatenate([h_in[d], h_in[d]], axis=0)

        def fix(k, c, d=d, h_in2=h_in2):
            rows = pl.ds(pl.multiple_of(k * 2 * S8, 2 * S8), 2 * S8)
            h_bf[d, rows, :] = (b_s[d, rows, :] + a_s[d, rows, :] * h_in2).astype(BF16)
            return c

        lax.fori_loop(0, SUBSEQ // 2, fix, 0, unroll=2)
        outs[d][...] = jnp.dot(unperm_ref[...], h_bf[d], preferred_element_type=F32).astype(BF16)


def _time_permutation(n_rows):
    p = jnp.arange(n_rows)
    src = (p // TT) * TT + (p % SUBLANES) * SUBSEQ + (p % TT) // SUBLANES
    return (src[:, None] == jnp.arange(n_rows)[None, :]).astype(BF16)


def _rglru(x, conv_w, conv_b, wcat, gate_b, lam, *, n_batch, seq, ctx_len):
    T = x.shape[0]
    assert ctx_len == TT and seq % TT == 0
    tps = seq // TT
    n_lat_t = n_batch * tps
    halo_rows = 2 * SUBLANES
    per_halo = TT // halo_rows
    per8 = TT // SUBLANES
    last8 = T // SUBLANES - 1
    unperm = _time_permutation(TT).T

    def ftile(b, j):
        return jnp.where(j == 0, n_lat_t + b, b * tps + j - 1)

    def btile(b, j):
        return jnp.where(j == 0, n_lat_t + b, b * tps + tps - j)

    def main(tile):
        return pl.BlockSpec((TT, D_RNN), lambda b, j: (tile(b, j), 0))

    def prev(tile):
        return pl.BlockSpec((halo_rows, D_RNN), lambda b, j: (jnp.maximum(tile(b, j) * per_halo - 1, 0), 0))

    def nxt(tile):
        return pl.BlockSpec((SUBLANES, D_RNN),
                            lambda b, j: (jnp.minimum((tile(b, j) + 1) * per8, last8), 0))

    def const(shape):
        return pl.BlockSpec(shape, lambda b, j: (0,) * len(shape))

    return pl.pallas_call(
        _rglru_kernel,
        grid=(n_batch, tps + 1),
        in_specs=[main(ftile), prev(ftile), nxt(ftile), main(btile), prev(btile), nxt(btile),
                  const(conv_w.shape), const(conv_b.shape), const(wcat.shape), const(gate_b.shape),
                  const(lam.shape), const(unperm.shape)],
        out_specs=(main(ftile), main(btile)),
        out_shape=(jax.ShapeDtypeStruct((T, D_RNN), BF16), jax.ShapeDtypeStruct((T, D_RNN), BF16)),
        scratch_shapes=[pltpu.VMEM((2, TT, D_RNN), F32),
                        pltpu.VMEM((2, TT, D_RNN), F32),
                        pltpu.VMEM((2, TT, D_RNN), BF16),
                        pltpu.VMEM((2, SUBLANES, D_RNN), F32)],
        compiler_params=_cparams(2),
        name="rglru",
    )(x, x, x, x, x, x, conv_w, conv_b, wcat, gate_b, lam, unperm)


def _merge_kernel(h_lat_ref, h_ctx_ref, attn_ref, attn_ctx_ref, hf_ref, hb_ref, gy_ref, sga_ref, sgr_ref,
                  g1_ref, woa_ref, wol_ref, wout_ref, o_ref):
    rec = ((hf_ref[...].astype(F32) + hb_ref[...].astype(F32)) * gy_ref[...].astype(F32)).astype(BF16)
    is_ctx_tile = pl.program_id(0) == pl.num_programs(0) - 1
    attn = jnp.where(is_ctx_tile, attn_ctx_ref[...], attn_ref[...])
    ta = jnp.dot(attn, woa_ref[...], preferred_element_type=F32)
    tl = jnp.dot(rec, wol_ref[...], preferred_element_type=F32)
    m = sga_ref[...].astype(F32) * ta + sgr_ref[...].astype(F32) * tl
    y = jnp.dot(m.astype(BF16), wout_ref[...], preferred_element_type=F32)
    o_ref[...] = _stream_tile(h_lat_ref, h_ctx_ref) + g1_ref[...] * y


def _merge(stream, attn, attn_ctx, hf, hb, gy, sga, sgr, mod3, woa, wol, wout, layer, *,
           tiles_per_batch, n_batch):
    n_lat_tiles = attn.shape[0] // TM
    T = (n_lat_tiles + 1) * TM
    assert attn_ctx.shape[0] == TM and hf.shape[0] == T

    def grp(i):
        return jnp.minimum(i // tiles_per_batch, n_batch)

    tok = pl.BlockSpec((TM, D_MODEL), lambda i: (i, 0))
    wsp = pl.BlockSpec((None, D_MODEL, D_MODEL), lambda i: (layer, 0, 0))
    return pl.pallas_call(
        _merge_kernel,
        grid=(T // TM,),
        in_specs=_stream_specs(stream, n_lat_tiles) + [
                  pl.BlockSpec((TM, Q_COLS), lambda i: (jnp.minimum(i, n_lat_tiles - 1), 0)),
                  pl.BlockSpec((TM, Q_COLS), lambda i: (0, 0)),
                  tok, tok, tok, tok, tok,
                  pl.BlockSpec((None, 1, D_MODEL), lambda i: (grp(i), 0, 2)),
                  wsp, wsp, wsp],
        out_specs=tok,
        out_shape=jax.ShapeDtypeStruct((T, D_MODEL), F32),
        compiler_params=_cparams(1),
        name="merge",
    )(stream[0], stream[1], attn, attn_ctx, hf, hb, gy, sga, sgr, mod3, woa, wol, wout)


def _swiglu_partial(n, wg, wu, wd):
    gt = jnp.dot(n, wg, preferred_element_type=F32)
    ut = jnp.dot(n, wu, preferred_element_type=F32)
    act = (gt * _sigmoid(gt) * ut).astype(BF16)
    return jnp.dot(act, wd, preferred_element_type=F32)


FFN_F_CHUNKS = 2


def _ffn_kernel(h_ref, g_ref, sh_ref, sc_ref, g2_ref, wg_ref, wu_ref, wd_ref, o_ref):
    h = h_ref[...]
    n = _norm_mod(h, g_ref[...], sh_ref[...], sc_ref[...]).astype(BF16)
    tf = D_FF // FFN_F_CHUNKS
    acc = None
    for c in range(FFN_F_CHUNKS):
        cols = slice(c * tf, (c + 1) * tf)
        part = _swiglu_partial(n, wg_ref[:, cols], wu_ref[:, cols], wd_ref[cols, :])
        acc = part if acc is None else acc + part
    o_ref[...] = h + g2_ref[...] * acc


def _ffn(h, g, mod3, wg, wu, wd, layer, ff_layer, *, tiles_per_batch, n_batch):
    T = h.shape[0]
    assert (D_FF // FFN_F_CHUNKS) % LANES == 0

    def grp(i):
        return jnp.minimum(i // tiles_per_batch, n_batch)

    def modspec(k):
        return pl.BlockSpec((None, 1, D_MODEL), lambda i: (grp(i), 0, k))

    tok = pl.BlockSpec((TM, D_MODEL), lambda i: (i, 0))
    return pl.pallas_call(
        _ffn_kernel,
        grid=(T // TM,),
        in_specs=[tok, pl.BlockSpec((None, 1, D_MODEL), lambda i: (layer, 0, 0)),
                  modspec(3), modspec(4), modspec(5),
                  pl.BlockSpec((None, D_MODEL, D_FF), lambda i: (ff_layer, 0, 0)),
                  pl.BlockSpec((None, D_MODEL, D_FF), lambda i: (ff_layer, 0, 0)),
                  pl.BlockSpec((None, D_FF, D_MODEL), lambda i: (ff_layer, 0, 0))],
        out_specs=tok,
        out_shape=jax.ShapeDtypeStruct((T, D_MODEL), F32),
        compiler_params=_cparams(1),
        name="dense_ffn",
    )(h, g, mod3, mod3, mod3, wg, wu, wd)


ROUTE_E1, ROUTE_E2, ROUTE_W1, ROUTE_W2, ROUTE_R1, ROUTE_R2 = range(6)


def _dot_split(a, b):
    a_hi = a.astype(BF16)
    a_lo = (a - a_hi.astype(F32)).astype(BF16)
    b_hi = b.astype(BF16)
    b_lo = (b - b_hi.astype(F32)).astype(BF16)

    def mm(x, y):
        return jnp.dot(x, y, preferred_element_type=F32)

    return mm(a_hi, b_hi) + (mm(a_hi, b_lo) + mm(a_lo, b_hi))


def _router_kernel(h_ref, g_ref, sh_ref, sc_ref, wr_ref, route_ref, cnt_ref, zero_ref, run):
    @pl.when(pl.program_id(0) == 0)
    def _():
        run[...] = jnp.zeros_like(run)

    n = _norm_mod(h_ref[...], g_ref[...], sh_ref[...], sc_ref[...])
    logits = _dot_split(n, wr_ref[...])
    lane = lax.broadcasted_iota(jnp.int32, logits.shape, 1)
    logits = jnp.where(lane < N_EXPERTS, logits, -jnp.inf)
    m1 = jnp.max(logits, axis=1, keepdims=True)
    i1 = jnp.min(jnp.where(logits == m1, lane, LANES), axis=1, keepdims=True)
    rest = jnp.where(lane == i1, -jnp.inf, logits)
    m2 = jnp.max(rest, axis=1, keepdims=True)
    i2 = jnp.min(jnp.where(rest == m2, lane, LANES), axis=1, keepdims=True)
    e2 = jnp.exp(m2 - m1)
    w1 = 1.0 / (1.0 + e2)
    w2 = e2 / (1.0 + e2)

    hit1 = lane == i1
    hit2 = lane == i2
    onehot = jnp.where(hit1 | hit2, 1.0, 0.0)
    r_i = lax.broadcasted_iota(jnp.int32, (TM, TM), 0)
    c_i = lax.broadcasted_iota(jnp.int32, (TM, TM), 1)
    lower = jnp.where(c_i < r_i, 1.0, 0.0).astype(BF16)
    prefix = jnp.dot(lower, onehot.astype(BF16), preferred_element_type=F32) + run[0:1, :]
    rank1 = jnp.sum(jnp.where(hit1, prefix, 0.0), axis=1, keepdims=True)
    rank2 = jnp.sum(jnp.where(hit2, prefix, 0.0), axis=1, keepdims=True)
    run[...] = run[...] + jnp.sum(onehot, axis=0, keepdims=True)
    cnt_ref[...] = run[...]

    rec = jnp.zeros(logits.shape, F32)
    for k, v in ((ROUTE_E1, i1.astype(F32)), (ROUTE_E2, i2.astype(F32)), (ROUTE_W1, w1), (ROUTE_W2, w2),
                 (ROUTE_R1, rank1), (ROUTE_R2, rank2)):
        rec = jnp.where(lane == k, v, rec)
    route_ref[...] = rec
    zero_ref[...] = jnp.zeros_like(zero_ref)


def _router(h, g2, mod3, w_r, layer, sorted_rows, *, tiles_per_batch, n_batch):
    T = h.shape[0]
    n_steps = T // TM
    zero_rows = -(-sorted_rows // (n_steps * SUBLANES)) * SUBLANES

    def grp(i):
        return jnp.minimum(i // tiles_per_batch, n_batch)

    def modspec(k):
        return pl.BlockSpec((None, 1, D_MODEL), lambda i: (grp(i), 0, k))

    return pl.pallas_call(
        _router_kernel,
        grid=(T // TM,),
        in_specs=[pl.BlockSpec((TM, D_MODEL), lambda i: (i, 0)),
                  pl.BlockSpec((None, 1, D_MODEL), lambda i: (layer, 0, 0)),
                  modspec(3), modspec(4),
                  pl.BlockSpec((D_MODEL, LANES), lambda i: (0, 0))],
        out_specs=(pl.BlockSpec((TM, LANES), lambda i: (i, 0)),
                   pl.BlockSpec((SUBLANES, LANES), lambda i: (0, 0)),
                   pl.BlockSpec((zero_rows, D_MODEL), lambda i: (i, 0))),
        out_shape=(jax.ShapeDtypeStruct((T, LANES), F32), jax.ShapeDtypeStruct((SUBLANES, LANES), F32),
                   jax.ShapeDtypeStruct((n_steps * zero_rows, D_MODEL), F32)),
        scratch_shapes=[pltpu.VMEM((SUBLANES, LANES), F32)],
        compiler_params=_cparams(1),
        name="moe_router",
    )(h, g2, mod3, mod3, w_r)


def _row_copy(src_ref, src_row, dst_ref, dst_row, sem):
    return pltpu.make_async_copy(src_ref.at[pl.ds(src_row, 1)], dst_ref.at[pl.ds(dst_row, 1)], sem)


def _dispatch_kernel(dest_ref, h_ref, g_ref, sh_ref, sc_ref, xs_in_ref, xs_ref, n_scr, sem):
    del xs_in_ref
    i = pl.program_id(0)
    base = i * (2 * TM)
    cur = i % 2
    n_scr[cur] = _norm_mod(h_ref[...], g_ref[...], sh_ref[...], sc_ref[...])

    def start(r, carry):
        for s in range(2):
            _row_copy(n_scr.at[cur], r, xs_ref, dest_ref[base + 2 * r + s], sem.at[cur]).start(priority=s)
        return carry

    lax.fori_loop(0, TM, start, 0, unroll=ROW_DMA_UNROLL)

    def wait_all(buf):
        for s in range(2):
            pltpu.make_async_copy(n_scr.at[buf], xs_ref.at[pl.ds(0, TM)], sem.at[buf]).wait()

    @pl.when(i > 0)
    def _():
        wait_all(1 - cur)

    @pl.when(i == pl.num_programs(0) - 1)
    def _():
        wait_all(cur)


def _dispatch(dest, h, g2, mod3, xs_zero, layer, *, tiles_per_batch, n_batch):
    T = h.shape[0]

    def grp(i):
        return jnp.minimum(i // tiles_per_batch, n_batch)

    def modspec(k):
        return pl.BlockSpec((None, 1, D_MODEL), lambda i, d: (grp(i), 0, k))

    grid_spec = pltpu.PrefetchScalarGridSpec(
        num_scalar_prefetch=1,
        grid=(T // TM,),
        in_specs=[pl.BlockSpec((TM, D_MODEL), lambda i, d: (i, 0)),
                  pl.BlockSpec((None, 1, D_MODEL), lambda i, d: (layer, 0, 0)),
                  modspec(3), modspec(4),
                  pl.BlockSpec(memory_space=pl.ANY)],
        out_specs=pl.BlockSpec(memory_space=pl.ANY),
        scratch_shapes=[pltpu.VMEM((2, TM, D_MODEL), F32), pltpu.SemaphoreType.DMA((2,))],
    )
    return pl.pallas_call(
        _dispatch_kernel,
        grid_spec=grid_spec,
        out_shape=jax.ShapeDtypeStruct(xs_zero.shape, xs_zero.dtype),
        input_output_aliases={5: 0},
        compiler_params=_cparams(1),
        name="moe_dispatch",
    )(dest, h, g2, mod3, mod3, xs_zero)


def _expert_kernel(te_ref, nu_ref, rows_ref, x_ref, wg_ref, wu_ref, wd_ref, o_ref, xb, acc):
    del te_ref, nu_ref
    k = pl.program_id(0)
    f = pl.program_id(1)

    @pl.when(f == 0)
    def _():
        xb[...] = x_ref[...].astype(BF16)
        acc[...] = jnp.zeros_like(acc)

    wg = wg_ref[...].astype(BF16)
    wu = wu_ref[...].astype(BF16)
    wd = wd_ref[...].astype(BF16)
    acc[:TG_SUB, :] += _swiglu_partial(xb[:TG_SUB, :], wg, wu, wd)
    for sub in range(1, TG // TG_SUB):
        rows = slice(sub * TG_SUB, (sub + 1) * TG_SUB)

        @pl.when(rows_ref[k] > sub * TG_SUB)
        def _():
            acc[rows, :] += _swiglu_partial(xb[rows, :], wg, wu, wd)

    @pl.when(f == pl.num_programs(1) - 1)
    def _():
        o_ref[...] = acc[...]


def _experts(tile_expert, n_used, tile_rows, xs, wg, wu, wd, layer, n_tiles):
    P = n_tiles * TG
    assert xs.shape[0] >= P
    n_f = MOE_F_CHUNKS
    tf = D_FF_EXPERT // n_f
    assert tf % LANES == 0

    def fsel(k, f, nu):
        return jnp.where(k < nu[0], f, n_f - 1)

    grid_spec = pltpu.PrefetchScalarGridSpec(
        num_scalar_prefetch=3,
        grid=(P // TG, n_f),
        in_specs=[pl.BlockSpec((TG, D_MODEL), lambda k, f, te, nu, tr: (k, 0)),
                  pl.BlockSpec((None, None, D_MODEL, tf), lambda k, f, te, nu, tr: (layer, te[k], 0, fsel(k, f, nu))),
                  pl.BlockSpec((None, None, D_MODEL, tf), lambda k, f, te, nu, tr: (layer, te[k], 0, fsel(k, f, nu))),
                  pl.BlockSpec((None, None, tf, D_MODEL), lambda k, f, te, nu, tr: (layer, te[k], fsel(k, f, nu), 0))],
        out_specs=pl.BlockSpec((TG, D_MODEL), lambda k, f, te, nu, tr: (k, 0)),
        scratch_shapes=[pltpu.VMEM((TG, D_MODEL), BF16), pltpu.VMEM((TG, D_MODEL), F32)],
    )
    return pl.pallas_call(
        _expert_kernel,
        grid_spec=grid_spec,
        out_shape=jax.ShapeDtypeStruct((P, D_MODEL), F32),
        compiler_params=_cparams(2),
        name="moe_experts",
    )(tile_expert, n_used, tile_rows, xs, wg, wu, wd)


def _combine_kernel(dest_ref, h_ref, g2_ref, route_ref, fg_ref, y_ref, o_ref, ybuf, sem, *, final):
    i = pl.program_id(0)
    n_steps = pl.num_programs(0)

    def gather(tile, buf):
        base = tile * (2 * TM)

        def start(r, carry):
            for s in range(2):
                _row_copy(y_ref, dest_ref[base + 2 * r + s], ybuf.at[buf], s * TM + r,
                          sem.at[buf]).start(priority=s)
            return carry

        lax.fori_loop(0, TM, start, 0, unroll=ROW_DMA_UNROLL)

    @pl.when(i == 0)
    def _():
        gather(0, 0)

    @pl.when(i + 1 < n_steps)
    def _():
        gather(i + 1, (i + 1) % 2)

    cur = i % 2
    for s in range(2):
        pltpu.make_async_copy(y_ref.at[pl.ds(0, TM)], ybuf.at[cur, pl.ds(s * TM, TM)], sem.at[cur]).wait()
    route = route_ref[...]
    f = (route[:, ROUTE_W1:ROUTE_W1 + 1] * ybuf[cur, 0:TM, :]
         + route[:, ROUTE_W2:ROUTE_W2 + 1] * ybuf[cur, TM:, :])
    out = h_ref[...] + g2_ref[...] * f
    if final:
        ms = jnp.mean(out * out, axis=-1, keepdims=True)
        out = out * lax.rsqrt(ms + EPS) * fg_ref[...]
    o_ref[...] = out


def _combine(dest, h, mod3, route, final_g, y, *, final, tiles_per_batch, n_batch):
    n_tok_tiles = h.shape[0] // TM - (1 if final else 0)

    def grp(i):
        return jnp.minimum(i // tiles_per_batch, n_batch)

    grid_spec = pltpu.PrefetchScalarGridSpec(
        num_scalar_prefetch=1,
        grid=(n_tok_tiles,),
        in_specs=[pl.BlockSpec((TM, D_MODEL), lambda i, d: (i, 0)),
                  pl.BlockSpec((None, 1, D_MODEL), lambda i, d: (grp(i), 0, 5)),
                  pl.BlockSpec((TM, LANES), lambda i, d: (i, 0)),
                  pl.BlockSpec((1, D_MODEL), lambda i, d: (0, 0)),
                  pl.BlockSpec(memory_space=pl.ANY)],
        out_specs=pl.BlockSpec((TM, D_MODEL), lambda i, d: (i, 0)),
        scratch_shapes=[pltpu.VMEM((2, 2 * TM, D_MODEL), F32), pltpu.SemaphoreType.DMA((2,))],
    )
    return pl.pallas_call(
        functools.partial(_combine_kernel, final=final),
        grid_spec=grid_spec,
        out_shape=jax.ShapeDtypeStruct((n_tok_tiles * TM, D_MODEL), F32),
        compiler_params=_cparams(1),
        name="moe_combine",
    )(dest, h, mod3, route, final_g, y)


def _moe_layer(h, g2, mod3, w_r, wg, wu, wd, final_g, layer, moe_layer, *, final, **geo):
    T = h.shape[0]
    n_tiles = (2 * T) // TG + N_EXPERTS
    route, cnt, xs_zero = _router(h, g2, mod3, w_r, layer, n_tiles * TG, **geo)

    counts = cnt[0, :N_EXPERTS].astype(jnp.int32)
    padded = ((counts + TG - 1) // TG) * TG
    ends = jnp.cumsum(padded)
    offs = ends - padded
    e12 = route[:, ROUTE_E1:ROUTE_E2 + 1].astype(jnp.int32)
    r12 = route[:, ROUTE_R1:ROUTE_R2 + 1].astype(jnp.int32)
    onehot = e12[:, :, None] == jnp.arange(N_EXPERTS)[None, None, :]
    dest = (jnp.sum(jnp.where(onehot, offs[None, None, :], 0), axis=-1) + r12).reshape(2 * T)
    n_used = (ends[-1] // TG).reshape(1)
    tiles = jnp.arange(n_tiles)
    te_raw = jnp.sum(tiles[:, None] >= (ends // TG)[None, :], axis=1)
    tile_expert = jnp.minimum(te_raw, N_EXPERTS - 1).astype(jnp.int32)
    sel = tile_expert[:, None] == jnp.arange(N_EXPERTS)[None, :]
    cnt_k = jnp.sum(jnp.where(sel, counts[None, :], 0), axis=1)
    off_k = jnp.sum(jnp.where(sel, offs[None, :], 0), axis=1)
    tile_rows = jnp.where(te_raw < N_EXPERTS, jnp.clip(cnt_k - (tiles * TG - off_k), 0, TG), 0)

    xs = _dispatch(dest, h, g2, mod3, xs_zero, layer, **geo)
    y = _experts(tile_expert, n_used, tile_rows.astype(jnp.int32), xs, wg, wu, wd, moe_layer, n_tiles)
    return _combine(dest, h, mod3, route, final_g, y, final=final, **geo)


def _rope_tables(seq):
    t = jnp.arange(seq)
    inv = ROPE_THETA ** (-jnp.arange(ROPE_FREQS, dtype=F32) / ROPE_FREQS)
    ang_r = (t // GRID_W).astype(F32)[:, None] * inv
    ang_c = (t % GRID_W).astype(F32)[:, None] * inv
    cos = jnp.concatenate([jnp.cos(ang_r)] * 2 + [jnp.cos(ang_c)] * 2, axis=1)
    sin = jnp.concatenate([-jnp.sin(ang_r), jnp.sin(ang_r), -jnp.sin(ang_c), jnp.sin(ang_c)], axis=1)
    cos = jnp.concatenate([cos, jnp.ones((TM, HEAD_DIM), F32)], axis=0)
    sin = jnp.concatenate([sin, jnp.zeros((TM, HEAD_DIM), F32)], axis=0)
    return cos, sin


def kernel(x, c, ctx, c_ctx, w_mod, b_mod, norm1_g, norm2_g, w_in, attn_sink, conv_w, conv_b, gate_a_w, gate_a_b, gate_x_w, gate_x_b, lru_lambda, w_o_attn, w_o_lru, w_out, ff_w_gate, ff_w_up, ff_w_down, router_w, exp_w_gate, exp_w_up, exp_w_down, final_g):
    n_batch, seq, _ = x.shape
    ctx_len = ctx.shape[1]
    assert n_batch * ctx_len == TM and seq % TM == 0 and n_batch + 1 <= MOD_ROWS
    n_lat = n_batch * seq
    tiles_per_batch = seq // TM
    geo = dict(tiles_per_batch=tiles_per_batch, n_batch=n_batch)
    shp = dict(n_batch=n_batch, seq=seq, ctx_len=ctx_len)

    cpad = jnp.zeros((MOD_ROWS, D_MODEL), F32).at[:n_batch].set(c).at[n_batch].set(c_ctx)
    mod = _modulation(cpad, w_mod, b_mod)
    cos_t, sin_t = _rope_tables(seq)
    perm = _time_permutation(TM)
    stream = (x.reshape(n_lat, D_MODEL), ctx.reshape(n_batch * ctx_len, D_MODEL), 0)

    g1 = norm1_g.reshape(DEPTH, 1, D_MODEL)
    g2 = norm2_g.reshape(DEPTH, 1, D_MODEL)
    w_in_b = w_in.astype(BF16)
    woa_b, wol_b, wout_b = w_o_attn.astype(BF16), w_o_lru.astype(BF16), w_out.astype(BF16)
    ffg_b, ffu_b, ffd_b = ff_w_gate.astype(BF16), ff_w_up.astype(BF16), ff_w_down.astype(BF16)

    for l in range(DEPTH):
        mod3 = mod[l].reshape(MOD_ROWS, 1, 6 * D_MODEL)
        if l > 0:
            stream = (h, h, n_lat // TM)
        q, kv, xr, gy, sga, sgr = _in_proj(stream, g1, mod3, w_in_b, l, cos_t, sin_t, perm,
                                           n_lat_tiles=n_lat // TM, **geo)
        attn = _latent_attention(attn_sink[l], q, kv, **shp)
        attn_ctx = _context_attention(attn_sink[l], q, kv, **shp)
        wcat = jnp.concatenate([gate_a_w[l], gate_x_w[l]], axis=-1).astype(BF16)
        gate_b = jnp.stack([gate_a_b[l], gate_x_b[l]], axis=1)
        hf, hb = _rglru(xr, conv_w[l], conv_b[l].reshape(1, D_RNN), wcat, gate_b, lru_lambda[l], **shp)
        h = _merge(stream, attn, attn_ctx, hf, hb, gy, sga, sgr, mod3, woa_b, wol_b, wout_b, l, **geo)
        i = l // 2
        if l % 2 == 0:
            h = _ffn(h, g2, mod3, ffg_b, ffu_b, ffd_b, l, i, **geo)
        else:
            w_r = jnp.zeros((D_MODEL, LANES), F32).at[:, :N_EXPERTS].set(router_w[i])
            h = _moe_layer(h, g2, mod3, w_r, exp_w_gate, exp_w_up, exp_w_down,
                           final_g.reshape(1, D_MODEL), l, i, final=(l == DEPTH - 1), **geo)

    assert DEPTH % 2 == 0 and h.shape[0] == n_lat
    return h.reshape(n_batch, seq, D_MODEL)
```

```python
import functools

import jax
import jax.numpy as jnp
from jax import lax
from jax.experimental import pallas as pl
from jax.experimental.pallas import tpu as pltpu

F32 = jnp.float32
BF16 = jnp.bfloat16

D_MODEL = 1024
DEPTH = 4
GRID_W = 64
N_HEADS = 8
N_KV_HEADS = 2
HEAD_DIM = 128
N_GROUPS = N_HEADS // N_KV_HEADS
ATTN_BLOCK = 128
ROPE_THETA = 10000.0
ROPE_FREQS = HEAD_DIM // 4
D_RNN = 1024
N_RNN_BLOCKS = 8
RNN_BLOCK = D_RNN // N_RNN_BLOCKS
LRU_C = 8.0
D_FF = 2816
N_EXPERTS = 8
D_FF_EXPERT = 3584
EPS = 1e-6
NEG_INF = -1e30
Q_COLS = N_HEADS * HEAD_DIM
KV_COLS = N_KV_HEADS * HEAD_DIM
IN_COLS = Q_COLS + 2 * KV_COLS + 2 * D_RNN + 2 * D_MODEL
LOG2E = 1.4426950408889634
ATTN_SCALE = HEAD_DIM ** -0.5 * LOG2E

LANES = 128
SUBLANES = 8
TM = 512
TN_IN = 512
TQ = 1024
ATTN_UNROLL = 4
TT = 256
SUBSEQ = TT // SUBLANES
TG = 1024
TG_SUB = 512
MOE_F_CHUNKS = 7
ROW_DMA_UNROLL = 8
MOD_ROWS = 8
VMEM_LIMIT = 56 * 1024 * 1024


def _cparams(n_axes):
    return pltpu.CompilerParams(dimension_semantics=("arbitrary",) * n_axes,
                                vmem_limit_bytes=VMEM_LIMIT)


def _sigmoid(z):
    return 0.5 * jnp.tanh(0.5 * z) + 0.5


def _norm_mod(h, g, shift, scale):
    ms = jnp.mean(h * h, axis=-1, keepdims=True)
    y = h * lax.rsqrt(ms + EPS) * g
    return y * (1.0 + scale) + shift


def _mod_kernel(c_ref, w_ref, b_ref, o_ref):
    cv = c_ref[...]
    s = cv * _sigmoid(cv)
    o_ref[...] = _dot_split(s, w_ref[...]) + b_ref[...]


def _modulation(cpad, w_mod, b_mod):
    nchunk = 6
    return pl.pallas_call(
        _mod_kernel,
        grid=(DEPTH, nchunk),
        in_specs=[
            pl.BlockSpec((MOD_ROWS, D_MODEL), lambda l, n: (0, 0)),
            pl.BlockSpec((None, D_MODEL, D_MODEL), lambda l, n: (l, 0, n)),
            pl.BlockSpec((None, 1, D_MODEL), lambda l, n: (l, 0, n)),
        ],
        out_specs=pl.BlockSpec((None, MOD_ROWS, D_MODEL), lambda l, n: (l, 0, n)),
        out_shape=jax.ShapeDtypeStruct((DEPTH, MOD_ROWS, 6 * D_MODEL), F32),
        compiler_params=_cparams(2),
        name="modulation",
    )(cpad, w_mod, b_mod.reshape(DEPTH, 1, 6 * D_MODEL))


def _rope(xh, cos, sin_signed, first_half):
    sw = jnp.where(first_half, pltpu.roll(xh, 96, 1), pltpu.roll(xh, 32, 1))
    return xh * cos + sw * sin_signed


def _stream_tile(h_lat_ref, h_ctx_ref):
    is_ctx_tile = pl.program_id(0) == pl.num_programs(0) - 1
    return jnp.where(is_ctx_tile, h_ctx_ref[...], h_lat_ref[...])


def _stream_specs(stream, n_lat_tiles):
    _, _, ctx_block = stream
    return [pl.BlockSpec((TM, D_MODEL), lambda i: (jnp.minimum(i, n_lat_tiles - 1), 0)),
            pl.BlockSpec((TM, D_MODEL), lambda i: (ctx_block, 0))]


def _in_proj_kernel(h_lat_ref, h_ctx_ref, g_ref, sh_ref, sc_ref, w_ref, cos_ref, sin_ref, perm_ref,
                    q_ref, kv_ref, x_ref, gy_ref, sga_ref, sgr_ref):
    n = _norm_mod(_stream_tile(h_lat_ref, h_ctx_ref), g_ref[...], sh_ref[...], sc_ref[...]).astype(BF16)
    cos = cos_ref[...]
    sin = sin_ref[...]
    lane = lax.broadcasted_iota(jnp.int32, (TM, LANES), 1)
    first_half = (lane & 32) == 0

    def proj(chunk):
        return jnp.dot(n, w_ref[:, chunk * TN_IN:(chunk + 1) * TN_IN], preferred_element_type=F32)

    def rope_store(acc, n_heads, scale, ref, col0):
        for hh in range(n_heads):
            y = _rope(acc[:, hh * HEAD_DIM:(hh + 1) * HEAD_DIM], cos, sin, first_half)
            if scale != 1.0:
                y = y * scale
            ref[:, col0 + hh * HEAD_DIM:col0 + (hh + 1) * HEAD_DIM] = y.astype(BF16)

    for c in range(2):
        rope_store(proj(c), 4, ATTN_SCALE, q_ref, c * TN_IN)
    acc = proj(2)
    rope_store(acc, 2, 1.0, kv_ref, 0)
    kv_ref[:, KV_COLS:] = acc[:, KV_COLS:].astype(BF16)
    n_perm = jnp.dot(perm_ref[...], n, preferred_element_type=F32).astype(BF16)
    for c in range(2):
        cols = slice(c * TN_IN, (c + 1) * TN_IN)
        x_ref[:, cols] = jnp.dot(n_perm, w_ref[:, (3 + c) * TN_IN:(4 + c) * TN_IN],
                                 preferred_element_type=F32)
        gy_ref[:, cols] = jax.nn.gelu(proj(5 + c)).astype(BF16)
        sga_ref[:, cols] = _sigmoid(proj(7 + c)).astype(BF16)
        sgr_ref[:, cols] = _sigmoid(proj(9 + c)).astype(BF16)


def _in_proj(stream, g, mod3, w_in, layer, cos_t, sin_t, perm, *, n_lat_tiles, tiles_per_batch, n_batch):
    T = (n_lat_tiles + 1) * TM

    def grp(i):
        return jnp.minimum(i // tiles_per_batch, n_batch)

    def pos_tile(i):
        return jnp.where(i < n_lat_tiles, i % tiles_per_batch, tiles_per_batch)

    def tok(width):
        return pl.BlockSpec((TM, width), lambda i: (i, 0))

    out_shape = (
        jax.ShapeDtypeStruct((T, Q_COLS), BF16),
        jax.ShapeDtypeStruct((T, 2 * KV_COLS), BF16),
        jax.ShapeDtypeStruct((T, D_RNN), F32),
        jax.ShapeDtypeStruct((T, D_RNN), BF16),
        jax.ShapeDtypeStruct((T, D_MODEL), BF16),
        jax.ShapeDtypeStruct((T, D_MODEL), BF16),
    )
    return pl.pallas_call(
        _in_proj_kernel,
        grid=(T // TM,),
        in_specs=_stream_specs(stream, n_lat_tiles) + [
            pl.BlockSpec((None, 1, D_MODEL), lambda i: (layer, 0, 0)),
            pl.BlockSpec((None, 1, D_MODEL), lambda i: (grp(i), 0, 0)),
            pl.BlockSpec((None, 1, D_MODEL), lambda i: (grp(i), 0, 1)),
            pl.BlockSpec((None, D_MODEL, IN_COLS), lambda i: (layer, 0, 0)),
            pl.BlockSpec((TM, HEAD_DIM), lambda i: (pos_tile(i), 0)),
            pl.BlockSpec((TM, HEAD_DIM), lambda i: (pos_tile(i), 0)),
            pl.BlockSpec((TM, TM), lambda i: (0, 0)),
        ],
        out_specs=(tok(Q_COLS), tok(2 * KV_COLS), tok(D_RNN), tok(D_RNN), tok(D_MODEL), tok(D_MODEL)),
        out_shape=out_shape,
        compiler_params=_cparams(1),
        name="in_proj",
    )(stream[0], stream[1], g, mod3, mod3, w_in, cos_t, sin_t, perm)


def _stack_heads(qt):
    return jnp.concatenate([qt[:, g * HEAD_DIM:(g + 1) * HEAD_DIM] for g in range(N_GROUPS)], axis=0)


def _sink_col(sink_ref, kvh, rows):
    return jnp.concatenate(
        [jnp.full((rows, 1), sink_ref[kvh * N_GROUPS + g] * LOG2E, F32) for g in range(N_GROUPS)], axis=0)


def _ones_column(n_keys):
    lane = lax.broadcasted_iota(jnp.int32, (n_keys, HEAD_DIM), 1)
    return jnp.where(lane == 0, 1.0, 0.0).astype(BF16)


def _softmax_pv(s, sink, vall):
    m = jnp.maximum(jnp.max(s, axis=1, keepdims=True), sink)
    p = jnp.exp2(s - m).astype(BF16)
    v_aug = jnp.concatenate([vall, _ones_column(vall.shape[0])], axis=1)
    oa = jnp.dot(p, v_aug, preferred_element_type=F32)
    denom = oa[:, HEAD_DIM:HEAD_DIM + 1] + jnp.exp2(sink - m)
    return oa[:, :HEAD_DIM] / denom


def _attn_kernel(sink_ref, q_ref, k_ref, v_ref, kc_ref, vc_ref, o_ref, *, n_blocks):
    kvh = pl.program_id(1)
    i = pl.program_id(2)
    rows = ATTN_BLOCK * N_GROUPS
    qi = lax.broadcasted_iota(jnp.int32, (rows, ATTN_BLOCK), 0) & (ATTN_BLOCK - 1)
    kj = lax.broadcasted_iota(jnp.int32, (rows, ATTN_BLOCK), 1)
    tri_prev = kj >= qi
    tri_next = kj <= qi
    sink = _sink_col(sink_ref, kvh, ATTN_BLOCK)
    kc = kc_ref[...]
    vc = vc_ref[...]

    def body(qb, carry):
        n = i * (TQ // ATTN_BLOCK) + qb
        r0 = pl.multiple_of(qb * ATTN_BLOCK, ATTN_BLOCK)
        p0 = pl.multiple_of(jnp.maximum(n - 1, 0) * ATTN_BLOCK, ATTN_BLOCK)
        c0 = pl.multiple_of(n * ATTN_BLOCK, ATTN_BLOCK)
        n0 = pl.multiple_of(jnp.minimum(n + 1, n_blocks - 1) * ATTN_BLOCK, ATTN_BLOCK)
        qs = _stack_heads(q_ref[pl.ds(r0, ATTN_BLOCK), :])
        kall = jnp.concatenate([k_ref[pl.ds(p0, ATTN_BLOCK), :], k_ref[pl.ds(c0, ATTN_BLOCK), :],
                                k_ref[pl.ds(n0, ATTN_BLOCK), :], kc], axis=0)
        vall = jnp.concatenate([v_ref[pl.ds(p0, ATTN_BLOCK), :], v_ref[pl.ds(c0, ATTN_BLOCK), :],
                                v_ref[pl.ds(n0, ATTN_BLOCK), :], vc], axis=0)
        s = lax.dot_general(qs, kall, (((1,), (1,)), ((), ())), preferred_element_type=F32)
        pen_prev = jnp.where(n > 0, 0.0, NEG_INF)
        pen_next = jnp.where(n < n_blocks - 1, 0.0, NEG_INF)
        sp = jnp.where(tri_prev, s[:, :ATTN_BLOCK] + pen_prev, NEG_INF)
        sn = jnp.where(tri_next, s[:, 2 * ATTN_BLOCK:3 * ATTN_BLOCK] + pen_next, NEG_INF)
        s = jnp.concatenate([sp, s[:, ATTN_BLOCK:2 * ATTN_BLOCK], sn, s[:, 3 * ATTN_BLOCK:]], axis=1)
        o = _softmax_pv(s, sink, vall)
        for g in range(N_GROUPS):
            o_ref[pl.ds(r0, ATTN_BLOCK), g * HEAD_DIM:(g + 1) * HEAD_DIM] = (
                o[g * ATTN_BLOCK:(g + 1) * ATTN_BLOCK].astype(BF16))
        return carry

    lax.fori_loop(0, TQ // ATTN_BLOCK, body, 0, unroll=ATTN_UNROLL)


def _latent_attention(sink, q, kv, *, n_batch, seq, ctx_len):
    gw = N_GROUPS * HEAD_DIM
    tiles_per_batch = seq // TQ
    ctx_blk0 = n_batch * seq // ctx_len
    grid_spec = pltpu.PrefetchScalarGridSpec(
        num_scalar_prefetch=1,
        grid=(n_batch, N_KV_HEADS, tiles_per_batch),
        in_specs=[
            pl.BlockSpec((TQ, gw), lambda b, k, i, s: (b * tiles_per_batch + i, k)),
            pl.BlockSpec((seq, HEAD_DIM), lambda b, k, i, s: (b, k)),
            pl.BlockSpec((seq, HEAD_DIM), lambda b, k, i, s: (b, N_KV_HEADS + k)),
            pl.BlockSpec((ctx_len, HEAD_DIM), lambda b, k, i, s: (ctx_blk0 + b, k)),
            pl.BlockSpec((ctx_len, HEAD_DIM), lambda b, k, i, s: (ctx_blk0 + b, N_KV_HEADS + k)),
        ],
        out_specs=pl.BlockSpec((TQ, gw), lambda b, k, i, s: (b * tiles_per_batch + i, k)),
    )
    return pl.pallas_call(
        functools.partial(_attn_kernel, n_blocks=seq // ATTN_BLOCK),
        grid_spec=grid_spec,
        out_shape=jax.ShapeDtypeStruct((n_batch * seq, Q_COLS), BF16),
        compiler_params=_cparams(3),
        name="latent_attention",
    )(sink, q, kv, kv, kv, kv)


def _ctx_attn_kernel(sink_ref, q_ref, kc_ref, vc_ref, o_ref):
    kvh = pl.program_id(1)
    rows = q_ref.shape[0]
    qs = _stack_heads(q_ref[...])
    s = lax.dot_general(qs, kc_ref[...], (((1,), (1,)), ((), ())), preferred_element_type=F32)
    o = _softmax_pv(s, _sink_col(sink_ref, kvh, rows), vc_ref[...])
    for g in range(N_GROUPS):
        o_ref[:, g * HEAD_DIM:(g + 1) * HEAD_DIM] = o[g * rows:(g + 1) * rows].astype(BF16)


def _context_attention(sink, q, kv, *, n_batch, seq, ctx_len):
    gw = N_GROUPS * HEAD_DIM
    ctx_blk0 = n_batch * seq // ctx_len
    grid_spec = pltpu.PrefetchScalarGridSpec(
        num_scalar_prefetch=1,
        grid=(n_batch, N_KV_HEADS),
        in_specs=[
            pl.BlockSpec((ctx_len, gw), lambda b, k, s: (ctx_blk0 + b, k)),
            pl.BlockSpec((ctx_len, HEAD_DIM), lambda b, k, s: (ctx_blk0 + b, k)),
            pl.BlockSpec((ctx_len, HEAD_DIM), lambda b, k, s: (ctx_blk0 + b, N_KV_HEADS + k)),
        ],
        out_specs=pl.BlockSpec((ctx_len, gw), lambda b, k, s: (b, k)),
    )
    return pl.pallas_call(
        _ctx_attn_kernel,
        grid_spec=grid_spec,
        out_shape=jax.ShapeDtypeStruct((n_batch * ctx_len, Q_COLS), BF16),
        compiler_params=_cparams(2),
        name="context_attention",
    )(sink, q, kv, kv)


def _softplus(z):
    return jnp.maximum(z, 0.0) + jnp.log1p(jnp.exp(-jnp.abs(z)))


def _scan8(a, b, row, reverse):
    for s in (1, 2, 4):
        if reverse:
            keep = row < SUBLANES - s
            shift = SUBLANES - s
        else:
            keep = row >= s
            shift = s
        a_sh = jnp.where(keep, pltpu.roll(a, shift, 0), 1.0)
        b_sh = jnp.where(keep, pltpu.roll(b, shift, 0), 0.0)
        b = a * b_sh + b
        a = a * a_sh
    return a, b


def _rglru_kernel(xf_ref, xfp_ref, xfn_ref, xb_ref, xbp_ref, xbn_ref, cw_ref, cb_ref, wcat_ref,
                  gb_ref, lam_ref, unperm_ref, hf_ref, hb_ref, a_s, b_s, h_bf, carry):
    j = pl.program_id(1)
    last_j = pl.num_programs(1) - 1

    @pl.when(j == 0)
    def _():
        carry[...] = jnp.zeros_like(carry)

    seg_start = (j <= 1, (j == 0) | (j == last_j))
    seg_end = ((j == 0) | (j == last_j), j <= 1)
    mains = (xf_ref, xb_ref)
    prevs = (xfp_ref, xbp_ref)
    nexts = (xfn_ref, xbn_ref)
    sub = lax.broadcasted_iota(jnp.int32, (SUBLANES, D_RNN), 0)
    S8 = SUBLANES

    for d in range(2):
        x0 = mains[d][...]
        halo = prevs[d][...]
        t_m1 = jnp.where(seg_start[d], 0.0, halo[2 * S8 - 1:2 * S8, :])
        t_m2 = jnp.where(seg_start[d], 0.0, halo[S8 - 1:S8, :])
        t_p1 = jnp.where(seg_end[d], 0.0, nexts[d][0:1, :])
        g_m1 = jnp.where(sub == 0, t_m1, pltpu.roll(x0[TT - S8:, :], 1, 0))
        g_m2 = jnp.where(sub == 0, t_m2, pltpu.roll(x0[TT - 2 * S8:TT - S8, :], 1, 0))
        g_p1 = jnp.where(sub == S8 - 1, t_p1, pltpu.roll(x0[:S8, :], S8 - 1, 0))
        xc = cb_ref[...] + jnp.concatenate([g_m2, g_m1, x0[:TT - 2 * S8, :]], axis=0) * cw_ref[0:1, :]
        xc = xc + jnp.concatenate([g_m1, x0[:TT - S8, :]], axis=0) * cw_ref[1:2, :]
        xc = xc + x0 * cw_ref[2:3, :]
        xc = xc + jnp.concatenate([x0[S8:, :], g_p1], axis=0) * cw_ref[3:4, :]
        c_d = (-LRU_C * LOG2E) * _softplus(-lam_ref[d:d + 1, :])
        for blk in range(N_RNN_BLOCKS):
            sl = slice(blk * RNN_BLOCK, (blk + 1) * RNN_BLOCK)
            xcb = xc[:, sl]
            z = jnp.dot(xcb.astype(BF16), wcat_ref[d, blk], preferred_element_type=F32)
            r = _sigmoid(z[:, :RNN_BLOCK] + gb_ref[d, 0:1, sl])
            ig = _sigmoid(z[:, RNN_BLOCK:] + gb_ref[d, 1:2, sl])
            a = jnp.exp2(r * c_d[:, sl])
            v = 1.0 - a * a
            root = jnp.where(v > 0.0, v * lax.rsqrt(v), 0.0)
            a_s[d, :, sl] = a
            b_s[d, :, sl] = root * (ig * xcb)

    def local(g, c):
        hf, pf, hb, pb = c
        rf = pl.multiple_of(g * S8, S8)
        rb = pl.multiple_of((SUBSEQ - 1 - g) * S8, S8)
        af = a_s[0, pl.ds(rf, S8), :]
        ab = a_s[1, pl.ds(rb, S8), :]
        hf = af * hf + b_s[0, pl.ds(rf, S8), :]
        hb = ab * hb + b_s[1, pl.ds(rb, S8), :]
        pf = af * pf
        pb = ab * pb
        b_s[0, pl.ds(rf, S8), :] = hf
        b_s[1, pl.ds(rb, S8), :] = hb
        a_s[0, pl.ds(rf, S8), :] = pf
        a_s[1, pl.ds(rb, S8), :] = pb
        return hf, pf, hb, pb

    zero = jnp.zeros((S8, D_RNN), F32)
    one = jnp.ones((S8, D_RNN), F32)
    hf, pf, hb, pb = lax.fori_loop(0, SUBSEQ, local, (zero, one, zero, one), unroll=2)

    af, bf = _scan8(pf, hf, sub, False)
    endf = af * carry[0] + bf
    h_in = [jnp.where(sub == 0, carry[0], pltpu.roll(endf, 1, 0))]
    carry[0] = jnp.broadcast_to(endf[S8 - 1:S8, :], (S8, D_RNN))
    ab, bb = _scan8(pb, hb, sub, True)
    endb = ab * carry[1] + bb
    h_in.append(jnp.where(sub == S8 - 1, carry[1], pltpu.roll(endb, S8 - 1, 0)))
    carry[1] = jnp.broadcast_to(endb[0:1, :], (S8, D_RNN))

    outs = (hf_ref, hb_ref)
    for d in range(2):
        h_in2 = jnp.concatenate([h_in[d], h_in[d]], axis=0)

        def fix(k, c, d=d, h_in2=h_in2):
            rows = pl.ds(pl.multiple_of(k * 2 * S8, 2 * S8), 2 * S8)
            h_bf[d, rows, :] = (b_s[d, rows, :] + a_s[d, rows, :] * h_in2).astype(BF16)
            return c

        lax.fori_loop(0, SUBSEQ // 2, fix, 0, unroll=2)
        outs[d][...] = jnp.dot(unperm_ref[...], h_bf[d], preferred_element_type=F32).astype(BF16)


def _time_permutation(n_rows):
    p = jnp.arange(n_rows)
    src = (p // TT) * TT + (p % SUBLANES) * SUBSEQ + (p % TT) // SUBLANES
    return (src[:, None] == jnp.arange(n_rows)[None, :]).astype(BF16)


def _rglru(x, conv_w, conv_b, wcat, gate_b, lam, *, n_batch, seq, ctx_len):
    T = x.shape[0]
    assert ctx_len == TT and seq % TT == 0
    tps = seq // TT
    n_lat_t = n_batch * tps
    halo_rows = 2 * SUBLANES
    per_halo = TT // halo_rows
    per8 = TT // SUBLANES
    last8 = T // SUBLANES - 1
    unperm = _time_permutation(TT).T

    def ftile(b, j):
        return jnp.where(j == 0, n_lat_t + b, b * tps + j - 1)

    def btile(b, j):
        return jnp.where(j == 0, n_lat_t + b, b * tps + tps - j)

    def main(tile):
        return pl.BlockSpec((TT, D_RNN), lambda b, j: (tile(b, j), 0))

    def prev(tile):
        return pl.BlockSpec((halo_rows, D_RNN), lambda b, j: (jnp.maximum(tile(b, j) * per_halo - 1, 0), 0))

    def nxt(tile):
        return pl.BlockSpec((SUBLANES, D_RNN),
                            lambda b, j: (jnp.minimum((tile(b, j) + 1) * per8, last8), 0))

    def const(shape):
        return pl.BlockSpec(shape, lambda b, j: (0,) * len(shape))

    return pl.pallas_call(
        _rglru_kernel,
        grid=(n_batch, tps + 1),
        in_specs=[main(ftile), prev(ftile), nxt(ftile), main(btile), prev(btile), nxt(btile),
                  const(conv_w.shape), const(conv_b.shape), const(wcat.shape), const(gate_b.shape),
                  const(lam.shape), const(unperm.shape)],
        out_specs=(main(ftile), main(btile)),
        out_shape=(jax.ShapeDtypeStruct((T, D_RNN), BF16), jax.ShapeDtypeStruct((T, D_RNN), BF16)),
        scratch_shapes=[pltpu.VMEM((2, TT, D_RNN), F32),
                        pltpu.VMEM((2, TT, D_RNN), F32),
                        pltpu.VMEM((2, TT, D_RNN), BF16),
                        pltpu.VMEM((2, SUBLANES, D_RNN), F32)],
        compiler_params=_cparams(2),
        name="rglru",
    )(x, x, x, x, x, x, conv_w, conv_b, wcat, gate_b, lam, unperm)


def _merge_kernel(h_lat_ref, h_ctx_ref, attn_ref, attn_ctx_ref, hf_ref, hb_ref, gy_ref, sga_ref, sgr_ref,
                  g1_ref, woa_ref, wol_ref, wout_ref, o_ref):
    rec = ((hf_ref[...].astype(F32) + hb_ref[...].astype(F32)) * gy_ref[...].astype(F32)).astype(BF16)
    is_ctx_tile = pl.program_id(0) == pl.num_programs(0) - 1
    attn = jnp.where(is_ctx_tile, attn_ctx_ref[...], attn_ref[...])
    ta = jnp.dot(attn, woa_ref[...], preferred_element_type=F32)
    tl = jnp.dot(rec, wol_ref[...], preferred_element_type=F32)
    m = sga_ref[...].astype(F32) * ta + sgr_ref[...].astype(F32) * tl
    y = jnp.dot(m.astype(BF16), wout_ref[...], preferred_element_type=F32)
    o_ref[...] = _stream_tile(h_lat_ref, h_ctx_ref) + g1_ref[...] * y


def _merge(stream, attn, attn_ctx, hf, hb, gy, sga, sgr, mod3, woa, wol, wout, layer, *,
           tiles_per_batch, n_batch):
    n_lat_tiles = attn.shape[0] // TM
    T = (n_lat_tiles + 1) * TM
    assert attn_ctx.shape[0] == TM and hf.shape[0] == T

    def grp(i):
        return jnp.minimum(i // tiles_per_batch, n_batch)

    tok = pl.BlockSpec((TM, D_MODEL), lambda i: (i, 0))
    wsp = pl.BlockSpec((None, D_MODEL, D_MODEL), lambda i: (layer, 0, 0))
    return pl.pallas_call(
        _merge_kernel,
        grid=(T // TM,),
        in_specs=_stream_specs(stream, n_lat_tiles) + [
                  pl.BlockSpec((TM, Q_COLS), lambda i: (jnp.minimum(i, n_lat_tiles - 1), 0)),
                  pl.BlockSpec((TM, Q_COLS), lambda i: (0, 0)),
                  tok, tok, tok, tok, tok,
                  pl.BlockSpec((None, 1, D_MODEL), lambda i: (grp(i), 0, 2)),
                  wsp, wsp, wsp],
        out_specs=tok,
        out_shape=jax.ShapeDtypeStruct((T, D_MODEL), F32),
        compiler_params=_cparams(1),
        name="merge",
    )(stream[0], stream[1], attn, attn_ctx, hf, hb, gy, sga, sgr, mod3, woa, wol, wout)


def _swiglu_partial(n, wg, wu, wd):
    gt = jnp.dot(n, wg, preferred_element_type=F32)
    ut = jnp.dot(n, wu, preferred_element_type=F32)
    act = (gt * _sigmoid(gt) * ut).astype(BF16)
    return jnp.dot(act, wd, preferred_element_type=F32)


FFN_F_CHUNKS = 2


def _ffn_kernel(h_ref, g_ref, sh_ref, sc_ref, g2_ref, wg_ref, wu_ref, wd_ref, o_ref):
    h = h_ref[...]
    n = _norm_mod(h, g_ref[...], sh_ref[...], sc_ref[...]).astype(BF16)
    tf = D_FF // FFN_F_CHUNKS
    acc = None
    for c in range(FFN_F_CHUNKS):
        cols = slice(c * tf, (c + 1) * tf)
        part = _swiglu_partial(n, wg_ref[:, cols], wu_ref[:, cols], wd_ref[cols, :])
        acc = part if acc is None else acc + part
    o_ref[...] = h + g2_ref[...] * acc


def _ffn(h, g, mod3, wg, wu, wd, layer, ff_layer, *, tiles_per_batch, n_batch):
    T = h.shape[0]
    assert (D_FF // FFN_F_CHUNKS) % LANES == 0

    def grp(i):
        return jnp.minimum(i // tiles_per_batch, n_batch)

    def modspec(k):
        return pl.BlockSpec((None, 1, D_MODEL), lambda i: (grp(i), 0, k))

    tok = pl.BlockSpec((TM, D_MODEL), lambda i: (i, 0))
    return pl.pallas_call(
        _ffn_kernel,
        grid=(T // TM,),
        in_specs=[tok, pl.BlockSpec((None, 1, D_MODEL), lambda i: (layer, 0, 0)),
                  modspec(3), modspec(4), modspec(5),
                  pl.BlockSpec((None, D_MODEL, D_FF), lambda i: (ff_layer, 0, 0)),
                  pl.BlockSpec((None, D_MODEL, D_FF), lambda i: (ff_layer, 0, 0)),
                  pl.BlockSpec((None, D_FF, D_MODEL), lambda i: (ff_layer, 0, 0))],
        out_specs=tok,
        out_shape=jax.ShapeDtypeStruct((T, D_MODEL), F32),
        compiler_params=_cparams(1),
        name="dense_ffn",
    )(h, g, mod3, mod3, mod3, wg, wu, wd)


ROUTE_E1, ROUTE_E2, ROUTE_W1, ROUTE_W2, ROUTE_R1, ROUTE_R2 = range(6)


def _dot_split(a, b):
    a_hi = a.astype(BF16)
    a_lo = (a - a_hi.astype(F32)).astype(BF16)
    b_hi = b.astype(BF16)
    b_lo = (b - b_hi.astype(F32)).astype(BF16)

    def mm(x, y):
        return jnp.dot(x, y, preferred_element_type=F32)

    return mm(a_hi, b_hi) + (mm(a_hi, b_lo) + mm(a_lo, b_hi))


def _router_kernel(h_ref, g_ref, sh_ref, sc_ref, wr_ref, route_ref, cnt_ref, zero_ref, run):
    @pl.when(pl.program_id(0) == 0)
    def _():
        run[...] = jnp.zeros_like(run)

    n = _norm_mod(h_ref[...], g_ref[...], sh_ref[...], sc_ref[...])
    logits = _dot_split(n, wr_ref[...])
    lane = lax.broadcasted_iota(jnp.int32, logits.shape, 1)
    logits = jnp.where(lane < N_EXPERTS, logits, -jnp.inf)
    m1 = jnp.max(logits, axis=1, keepdims=True)
    i1 = jnp.min(jnp.where(logits == m1, lane, LANES), axis=1, keepdims=True)
    rest = jnp.where(lane == i1, -jnp.inf, logits)
    m2 = jnp.max(rest, axis=1, keepdims=True)
    i2 = jnp.min(jnp.where(rest == m2, lane, LANES), axis=1, keepdims=True)
    e2 = jnp.exp(m2 - m1)
    w1 = 1.0 / (1.0 + e2)
    w2 = e2 / (1.0 + e2)

    hit1 = lane == i1
    hit2 = lane == i2
    onehot = jnp.where(hit1 | hit2, 1.0, 0.0)
    r_i = lax.broadcasted_iota(jnp.int32, (TM, TM), 0)
    c_i = lax.broadcasted_iota(jnp.int32, (TM, TM), 1)
    lower = jnp.where(c_i < r_i, 1.0, 0.0).astype(BF16)
    prefix = jnp.dot(lower, onehot.astype(BF16), preferred_element_type=F32) + run[0:1, :]
    rank1 = jnp.sum(jnp.where(hit1, prefix, 0.0), axis=1, keepdims=True)
    rank2 = jnp.sum(jnp.where(hit2, prefix, 0.0), axis=1, keepdims=True)
    run[...] = run[...] + jnp.sum(onehot, axis=0, keepdims=True)
    cnt_ref[...] = run[...]

    rec = jnp.zeros(logits.shape, F32)
    for k, v in ((ROUTE_E1, i1.astype(F32)), (ROUTE_E2, i2.astype(F32)), (ROUTE_W1, w1), (ROUTE_W2, w2),
                 (ROUTE_R1, rank1), (ROUTE_R2, rank2)):
        rec = jnp.where(lane == k, v, rec)
    route_ref[...] = rec
    zero_ref[...] = jnp.zeros_like(zero_ref)


def _router(h, g2, mod3, w_r, layer, sorted_rows, *, tiles_per_batch, n_batch):
    T = h.shape[0]
    n_steps = T // TM
    zero_rows = -(-sorted_rows // (n_steps * SUBLANES)) * SUBLANES

    def grp(i):
        return jnp.minimum(i // tiles_per_batch, n_batch)

    def modspec(k):
        return pl.BlockSpec((None, 1, D_MODEL), lambda i: (grp(i), 0, k))

    return pl.pallas_call(
        _router_kernel,
        grid=(T // TM,),
        in_specs=[pl.BlockSpec((TM, D_MODEL), lambda i: (i, 0)),
                  pl.BlockSpec((None, 1, D_MODEL), lambda i: (layer, 0, 0)),
                  modspec(3), modspec(4),
                  pl.BlockSpec((D_MODEL, LANES), lambda i: (0, 0))],
        out_specs=(pl.BlockSpec((TM, LANES), lambda i: (i, 0)),
                   pl.BlockSpec((SUBLANES, LANES), lambda i: (0, 0)),
                   pl.BlockSpec((zero_rows, D_MODEL), lambda i: (i, 0))),
        out_shape=(jax.ShapeDtypeStruct((T, LANES), F32), jax.ShapeDtypeStruct((SUBLANES, LANES), F32),
                   jax.ShapeDtypeStruct((n_steps * zero_rows, D_MODEL), F32)),
        scratch_shapes=[pltpu.VMEM((SUBLANES, LANES), F32)],
        compiler_params=_cparams(1),
        name="moe_router",
    )(h, g2, mod3, mod3, w_r)


def _row_copy(src_ref, src_row, dst_ref, dst_row, sem):
    return pltpu.make_async_copy(src_ref.at[pl.ds(src_row, 1)], dst_ref.at[pl.ds(dst_row, 1)], sem)


def _dispatch_kernel(dest_ref, h_ref, g_ref, sh_ref, sc_ref, xs_in_ref, xs_ref, n_scr, sem):
    del xs_in_ref
    i = pl.program_id(0)
    base = i * (2 * TM)
    cur = i % 2
    n_scr[cur] = _norm_mod(h_ref[...], g_ref[...], sh_ref[...], sc_ref[...])

    def start(r, carry):
        for s in range(2):
            _row_copy(n_scr.at[cur], r, xs_ref, dest_ref[base + 2 * r + s], sem.at[cur]).start(priority=s)
        return carry

    lax.fori_loop(0, TM, start, 0, unroll=ROW_DMA_UNROLL)

    def wait_all(buf):
        for s in range(2):
            pltpu.make_async_copy(n_scr.at[buf], xs_ref.at[pl.ds(0, TM)], sem.at[buf]).wait()

    @pl.when(i > 0)
    def _():
        wait_all(1 - cur)

    @pl.when(i == pl.num_programs(0) - 1)
    def _():
        wait_all(cur)


def _dispatch(dest, h, g2, mod3, xs_zero, layer, *, tiles_per_batch, n_batch):
    T = h.shape[0]

    def grp(i):
        return jnp.minimum(i // tiles_per_batch, n_batch)

    def modspec(k):
        return pl.BlockSpec((None, 1, D_MODEL), lambda i, d: (grp(i), 0, k))

    grid_spec = pltpu.PrefetchScalarGridSpec(
        num_scalar_prefetch=1,
        grid=(T // TM,),
        in_specs=[pl.BlockSpec((TM, D_MODEL), lambda i, d: (i, 0)),
                  pl.BlockSpec((None, 1, D_MODEL), lambda i, d: (layer, 0, 0)),
                  modspec(3), modspec(4),
                  pl.BlockSpec(memory_space=pl.ANY)],
        out_specs=pl.BlockSpec(memory_space=pl.ANY),
        scratch_shapes=[pltpu.VMEM((2, TM, D_MODEL), F32), pltpu.SemaphoreType.DMA((2,))],
    )
    return pl.pallas_call(
        _dispatch_kernel,
        grid_spec=grid_spec,
        out_shape=jax.ShapeDtypeStruct(xs_zero.shape, xs_zero.dtype),
        input_output_aliases={5: 0},
        compiler_params=_cparams(1),
        name="moe_dispatch",
    )(dest, h, g2, mod3, mod3, xs_zero)


def _expert_kernel(te_ref, nu_ref, rows_ref, x_ref, wg_ref, wu_ref, wd_ref, acc, xb):
    del te_ref, nu_ref
    k = pl.program_id(0)
    f = pl.program_id(1)

    @pl.when(f == 0)
    def _():
        xb[...] = x_ref[...].astype(BF16)
        acc[...] = jnp.zeros_like(acc)

    wg = wg_ref[...].astype(BF16)
    wu = wu_ref[...].astype(BF16)
    wd = wd_ref[...].astype(BF16)
    acc[:TG_SUB, :] += _swiglu_partial(xb[:TG_SUB, :], wg, wu, wd)
    for sub in range(1, TG // TG_SUB):
        rows = slice(sub * TG_SUB, (sub + 1) * TG_SUB)

        @pl.when(rows_ref[k] > sub * TG_SUB)
        def _():
            acc[rows, :] += _swiglu_partial(xb[rows, :], wg, wu, wd)


def _experts(tile_expert, n_used, tile_rows, xs, wg, wu, wd, layer, n_tiles):
    P = n_tiles * TG
    assert xs.shape[0] >= P
    n_f = MOE_F_CHUNKS
    tf = D_FF_EXPERT // n_f
    assert tf % LANES == 0

    def fsel(k, f, nu):
        return jnp.where(k < nu[0], f, n_f - 1)

    grid_spec = pltpu.PrefetchScalarGridSpec(
        num_scalar_prefetch=3,
        grid=(P // TG, n_f),
        in_specs=[pl.BlockSpec((TG, D_MODEL), lambda k, f, te, nu, tr: (k, 0)),
                  pl.BlockSpec((None, None, D_MODEL, tf), lambda k, f, te, nu, tr: (layer, te[k], 0, fsel(k, f, nu))),
                  pl.BlockSpec((None, None, D_MODEL, tf), lambda k, f, te, nu, tr: (layer, te[k], 0, fsel(k, f, nu))),
                  pl.BlockSpec((None, None, tf, D_MODEL), lambda k, f, te, nu, tr: (layer, te[k], fsel(k, f, nu), 0))],
        out_specs=pl.BlockSpec((TG, D_MODEL), lambda k, f, te, nu, tr: (k, 0)),
        scratch_shapes=[pltpu.VMEM((TG, D_MODEL), BF16)],
    )
    return pl.pallas_call(
        _expert_kernel,
        grid_spec=grid_spec,
        out_shape=jax.ShapeDtypeStruct((P, D_MODEL), F32),
        compiler_params=_cparams(2),
        name="moe_experts",
    )(tile_expert, n_used, tile_rows, xs, wg, wu, wd)


def _combine_kernel(dest_ref, h_ref, g2_ref, route_ref, fg_ref, y_ref, o_ref, ybuf, sem, *, final):
    i = pl.program_id(0)
    n_steps = pl.num_programs(0)

    def gather(tile, buf):
        base = tile * (2 * TM)

        def start(r, carry):
            for s in range(2):
                _row_copy(y_ref, dest_ref[base + 2 * r + s], ybuf.at[buf], s * TM + r,
                          sem.at[buf]).start(priority=s)
            return carry

        lax.fori_loop(0, TM, start, 0, unroll=ROW_DMA_UNROLL)

    @pl.when(i == 0)
    def _():
        gather(0, 0)

    @pl.when(i + 1 < n_steps)
    def _():
        gather(i + 1, (i + 1) % 2)

    cur = i % 2
    for s in range(2):
        pltpu.make_async_copy(y_ref.at[pl.ds(0, TM)], ybuf.at[cur, pl.ds(s * TM, TM)], sem.at[cur]).wait()
    route = route_ref[...]
    f = (route[:, ROUTE_W1:ROUTE_W1 + 1] * ybuf[cur, 0:TM, :]
         + route[:, ROUTE_W2:ROUTE_W2 + 1] * ybuf[cur, TM:, :])
    out = h_ref[...] + g2_ref[...] * f
    if final:
        ms = jnp.mean(out * out, axis=-1, keepdims=True)
        out = out * lax.rsqrt(ms + EPS) * fg_ref[...]
    o_ref[...] = out


def _combine(dest, h, mod3, route, final_g, y, *, final, tiles_per_batch, n_batch):
    n_tok_tiles = h.shape[0] // TM - (1 if final else 0)

    def grp(i):
        return jnp.minimum(i // tiles_per_batch, n_batch)

    grid_spec = pltpu.PrefetchScalarGridSpec(
        num_scalar_prefetch=1,
        grid=(n_tok_tiles,),
        in_specs=[pl.BlockSpec((TM, D_MODEL), lambda i, d: (i, 0)),
                  pl.BlockSpec((None, 1, D_MODEL), lambda i, d: (grp(i), 0, 5)),
                  pl.BlockSpec((TM, LANES), lambda i, d: (i, 0)),
                  pl.BlockSpec((1, D_MODEL), lambda i, d: (0, 0)),
                  pl.BlockSpec(memory_space=pl.ANY)],
        out_specs=pl.BlockSpec((TM, D_MODEL), lambda i, d: (i, 0)),
        scratch_shapes=[pltpu.VMEM((2, 2 * TM, D_MODEL), F32), pltpu.SemaphoreType.DMA((2,))],
    )
    return pl.pallas_call(
        functools.partial(_combine_kernel, final=final),
        grid_spec=grid_spec,
        out_shape=jax.ShapeDtypeStruct((n_tok_tiles * TM, D_MODEL), F32),
        compiler_params=_cparams(1),
        name="moe_combine",
    )(dest, h, mod3, route, final_g, y)


def _moe_layer(h, g2, mod3, w_r, wg, wu, wd, final_g, layer, moe_layer, *, final, **geo):
    T = h.shape[0]
    n_tiles = (2 * T) // TG + N_EXPERTS
    route, cnt, xs_zero = _router(h, g2, mod3, w_r, layer, n_tiles * TG, **geo)

    counts = cnt[0, :N_EXPERTS].astype(jnp.int32)
    padded = ((counts + TG - 1) // TG) * TG
    ends = jnp.cumsum(padded)
    offs = ends - padded
    e12 = route[:, ROUTE_E1:ROUTE_E2 + 1].astype(jnp.int32)
    r12 = route[:, ROUTE_R1:ROUTE_R2 + 1].astype(jnp.int32)
    onehot = e12[:, :, None] == jnp.arange(N_EXPERTS)[None, None, :]
    dest = (jnp.sum(jnp.where(onehot, offs[None, None, :], 0), axis=-1) + r12).reshape(2 * T)
    n_used = (ends[-1] // TG).reshape(1)
    tiles = jnp.arange(n_tiles)
    te_raw = jnp.sum(tiles[:, None] >= (ends // TG)[None, :], axis=1)
    tile_expert = jnp.minimum(te_raw, N_EXPERTS - 1).astype(jnp.int32)
    sel = tile_expert[:, None] == jnp.arange(N_EXPERTS)[None, :]
    cnt_k = jnp.sum(jnp.where(sel, counts[None, :], 0), axis=1)
    off_k = jnp.sum(jnp.where(sel, offs[None, :], 0), axis=1)
    tile_rows = jnp.where(te_raw < N_EXPERTS, jnp.clip(cnt_k - (tiles * TG - off_k), 0, TG), 0)

    xs = _dispatch(dest, h, g2, mod3, xs_zero, layer, **geo)
    y = _experts(tile_expert, n_used, tile_rows.astype(jnp.int32), xs, wg, wu, wd, moe_layer, n_tiles)
    return _combine(dest, h, mod3, route, final_g, y, final=final, **geo)


def _rope_tables(seq):
    assert seq % GRID_W == 0
    n_rows = seq // GRID_W
    inv = ROPE_THETA ** (-jnp.arange(ROPE_FREQS, dtype=F32) / ROPE_FREQS)
    ang_r = jnp.arange(n_rows, dtype=F32)[:, None] * inv
    ang_c = jnp.arange(GRID_W, dtype=F32)[:, None] * inv
    cos_r, sin_r = (jnp.repeat(f(ang_r), GRID_W, axis=0) for f in (jnp.cos, jnp.sin))
    cos_c, sin_c = (jnp.tile(f(ang_c), (n_rows, 1)) for f in (jnp.cos, jnp.sin))
    cos = jnp.concatenate([cos_r, cos_r, cos_c, cos_c], axis=1)
    sin = jnp.concatenate([-sin_r, sin_r, -sin_c, sin_c], axis=1)
    cos = jnp.concatenate([cos, jnp.ones((TM, HEAD_DIM), F32)], axis=0)
    sin = jnp.concatenate([sin, jnp.zeros((TM, HEAD_DIM), F32)], axis=0)
    return cos, sin


def kernel(x, c, ctx, c_ctx, w_mod, b_mod, norm1_g, norm2_g, w_in, attn_sink, conv_w, conv_b, gate_a_w, gate_a_b, gate_x_w, gate_x_b, lru_lambda, w_o_attn, w_o_lru, w_out, ff_w_gate, ff_w_up, ff_w_down, router_w, exp_w_gate, exp_w_up, exp_w_down, final_g):
    n_batch, seq, _ = x.shape
    ctx_len = ctx.shape[1]
    assert n_batch * ctx_len == TM and seq % TM == 0 and n_batch + 1 <= MOD_ROWS
    n_lat = n_batch * seq
    tiles_per_batch = seq // TM
    geo = dict(tiles_per_batch=tiles_per_batch, n_batch=n_batch)
    shp = dict(n_batch=n_batch, seq=seq, ctx_len=ctx_len)

    cpad = jnp.zeros((MOD_ROWS, D_MODEL), F32).at[:n_batch].set(c).at[n_batch].set(c_ctx)
    mod = _modulation(cpad, w_mod, b_mod)
    cos_t, sin_t = _rope_tables(seq)
    perm = _time_permutation(TM)
    stream = (x.reshape(n_lat, D_MODEL), ctx.reshape(n_batch * ctx_len, D_MODEL), 0)

    g1 = norm1_g.reshape(DEPTH, 1, D_MODEL)
    g2 = norm2_g.reshape(DEPTH, 1, D_MODEL)
    w_in_b = w_in.astype(BF16)
    woa_b, wol_b, wout_b = w_o_attn.astype(BF16), w_o_lru.astype(BF16), w_out.astype(BF16)
    ffg_b, ffu_b, ffd_b = ff_w_gate.astype(BF16), ff_w_up.astype(BF16), ff_w_down.astype(BF16)

    for l in range(DEPTH):
        mod3 = mod[l].reshape(MOD_ROWS, 1, 6 * D_MODEL)
        if l > 0:
            stream = (h, h, n_lat // TM)
        q, kv, xr, gy, sga, sgr = _in_proj(stream, g1, mod3, w_in_b, l, cos_t, sin_t, perm,
                                           n_lat_tiles=n_lat // TM, **geo)
        attn = _latent_attention(attn_sink[l], q, kv, **shp)
        attn_ctx = _context_attention(attn_sink[l], q, kv, **shp)
        wcat = jnp.concatenate([gate_a_w[l], gate_x_w[l]], axis=-1).astype(BF16)
        gate_b = jnp.stack([gate_a_b[l], gate_x_b[l]], axis=1)
        hf, hb = _rglru(xr, conv_w[l], conv_b[l].reshape(1, D_RNN), wcat, gate_b, lru_lambda[l], **shp)
        h = _merge(stream, attn, attn_ctx, hf, hb, gy, sga, sgr, mod3, woa_b, wol_b, wout_b, l, **geo)
        i = l // 2
        if l % 2 == 0:
            h = _ffn(h, g2, mod3, ffg_b, ffu_b, ffd_b, l, i, **geo)
        else:
            w_r = jnp.zeros((D_MODEL, LANES), F32).at[:, :N_EXPERTS].set(router_w[i])
            h = _moe_layer(h, g2, mod3, w_r, exp_w_gate, exp_w_up, exp_w_down,
                           final_g.reshape(1, D_MODEL), l, i, final=(l == DEPTH - 1), **geo)

    assert DEPTH % 2 == 0 and h.shape[0] == n_lat
    return h.reshape(n_batch, seq, D_MODEL)
```

```python
import functools

import jax
import jax.numpy as jnp
from jax import lax
from jax.experimental import pallas as pl
from jax.experimental.pallas import tpu as pltpu

F32 = jnp.float32
BF16 = jnp.bfloat16

D_MODEL = 1024
DEPTH = 4
GRID_W = 64
N_HEADS = 8
N_KV_HEADS = 2
HEAD_DIM = 128
N_GROUPS = N_HEADS // N_KV_HEADS
ATTN_BLOCK = 128
ROPE_THETA = 10000.0
ROPE_FREQS = HEAD_DIM // 4
D_RNN = 1024
N_RNN_BLOCKS = 8
RNN_BLOCK = D_RNN // N_RNN_BLOCKS
LRU_C = 8.0
D_FF = 2816
N_EXPERTS = 8
D_FF_EXPERT = 3584
EPS = 1e-6
NEG_INF = -1e30
Q_COLS = N_HEADS * HEAD_DIM
KV_COLS = N_KV_HEADS * HEAD_DIM
IN_COLS = Q_COLS + 2 * KV_COLS + 2 * D_RNN + 2 * D_MODEL
LOG2E = 1.4426950408889634
ATTN_SCALE = HEAD_DIM ** -0.5 * LOG2E

LANES = 128
SUBLANES = 8
TM = 512
TN_IN = 512
TQ = 1024
ATTN_UNROLL = 8
TT = 256
SUBSEQ = TT // SUBLANES
TG = 1024
TG_SUB = 512
MOE_F_CHUNKS = 7
ROW_DMA_UNROLL = 8
MOD_ROWS = 8
VMEM_LIMIT = 56 * 1024 * 1024


def _cparams(n_axes):
    return pltpu.CompilerParams(dimension_semantics=("arbitrary",) * n_axes,
                                vmem_limit_bytes=VMEM_LIMIT)


def _sigmoid(z):
    return 0.5 * jnp.tanh(0.5 * z) + 0.5


def _norm_mod(h, g, shift, scale):
    ms = jnp.mean(h * h, axis=-1, keepdims=True)
    y = h * lax.rsqrt(ms + EPS) * g
    return y * (1.0 + scale) + shift


def _mod_kernel(c_ref, w_ref, b_ref, o_ref):
    cv = c_ref[...]
    s = cv * _sigmoid(cv)
    o_ref[...] = _dot_split(s, w_ref[...]) + b_ref[...]


def _modulation(cpad, w_mod, b_mod):
    nchunk = 6
    return pl.pallas_call(
        _mod_kernel,
        grid=(DEPTH, nchunk),
        in_specs=[
            pl.BlockSpec((MOD_ROWS, D_MODEL), lambda l, n: (0, 0)),
            pl.BlockSpec((None, D_MODEL, D_MODEL), lambda l, n: (l, 0, n)),
            pl.BlockSpec((None, 1, D_MODEL), lambda l, n: (l, 0, n)),
        ],
        out_specs=pl.BlockSpec((None, MOD_ROWS, D_MODEL), lambda l, n: (l, 0, n)),
        out_shape=jax.ShapeDtypeStruct((DEPTH, MOD_ROWS, 6 * D_MODEL), F32),
        compiler_params=_cparams(2),
        name="modulation",
    )(cpad, w_mod, b_mod.reshape(DEPTH, 1, 6 * D_MODEL))


def _rope(xh, cos, sin_signed, first_half):
    sw = jnp.where(first_half, pltpu.roll(xh, 96, 1), pltpu.roll(xh, 32, 1))
    return xh * cos + sw * sin_signed


def _stream_tile(h_lat_ref, h_ctx_ref):
    is_ctx_tile = pl.program_id(0) == pl.num_programs(0) - 1
    return jnp.where(is_ctx_tile, h_ctx_ref[...], h_lat_ref[...])


def _stream_specs(stream, n_lat_tiles):
    _, _, ctx_block = stream
    return [pl.BlockSpec((TM, D_MODEL), lambda i: (jnp.minimum(i, n_lat_tiles - 1), 0)),
            pl.BlockSpec((TM, D_MODEL), lambda i: (ctx_block, 0))]


def _in_proj_kernel(h_lat_ref, h_ctx_ref, g_ref, sh_ref, sc_ref, w_ref, cos_ref, sin_ref, perm_ref,
                    q_ref, kv_ref, x_ref, gy_ref, sga_ref, sgr_ref):
    n = _norm_mod(_stream_tile(h_lat_ref, h_ctx_ref), g_ref[...], sh_ref[...], sc_ref[...]).astype(BF16)
    cos = cos_ref[...]
    sin = sin_ref[...]
    lane = lax.broadcasted_iota(jnp.int32, (TM, LANES), 1)
    first_half = (lane & 32) == 0

    def proj(chunk):
        return jnp.dot(n, w_ref[:, chunk * TN_IN:(chunk + 1) * TN_IN], preferred_element_type=F32)

    def rope_store(acc, n_heads, scale, ref, col0):
        for hh in range(n_heads):
            y = _rope(acc[:, hh * HEAD_DIM:(hh + 1) * HEAD_DIM], cos, sin, first_half)
            if scale != 1.0:
                y = y * scale
            ref[:, col0 + hh * HEAD_DIM:col0 + (hh + 1) * HEAD_DIM] = y.astype(BF16)

    for c in range(2):
        rope_store(proj(c), 4, ATTN_SCALE, q_ref, c * TN_IN)
    acc = proj(2)
    rope_store(acc, 2, 1.0, kv_ref, 0)
    kv_ref[:, KV_COLS:] = acc[:, KV_COLS:].astype(BF16)
    n_perm = jnp.dot(perm_ref[...], n, preferred_element_type=F32).astype(BF16)
    for c in range(2):
        cols = slice(c * TN_IN, (c + 1) * TN_IN)
        x_ref[:, cols] = jnp.dot(n_perm, w_ref[:, (3 + c) * TN_IN:(4 + c) * TN_IN],
                                 preferred_element_type=F32)
        gy_ref[:, cols] = jax.nn.gelu(proj(5 + c)).astype(BF16)
        sga_ref[:, cols] = _sigmoid(proj(7 + c)).astype(BF16)
        sgr_ref[:, cols] = _sigmoid(proj(9 + c)).astype(BF16)


def _in_proj(stream, g, mod3, w_in, layer, cos_t, sin_t, perm, *, n_lat_tiles, tiles_per_batch, n_batch):
    T = (n_lat_tiles + 1) * TM

    def grp(i):
        return jnp.minimum(i // tiles_per_batch, n_batch)

    def pos_tile(i):
        return jnp.where(i < n_lat_tiles, i % tiles_per_batch, tiles_per_batch)

    def tok(width):
        return pl.BlockSpec((TM, width), lambda i: (i, 0))

    out_shape = (
        jax.ShapeDtypeStruct((T, Q_COLS), BF16),
        jax.ShapeDtypeStruct((T, 2 * KV_COLS), BF16),
        jax.ShapeDtypeStruct((T, D_RNN), F32),
        jax.ShapeDtypeStruct((T, D_RNN), BF16),
        jax.ShapeDtypeStruct((T, D_MODEL), BF16),
        jax.ShapeDtypeStruct((T, D_MODEL), BF16),
    )
    return pl.pallas_call(
        _in_proj_kernel,
        grid=(T // TM,),
        in_specs=_stream_specs(stream, n_lat_tiles) + [
            pl.BlockSpec((None, 1, D_MODEL), lambda i: (layer, 0, 0)),
            pl.BlockSpec((None, 1, D_MODEL), lambda i: (grp(i), 0, 0)),
            pl.BlockSpec((None, 1, D_MODEL), lambda i: (grp(i), 0, 1)),
            pl.BlockSpec((None, D_MODEL, IN_COLS), lambda i: (layer, 0, 0)),
            pl.BlockSpec((TM, HEAD_DIM), lambda i: (pos_tile(i), 0)),
            pl.BlockSpec((TM, HEAD_DIM), lambda i: (pos_tile(i), 0)),
            pl.BlockSpec((TM, TM), lambda i: (0, 0)),
        ],
        out_specs=(tok(Q_COLS), tok(2 * KV_COLS), tok(D_RNN), tok(D_RNN), tok(D_MODEL), tok(D_MODEL)),
        out_shape=out_shape,
        compiler_params=_cparams(1),
        name="in_proj",
    )(stream[0], stream[1], g, mod3, mod3, w_in, cos_t, sin_t, perm)


def _stack_heads(qt):
    return jnp.concatenate([qt[:, g * HEAD_DIM:(g + 1) * HEAD_DIM] for g in range(N_GROUPS)], axis=0)


def _sink_col(sink_ref, kvh, rows):
    return jnp.concatenate(
        [jnp.full((rows, 1), sink_ref[kvh * N_GROUPS + g] * LOG2E, F32) for g in range(N_GROUPS)], axis=0)


def _ones_column(n_keys):
    lane = lax.broadcasted_iota(jnp.int32, (n_keys, HEAD_DIM), 1)
    return jnp.where(lane == 0, 1.0, 0.0).astype(BF16)


def _softmax_pv(s, sink, vall):
    m = jnp.maximum(jnp.max(s, axis=1, keepdims=True), sink)
    p = jnp.exp2(s - m).astype(BF16)
    v_aug = jnp.concatenate([vall, _ones_column(vall.shape[0])], axis=1)
    oa = jnp.dot(p, v_aug, preferred_element_type=F32)
    denom = oa[:, HEAD_DIM:HEAD_DIM + 1] + jnp.exp2(sink - m)
    return oa[:, :HEAD_DIM] / denom


def _attn_kernel(sink_ref, q_ref, k_ref, v_ref, kc_ref, vc_ref, o_ref, *, n_blocks):
    kvh = pl.program_id(1)
    i = pl.program_id(2)
    rows = ATTN_BLOCK * N_GROUPS
    qi = lax.broadcasted_iota(jnp.int32, (rows, ATTN_BLOCK), 0) & (ATTN_BLOCK - 1)
    kj = lax.broadcasted_iota(jnp.int32, (rows, ATTN_BLOCK), 1)
    tri_prev = kj >= qi
    tri_next = kj <= qi
    sink = _sink_col(sink_ref, kvh, ATTN_BLOCK)
    kc = kc_ref[...]
    vc = vc_ref[...]

    def body(qb, carry):
        n = i * (TQ // ATTN_BLOCK) + qb
        r0 = pl.multiple_of(qb * ATTN_BLOCK, ATTN_BLOCK)
        p0 = pl.multiple_of(jnp.maximum(n - 1, 0) * ATTN_BLOCK, ATTN_BLOCK)
        c0 = pl.multiple_of(n * ATTN_BLOCK, ATTN_BLOCK)
        n0 = pl.multiple_of(jnp.minimum(n + 1, n_blocks - 1) * ATTN_BLOCK, ATTN_BLOCK)
        qs = _stack_heads(q_ref[pl.ds(r0, ATTN_BLOCK), :])
        kall = jnp.concatenate([k_ref[pl.ds(p0, ATTN_BLOCK), :], k_ref[pl.ds(c0, ATTN_BLOCK), :],
                                k_ref[pl.ds(n0, ATTN_BLOCK), :], kc], axis=0)
        vall = jnp.concatenate([v_ref[pl.ds(p0, ATTN_BLOCK), :], v_ref[pl.ds(c0, ATTN_BLOCK), :],
                                v_ref[pl.ds(n0, ATTN_BLOCK), :], vc], axis=0)
        s = lax.dot_general(qs, kall, (((1,), (1,)), ((), ())), preferred_element_type=F32)
        pen_prev = jnp.where(n > 0, 0.0, NEG_INF)
        pen_next = jnp.where(n < n_blocks - 1, 0.0, NEG_INF)
        sp = jnp.where(tri_prev, s[:, :ATTN_BLOCK] + pen_prev, NEG_INF)
        sn = jnp.where(tri_next, s[:, 2 * ATTN_BLOCK:3 * ATTN_BLOCK] + pen_next, NEG_INF)
        s = jnp.concatenate([sp, s[:, ATTN_BLOCK:2 * ATTN_BLOCK], sn, s[:, 3 * ATTN_BLOCK:]], axis=1)
        o = _softmax_pv(s, sink, vall)
        for g in range(N_GROUPS):
            o_ref[pl.ds(r0, ATTN_BLOCK), g * HEAD_DIM:(g + 1) * HEAD_DIM] = (
                o[g * ATTN_BLOCK:(g + 1) * ATTN_BLOCK].astype(BF16))
        return carry

    lax.fori_loop(0, TQ // ATTN_BLOCK, body, 0, unroll=ATTN_UNROLL)


def _latent_attention(sink, q, kv, *, n_batch, seq, ctx_len):
    gw = N_GROUPS * HEAD_DIM
    tiles_per_batch = seq // TQ
    ctx_blk0 = n_batch * seq // ctx_len
    grid_spec = pltpu.PrefetchScalarGridSpec(
        num_scalar_prefetch=1,
        grid=(n_batch, N_KV_HEADS, tiles_per_batch),
        in_specs=[
            pl.BlockSpec((TQ, gw), lambda b, k, i, s: (b * tiles_per_batch + i, k)),
            pl.BlockSpec((seq, HEAD_DIM), lambda b, k, i, s: (b, k)),
            pl.BlockSpec((seq, HEAD_DIM), lambda b, k, i, s: (b, N_KV_HEADS + k)),
            pl.BlockSpec((ctx_len, HEAD_DIM), lambda b, k, i, s: (ctx_blk0 + b, k)),
            pl.BlockSpec((ctx_len, HEAD_DIM), lambda b, k, i, s: (ctx_blk0 + b, N_KV_HEADS + k)),
        ],
        out_specs=pl.BlockSpec((TQ, gw), lambda b, k, i, s: (b * tiles_per_batch + i, k)),
    )
    return pl.pallas_call(
        functools.partial(_attn_kernel, n_blocks=seq // ATTN_BLOCK),
        grid_spec=grid_spec,
        out_shape=jax.ShapeDtypeStruct((n_batch * seq, Q_COLS), BF16),
        compiler_params=_cparams(3),
        name="latent_attention",
    )(sink, q, kv, kv, kv, kv)


def _ctx_attn_kernel(sink_ref, q_ref, kc_ref, vc_ref, o_ref):
    kvh = pl.program_id(1)
    rows = q_ref.shape[0]
    qs = _stack_heads(q_ref[...])
    s = lax.dot_general(qs, kc_ref[...], (((1,), (1,)), ((), ())), preferred_element_type=F32)
    o = _softmax_pv(s, _sink_col(sink_ref, kvh, rows), vc_ref[...])
    for g in range(N_GROUPS):
        o_ref[:, g * HEAD_DIM:(g + 1) * HEAD_DIM] = o[g * rows:(g + 1) * rows].astype(BF16)


def _context_attention(sink, q, kv, *, n_batch, seq, ctx_len):
    gw = N_GROUPS * HEAD_DIM
    ctx_blk0 = n_batch * seq // ctx_len
    grid_spec = pltpu.PrefetchScalarGridSpec(
        num_scalar_prefetch=1,
        grid=(n_batch, N_KV_HEADS),
        in_specs=[
            pl.BlockSpec((ctx_len, gw), lambda b, k, s: (ctx_blk0 + b, k)),
            pl.BlockSpec((ctx_len, HEAD_DIM), lambda b, k, s: (ctx_blk0 + b, k)),
            pl.BlockSpec((ctx_len, HEAD_DIM), lambda b, k, s: (ctx_blk0 + b, N_KV_HEADS + k)),
        ],
        out_specs=pl.BlockSpec((ctx_len, gw), lambda b, k, s: (b, k)),
    )
    return pl.pallas_call(
        _ctx_attn_kernel,
        grid_spec=grid_spec,
        out_shape=jax.ShapeDtypeStruct((n_batch * ctx_len, Q_COLS), BF16),
        compiler_params=_cparams(2),
        name="context_attention",
    )(sink, q, kv, kv)


def _softplus(z):
    return jnp.maximum(z, 0.0) + jnp.log1p(jnp.exp(-jnp.abs(z)))


def _scan8(a, b, row, reverse):
    for s in (1, 2, 4):
        if reverse:
            keep = row < SUBLANES - s
            shift = SUBLANES - s
        else:
            keep = row >= s
            shift = s
        a_sh = jnp.where(keep, pltpu.roll(a, shift, 0), 1.0)
        b_sh = jnp.where(keep, pltpu.roll(b, shift, 0), 0.0)
        b = a * b_sh + b
        a = a * a_sh
    return a, b


def _attend_tile(sink_ref, q_ref, kv_ref, kvc_ref, o_ref, tile, n_blocks, kv_heads):
    rows = ATTN_BLOCK * N_GROUPS
    qi = lax.broadcasted_iota(jnp.int32, (rows, ATTN_BLOCK), 0) & (ATTN_BLOCK - 1)
    kj = lax.broadcasted_iota(jnp.int32, (rows, ATTN_BLOCK), 1)
    tri_prev = kj >= qi
    tri_next = kj <= qi
    q_per_tile = TT // ATTN_BLOCK
    gw = N_GROUPS * HEAD_DIM
    for kvh in kv_heads:
        kcol = slice(kvh * HEAD_DIM, (kvh + 1) * HEAD_DIM)
        vcol = slice(KV_COLS + kvh * HEAD_DIM, KV_COLS + (kvh + 1) * HEAD_DIM)
        sink = _sink_col(sink_ref, kvh, ATTN_BLOCK)
        for qb in range(q_per_tile):
            n = tile * q_per_tile + qb
            starts = [pl.multiple_of(jnp.maximum(n - 1, 0) * ATTN_BLOCK, ATTN_BLOCK),
                      pl.multiple_of(n * ATTN_BLOCK, ATTN_BLOCK),
                      pl.multiple_of(jnp.minimum(n + 1, n_blocks - 1) * ATTN_BLOCK, ATTN_BLOCK)]
            qrows = slice(qb * ATTN_BLOCK, (qb + 1) * ATTN_BLOCK)
            qs = _stack_heads(q_ref[qrows, kvh * gw:(kvh + 1) * gw])
            kall = jnp.concatenate([kv_ref[pl.ds(r, ATTN_BLOCK), kcol] for r in starts] + [kvc_ref[:, kcol]],
                                   axis=0)
            vall = jnp.concatenate([kv_ref[pl.ds(r, ATTN_BLOCK), vcol] for r in starts] + [kvc_ref[:, vcol]],
                                   axis=0)
            s = lax.dot_general(qs, kall, (((1,), (1,)), ((), ())), preferred_element_type=F32)
            pen_prev = jnp.where(n > 0, 0.0, NEG_INF)
            pen_next = jnp.where(n < n_blocks - 1, 0.0, NEG_INF)
            sp = jnp.where(tri_prev, s[:, :ATTN_BLOCK] + pen_prev, NEG_INF)
            sn = jnp.where(tri_next, s[:, 2 * ATTN_BLOCK:3 * ATTN_BLOCK] + pen_next, NEG_INF)
            s = jnp.concatenate([sp, s[:, ATTN_BLOCK:2 * ATTN_BLOCK], sn, s[:, 3 * ATTN_BLOCK:]], axis=1)
            o = _softmax_pv(s, sink, vall)
            for g in range(N_GROUPS):
                col0 = kvh * gw + g * HEAD_DIM
                o_ref[qrows, col0:col0 + HEAD_DIM] = o[g * ATTN_BLOCK:(g + 1) * ATTN_BLOCK].astype(BF16)


def _mixer_kernel(xf_ref, xfp_ref, xfn_ref, xb_ref, xbp_ref, xbn_ref, cw_ref, cb_ref, wcat_ref,
                  gb_ref, lam_ref, unperm_ref, sink_ref, q_ref, kv_ref, kvc_ref,
                  hf_ref, hb_ref, attn_ref, a_s, b_s, h_bf, carry, *, n_blocks):
    j = pl.program_id(1)
    last_j = pl.num_programs(1) - 1

    @pl.when(j == 0)
    def _():
        carry[...] = jnp.zeros_like(carry)

    seg_start = (j <= 1, (j == 0) | (j == last_j))
    seg_end = ((j == 0) | (j == last_j), j <= 1)
    mains = (xf_ref, xb_ref)
    prevs = (xfp_ref, xbp_ref)
    nexts = (xfn_ref, xbn_ref)
    sub = lax.broadcasted_iota(jnp.int32, (SUBLANES, D_RNN), 0)
    S8 = SUBLANES

    for d in range(2):
        x0 = mains[d][...]
        halo = prevs[d][...]
        t_m1 = jnp.where(seg_start[d], 0.0, halo[2 * S8 - 1:2 * S8, :])
        t_m2 = jnp.where(seg_start[d], 0.0, halo[S8 - 1:S8, :])
        t_p1 = jnp.where(seg_end[d], 0.0, nexts[d][0:1, :])
        g_m1 = jnp.where(sub == 0, t_m1, pltpu.roll(x0[TT - S8:, :], 1, 0))
        g_m2 = jnp.where(sub == 0, t_m2, pltpu.roll(x0[TT - 2 * S8:TT - S8, :], 1, 0))
        g_p1 = jnp.where(sub == S8 - 1, t_p1, pltpu.roll(x0[:S8, :], S8 - 1, 0))
        xc = cb_ref[...] + jnp.concatenate([g_m2, g_m1, x0[:TT - 2 * S8, :]], axis=0) * cw_ref[0:1, :]
        xc = xc + jnp.concatenate([g_m1, x0[:TT - S8, :]], axis=0) * cw_ref[1:2, :]
        xc = xc + x0 * cw_ref[2:3, :]
        xc = xc + jnp.concatenate([x0[S8:, :], g_p1], axis=0) * cw_ref[3:4, :]
        c_d = (-LRU_C * LOG2E) * _softplus(-lam_ref[d:d + 1, :])
        for blk in range(N_RNN_BLOCKS):
            sl = slice(blk * RNN_BLOCK, (blk + 1) * RNN_BLOCK)
            xcb = xc[:, sl]
            z = jnp.dot(xcb.astype(BF16), wcat_ref[d, blk], preferred_element_type=F32)
            r = _sigmoid(z[:, :RNN_BLOCK] + gb_ref[d, 0:1, sl])
            ig = _sigmoid(z[:, RNN_BLOCK:] + gb_ref[d, 1:2, sl])
            a = jnp.exp2(r * c_d[:, sl])
            v = 1.0 - a * a
            root = jnp.where(v > 0.0, v * lax.rsqrt(v), 0.0)
            a_s[d, :, sl] = a
            b_s[d, :, sl] = root * (ig * xcb)
        _attend_tile(sink_ref, q_ref, kv_ref, kvc_ref, attn_ref, jnp.maximum(j - 1, 0), n_blocks, (d,))

    def local(g, c):
        hf, pf, hb, pb = c
        rf = pl.multiple_of(g * S8, S8)
        rb = pl.multiple_of((SUBSEQ - 1 - g) * S8, S8)
        af = a_s[0, pl.ds(rf, S8), :]
        ab = a_s[1, pl.ds(rb, S8), :]
        hf = af * hf + b_s[0, pl.ds(rf, S8), :]
        hb = ab * hb + b_s[1, pl.ds(rb, S8), :]
        pf = af * pf
        pb = ab * pb
        b_s[0, pl.ds(rf, S8), :] = hf
        b_s[1, pl.ds(rb, S8), :] = hb
        a_s[0, pl.ds(rf, S8), :] = pf
        a_s[1, pl.ds(rb, S8), :] = pb
        return hf, pf, hb, pb

    zero = jnp.zeros((S8, D_RNN), F32)
    one = jnp.ones((S8, D_RNN), F32)
    hf, pf, hb, pb = lax.fori_loop(0, SUBSEQ, local, (zero, one, zero, one), unroll=4)

    af, bf = _scan8(pf, hf, sub, False)
    endf = af * carry[0] + bf
    h_in = [jnp.where(sub == 0, carry[0], pltpu.roll(endf, 1, 0))]
    carry[0] = jnp.broadcast_to(endf[S8 - 1:S8, :], (S8, D_RNN))
    ab, bb = _scan8(pb, hb, sub, True)
    endb = ab * carry[1] + bb
    h_in.append(jnp.where(sub == S8 - 1, carry[1], pltpu.roll(endb, S8 - 1, 0)))
    carry[1] = jnp.broadcast_to(endb[0:1, :], (S8, D_RNN))

    outs = (hf_ref, hb_ref)
    for d in range(2):
        h_in2 = jnp.concatenate([h_in[d], h_in[d]], axis=0)

        def fix(k, c, d=d, h_in2=h_in2):
            rows = pl.ds(pl.multiple_of(k * 2 * S8, 2 * S8), 2 * S8)
            h_bf[d, rows, :] = (b_s[d, rows, :] + a_s[d, rows, :] * h_in2).astype(BF16)
            return c

        lax.fori_loop(0, SUBSEQ // 2, fix, 0, unroll=2)
        outs[d][...] = jnp.dot(unperm_ref[...], h_bf[d], preferred_element_type=F32).astype(BF16)


def _time_permutation(n_rows):
    p = jnp.arange(n_rows)
    src = (p // TT) * TT + (p % SUBLANES) * SUBSEQ + (p % TT) // SUBLANES
    return (src[:, None] == jnp.arange(n_rows)[None, :]).astype(BF16)


def _token_mixer(x, conv_w, conv_b, wcat, gate_b, lam, sink, q, kv, *, n_batch, seq, ctx_len):
    T = x.shape[0]
    assert ctx_len == TT and seq % TT == 0
    tps = seq // TT
    n_lat_t = n_batch * tps
    halo_rows = 2 * SUBLANES
    per_halo = TT // halo_rows
    per8 = TT // SUBLANES
    last8 = T // SUBLANES - 1
    unperm = _time_permutation(TT).T

    def ftile(b, j):
        return jnp.where(j == 0, n_lat_t + b, b * tps + j - 1)

    def btile(b, j):
        return jnp.where(j == 0, n_lat_t + b, b * tps + tps - j)

    def main(tile):
        return pl.BlockSpec((TT, D_RNN), lambda b, j: (tile(b, j), 0))

    def prev(tile):
        return pl.BlockSpec((halo_rows, D_RNN), lambda b, j: (jnp.maximum(tile(b, j) * per_halo - 1, 0), 0))

    def nxt(tile):
        return pl.BlockSpec((SUBLANES, D_RNN),
                            lambda b, j: (jnp.minimum((tile(b, j) + 1) * per8, last8), 0))

    def const(shape):
        return pl.BlockSpec(shape, lambda b, j: (0,) * len(shape))

    def qtile(b, j):
        return b * tps + jnp.maximum(j - 1, 0)

    q_spec = pl.BlockSpec((TT, Q_COLS), lambda b, j: (qtile(b, j), 0))
    return pl.pallas_call(
        functools.partial(_mixer_kernel, n_blocks=seq // ATTN_BLOCK),
        grid=(n_batch, tps + 1),
        in_specs=[main(ftile), prev(ftile), nxt(ftile), main(btile), prev(btile), nxt(btile),
                  const(conv_w.shape), const(conv_b.shape), const(wcat.shape), const(gate_b.shape),
                  const(lam.shape), const(unperm.shape),
                  pl.BlockSpec(memory_space=pltpu.SMEM),
                  q_spec,
                  pl.BlockSpec((seq, 2 * KV_COLS), lambda b, j: (b, 0)),
                  pl.BlockSpec((ctx_len, 2 * KV_COLS), lambda b, j: (n_lat_t + b, 0))],
        out_specs=(main(ftile), main(btile), q_spec),
        out_shape=(jax.ShapeDtypeStruct((T, D_RNN), BF16), jax.ShapeDtypeStruct((T, D_RNN), BF16),
                   jax.ShapeDtypeStruct((n_batch * seq, Q_COLS), BF16)),
        scratch_shapes=[pltpu.VMEM((2, TT, D_RNN), F32),
                        pltpu.VMEM((2, TT, D_RNN), F32),
                        pltpu.VMEM((2, TT, D_RNN), BF16),
                        pltpu.VMEM((2, SUBLANES, D_RNN), F32)],
        compiler_params=_cparams(2),
        name="token_mixer",
    )(x, x, x, x, x, x, conv_w, conv_b, wcat, gate_b, lam, unperm, sink, q, kv, kv)


def _merge_kernel(h_lat_ref, h_ctx_ref, attn_ref, attn_ctx_ref, hf_ref, hb_ref, gy_ref, sga_ref, sgr_ref,
                  g1_ref, woa_ref, wol_ref, wout_ref, o_ref):
    rec = ((hf_ref[...].astype(F32) + hb_ref[...].astype(F32)) * gy_ref[...].astype(F32)).astype(BF16)
    is_ctx_tile = pl.program_id(0) == pl.num_programs(0) - 1
    attn = jnp.where(is_ctx_tile, attn_ctx_ref[...], attn_ref[...])
    ta = jnp.dot(attn, woa_ref[...], preferred_element_type=F32)
    tl = jnp.dot(rec, wol_ref[...], preferred_element_type=F32)
    m = sga_ref[...].astype(F32) * ta + sgr_ref[...].astype(F32) * tl
    y = jnp.dot(m.astype(BF16), wout_ref[...], preferred_element_type=F32)
    o_ref[...] = _stream_tile(h_lat_ref, h_ctx_ref) + g1_ref[...] * y


def _merge(stream, attn, attn_ctx, hf, hb, gy, sga, sgr, mod3, woa, wol, wout, layer, *,
           tiles_per_batch, n_batch):
    n_lat_tiles = attn.shape[0] // TM
    T = (n_lat_tiles + 1) * TM
    assert attn_ctx.shape[0] == TM and hf.shape[0] == T

    def grp(i):
        return jnp.minimum(i // tiles_per_batch, n_batch)

    tok = pl.BlockSpec((TM, D_MODEL), lambda i: (i, 0))
    wsp = pl.BlockSpec((None, D_MODEL, D_MODEL), lambda i: (layer, 0, 0))
    return pl.pallas_call(
        _merge_kernel,
        grid=(T // TM,),
        in_specs=_stream_specs(stream, n_lat_tiles) + [
                  pl.BlockSpec((TM, Q_COLS), lambda i: (jnp.minimum(i, n_lat_tiles - 1), 0)),
                  pl.BlockSpec((TM, Q_COLS), lambda i: (0, 0)),
                  tok, tok, tok, tok, tok,
                  pl.BlockSpec((None, 1, D_MODEL), lambda i: (grp(i), 0, 2)),
                  wsp, wsp, wsp],
        out_specs=tok,
        out_shape=jax.ShapeDtypeStruct((T, D_MODEL), F32),
        compiler_params=_cparams(1),
        name="merge",
    )(stream[0], stream[1], attn, attn_ctx, hf, hb, gy, sga, sgr, mod3, woa, wol, wout)


def _swiglu_partial(n, wg, wu, wd):
    gt = jnp.dot(n, wg, preferred_element_type=F32)
    ut = jnp.dot(n, wu, preferred_element_type=F32)
    act = (gt * _sigmoid(gt) * ut).astype(BF16)
    return jnp.dot(act, wd, preferred_element_type=F32)


FFN_F_CHUNKS = 2


def _ffn_kernel(h_ref, g_ref, sh_ref, sc_ref, g2_ref, wg_ref, wu_ref, wd_ref, o_ref):
    h = h_ref[...]
    n = _norm_mod(h, g_ref[...], sh_ref[...], sc_ref[...]).astype(BF16)
    tf = D_FF // FFN_F_CHUNKS
    acc = None
    for c in range(FFN_F_CHUNKS):
        cols = slice(c * tf, (c + 1) * tf)
        part = _swiglu_partial(n, wg_ref[:, cols], wu_ref[:, cols], wd_ref[cols, :])
        acc = part if acc is None else acc + part
    o_ref[...] = h + g2_ref[...] * acc


def _ffn(h, g, mod3, wg, wu, wd, layer, ff_layer, *, tiles_per_batch, n_batch):
    T = h.shape[0]
    assert (D_FF // FFN_F_CHUNKS) % LANES == 0

    def grp(i):
        return jnp.minimum(i // tiles_per_batch, n_batch)

    def modspec(k):
        return pl.BlockSpec((None, 1, D_MODEL), lambda i: (grp(i), 0, k))

    tok = pl.BlockSpec((TM, D_MODEL), lambda i: (i, 0))
    return pl.pallas_call(
        _ffn_kernel,
        grid=(T // TM,),
        in_specs=[tok, pl.BlockSpec((None, 1, D_MODEL), lambda i: (layer, 0, 0)),
                  modspec(3), modspec(4), modspec(5),
                  pl.BlockSpec((None, D_MODEL, D_FF), lambda i: (ff_layer, 0, 0)),
                  pl.BlockSpec((None, D_MODEL, D_FF), lambda i: (ff_layer, 0, 0)),
                  pl.BlockSpec((None, D_FF, D_MODEL), lambda i: (ff_layer, 0, 0))],
        out_specs=tok,
        out_shape=jax.ShapeDtypeStruct((T, D_MODEL), F32),
        compiler_params=_cparams(1),
        name="dense_ffn",
    )(h, g, mod3, mod3, mod3, wg, wu, wd)


ROUTE_E1, ROUTE_E2, ROUTE_W1, ROUTE_W2, ROUTE_R1, ROUTE_R2 = range(6)


def _dot_split(a, b):
    a_hi = a.astype(BF16)
    a_lo = (a - a_hi.astype(F32)).astype(BF16)
    b_hi = b.astype(BF16)
    b_lo = (b - b_hi.astype(F32)).astype(BF16)

    def mm(x, y):
        return jnp.dot(x, y, preferred_element_type=F32)

    return mm(a_hi, b_hi) + (mm(a_hi, b_lo) + mm(a_lo, b_hi))


def _router_kernel(h_ref, g_ref, sh_ref, sc_ref, wr_ref, route_ref, cnt_ref, zero_ref, run):
    @pl.when(pl.program_id(0) == 0)
    def _():
        run[...] = jnp.zeros_like(run)

    n = _norm_mod(h_ref[...], g_ref[...], sh_ref[...], sc_ref[...])
    logits = _dot_split(n, wr_ref[...])
    lane = lax.broadcasted_iota(jnp.int32, logits.shape, 1)
    logits = jnp.where(lane < N_EXPERTS, logits, -jnp.inf)
    m1 = jnp.max(logits, axis=1, keepdims=True)
    i1 = jnp.min(jnp.where(logits == m1, lane, LANES), axis=1, keepdims=True)
    rest = jnp.where(lane == i1, -jnp.inf, logits)
    m2 = jnp.max(rest, axis=1, keepdims=True)
    i2 = jnp.min(jnp.where(rest == m2, lane, LANES), axis=1, keepdims=True)
    e2 = jnp.exp(m2 - m1)
    w1 = 1.0 / (1.0 + e2)
    w2 = e2 / (1.0 + e2)

    hit1 = lane == i1
    hit2 = lane == i2
    onehot = jnp.where(hit1 | hit2, 1.0, 0.0)
    r_i = lax.broadcasted_iota(jnp.int32, (TM, TM), 0)
    c_i = lax.broadcasted_iota(jnp.int32, (TM, TM), 1)
    lower = jnp.where(c_i < r_i, 1.0, 0.0).astype(BF16)
    prefix = jnp.dot(lower, onehot.astype(BF16), preferred_element_type=F32) + run[0:1, :]
    rank1 = jnp.sum(jnp.where(hit1, prefix, 0.0), axis=1, keepdims=True)
    rank2 = jnp.sum(jnp.where(hit2, prefix, 0.0), axis=1, keepdims=True)
    run[...] = run[...] + jnp.sum(onehot, axis=0, keepdims=True)
    cnt_ref[...] = run[...]

    rec = jnp.zeros(logits.shape, F32)
    for k, v in ((ROUTE_E1, i1.astype(F32)), (ROUTE_E2, i2.astype(F32)), (ROUTE_W1, w1), (ROUTE_W2, w2),
                 (ROUTE_R1, rank1), (ROUTE_R2, rank2)):
        rec = jnp.where(lane == k, v, rec)
    route_ref[...] = rec
    zero_ref[...] = jnp.zeros_like(zero_ref)


def _router(h, g2, mod3, w_r, layer, sorted_rows, *, tiles_per_batch, n_batch):
    T = h.shape[0]
    n_steps = T // TM
    zero_rows = -(-sorted_rows // (n_steps * SUBLANES)) * SUBLANES

    def grp(i):
        return jnp.minimum(i // tiles_per_batch, n_batch)

    def modspec(k):
        return pl.BlockSpec((None, 1, D_MODEL), lambda i: (grp(i), 0, k))

    return pl.pallas_call(
        _router_kernel,
        grid=(T // TM,),
        in_specs=[pl.BlockSpec((TM, D_MODEL), lambda i: (i, 0)),
                  pl.BlockSpec((None, 1, D_MODEL), lambda i: (layer, 0, 0)),
                  modspec(3), modspec(4),
                  pl.BlockSpec((D_MODEL, LANES), lambda i: (0, 0))],
        out_specs=(pl.BlockSpec((TM, LANES), lambda i: (i, 0)),
                   pl.BlockSpec((SUBLANES, LANES), lambda i: (0, 0)),
                   pl.BlockSpec((zero_rows, D_MODEL), lambda i: (i, 0))),
        out_shape=(jax.ShapeDtypeStruct((T, LANES), F32), jax.ShapeDtypeStruct((SUBLANES, LANES), F32),
                   jax.ShapeDtypeStruct((n_steps * zero_rows, D_MODEL), F32)),
        scratch_shapes=[pltpu.VMEM((SUBLANES, LANES), F32)],
        compiler_params=_cparams(1),
        name="moe_router",
    )(h, g2, mod3, mod3, w_r)


def _row_copy(src_ref, src_row, dst_ref, dst_row, sem):
    return pltpu.make_async_copy(src_ref.at[pl.ds(src_row, 1)], dst_ref.at[pl.ds(dst_row, 1)], sem)


def _dispatch_kernel(dest_ref, h_ref, g_ref, sh_ref, sc_ref, xs_in_ref, xs_ref, n_scr, sem):
    del xs_in_ref
    i = pl.program_id(0)
    base = i * (2 * TM)
    cur = i % 2
    n_scr[cur] = _norm_mod(h_ref[...], g_ref[...], sh_ref[...], sc_ref[...])

    def start(r, carry):
        for s in range(2):
            _row_copy(n_scr.at[cur], r, xs_ref, dest_ref[base + 2 * r + s], sem.at[cur]).start(priority=s)
        return carry

    lax.fori_loop(0, TM, start, 0, unroll=ROW_DMA_UNROLL)

    def wait_all(buf):
        for s in range(2):
            pltpu.make_async_copy(n_scr.at[buf], xs_ref.at[pl.ds(0, TM)], sem.at[buf]).wait()

    @pl.when(i > 0)
    def _():
        wait_all(1 - cur)

    @pl.when(i == pl.num_programs(0) - 1)
    def _():
        wait_all(cur)


def _dispatch(dest, h, g2, mod3, xs_zero, layer, *, tiles_per_batch, n_batch):
    T = h.shape[0]

    def grp(i):
        return jnp.minimum(i // tiles_per_batch, n_batch)

    def modspec(k):
        return pl.BlockSpec((None, 1, D_MODEL), lambda i, d: (grp(i), 0, k))

    grid_spec = pltpu.PrefetchScalarGridSpec(
        num_scalar_prefetch=1,
        grid=(T // TM,),
        in_specs=[pl.BlockSpec((TM, D_MODEL), lambda i, d: (i, 0)),
                  pl.BlockSpec((None, 1, D_MODEL), lambda i, d: (layer, 0, 0)),
                  modspec(3), modspec(4),
                  pl.BlockSpec(memory_space=pl.ANY)],
        out_specs=pl.BlockSpec(memory_space=pl.ANY),
        scratch_shapes=[pltpu.VMEM((2, TM, D_MODEL), F32), pltpu.SemaphoreType.DMA((2,))],
    )
    return pl.pallas_call(
        _dispatch_kernel,
        grid_spec=grid_spec,
        out_shape=jax.ShapeDtypeStruct(xs_zero.shape, xs_zero.dtype),
        input_output_aliases={5: 0},
        compiler_params=_cparams(1),
        name="moe_dispatch",
    )(dest, h, g2, mod3, mod3, xs_zero)


def _expert_kernel(te_ref, nu_ref, rows_ref, x_ref, wg_ref, wu_ref, wd_ref, acc, xb):
    del te_ref, nu_ref
    k = pl.program_id(0)
    f = pl.program_id(1)

    @pl.when(f == 0)
    def _():
        xb[...] = x_ref[...].astype(BF16)
        acc[...] = jnp.zeros_like(acc)

    wg = wg_ref[...].astype(BF16)
    wu = wu_ref[...].astype(BF16)
    wd = wd_ref[...].astype(BF16)
    acc[:TG_SUB, :] += _swiglu_partial(xb[:TG_SUB, :], wg, wu, wd)
    for sub in range(1, TG // TG_SUB):
        rows = slice(sub * TG_SUB, (sub + 1) * TG_SUB)

        @pl.when(rows_ref[k] > sub * TG_SUB)
        def _():
            acc[rows, :] += _swiglu_partial(xb[rows, :], wg, wu, wd)


def _experts(tile_expert, n_used, tile_rows, xs, wg, wu, wd, layer, n_tiles):
    P = n_tiles * TG
    assert xs.shape[0] >= P
    n_f = MOE_F_CHUNKS
    tf = D_FF_EXPERT // n_f
    assert tf % LANES == 0

    def fsel(k, f, nu):
        return jnp.where(k < nu[0], f, n_f - 1)

    grid_spec = pltpu.PrefetchScalarGridSpec(
        num_scalar_prefetch=3,
        grid=(P // TG, n_f),
        in_specs=[pl.BlockSpec((TG, D_MODEL), lambda k, f, te, nu, tr: (k, 0)),
                  pl.BlockSpec((None, None, D_MODEL, tf), lambda k, f, te, nu, tr: (layer, te[k], 0, fsel(k, f, nu))),
                  pl.BlockSpec((None, None, D_MODEL, tf), lambda k, f, te, nu, tr: (layer, te[k], 0, fsel(k, f, nu))),
                  pl.BlockSpec((None, None, tf, D_MODEL), lambda k, f, te, nu, tr: (layer, te[k], fsel(k, f, nu), 0))],
        out_specs=pl.BlockSpec((TG, D_MODEL), lambda k, f, te, nu, tr: (k, 0)),
        scratch_shapes=[pltpu.VMEM((TG, D_MODEL), BF16)],
    )
    return pl.pallas_call(
        _expert_kernel,
        grid_spec=grid_spec,
        out_shape=jax.ShapeDtypeStruct((P, D_MODEL), F32),
        compiler_params=_cparams(2),
        name="moe_experts",
    )(tile_expert, n_used, tile_rows, xs, wg, wu, wd)


def _combine_kernel(dest_ref, h_ref, g2_ref, route_ref, fg_ref, y_ref, o_ref, ybuf, sem, *, final):
    i = pl.program_id(0)
    n_steps = pl.num_programs(0)

    def gather(tile, buf):
        base = tile * (2 * TM)

        def start(r, carry):
            for s in range(2):
                _row_copy(y_ref, dest_ref[base + 2 * r + s], ybuf.at[buf], s * TM + r,
                          sem.at[buf]).start(priority=s)
            return carry

        lax.fori_loop(0, TM, start, 0, unroll=ROW_DMA_UNROLL)

    @pl.when(i == 0)
    def _():
        gather(0, 0)

    @pl.when(i + 1 < n_steps)
    def _():
        gather(i + 1, (i + 1) % 2)

    cur = i % 2
    for s in range(2):
        pltpu.make_async_copy(y_ref.at[pl.ds(0, TM)], ybuf.at[cur, pl.ds(s * TM, TM)], sem.at[cur]).wait()
    route = route_ref[...]
    f = (route[:, ROUTE_W1:ROUTE_W1 + 1] * ybuf[cur, 0:TM, :]
         + route[:, ROUTE_W2:ROUTE_W2 + 1] * ybuf[cur, TM:, :])
    out = h_ref[...] + g2_ref[...] * f
    if final:
        ms = jnp.mean(out * out, axis=-1, keepdims=True)
        out = out * lax.rsqrt(ms + EPS) * fg_ref[...]
    o_ref[...] = out


def _combine(dest, h, mod3, route, final_g, y, *, final, tiles_per_batch, n_batch):
    n_tok_tiles = h.shape[0] // TM - (1 if final else 0)

    def grp(i):
        return jnp.minimum(i // tiles_per_batch, n_batch)

    grid_spec = pltpu.PrefetchScalarGridSpec(
        num_scalar_prefetch=1,
        grid=(n_tok_tiles,),
        in_specs=[pl.BlockSpec((TM, D_MODEL), lambda i, d: (i, 0)),
                  pl.BlockSpec((None, 1, D_MODEL), lambda i, d: (grp(i), 0, 5)),
                  pl.BlockSpec((TM, LANES), lambda i, d: (i, 0)),
                  pl.BlockSpec((1, D_MODEL), lambda i, d: (0, 0)),
                  pl.BlockSpec(memory_space=pl.ANY)],
        out_specs=pl.BlockSpec((TM, D_MODEL), lambda i, d: (i, 0)),
        scratch_shapes=[pltpu.VMEM((2, 2 * TM, D_MODEL), F32), pltpu.SemaphoreType.DMA((2,))],
    )
    return pl.pallas_call(
        functools.partial(_combine_kernel, final=final),
        grid_spec=grid_spec,
        out_shape=jax.ShapeDtypeStruct((n_tok_tiles * TM, D_MODEL), F32),
        compiler_params=_cparams(1),
        name="moe_combine",
    )(dest, h, mod3, route, final_g, y)


def _moe_layer(h, g2, mod3, w_r, wg, wu, wd, final_g, layer, moe_layer, *, final, **geo):
    T = h.shape[0]
    n_tiles = (2 * T) // TG + N_EXPERTS
    route, cnt, xs_zero = _router(h, g2, mod3, w_r, layer, n_tiles * TG, **geo)

    counts = cnt[0, :N_EXPERTS].astype(jnp.int32)
    padded = ((counts + TG - 1) // TG) * TG
    ends = jnp.cumsum(padded)
    offs = ends - padded
    e12 = route[:, ROUTE_E1:ROUTE_E2 + 1].astype(jnp.int32)
    r12 = route[:, ROUTE_R1:ROUTE_R2 + 1].astype(jnp.int32)
    onehot = e12[:, :, None] == jnp.arange(N_EXPERTS)[None, None, :]
    dest = (jnp.sum(jnp.where(onehot, offs[None, None, :], 0), axis=-1) + r12).reshape(2 * T)
    n_used = (ends[-1] // TG).reshape(1)
    tiles = jnp.arange(n_tiles)
    te_raw = jnp.sum(tiles[:, None] >= (ends // TG)[None, :], axis=1)
    tile_expert = jnp.minimum(te_raw, N_EXPERTS - 1).astype(jnp.int32)
    sel = tile_expert[:, None] == jnp.arange(N_EXPERTS)[None, :]
    cnt_k = jnp.sum(jnp.where(sel, counts[None, :], 0), axis=1)
    off_k = jnp.sum(jnp.where(sel, offs[None, :], 0), axis=1)
    tile_rows = jnp.where(te_raw < N_EXPERTS, jnp.clip(cnt_k - (tiles * TG - off_k), 0, TG), 0)

    xs = _dispatch(dest, h, g2, mod3, xs_zero, layer, **geo)
    y = _experts(tile_expert, n_used, tile_rows.astype(jnp.int32), xs, wg, wu, wd, moe_layer, n_tiles)
    return _combine(dest, h, mod3, route, final_g, y, final=final, **geo)


def _rope_tables(seq):
    assert seq % GRID_W == 0
    n_rows = seq // GRID_W
    inv = ROPE_THETA ** (-jnp.arange(ROPE_FREQS, dtype=F32) / ROPE_FREQS)
    ang_r = jnp.arange(n_rows, dtype=F32)[:, None] * inv
    ang_c = jnp.arange(GRID_W, dtype=F32)[:, None] * inv
    cos_r, sin_r = (jnp.repeat(f(ang_r), GRID_W, axis=0) for f in (jnp.cos, jnp.sin))
    cos_c, sin_c = (jnp.tile(f(ang_c), (n_rows, 1)) for f in (jnp.cos, jnp.sin))
    cos = jnp.concatenate([cos_r, cos_r, cos_c, cos_c], axis=1)
    sin = jnp.concatenate([-sin_r, sin_r, -sin_c, sin_c], axis=1)
    cos = jnp.concatenate([cos, jnp.ones((TM, HEAD_DIM), F32)], axis=0)
    sin = jnp.concatenate([sin, jnp.zeros((TM, HEAD_DIM), F32)], axis=0)
    return cos, sin


def kernel(x, c, ctx, c_ctx, w_mod, b_mod, norm1_g, norm2_g, w_in, attn_sink, conv_w, conv_b, gate_a_w, gate_a_b, gate_x_w, gate_x_b, lru_lambda, w_o_attn, w_o_lru, w_out, ff_w_gate, ff_w_up, ff_w_down, router_w, exp_w_gate, exp_w_up, exp_w_down, final_g):
    n_batch, seq, _ = x.shape
    ctx_len = ctx.shape[1]
    assert n_batch * ctx_len == TM and seq % TM == 0 and n_batch + 1 <= MOD_ROWS
    n_lat = n_batch * seq
    tiles_per_batch = seq // TM
    geo = dict(tiles_per_batch=tiles_per_batch, n_batch=n_batch)
    shp = dict(n_batch=n_batch, seq=seq, ctx_len=ctx_len)

    cpad = jnp.zeros((MOD_ROWS, D_MODEL), F32).at[:n_batch].set(c).at[n_batch].set(c_ctx)
    mod = _modulation(cpad, w_mod, b_mod)
    cos_t, sin_t = _rope_tables(seq)
    perm = _time_permutation(TM)
    stream = (x.reshape(n_lat, D_MODEL), ctx.reshape(n_batch * ctx_len, D_MODEL), 0)

    g1 = norm1_g.reshape(DEPTH, 1, D_MODEL)
    g2 = norm2_g.reshape(DEPTH, 1, D_MODEL)
    w_in_b = w_in.astype(BF16)
    woa_b, wol_b, wout_b = w_o_attn.astype(BF16), w_o_lru.astype(BF16), w_out.astype(BF16)
    ffg_b, ffu_b, ffd_b = ff_w_gate.astype(BF16), ff_w_up.astype(BF16), ff_w_down.astype(BF16)

    for l in range(DEPTH):
        mod3 = mod[l].reshape(MOD_ROWS, 1, 6 * D_MODEL)
        if l > 0:
            stream = (h, h, n_lat // TM)
        q, kv, xr, gy, sga, sgr = _in_proj(stream, g1, mod3, w_in_b, l, cos_t, sin_t, perm,
                                           n_lat_tiles=n_lat // TM, **geo)
        attn_ctx = _context_attention(attn_sink[l], q, kv, **shp)
        wcat = jnp.concatenate([gate_a_w[l], gate_x_w[l]], axis=-1).astype(BF16)
        gate_b = jnp.stack([gate_a_b[l], gate_x_b[l]], axis=1)
        hf, hb, attn = _token_mixer(xr, conv_w[l], conv_b[l].reshape(1, D_RNN), wcat, gate_b, lru_lambda[l],
                                    attn_sink[l], q, kv, **shp)
        h = _merge(stream, attn, attn_ctx, hf, hb, gy, sga, sgr, mod3, woa_b, wol_b, wout_b, l, **geo)
        i = l // 2
        if l % 2 == 0:
            h = _ffn(h, g2, mod3, ffg_b, ffu_b, ffd_b, l, i, **geo)
        else:
            w_r = jnp.zeros((D_MODEL, LANES), F32).at[:, :N_EXPERTS].set(router_w[i])
            h = _moe_layer(h, g2, mod3, w_r, exp_w_gate, exp_w_up, exp_w_down,
                           final_g.reshape(1, D_MODEL), l, i, final=(l == DEPTH - 1), **geo)

    assert DEPTH % 2 == 0 and h.shape[0] == n_lat
    return h.reshape(n_batch, seq, D_MODEL)
```

```python
import functools

import jax
import jax.numpy as jnp
from jax import lax
from jax.experimental import pallas as pl
from jax.experimental.pallas import tpu as pltpu

F32 = jnp.float32
BF16 = jnp.bfloat16

D_MODEL = 1024
DEPTH = 4
GRID_W = 64
N_HEADS = 8
N_KV_HEADS = 2
HEAD_DIM = 128
N_GROUPS = N_HEADS // N_KV_HEADS
ATTN_BLOCK = 128
ROPE_THETA = 10000.0
ROPE_FREQS = HEAD_DIM // 4
D_RNN = 1024
N_RNN_BLOCKS = 8
RNN_BLOCK = D_RNN // N_RNN_BLOCKS
LRU_C = 8.0
D_FF = 2816
N_EXPERTS = 8
D_FF_EXPERT = 3584
EPS = 1e-6
NEG_INF = -1e30
Q_COLS = N_HEADS * HEAD_DIM
KV_COLS = N_KV_HEADS * HEAD_DIM
IN_COLS = Q_COLS + 2 * KV_COLS + 2 * D_RNN + 2 * D_MODEL
LOG2E = 1.4426950408889634
ATTN_SCALE = HEAD_DIM ** -0.5 * LOG2E

LANES = 128
SUBLANES = 8
TM = 512
TN_IN = 512
TT = 256
SUBSEQ = TT // SUBLANES
TG = 1024
TG_SUB = 512
MOE_F_CHUNKS = 7
ROW_DMA_UNROLL = 8
MOD_ROWS = 8
VMEM_LIMIT = 56 * 1024 * 1024


def _cparams(n_axes):
    return pltpu.CompilerParams(dimension_semantics=("arbitrary",) * n_axes,
                                vmem_limit_bytes=VMEM_LIMIT)


def _sigmoid(z):
    return 0.5 * jnp.tanh(0.5 * z) + 0.5


def _norm_mod(h, g, shift, scale):
    ms = jnp.mean(h * h, axis=-1, keepdims=True)
    y = h * lax.rsqrt(ms + EPS) * g
    return y * (1.0 + scale) + shift


def _mod_kernel(c_ref, w_ref, b_ref, o_ref):
    cv = c_ref[...]
    s = cv * _sigmoid(cv)
    o_ref[...] = _dot_split(s, w_ref[...]) + b_ref[...]


def _modulation(cpad, w_mod, b_mod):
    nchunk = 6
    return pl.pallas_call(
        _mod_kernel,
        grid=(DEPTH, nchunk),
        in_specs=[
            pl.BlockSpec((MOD_ROWS, D_MODEL), lambda l, n: (0, 0)),
            pl.BlockSpec((None, D_MODEL, D_MODEL), lambda l, n: (l, 0, n)),
            pl.BlockSpec((None, 1, D_MODEL), lambda l, n: (l, 0, n)),
        ],
        out_specs=pl.BlockSpec((None, MOD_ROWS, D_MODEL), lambda l, n: (l, 0, n)),
        out_shape=jax.ShapeDtypeStruct((DEPTH, MOD_ROWS, 6 * D_MODEL), F32),
        compiler_params=_cparams(2),
        name="modulation",
    )(cpad, w_mod, b_mod.reshape(DEPTH, 1, 6 * D_MODEL))


def _rope(xh, cos, sin_signed, first_half):
    sw = jnp.where(first_half, pltpu.roll(xh, HEAD_DIM - ROPE_FREQS, 1), pltpu.roll(xh, ROPE_FREQS, 1))
    return xh * cos + sw * sin_signed


def _stream_tile(h_lat_ref, h_ctx_ref):
    is_ctx_tile = pl.program_id(0) == pl.num_programs(0) - 1
    return jnp.where(is_ctx_tile, h_ctx_ref[...], h_lat_ref[...])


def _stream_specs(stream, n_lat_tiles):
    _, _, ctx_block = stream
    return [pl.BlockSpec((TM, D_MODEL), lambda i: (jnp.minimum(i, n_lat_tiles - 1), 0)),
            pl.BlockSpec((TM, D_MODEL), lambda i: (ctx_block, 0))]


def _in_proj_kernel(h_lat_ref, h_ctx_ref, g_ref, sh_ref, sc_ref, w_ref, cos_ref, sin_ref, perm_ref,
                    q_ref, kv_ref, x_ref, gy_ref, sga_ref, sgr_ref):
    n = _norm_mod(_stream_tile(h_lat_ref, h_ctx_ref), g_ref[...], sh_ref[...], sc_ref[...]).astype(BF16)
    cos = cos_ref[...]
    sin = sin_ref[...]
    lane = lax.broadcasted_iota(jnp.int32, (TM, LANES), 1)
    first_half = (lane & ROPE_FREQS) == 0

    def proj(chunk):
        return jnp.dot(n, w_ref[:, chunk * TN_IN:(chunk + 1) * TN_IN], preferred_element_type=F32)

    def rope_store(acc, n_heads, scale, ref, col0):
        for hh in range(n_heads):
            y = _rope(acc[:, hh * HEAD_DIM:(hh + 1) * HEAD_DIM], cos, sin, first_half)
            if scale != 1.0:
                y = y * scale
            ref[:, col0 + hh * HEAD_DIM:col0 + (hh + 1) * HEAD_DIM] = y.astype(BF16)

    for c in range(2):
        rope_store(proj(c), 4, ATTN_SCALE, q_ref, c * TN_IN)
    acc = proj(2)
    rope_store(acc, 2, 1.0, kv_ref, 0)
    kv_ref[:, KV_COLS:] = acc[:, KV_COLS:].astype(BF16)
    n_perm = jnp.dot(perm_ref[...], n, preferred_element_type=F32).astype(BF16)
    for c in range(2):
        cols = slice(c * TN_IN, (c + 1) * TN_IN)
        x_ref[:, cols] = jnp.dot(n_perm, w_ref[:, (3 + c) * TN_IN:(4 + c) * TN_IN],
                                 preferred_element_type=F32)
        gy_ref[:, cols] = jax.nn.gelu(proj(5 + c)).astype(BF16)
        sga_ref[:, cols] = _sigmoid(proj(7 + c)).astype(BF16)
        sgr_ref[:, cols] = _sigmoid(proj(9 + c)).astype(BF16)


def _in_proj(stream, g, mod3, w_in, layer, cos_t, sin_t, perm, *, n_lat_tiles, tiles_per_batch, n_batch):
    T = (n_lat_tiles + 1) * TM

    def grp(i):
        return jnp.minimum(i // tiles_per_batch, n_batch)

    def pos_tile(i):
        return jnp.where(i < n_lat_tiles, i % tiles_per_batch, tiles_per_batch)

    def tok(width):
        return pl.BlockSpec((TM, width), lambda i: (i, 0))

    out_shape = (
        jax.ShapeDtypeStruct((T, Q_COLS), BF16),
        jax.ShapeDtypeStruct((T, 2 * KV_COLS), BF16),
        jax.ShapeDtypeStruct((T, D_RNN), F32),
        jax.ShapeDtypeStruct((T, D_RNN), BF16),
        jax.ShapeDtypeStruct((T, D_MODEL), BF16),
        jax.ShapeDtypeStruct((T, D_MODEL), BF16),
    )
    return pl.pallas_call(
        _in_proj_kernel,
        grid=(T // TM,),
        in_specs=_stream_specs(stream, n_lat_tiles) + [
            pl.BlockSpec((None, 1, D_MODEL), lambda i: (layer, 0, 0)),
            pl.BlockSpec((None, 1, D_MODEL), lambda i: (grp(i), 0, 0)),
            pl.BlockSpec((None, 1, D_MODEL), lambda i: (grp(i), 0, 1)),
            pl.BlockSpec((None, D_MODEL, IN_COLS), lambda i: (layer, 0, 0)),
            pl.BlockSpec((TM, HEAD_DIM), lambda i: (pos_tile(i), 0)),
            pl.BlockSpec((TM, HEAD_DIM), lambda i: (pos_tile(i), 0)),
            pl.BlockSpec((TM, TM), lambda i: (0, 0)),
        ],
        out_specs=(tok(Q_COLS), tok(2 * KV_COLS), tok(D_RNN), tok(D_RNN), tok(D_MODEL), tok(D_MODEL)),
        out_shape=out_shape,
        compiler_params=_cparams(1),
        name="in_proj",
    )(stream[0], stream[1], g, mod3, mod3, w_in, cos_t, sin_t, perm)


def _stack_heads(qt):
    return jnp.concatenate([qt[:, g * HEAD_DIM:(g + 1) * HEAD_DIM] for g in range(N_GROUPS)], axis=0)


def _sink_col(sink_ref, kvh, rows):
    return jnp.concatenate(
        [jnp.full((rows, 1), sink_ref[kvh * N_GROUPS + g] * LOG2E, F32) for g in range(N_GROUPS)], axis=0)


def _ones_column(n_keys):
    lane = lax.broadcasted_iota(jnp.int32, (n_keys, HEAD_DIM), 1)
    return jnp.where(lane == 0, 1.0, 0.0).astype(BF16)


def _softmax_pv(s, sink, vall):
    m = jnp.maximum(jnp.max(s, axis=1, keepdims=True), sink)
    p = jnp.exp2(s - m).astype(BF16)
    v_aug = jnp.concatenate([vall, _ones_column(vall.shape[0])], axis=1)
    oa = jnp.dot(p, v_aug, preferred_element_type=F32)
    denom = oa[:, HEAD_DIM:HEAD_DIM + 1] + jnp.exp2(sink - m)
    return oa[:, :HEAD_DIM] / denom


def _ctx_attn_kernel(sink_ref, q_ref, kc_ref, vc_ref, o_ref):
    kvh = pl.program_id(1)
    rows = q_ref.shape[0]
    qs = _stack_heads(q_ref[...])
    s = lax.dot_general(qs, kc_ref[...], (((1,), (1,)), ((), ())), preferred_element_type=F32)
    o = _softmax_pv(s, _sink_col(sink_ref, kvh, rows), vc_ref[...])
    for g in range(N_GROUPS):
        o_ref[:, g * HEAD_DIM:(g + 1) * HEAD_DIM] = o[g * rows:(g + 1) * rows].astype(BF16)


def _context_attention(sink, q, kv, *, n_batch, seq, ctx_len):
    gw = N_GROUPS * HEAD_DIM
    ctx_blk0 = n_batch * seq // ctx_len
    grid_spec = pltpu.PrefetchScalarGridSpec(
        num_scalar_prefetch=1,
        grid=(n_batch, N_KV_HEADS),
        in_specs=[
            pl.BlockSpec((ctx_len, gw), lambda b, k, s: (ctx_blk0 + b, k)),
            pl.BlockSpec((ctx_len, HEAD_DIM), lambda b, k, s: (ctx_blk0 + b, k)),
            pl.BlockSpec((ctx_len, HEAD_DIM), lambda b, k, s: (ctx_blk0 + b, N_KV_HEADS + k)),
        ],
        out_specs=pl.BlockSpec((ctx_len, gw), lambda b, k, s: (b, k)),
    )
    return pl.pallas_call(
        _ctx_attn_kernel,
        grid_spec=grid_spec,
        out_shape=jax.ShapeDtypeStruct((n_batch * ctx_len, Q_COLS), BF16),
        compiler_params=_cparams(2),
        name="context_attention",
    )(sink, q, kv, kv)


def _softplus(z):
    return jnp.maximum(z, 0.0) + jnp.log1p(jnp.exp(-jnp.abs(z)))


def _scan8(a, b, row, reverse):
    for s in (1, 2, 4):
        if reverse:
            keep = row < SUBLANES - s
            shift = SUBLANES - s
        else:
            keep = row >= s
            shift = s
        a_sh = jnp.where(keep, pltpu.roll(a, shift, 0), 1.0)
        b_sh = jnp.where(keep, pltpu.roll(b, shift, 0), 0.0)
        b = a * b_sh + b
        a = a * a_sh
    return a, b


def _attend_tile(sink_ref, q_ref, kv_ref, kvc_ref, o_ref, tile, n_blocks, kv_heads):
    rows = ATTN_BLOCK * N_GROUPS
    qi = lax.broadcasted_iota(jnp.int32, (rows, ATTN_BLOCK), 0) & (ATTN_BLOCK - 1)
    kj = lax.broadcasted_iota(jnp.int32, (rows, ATTN_BLOCK), 1)
    tri_prev = kj >= qi
    tri_next = kj <= qi
    q_per_tile = TT // ATTN_BLOCK
    gw = N_GROUPS * HEAD_DIM
    for kvh in kv_heads:
        kcol = slice(kvh * HEAD_DIM, (kvh + 1) * HEAD_DIM)
        vcol = slice(KV_COLS + kvh * HEAD_DIM, KV_COLS + (kvh + 1) * HEAD_DIM)
        sink = _sink_col(sink_ref, kvh, ATTN_BLOCK)
        for qb in range(q_per_tile):
            n = tile * q_per_tile + qb
            starts = [pl.multiple_of(jnp.maximum(n - 1, 0) * ATTN_BLOCK, ATTN_BLOCK),
                      pl.multiple_of(n * ATTN_BLOCK, ATTN_BLOCK),
                      pl.multiple_of(jnp.minimum(n + 1, n_blocks - 1) * ATTN_BLOCK, ATTN_BLOCK)]
            qrows = slice(qb * ATTN_BLOCK, (qb + 1) * ATTN_BLOCK)
            qs = _stack_heads(q_ref[qrows, kvh * gw:(kvh + 1) * gw])
            kall = jnp.concatenate([kv_ref[pl.ds(r, ATTN_BLOCK), kcol] for r in starts] + [kvc_ref[:, kcol]],
                                   axis=0)
            vall = jnp.concatenate([kv_ref[pl.ds(r, ATTN_BLOCK), vcol] for r in starts] + [kvc_ref[:, vcol]],
                                   axis=0)
            s = lax.dot_general(qs, kall, (((1,), (1,)), ((), ())), preferred_element_type=F32)
            pen_prev = jnp.where(n > 0, 0.0, NEG_INF)
            pen_next = jnp.where(n < n_blocks - 1, 0.0, NEG_INF)
            sp = jnp.where(tri_prev, s[:, :ATTN_BLOCK] + pen_prev, NEG_INF)
            sn = jnp.where(tri_next, s[:, 2 * ATTN_BLOCK:3 * ATTN_BLOCK] + pen_next, NEG_INF)
            s = jnp.concatenate([sp, s[:, ATTN_BLOCK:2 * ATTN_BLOCK], sn, s[:, 3 * ATTN_BLOCK:]], axis=1)
            o = _softmax_pv(s, sink, vall)
            for g in range(N_GROUPS):
                col0 = kvh * gw + g * HEAD_DIM
                o_ref[qrows, col0:col0 + HEAD_DIM] = o[g * ATTN_BLOCK:(g + 1) * ATTN_BLOCK].astype(BF16)


def _mixer_kernel(xf_ref, xfp_ref, xfn_ref, xb_ref, xbp_ref, xbn_ref, cw_ref, cb_ref, wcat_ref,
                  gb_ref, lam_ref, unperm_ref, sink_ref, q_ref, kv_ref, kvc_ref,
                  hf_ref, hb_ref, attn_ref, a_s, b_s, h_bf, carry, *, n_blocks):
    j = pl.program_id(1)
    last_j = pl.num_programs(1) - 1

    @pl.when(j == 0)
    def _():
        carry[...] = jnp.zeros_like(carry)

    seg_start = (j <= 1, (j == 0) | (j == last_j))
    seg_end = ((j == 0) | (j == last_j), j <= 1)
    mains = (xf_ref, xb_ref)
    prevs = (xfp_ref, xbp_ref)
    nexts = (xfn_ref, xbn_ref)
    sub = lax.broadcasted_iota(jnp.int32, (SUBLANES, D_RNN), 0)
    S8 = SUBLANES

    for d in range(2):
        x0 = mains[d][...]
        halo = prevs[d][...]
        t_m1 = jnp.where(seg_start[d], 0.0, halo[2 * S8 - 1:2 * S8, :])
        t_m2 = jnp.where(seg_start[d], 0.0, halo[S8 - 1:S8, :])
        t_p1 = jnp.where(seg_end[d], 0.0, nexts[d][0:1, :])
        g_m1 = jnp.where(sub == 0, t_m1, pltpu.roll(x0[TT - S8:, :], 1, 0))
        g_m2 = jnp.where(sub == 0, t_m2, pltpu.roll(x0[TT - 2 * S8:TT - S8, :], 1, 0))
        g_p1 = jnp.where(sub == S8 - 1, t_p1, pltpu.roll(x0[:S8, :], S8 - 1, 0))
        xc = cb_ref[...] + jnp.concatenate([g_m2, g_m1, x0[:TT - 2 * S8, :]], axis=0) * cw_ref[0:1, :]
        xc = xc + jnp.concatenate([g_m1, x0[:TT - S8, :]], axis=0) * cw_ref[1:2, :]
        xc = xc + x0 * cw_ref[2:3, :]
        xc = xc + jnp.concatenate([x0[S8:, :], g_p1], axis=0) * cw_ref[3:4, :]
        c_d = (-LRU_C * LOG2E) * _softplus(-lam_ref[d:d + 1, :])
        for blk in range(N_RNN_BLOCKS):
            sl = slice(blk * RNN_BLOCK, (blk + 1) * RNN_BLOCK)
            xcb = xc[:, sl]
            z = jnp.dot(xcb.astype(BF16), wcat_ref[d, blk], preferred_element_type=F32)
            r = _sigmoid(z[:, :RNN_BLOCK] + gb_ref[d, 0:1, sl])
            ig = _sigmoid(z[:, RNN_BLOCK:] + gb_ref[d, 1:2, sl])
            a = jnp.exp2(r * c_d[:, sl])
            v = 1.0 - a * a
            root = jnp.where(v > 0.0, v * lax.rsqrt(v), 0.0)
            a_s[d, :, sl] = a
            b_s[d, :, sl] = root * (ig * xcb)
        _attend_tile(sink_ref, q_ref, kv_ref, kvc_ref, attn_ref, jnp.maximum(j - 1, 0), n_blocks, (d,))

    def local(g, c):
        hf, pf, hb, pb = c
        rf = pl.multiple_of(g * S8, S8)
        rb = pl.multiple_of((SUBSEQ - 1 - g) * S8, S8)
        af = a_s[0, pl.ds(rf, S8), :]
        ab = a_s[1, pl.ds(rb, S8), :]
        hf = af * hf + b_s[0, pl.ds(rf, S8), :]
        hb = ab * hb + b_s[1, pl.ds(rb, S8), :]
        pf = af * pf
        pb = ab * pb
        b_s[0, pl.ds(rf, S8), :] = hf
        b_s[1, pl.ds(rb, S8), :] = hb
        a_s[0, pl.ds(rf, S8), :] = pf
        a_s[1, pl.ds(rb, S8), :] = pb
        return hf, pf, hb, pb

    zero = jnp.zeros((S8, D_RNN), F32)
    one = jnp.ones((S8, D_RNN), F32)
    hf, pf, hb, pb = lax.fori_loop(0, SUBSEQ, local, (zero, one, zero, one), unroll=4)

    af, bf = _scan8(pf, hf, sub, False)
    endf = af * carry[0] + bf
    h_in = [jnp.where(sub == 0, carry[0], pltpu.roll(endf, 1, 0))]
    carry[0] = jnp.broadcast_to(endf[S8 - 1:S8, :], (S8, D_RNN))
    ab, bb = _scan8(pb, hb, sub, True)
    endb = ab * carry[1] + bb
    h_in.append(jnp.where(sub == S8 - 1, carry[1], pltpu.roll(endb, S8 - 1, 0)))
    carry[1] = jnp.broadcast_to(endb[0:1, :], (S8, D_RNN))

    outs = (hf_ref, hb_ref)
    for d in range(2):
        h_in2 = jnp.concatenate([h_in[d], h_in[d]], axis=0)

        def fix(k, c, d=d, h_in2=h_in2):
            rows = pl.ds(pl.multiple_of(k * 2 * S8, 2 * S8), 2 * S8)
            h_bf[d, rows, :] = (b_s[d, rows, :] + a_s[d, rows, :] * h_in2).astype(BF16)
            return c

        lax.fori_loop(0, SUBSEQ // 2, fix, 0, unroll=2)
        outs[d][...] = jnp.dot(unperm_ref[...], h_bf[d], preferred_element_type=F32).astype(BF16)


def _time_permutation(n_rows):
    p = jnp.arange(n_rows)
    src = (p // TT) * TT + (p % SUBLANES) * SUBSEQ + (p % TT) // SUBLANES
    return (src[:, None] == jnp.arange(n_rows)[None, :]).astype(BF16)


def _token_mixer(x, conv_w, conv_b, wcat, gate_b, lam, sink, q, kv, *, n_batch, seq, ctx_len):
    T = x.shape[0]
    assert ctx_len == TT and seq % TT == 0
    tps = seq // TT
    n_lat_t = n_batch * tps
    halo_rows = 2 * SUBLANES
    per_halo = TT // halo_rows
    per8 = TT // SUBLANES
    last8 = T // SUBLANES - 1
    unperm = _time_permutation(TT).T

    def ftile(b, j):
        return jnp.where(j == 0, n_lat_t + b, b * tps + j - 1)

    def btile(b, j):
        return jnp.where(j == 0, n_lat_t + b, b * tps + tps - j)

    def main(tile):
        return pl.BlockSpec((TT, D_RNN), lambda b, j: (tile(b, j), 0))

    def prev(tile):
        return pl.BlockSpec((halo_rows, D_RNN), lambda b, j: (jnp.maximum(tile(b, j) * per_halo - 1, 0), 0))

    def nxt(tile):
        return pl.BlockSpec((SUBLANES, D_RNN),
                            lambda b, j: (jnp.minimum((tile(b, j) + 1) * per8, last8), 0))

    def const(shape):
        return pl.BlockSpec(shape, lambda b, j: (0,) * len(shape))

    def qtile(b, j):
        return b * tps + jnp.maximum(j - 1, 0)

    q_spec = pl.BlockSpec((TT, Q_COLS), lambda b, j: (qtile(b, j), 0))
    return pl.pallas_call(
        functools.partial(_mixer_kernel, n_blocks=seq // ATTN_BLOCK),
        grid=(n_batch, tps + 1),
        in_specs=[main(ftile), prev(ftile), nxt(ftile), main(btile), prev(btile), nxt(btile),
                  const(conv_w.shape), const(conv_b.shape), const(wcat.shape), const(gate_b.shape),
                  const(lam.shape), const(unperm.shape),
                  pl.BlockSpec(memory_space=pltpu.SMEM),
                  q_spec,
                  pl.BlockSpec((seq, 2 * KV_COLS), lambda b, j: (b, 0)),
                  pl.BlockSpec((ctx_len, 2 * KV_COLS), lambda b, j: (n_lat_t + b, 0))],
        out_specs=(main(ftile), main(btile), q_spec),
        out_shape=(jax.ShapeDtypeStruct((T, D_RNN), BF16), jax.ShapeDtypeStruct((T, D_RNN), BF16),
                   jax.ShapeDtypeStruct((n_batch * seq, Q_COLS), BF16)),
        scratch_shapes=[pltpu.VMEM((2, TT, D_RNN), F32),
                        pltpu.VMEM((2, TT, D_RNN), F32),
                        pltpu.VMEM((2, TT, D_RNN), BF16),
                        pltpu.VMEM((2, SUBLANES, D_RNN), F32)],
        compiler_params=_cparams(2),
        name="token_mixer",
    )(x, x, x, x, x, x, conv_w, conv_b, wcat, gate_b, lam, unperm, sink, q, kv, kv)


def _merge_kernel(h_lat_ref, h_ctx_ref, attn_ref, attn_ctx_ref, hf_ref, hb_ref, gy_ref, sga_ref, sgr_ref,
                  g1_ref, woa_ref, wol_ref, wout_ref, o_ref):
    rec = ((hf_ref[...].astype(F32) + hb_ref[...].astype(F32)) * gy_ref[...].astype(F32)).astype(BF16)
    is_ctx_tile = pl.program_id(0) == pl.num_programs(0) - 1
    attn = jnp.where(is_ctx_tile, attn_ctx_ref[...], attn_ref[...])
    ta = jnp.dot(attn, woa_ref[...], preferred_element_type=F32)
    tl = jnp.dot(rec, wol_ref[...], preferred_element_type=F32)
    m = sga_ref[...].astype(F32) * ta + sgr_ref[...].astype(F32) * tl
    y = jnp.dot(m.astype(BF16), wout_ref[...], preferred_element_type=F32)
    o_ref[...] = _stream_tile(h_lat_ref, h_ctx_ref) + g1_ref[...] * y


def _merge(stream, attn, attn_ctx, hf, hb, gy, sga, sgr, mod3, woa, wol, wout, layer, *,
           tiles_per_batch, n_batch):
    n_lat_tiles = attn.shape[0] // TM
    T = (n_lat_tiles + 1) * TM
    assert attn_ctx.shape[0] == TM and hf.shape[0] == T

    def grp(i):
        return jnp.minimum(i // tiles_per_batch, n_batch)

    tok = pl.BlockSpec((TM, D_MODEL), lambda i: (i, 0))
    wsp = pl.BlockSpec((None, D_MODEL, D_MODEL), lambda i: (layer, 0, 0))
    return pl.pallas_call(
        _merge_kernel,
        grid=(T // TM,),
        in_specs=_stream_specs(stream, n_lat_tiles) + [
                  pl.BlockSpec((TM, Q_COLS), lambda i: (jnp.minimum(i, n_lat_tiles - 1), 0)),
                  pl.BlockSpec((TM, Q_COLS), lambda i: (0, 0)),
                  tok, tok, tok, tok, tok,
                  pl.BlockSpec((None, 1, D_MODEL), lambda i: (grp(i), 0, 2)),
                  wsp, wsp, wsp],
        out_specs=tok,
        out_shape=jax.ShapeDtypeStruct((T, D_MODEL), F32),
        compiler_params=_cparams(1),
        name="merge",
    )(stream[0], stream[1], attn, attn_ctx, hf, hb, gy, sga, sgr, mod3, woa, wol, wout)


def _swiglu_partial(n, wg, wu, wd):
    gt = jnp.dot(n, wg, preferred_element_type=F32)
    ut = jnp.dot(n, wu, preferred_element_type=F32)
    act = (gt * _sigmoid(gt) * ut).astype(BF16)
    return jnp.dot(act, wd, preferred_element_type=F32)


FFN_F_CHUNKS = 2


def _ffn_kernel(h_ref, g_ref, sh_ref, sc_ref, g2_ref, wg_ref, wu_ref, wd_ref, o_ref):
    h = h_ref[...]
    n = _norm_mod(h, g_ref[...], sh_ref[...], sc_ref[...]).astype(BF16)
    tf = D_FF // FFN_F_CHUNKS
    acc = None
    for c in range(FFN_F_CHUNKS):
        cols = slice(c * tf, (c + 1) * tf)
        part = _swiglu_partial(n, wg_ref[:, cols], wu_ref[:, cols], wd_ref[cols, :])
        acc = part if acc is None else acc + part
    o_ref[...] = h + g2_ref[...] * acc


def _ffn(h, g, mod3, wg, wu, wd, layer, ff_layer, *, tiles_per_batch, n_batch):
    T = h.shape[0]
    assert (D_FF // FFN_F_CHUNKS) % LANES == 0

    def grp(i):
        return jnp.minimum(i // tiles_per_batch, n_batch)

    def modspec(k):
        return pl.BlockSpec((None, 1, D_MODEL), lambda i: (grp(i), 0, k))

    tok = pl.BlockSpec((TM, D_MODEL), lambda i: (i, 0))
    return pl.pallas_call(
        _ffn_kernel,
        grid=(T // TM,),
        in_specs=[tok, pl.BlockSpec((None, 1, D_MODEL), lambda i: (layer, 0, 0)),
                  modspec(3), modspec(4), modspec(5),
                  pl.BlockSpec((None, D_MODEL, D_FF), lambda i: (ff_layer, 0, 0)),
                  pl.BlockSpec((None, D_MODEL, D_FF), lambda i: (ff_layer, 0, 0)),
                  pl.BlockSpec((None, D_FF, D_MODEL), lambda i: (ff_layer, 0, 0))],
        out_specs=tok,
        out_shape=jax.ShapeDtypeStruct((T, D_MODEL), F32),
        compiler_params=_cparams(1),
        name="dense_ffn",
    )(h, g, mod3, mod3, mod3, wg, wu, wd)


ROUTE_E1, ROUTE_E2, ROUTE_W1, ROUTE_W2, ROUTE_R1, ROUTE_R2 = range(6)


def _dot_split(a, b):
    a_hi = a.astype(BF16)
    a_lo = (a - a_hi.astype(F32)).astype(BF16)
    b_hi = b.astype(BF16)
    b_lo = (b - b_hi.astype(F32)).astype(BF16)

    def mm(x, y):
        return jnp.dot(x, y, preferred_element_type=F32)

    return mm(a_hi, b_hi) + (mm(a_hi, b_lo) + mm(a_lo, b_hi))


def _router_kernel(h_ref, g_ref, sh_ref, sc_ref, wr_ref, route_ref, cnt_ref, zero_ref, run):
    @pl.when(pl.program_id(0) == 0)
    def _():
        run[...] = jnp.zeros_like(run)

    n = _norm_mod(h_ref[...], g_ref[...], sh_ref[...], sc_ref[...])
    logits = _dot_split(n, wr_ref[...])
    lane = lax.broadcasted_iota(jnp.int32, logits.shape, 1)
    logits = jnp.where(lane < N_EXPERTS, logits, -jnp.inf)
    m1 = jnp.max(logits, axis=1, keepdims=True)
    i1 = jnp.min(jnp.where(logits == m1, lane, LANES), axis=1, keepdims=True)
    rest = jnp.where(lane == i1, -jnp.inf, logits)
    m2 = jnp.max(rest, axis=1, keepdims=True)
    i2 = jnp.min(jnp.where(rest == m2, lane, LANES), axis=1, keepdims=True)
    e2 = jnp.exp(m2 - m1)
    w1 = 1.0 / (1.0 + e2)
    w2 = e2 / (1.0 + e2)

    hit1 = lane == i1
    hit2 = lane == i2
    onehot = jnp.where(hit1 | hit2, 1.0, 0.0)
    r_i = lax.broadcasted_iota(jnp.int32, (TM, TM), 0)
    c_i = lax.broadcasted_iota(jnp.int32, (TM, TM), 1)
    lower = jnp.where(c_i < r_i, 1.0, 0.0).astype(BF16)
    prefix = jnp.dot(lower, onehot.astype(BF16), preferred_element_type=F32) + run[0:1, :]
    rank1 = jnp.sum(jnp.where(hit1, prefix, 0.0), axis=1, keepdims=True)
    rank2 = jnp.sum(jnp.where(hit2, prefix, 0.0), axis=1, keepdims=True)
    run[...] = run[...] + jnp.sum(onehot, axis=0, keepdims=True)
    cnt_ref[...] = run[...]

    rec = jnp.zeros(logits.shape, F32)
    for k, v in ((ROUTE_E1, i1.astype(F32)), (ROUTE_E2, i2.astype(F32)), (ROUTE_W1, w1), (ROUTE_W2, w2),
                 (ROUTE_R1, rank1), (ROUTE_R2, rank2)):
        rec = jnp.where(lane == k, v, rec)
    route_ref[...] = rec
    zero_ref[...] = jnp.zeros_like(zero_ref)


def _router(h, g2, mod3, w_r, layer, sorted_rows, *, tiles_per_batch, n_batch):
    T = h.shape[0]
    n_steps = T // TM
    zero_rows = -(-sorted_rows // (n_steps * SUBLANES)) * SUBLANES

    def grp(i):
        return jnp.minimum(i // tiles_per_batch, n_batch)

    def modspec(k):
        return pl.BlockSpec((None, 1, D_MODEL), lambda i: (grp(i), 0, k))

    return pl.pallas_call(
        _router_kernel,
        grid=(T // TM,),
        in_specs=[pl.BlockSpec((TM, D_MODEL), lambda i: (i, 0)),
                  pl.BlockSpec((None, 1, D_MODEL), lambda i: (layer, 0, 0)),
                  modspec(3), modspec(4),
                  pl.BlockSpec((D_MODEL, LANES), lambda i: (0, 0))],
        out_specs=(pl.BlockSpec((TM, LANES), lambda i: (i, 0)),
                   pl.BlockSpec((SUBLANES, LANES), lambda i: (0, 0)),
                   pl.BlockSpec((zero_rows, D_MODEL), lambda i: (i, 0))),
        out_shape=(jax.ShapeDtypeStruct((T, LANES), F32), jax.ShapeDtypeStruct((SUBLANES, LANES), F32),
                   jax.ShapeDtypeStruct((n_steps * zero_rows, D_MODEL), F32)),
        scratch_shapes=[pltpu.VMEM((SUBLANES, LANES), F32)],
        compiler_params=_cparams(1),
        name="moe_router",
    )(h, g2, mod3, mod3, w_r)


def _row_copy(src_ref, src_row, dst_ref, dst_row, sem):
    return pltpu.make_async_copy(src_ref.at[pl.ds(src_row, 1)], dst_ref.at[pl.ds(dst_row, 1)], sem)


def _dispatch_kernel(dest_ref, h_ref, g_ref, sh_ref, sc_ref, xs_in_ref, xs_ref, n_scr, sem):
    del xs_in_ref
    i = pl.program_id(0)
    base = i * (2 * TM)
    cur = i % 2
    n_scr[cur] = _norm_mod(h_ref[...], g_ref[...], sh_ref[...], sc_ref[...])

    def start(r, carry):
        for s in range(2):
            _row_copy(n_scr.at[cur], r, xs_ref, dest_ref[base + 2 * r + s], sem.at[cur]).start(priority=s)
        return carry

    lax.fori_loop(0, TM, start, 0, unroll=ROW_DMA_UNROLL)

    def wait_all(buf):
        for s in range(2):
            pltpu.make_async_copy(n_scr.at[buf], xs_ref.at[pl.ds(0, TM)], sem.at[buf]).wait()

    @pl.when(i > 0)
    def _():
        wait_all(1 - cur)

    @pl.when(i == pl.num_programs(0) - 1)
    def _():
        wait_all(cur)


def _dispatch(dest, h, g2, mod3, xs_zero, layer, *, tiles_per_batch, n_batch):
    T = h.shape[0]

    def grp(i):
        return jnp.minimum(i // tiles_per_batch, n_batch)

    def modspec(k):
        return pl.BlockSpec((None, 1, D_MODEL), lambda i, d: (grp(i), 0, k))

    grid_spec = pltpu.PrefetchScalarGridSpec(
        num_scalar_prefetch=1,
        grid=(T // TM,),
        in_specs=[pl.BlockSpec((TM, D_MODEL), lambda i, d: (i, 0)),
                  pl.BlockSpec((None, 1, D_MODEL), lambda i, d: (layer, 0, 0)),
                  modspec(3), modspec(4),
                  pl.BlockSpec(memory_space=pl.ANY)],
        out_specs=pl.BlockSpec(memory_space=pl.ANY),
        scratch_shapes=[pltpu.VMEM((2, TM, D_MODEL), F32), pltpu.SemaphoreType.DMA((2,))],
    )
    return pl.pallas_call(
        _dispatch_kernel,
        grid_spec=grid_spec,
        out_shape=jax.ShapeDtypeStruct(xs_zero.shape, xs_zero.dtype),
        input_output_aliases={5: 0},
        compiler_params=_cparams(1),
        name="moe_dispatch",
    )(dest, h, g2, mod3, mod3, xs_zero)


def _expert_kernel(te_ref, nu_ref, rows_ref, x_ref, wg_ref, wu_ref, wd_ref, acc, xb):
    del te_ref, nu_ref
    k = pl.program_id(0)
    f = pl.program_id(1)

    @pl.when(f == 0)
    def _():
        xb[...] = x_ref[...].astype(BF16)
        acc[...] = jnp.zeros_like(acc)

    wg = wg_ref[...].astype(BF16)
    wu = wu_ref[...].astype(BF16)
    wd = wd_ref[...].astype(BF16)
    acc[:TG_SUB, :] += _swiglu_partial(xb[:TG_SUB, :], wg, wu, wd)
    for sub in range(1, TG // TG_SUB):
        rows = slice(sub * TG_SUB, (sub + 1) * TG_SUB)

        @pl.when(rows_ref[k] > sub * TG_SUB)
        def _():
            acc[rows, :] += _swiglu_partial(xb[rows, :], wg, wu, wd)


def _experts(tile_expert, n_used, tile_rows, xs, wg, wu, wd, layer, n_tiles):
    P = n_tiles * TG
    assert xs.shape[0] >= P
    n_f = MOE_F_CHUNKS
    tf = D_FF_EXPERT // n_f
    assert tf % LANES == 0

    def fsel(k, f, nu):
        return jnp.where(k < nu[0], f, n_f - 1)

    grid_spec = pltpu.PrefetchScalarGridSpec(
        num_scalar_prefetch=3,
        grid=(P // TG, n_f),
        in_specs=[pl.BlockSpec((TG, D_MODEL), lambda k, f, te, nu, tr: (k, 0)),
                  pl.BlockSpec((None, None, D_MODEL, tf), lambda k, f, te, nu, tr: (layer, te[k], 0, fsel(k, f, nu))),
                  pl.BlockSpec((None, None, D_MODEL, tf), lambda k, f, te, nu, tr: (layer, te[k], 0, fsel(k, f, nu))),
                  pl.BlockSpec((None, None, tf, D_MODEL), lambda k, f, te, nu, tr: (layer, te[k], fsel(k, f, nu), 0))],
        out_specs=pl.BlockSpec((TG, D_MODEL), lambda k, f, te, nu, tr: (k, 0)),
        scratch_shapes=[pltpu.VMEM((TG, D_MODEL), BF16)],
    )
    return pl.pallas_call(
        _expert_kernel,
        grid_spec=grid_spec,
        out_shape=jax.ShapeDtypeStruct((P, D_MODEL), F32),
        compiler_params=_cparams(2),
        name="moe_experts",
    )(tile_expert, n_used, tile_rows, xs, wg, wu, wd)


def _combine_kernel(dest_ref, h_ref, g2_ref, route_ref, fg_ref, y_ref, o_ref, ybuf, sem, *, final):
    i = pl.program_id(0)
    n_steps = pl.num_programs(0)

    def gather(tile, buf):
        base = tile * (2 * TM)

        def start(r, carry):
            for s in range(2):
                _row_copy(y_ref, dest_ref[base + 2 * r + s], ybuf.at[buf], s * TM + r,
                          sem.at[buf]).start(priority=s)
            return carry

        lax.fori_loop(0, TM, start, 0, unroll=ROW_DMA_UNROLL)

    @pl.when(i == 0)
    def _():
        gather(0, 0)

    @pl.when(i + 1 < n_steps)
    def _():
        gather(i + 1, (i + 1) % 2)

    cur = i % 2
    for s in range(2):
        pltpu.make_async_copy(y_ref.at[pl.ds(0, TM)], ybuf.at[cur, pl.ds(s * TM, TM)], sem.at[cur]).wait()
    route = route_ref[...]
    f = (route[:, ROUTE_W1:ROUTE_W1 + 1] * ybuf[cur, 0:TM, :]
         + route[:, ROUTE_W2:ROUTE_W2 + 1] * ybuf[cur, TM:, :])
    out = h_ref[...] + g2_ref[...] * f
    if final:
        ms = jnp.mean(out * out, axis=-1, keepdims=True)
        out = out * lax.rsqrt(ms + EPS) * fg_ref[...]
    o_ref[...] = out


def _combine(dest, h, mod3, route, final_g, y, *, final, tiles_per_batch, n_batch):
    n_tok_tiles = h.shape[0] // TM - (1 if final else 0)

    def grp(i):
        return jnp.minimum(i // tiles_per_batch, n_batch)

    grid_spec = pltpu.PrefetchScalarGridSpec(
        num_scalar_prefetch=1,
        grid=(n_tok_tiles,),
        in_specs=[pl.BlockSpec((TM, D_MODEL), lambda i, d: (i, 0)),
                  pl.BlockSpec((None, 1, D_MODEL), lambda i, d: (grp(i), 0, 5)),
                  pl.BlockSpec((TM, LANES), lambda i, d: (i, 0)),
                  pl.BlockSpec((1, D_MODEL), lambda i, d: (0, 0)),
                  pl.BlockSpec(memory_space=pl.ANY)],
        out_specs=pl.BlockSpec((TM, D_MODEL), lambda i, d: (i, 0)),
        scratch_shapes=[pltpu.VMEM((2, 2 * TM, D_MODEL), F32), pltpu.SemaphoreType.DMA((2,))],
    )
    return pl.pallas_call(
        functools.partial(_combine_kernel, final=final),
        grid_spec=grid_spec,
        out_shape=jax.ShapeDtypeStruct((n_tok_tiles * TM, D_MODEL), F32),
        compiler_params=_cparams(1),
        name="moe_combine",
    )(dest, h, mod3, route, final_g, y)


def _moe_layer(h, g2, mod3, w_r, wg, wu, wd, final_g, layer, moe_layer, *, final, **geo):
    T = h.shape[0]
    n_tiles = (2 * T) // TG + N_EXPERTS
    route, cnt, xs_zero = _router(h, g2, mod3, w_r, layer, n_tiles * TG, **geo)

    counts = cnt[0, :N_EXPERTS].astype(jnp.int32)
    padded = ((counts + TG - 1) // TG) * TG
    ends = jnp.cumsum(padded)
    offs = ends - padded
    e12 = route[:, ROUTE_E1:ROUTE_E2 + 1].astype(jnp.int32)
    r12 = route[:, ROUTE_R1:ROUTE_R2 + 1].astype(jnp.int32)
    onehot = e12[:, :, None] == jnp.arange(N_EXPERTS)[None, None, :]
    dest = (jnp.sum(jnp.where(onehot, offs[None, None, :], 0), axis=-1) + r12).reshape(2 * T)
    n_used = (ends[-1] // TG).reshape(1)
    tiles = jnp.arange(n_tiles)
    te_raw = jnp.sum(tiles[:, None] >= (ends // TG)[None, :], axis=1)
    tile_expert = jnp.minimum(te_raw, N_EXPERTS - 1).astype(jnp.int32)
    sel = tile_expert[:, None] == jnp.arange(N_EXPERTS)[None, :]
    cnt_k = jnp.sum(jnp.where(sel, counts[None, :], 0), axis=1)
    off_k = jnp.sum(jnp.where(sel, offs[None, :], 0), axis=1)
    tile_rows = jnp.where(te_raw < N_EXPERTS, jnp.clip(cnt_k - (tiles * TG - off_k), 0, TG), 0)

    xs = _dispatch(dest, h, g2, mod3, xs_zero, layer, **geo)
    y = _experts(tile_expert, n_used, tile_rows.astype(jnp.int32), xs, wg, wu, wd, moe_layer, n_tiles)
    return _combine(dest, h, mod3, route, final_g, y, final=final, **geo)


def _rope_tables(seq):
    assert seq % GRID_W == 0
    n_rows = seq // GRID_W
    inv = ROPE_THETA ** (-jnp.arange(ROPE_FREQS, dtype=F32) / ROPE_FREQS)
    ang_r = jnp.arange(n_rows, dtype=F32)[:, None] * inv
    ang_c = jnp.arange(GRID_W, dtype=F32)[:, None] * inv
    cos_r, sin_r = (jnp.repeat(f(ang_r), GRID_W, axis=0) for f in (jnp.cos, jnp.sin))
    cos_c, sin_c = (jnp.tile(f(ang_c), (n_rows, 1)) for f in (jnp.cos, jnp.sin))
    cos = jnp.concatenate([cos_r, cos_r, cos_c, cos_c], axis=1)
    sin = jnp.concatenate([-sin_r, sin_r, -sin_c, sin_c], axis=1)
    cos = jnp.concatenate([cos, jnp.ones((TM, HEAD_DIM), F32)], axis=0)
    sin = jnp.concatenate([sin, jnp.zeros((TM, HEAD_DIM), F32)], axis=0)
    return cos, sin


def kernel(x, c, ctx, c_ctx, w_mod, b_mod, norm1_g, norm2_g, w_in, attn_sink, conv_w, conv_b, gate_a_w, gate_a_b, gate_x_w, gate_x_b, lru_lambda, w_o_attn, w_o_lru, w_out, ff_w_gate, ff_w_up, ff_w_down, router_w, exp_w_gate, exp_w_up, exp_w_down, final_g):
    n_batch, seq, _ = x.shape
    ctx_len = ctx.shape[1]
    assert n_batch * ctx_len == TM and seq % TM == 0 and n_batch + 1 <= MOD_ROWS
    n_lat = n_batch * seq
    tiles_per_batch = seq // TM
    geo = dict(tiles_per_batch=tiles_per_batch, n_batch=n_batch)
    shp = dict(n_batch=n_batch, seq=seq, ctx_len=ctx_len)

    cpad = jnp.zeros((MOD_ROWS, D_MODEL), F32).at[:n_batch].set(c).at[n_batch].set(c_ctx)
    mod = _modulation(cpad, w_mod, b_mod)
    cos_t, sin_t = _rope_tables(seq)
    perm = _time_permutation(TM)
    stream = (x.reshape(n_lat, D_MODEL), ctx.reshape(n_batch * ctx_len, D_MODEL), 0)

    g1 = norm1_g.reshape(DEPTH, 1, D_MODEL)
    g2 = norm2_g.reshape(DEPTH, 1, D_MODEL)
    w_in_b = w_in.astype(BF16)
    woa_b, wol_b, wout_b = w_o_attn.astype(BF16), w_o_lru.astype(BF16), w_out.astype(BF16)
    ffg_b, ffu_b, ffd_b = ff_w_gate.astype(BF16), ff_w_up.astype(BF16), ff_w_down.astype(BF16)

    for l in range(DEPTH):
        mod3 = mod[l].reshape(MOD_ROWS, 1, 6 * D_MODEL)
        if l > 0:
            stream = (h, h, n_lat // TM)
        q, kv, xr, gy, sga, sgr = _in_proj(stream, g1, mod3, w_in_b, l, cos_t, sin_t, perm,
                                           n_lat_tiles=n_lat // TM, **geo)
        attn_ctx = _context_attention(attn_sink[l], q, kv, **shp)
        wcat = jnp.concatenate([gate_a_w[l], gate_x_w[l]], axis=-1).astype(BF16)
        gate_b = jnp.stack([gate_a_b[l], gate_x_b[l]], axis=1)
        hf, hb, attn = _token_mixer(xr, conv_w[l], conv_b[l].reshape(1, D_RNN), wcat, gate_b, lru_lambda[l],
                                    attn_sink[l], q, kv, **shp)
        h = _merge(stream, attn, attn_ctx, hf, hb, gy, sga, sgr, mod3, woa_b, wol_b, wout_b, l, **geo)
        i = l // 2
        if l % 2 == 0:
            h = _ffn(h, g2, mod3, ffg_b, ffu_b, ffd_b, l, i, **geo)
        else:
            w_r = jnp.zeros((D_MODEL, LANES), F32).at[:, :N_EXPERTS].set(router_w[i])
            h = _moe_layer(h, g2, mod3, w_r, exp_w_gate, exp_w_up, exp_w_down,
                           final_g.reshape(1, D_MODEL), l, i, final=(l == DEPTH - 1), **geo)

    assert DEPTH % 2 == 0 and h.shape[0] == n_lat
    return h.reshape(n_batch, seq, D_MODEL)
```

```python
import functools

import jax
import jax.numpy as jnp
from jax import lax
from jax.experimental import pallas as pl
from jax.experimental.pallas import tpu as pltpu

F32 = jnp.float32
BF16 = jnp.bfloat16

D_MODEL = 1024
DEPTH = 4
GRID_W = 64
N_HEADS = 8
N_KV_HEADS = 2
HEAD_DIM = 128
N_GROUPS = N_HEADS // N_KV_HEADS
ATTN_BLOCK = 128
ROPE_THETA = 10000.0
ROPE_FREQS = HEAD_DIM // 4
D_RNN = 1024
N_RNN_BLOCKS = 8
RNN_BLOCK = D_RNN // N_RNN_BLOCKS
LRU_C = 8.0
D_FF = 2816
N_EXPERTS = 8
D_FF_EXPERT = 3584
EPS = 1e-6
NEG_INF = -1e30
Q_COLS = N_HEADS * HEAD_DIM
KV_COLS = N_KV_HEADS * HEAD_DIM
IN_COLS = Q_COLS + 2 * KV_COLS + 2 * D_RNN + 2 * D_MODEL
LOG2E = 1.4426950408889634
ATTN_SCALE = HEAD_DIM ** -0.5 * LOG2E

LANES = 128
SUBLANES = 8
TM = 512
TN_IN = 512
TT = 256
SUBSEQ = TT // SUBLANES
TG = 1024
TG_SUB = 512
MOE_F_CHUNKS = 7
ROW_DMA_UNROLL = 8
MOD_ROWS = 8
VMEM_LIMIT = 56 * 1024 * 1024


def _cparams(n_axes):
    return pltpu.CompilerParams(dimension_semantics=("arbitrary",) * n_axes,
                                vmem_limit_bytes=VMEM_LIMIT)


def _sigmoid(z):
    return 0.5 * jnp.tanh(0.5 * z) + 0.5


def _norm_mod(h, g, shift, scale):
    ms = jnp.mean(h * h, axis=-1, keepdims=True)
    y = h * lax.rsqrt(ms + EPS) * g
    return y * (1.0 + scale) + shift


def _mod_kernel(c_ref, w_ref, b_ref, o_ref):
    cv = c_ref[...]
    s = cv * _sigmoid(cv)
    o_ref[...] = _dot_split(s, w_ref[...]) + b_ref[...]


def _modulation(cpad, w_mod, b_mod):
    nchunk = 6
    return pl.pallas_call(
        _mod_kernel,
        grid=(DEPTH, nchunk),
        in_specs=[
            pl.BlockSpec((MOD_ROWS, D_MODEL), lambda l, n: (0, 0)),
            pl.BlockSpec((None, D_MODEL, D_MODEL), lambda l, n: (l, 0, n)),
            pl.BlockSpec((None, 1, D_MODEL), lambda l, n: (l, 0, n)),
        ],
        out_specs=pl.BlockSpec((None, MOD_ROWS, D_MODEL), lambda l, n: (l, 0, n)),
        out_shape=jax.ShapeDtypeStruct((DEPTH, MOD_ROWS, 6 * D_MODEL), F32),
        compiler_params=_cparams(2),
        name="modulation",
    )(cpad, w_mod, b_mod.reshape(DEPTH, 1, 6 * D_MODEL))


def _rope(xh, cos, sin_signed, first_half):
    sw = jnp.where(first_half, pltpu.roll(xh, HEAD_DIM - ROPE_FREQS, 1), pltpu.roll(xh, ROPE_FREQS, 1))
    return xh * cos + sw * sin_signed


def _stream_tile(h_lat_ref, h_ctx_ref):
    is_ctx_tile = pl.program_id(0) == pl.num_programs(0) - 1
    return jnp.where(is_ctx_tile, h_ctx_ref[...], h_lat_ref[...])


def _stream_specs(stream, n_lat_tiles):
    _, _, ctx_block = stream
    return [pl.BlockSpec((TM, D_MODEL), lambda i: (jnp.minimum(i, n_lat_tiles - 1), 0)),
            pl.BlockSpec((TM, D_MODEL), lambda i: (ctx_block, 0))]


def _in_proj_kernel(h_lat_ref, h_ctx_ref, g_ref, sh_ref, sc_ref, w_ref, cos_ref, sin_ref, perm_ref,
                    q_ref, kv_ref, x_ref, gy_ref, sga_ref, sgr_ref):
    n = _norm_mod(_stream_tile(h_lat_ref, h_ctx_ref), g_ref[...], sh_ref[...], sc_ref[...]).astype(BF16)
    cos = cos_ref[...]
    sin = sin_ref[...]
    lane = lax.broadcasted_iota(jnp.int32, (TM, LANES), 1)
    first_half = (lane & ROPE_FREQS) == 0

    def proj(chunk):
        return jnp.dot(n, w_ref[:, chunk * TN_IN:(chunk + 1) * TN_IN], preferred_element_type=F32)

    def rope_store(acc, n_heads, scale, ref, col0):
        for hh in range(n_heads):
            y = _rope(acc[:, hh * HEAD_DIM:(hh + 1) * HEAD_DIM], cos, sin, first_half)
            if scale != 1.0:
                y = y * scale
            ref[:, col0 + hh * HEAD_DIM:col0 + (hh + 1) * HEAD_DIM] = y.astype(BF16)

    for c in range(2):
        rope_store(proj(c), 4, ATTN_SCALE, q_ref, c * TN_IN)
    acc = proj(2)
    rope_store(acc, 2, 1.0, kv_ref, 0)
    kv_ref[:, KV_COLS:] = acc[:, KV_COLS:].astype(BF16)
    n_perm = jnp.dot(perm_ref[...], n, preferred_element_type=F32).astype(BF16)
    for c in range(2):
        cols = slice(c * TN_IN, (c + 1) * TN_IN)
        x_ref[:, cols] = jnp.dot(n_perm, w_ref[:, (3 + c) * TN_IN:(4 + c) * TN_IN],
                                 preferred_element_type=F32)
        gy_ref[:, cols] = jax.nn.gelu(proj(5 + c)).astype(BF16)
        sga_ref[:, cols] = _sigmoid(proj(7 + c)).astype(BF16)
        sgr_ref[:, cols] = _sigmoid(proj(9 + c)).astype(BF16)


def _in_proj(stream, g, mod3, w_in, layer, cos_t, sin_t, perm, *, n_lat_tiles, tiles_per_batch, n_batch):
    T = (n_lat_tiles + 1) * TM

    def grp(i):
        return jnp.minimum(i // tiles_per_batch, n_batch)

    def pos_tile(i):
        return jnp.where(i < n_lat_tiles, i % tiles_per_batch, tiles_per_batch)

    def tok(width):
        return pl.BlockSpec((TM, width), lambda i: (i, 0))

    out_shape = (
        jax.ShapeDtypeStruct((T, Q_COLS), BF16),
        jax.ShapeDtypeStruct((T, 2 * KV_COLS), BF16),
        jax.ShapeDtypeStruct((T, D_RNN), F32),
        jax.ShapeDtypeStruct((T, D_RNN), BF16),
        jax.ShapeDtypeStruct((T, D_MODEL), BF16),
        jax.ShapeDtypeStruct((T, D_MODEL), BF16),
    )
    return pl.pallas_call(
        _in_proj_kernel,
        grid=(T // TM,),
        in_specs=_stream_specs(stream, n_lat_tiles) + [
            pl.BlockSpec((None, 1, D_MODEL), lambda i: (layer, 0, 0)),
            pl.BlockSpec((None, 1, D_MODEL), lambda i: (grp(i), 0, 0)),
            pl.BlockSpec((None, 1, D_MODEL), lambda i: (grp(i), 0, 1)),
            pl.BlockSpec((None, D_MODEL, IN_COLS), lambda i: (layer, 0, 0)),
            pl.BlockSpec((TM, HEAD_DIM), lambda i: (pos_tile(i), 0)),
            pl.BlockSpec((TM, HEAD_DIM), lambda i: (pos_tile(i), 0)),
            pl.BlockSpec((TM, TM), lambda i: (0, 0)),
        ],
        out_specs=(tok(Q_COLS), tok(2 * KV_COLS), tok(D_RNN), tok(D_RNN), tok(D_MODEL), tok(D_MODEL)),
        out_shape=out_shape,
        compiler_params=_cparams(1),
        name="in_proj",
    )(stream[0], stream[1], g, mod3, mod3, w_in, cos_t, sin_t, perm)


def _stack_heads(qt):
    return jnp.concatenate([qt[:, g * HEAD_DIM:(g + 1) * HEAD_DIM] for g in range(N_GROUPS)], axis=0)


def _sink_col(sink_ref, kvh, rows):
    return jnp.concatenate(
        [jnp.full((rows, 1), sink_ref[kvh * N_GROUPS + g] * LOG2E, F32) for g in range(N_GROUPS)], axis=0)


def _ones_column(n_keys):
    lane = lax.broadcasted_iota(jnp.int32, (n_keys, HEAD_DIM), 1)
    return jnp.where(lane == 0, 1.0, 0.0).astype(BF16)


def _softmax_pv(s, sink, vall):
    m = jnp.maximum(jnp.max(s, axis=1, keepdims=True), sink)
    p = jnp.exp2(s - m).astype(BF16)
    v_aug = jnp.concatenate([vall, _ones_column(vall.shape[0])], axis=1)
    oa = jnp.dot(p, v_aug, preferred_element_type=F32)
    denom = oa[:, HEAD_DIM:HEAD_DIM + 1] + jnp.exp2(sink - m)
    return oa[:, :HEAD_DIM] / denom


def _softplus(z):
    return jnp.maximum(z, 0.0) + jnp.log1p(jnp.exp(-jnp.abs(z)))


def _scan8(a, b, row, reverse):
    for s in (1, 2, 4):
        if reverse:
            keep = row < SUBLANES - s
            shift = SUBLANES - s
        else:
            keep = row >= s
            shift = s
        a_sh = jnp.where(keep, pltpu.roll(a, shift, 0), 1.0)
        b_sh = jnp.where(keep, pltpu.roll(b, shift, 0), 0.0)
        b = a * b_sh + b
        a = a * a_sh
    return a, b


def _attend_tile(sink_ref, q_ref, kv_ref, kvc_ref, o_ref, tile, n_blocks, kv_heads, is_ctx):
    rows = ATTN_BLOCK * N_GROUPS
    qi = lax.broadcasted_iota(jnp.int32, (rows, ATTN_BLOCK), 0) & (ATTN_BLOCK - 1)
    kj = lax.broadcasted_iota(jnp.int32, (rows, ATTN_BLOCK), 1)
    tri_prev = kj >= qi
    tri_next = kj <= qi
    q_per_tile = TT // ATTN_BLOCK
    gw = N_GROUPS * HEAD_DIM
    for kvh in kv_heads:
        kcol = slice(kvh * HEAD_DIM, (kvh + 1) * HEAD_DIM)
        vcol = slice(KV_COLS + kvh * HEAD_DIM, KV_COLS + (kvh + 1) * HEAD_DIM)
        sink = _sink_col(sink_ref, kvh, ATTN_BLOCK)
        for qb in range(q_per_tile):
            n = tile * q_per_tile + qb
            starts = [pl.multiple_of(jnp.maximum(n - 1, 0) * ATTN_BLOCK, ATTN_BLOCK),
                      pl.multiple_of(n * ATTN_BLOCK, ATTN_BLOCK),
                      pl.multiple_of(jnp.minimum(n + 1, n_blocks - 1) * ATTN_BLOCK, ATTN_BLOCK)]
            qrows = slice(qb * ATTN_BLOCK, (qb + 1) * ATTN_BLOCK)
            qs = _stack_heads(q_ref[qrows, kvh * gw:(kvh + 1) * gw])
            kall = jnp.concatenate([kv_ref[pl.ds(r, ATTN_BLOCK), kcol] for r in starts] + [kvc_ref[:, kcol]],
                                   axis=0)
            vall = jnp.concatenate([kv_ref[pl.ds(r, ATTN_BLOCK), vcol] for r in starts] + [kvc_ref[:, vcol]],
                                   axis=0)
            s = lax.dot_general(qs, kall, (((1,), (1,)), ((), ())), preferred_element_type=F32)
            pen_band = jnp.where(is_ctx, NEG_INF, 0.0)
            pen_prev = jnp.where(n > 0, pen_band, NEG_INF)
            pen_next = jnp.where(n < n_blocks - 1, pen_band, NEG_INF)
            sp = jnp.where(tri_prev, s[:, :ATTN_BLOCK] + pen_prev, NEG_INF)
            sc = s[:, ATTN_BLOCK:2 * ATTN_BLOCK] + pen_band
            sn = jnp.where(tri_next, s[:, 2 * ATTN_BLOCK:3 * ATTN_BLOCK] + pen_next, NEG_INF)
            s = jnp.concatenate([sp, sc, sn, s[:, 3 * ATTN_BLOCK:]], axis=1)
            o = _softmax_pv(s, sink, vall)
            for g in range(N_GROUPS):
                col0 = kvh * gw + g * HEAD_DIM
                o_ref[qrows, col0:col0 + HEAD_DIM] = o[g * ATTN_BLOCK:(g + 1) * ATTN_BLOCK].astype(BF16)


def _mixer_kernel(xf_ref, xfp_ref, xfn_ref, xb_ref, xbp_ref, xbn_ref, cw_ref, cb_ref, wcat_ref,
                  gb_ref, lam_ref, unperm_ref, sink_ref, q_ref, kv_ref, kvc_ref,
                  hf_ref, hb_ref, attn_ref, a_s, b_s, h_bf, carry, *, n_blocks):
    j = pl.program_id(1)
    last_j = pl.num_programs(1) - 1

    @pl.when(j == 0)
    def _():
        carry[...] = jnp.zeros_like(carry)

    seg_start = (j <= 1, (j == 0) | (j == last_j))
    seg_end = ((j == 0) | (j == last_j), j <= 1)
    mains = (xf_ref, xb_ref)
    prevs = (xfp_ref, xbp_ref)
    nexts = (xfn_ref, xbn_ref)
    sub = lax.broadcasted_iota(jnp.int32, (SUBLANES, D_RNN), 0)
    S8 = SUBLANES

    for d in range(2):
        x0 = mains[d][...]
        halo = prevs[d][...]
        t_m1 = jnp.where(seg_start[d], 0.0, halo[2 * S8 - 1:2 * S8, :])
        t_m2 = jnp.where(seg_start[d], 0.0, halo[S8 - 1:S8, :])
        t_p1 = jnp.where(seg_end[d], 0.0, nexts[d][0:1, :])
        g_m1 = jnp.where(sub == 0, t_m1, pltpu.roll(x0[TT - S8:, :], 1, 0))
        g_m2 = jnp.where(sub == 0, t_m2, pltpu.roll(x0[TT - 2 * S8:TT - S8, :], 1, 0))
        g_p1 = jnp.where(sub == S8 - 1, t_p1, pltpu.roll(x0[:S8, :], S8 - 1, 0))
        xc = cb_ref[...] + jnp.concatenate([g_m2, g_m1, x0[:TT - 2 * S8, :]], axis=0) * cw_ref[0:1, :]
        xc = xc + jnp.concatenate([g_m1, x0[:TT - S8, :]], axis=0) * cw_ref[1:2, :]
        xc = xc + x0 * cw_ref[2:3, :]
        xc = xc + jnp.concatenate([x0[S8:, :], g_p1], axis=0) * cw_ref[3:4, :]
        c_d = (-LRU_C * LOG2E) * _softplus(-lam_ref[d:d + 1, :])
        for blk in range(N_RNN_BLOCKS):
            sl = slice(blk * RNN_BLOCK, (blk + 1) * RNN_BLOCK)
            xcb = xc[:, sl]
            z = jnp.dot(xcb.astype(BF16), wcat_ref[d, blk], preferred_element_type=F32)
            r = _sigmoid(z[:, :RNN_BLOCK] + gb_ref[d, 0:1, sl])
            ig = _sigmoid(z[:, RNN_BLOCK:] + gb_ref[d, 1:2, sl])
            a = jnp.exp2(r * c_d[:, sl])
            v = 1.0 - a * a
            root = jnp.where(v > 0.0, v * lax.rsqrt(v), 0.0)
            a_s[d, :, sl] = a
            b_s[d, :, sl] = root * (ig * xcb)
        _attend_tile(sink_ref, q_ref, kv_ref, kvc_ref, attn_ref, jnp.maximum(j - 1, 0), n_blocks, (d,), j == 0)

    def local(g, c):
        hf, pf, hb, pb = c
        rf = pl.multiple_of(g * S8, S8)
        rb = pl.multiple_of((SUBSEQ - 1 - g) * S8, S8)
        af = a_s[0, pl.ds(rf, S8), :]
        ab = a_s[1, pl.ds(rb, S8), :]
        hf = af * hf + b_s[0, pl.ds(rf, S8), :]
        hb = ab * hb + b_s[1, pl.ds(rb, S8), :]
        pf = af * pf
        pb = ab * pb
        b_s[0, pl.ds(rf, S8), :] = hf
        b_s[1, pl.ds(rb, S8), :] = hb
        a_s[0, pl.ds(rf, S8), :] = pf
        a_s[1, pl.ds(rb, S8), :] = pb
        return hf, pf, hb, pb

    zero = jnp.zeros((S8, D_RNN), F32)
    one = jnp.ones((S8, D_RNN), F32)
    hf, pf, hb, pb = lax.fori_loop(0, SUBSEQ, local, (zero, one, zero, one), unroll=4)

    af, bf = _scan8(pf, hf, sub, False)
    endf = af * carry[0] + bf
    h_in = [jnp.where(sub == 0, carry[0], pltpu.roll(endf, 1, 0))]
    carry[0] = jnp.broadcast_to(endf[S8 - 1:S8, :], (S8, D_RNN))
    ab, bb = _scan8(pb, hb, sub, True)
    endb = ab * carry[1] + bb
    h_in.append(jnp.where(sub == S8 - 1, carry[1], pltpu.roll(endb, S8 - 1, 0)))
    carry[1] = jnp.broadcast_to(endb[0:1, :], (S8, D_RNN))

    outs = (hf_ref, hb_ref)
    for d in range(2):
        h_in2 = jnp.concatenate([h_in[d], h_in[d]], axis=0)

        def fix(k, c, d=d, h_in2=h_in2):
            rows = pl.ds(pl.multiple_of(k * 2 * S8, 2 * S8), 2 * S8)
            h_bf[d, rows, :] = (b_s[d, rows, :] + a_s[d, rows, :] * h_in2).astype(BF16)
            return c

        lax.fori_loop(0, SUBSEQ // 2, fix, 0, unroll=2)
        outs[d][...] = jnp.dot(unperm_ref[...], h_bf[d], preferred_element_type=F32).astype(BF16)


def _time_permutation(n_rows):
    p = jnp.arange(n_rows)
    src = (p // TT) * TT + (p % SUBLANES) * SUBSEQ + (p % TT) // SUBLANES
    return (src[:, None] == jnp.arange(n_rows)[None, :]).astype(BF16)


def _token_mixer(x, conv_w, conv_b, wcat, gate_b, lam, sink, q, kv, *, n_batch, seq, ctx_len):
    T = x.shape[0]
    assert ctx_len == TT and seq % TT == 0
    tps = seq // TT
    n_lat_t = n_batch * tps
    halo_rows = 2 * SUBLANES
    per_halo = TT // halo_rows
    per8 = TT // SUBLANES
    last8 = T // SUBLANES - 1
    unperm = _time_permutation(TT).T

    def ftile(b, j):
        return jnp.where(j == 0, n_lat_t + b, b * tps + j - 1)

    def btile(b, j):
        return jnp.where(j == 0, n_lat_t + b, b * tps + tps - j)

    def main(tile):
        return pl.BlockSpec((TT, D_RNN), lambda b, j: (tile(b, j), 0))

    def prev(tile):
        return pl.BlockSpec((halo_rows, D_RNN), lambda b, j: (jnp.maximum(tile(b, j) * per_halo - 1, 0), 0))

    def nxt(tile):
        return pl.BlockSpec((SUBLANES, D_RNN),
                            lambda b, j: (jnp.minimum((tile(b, j) + 1) * per8, last8), 0))

    def const(shape):
        return pl.BlockSpec(shape, lambda b, j: (0,) * len(shape))

    q_spec = pl.BlockSpec((TT, Q_COLS), lambda b, j: (ftile(b, j), 0))
    return pl.pallas_call(
        functools.partial(_mixer_kernel, n_blocks=seq // ATTN_BLOCK),
        grid=(n_batch, tps + 1),
        in_specs=[main(ftile), prev(ftile), nxt(ftile), main(btile), prev(btile), nxt(btile),
                  const(conv_w.shape), const(conv_b.shape), const(wcat.shape), const(gate_b.shape),
                  const(lam.shape), const(unperm.shape),
                  pl.BlockSpec(memory_space=pltpu.SMEM),
                  q_spec,
                  pl.BlockSpec((seq, 2 * KV_COLS), lambda b, j: (b, 0)),
                  pl.BlockSpec((ctx_len, 2 * KV_COLS), lambda b, j: (n_lat_t + b, 0))],
        out_specs=(main(ftile), main(btile), q_spec),
        out_shape=(jax.ShapeDtypeStruct((T, D_RNN), BF16), jax.ShapeDtypeStruct((T, D_RNN), BF16),
                   jax.ShapeDtypeStruct((T, Q_COLS), BF16)),
        scratch_shapes=[pltpu.VMEM((2, TT, D_RNN), F32),
                        pltpu.VMEM((2, TT, D_RNN), F32),
                        pltpu.VMEM((2, TT, D_RNN), BF16),
                        pltpu.VMEM((2, SUBLANES, D_RNN), F32)],
        compiler_params=_cparams(2),
        name="token_mixer",
    )(x, x, x, x, x, x, conv_w, conv_b, wcat, gate_b, lam, unperm, sink, q, kv, kv)


def _merge_kernel(h_lat_ref, h_ctx_ref, attn_ref, hf_ref, hb_ref, gy_ref, sga_ref, sgr_ref,
                  g1_ref, woa_ref, wol_ref, wout_ref, o_ref):
    rec = ((hf_ref[...].astype(F32) + hb_ref[...].astype(F32)) * gy_ref[...].astype(F32)).astype(BF16)
    ta = jnp.dot(attn_ref[...], woa_ref[...], preferred_element_type=F32)
    tl = jnp.dot(rec, wol_ref[...], preferred_element_type=F32)
    m = sga_ref[...].astype(F32) * ta + sgr_ref[...].astype(F32) * tl
    y = jnp.dot(m.astype(BF16), wout_ref[...], preferred_element_type=F32)
    o_ref[...] = _stream_tile(h_lat_ref, h_ctx_ref) + g1_ref[...] * y


def _merge(stream, attn, hf, hb, gy, sga, sgr, mod3, woa, wol, wout, layer, *, tiles_per_batch, n_batch):
    T = attn.shape[0]
    n_lat_tiles = T // TM - 1

    def grp(i):
        return jnp.minimum(i // tiles_per_batch, n_batch)

    tok = pl.BlockSpec((TM, D_MODEL), lambda i: (i, 0))
    wsp = pl.BlockSpec((None, D_MODEL, D_MODEL), lambda i: (layer, 0, 0))
    return pl.pallas_call(
        _merge_kernel,
        grid=(T // TM,),
        in_specs=_stream_specs(stream, n_lat_tiles) + [
                  tok, tok, tok, tok, tok, tok,
                  pl.BlockSpec((None, 1, D_MODEL), lambda i: (grp(i), 0, 2)),
                  wsp, wsp, wsp],
        out_specs=tok,
        out_shape=jax.ShapeDtypeStruct((T, D_MODEL), F32),
        compiler_params=_cparams(1),
        name="merge",
    )(stream[0], stream[1], attn, hf, hb, gy, sga, sgr, mod3, woa, wol, wout)


def _swiglu_partial(n, wg, wu, wd):
    gt = jnp.dot(n, wg, preferred_element_type=F32)
    ut = jnp.dot(n, wu, preferred_element_type=F32)
    act = (gt * _sigmoid(gt) * ut).astype(BF16)
    return jnp.dot(act, wd, preferred_element_type=F32)


FFN_F_CHUNKS = 2


def _ffn_kernel(h_ref, g_ref, sh_ref, sc_ref, g2_ref, wg_ref, wu_ref, wd_ref, o_ref):
    h = h_ref[...]
    n = _norm_mod(h, g_ref[...], sh_ref[...], sc_ref[...]).astype(BF16)
    tf = D_FF // FFN_F_CHUNKS
    acc = None
    for c in range(FFN_F_CHUNKS):
        cols = slice(c * tf, (c + 1) * tf)
        part = _swiglu_partial(n, wg_ref[:, cols], wu_ref[:, cols], wd_ref[cols, :])
        acc = part if acc is None else acc + part
    o_ref[...] = h + g2_ref[...] * acc


def _ffn(h, g, mod3, wg, wu, wd, layer, ff_layer, *, tiles_per_batch, n_batch):
    T = h.shape[0]
    assert (D_FF // FFN_F_CHUNKS) % LANES == 0

    def grp(i):
        return jnp.minimum(i // tiles_per_batch, n_batch)

    def modspec(k):
        return pl.BlockSpec((None, 1, D_MODEL), lambda i: (grp(i), 0, k))

    tok = pl.BlockSpec((TM, D_MODEL), lambda i: (i, 0))
    return pl.pallas_call(
        _ffn_kernel,
        grid=(T // TM,),
        in_specs=[tok, pl.BlockSpec((None, 1, D_MODEL), lambda i: (layer, 0, 0)),
                  modspec(3), modspec(4), modspec(5),
                  pl.BlockSpec((None, D_MODEL, D_FF), lambda i: (ff_layer, 0, 0)),
                  pl.BlockSpec((None, D_MODEL, D_FF), lambda i: (ff_layer, 0, 0)),
                  pl.BlockSpec((None, D_FF, D_MODEL), lambda i: (ff_layer, 0, 0))],
        out_specs=tok,
        out_shape=jax.ShapeDtypeStruct((T, D_MODEL), F32),
        compiler_params=_cparams(1),
        name="dense_ffn",
    )(h, g, mod3, mod3, mod3, wg, wu, wd)


ROUTE_E1, ROUTE_E2, ROUTE_W1, ROUTE_W2, ROUTE_R1, ROUTE_R2 = range(6)


def _dot_split(a, b):
    a_hi = a.astype(BF16)
    a_lo = (a - a_hi.astype(F32)).astype(BF16)
    b_hi = b.astype(BF16)
    b_lo = (b - b_hi.astype(F32)).astype(BF16)

    def mm(x, y):
        return jnp.dot(x, y, preferred_element_type=F32)

    return mm(a_hi, b_hi) + (mm(a_hi, b_lo) + mm(a_lo, b_hi))


def _router_kernel(h_ref, g_ref, sh_ref, sc_ref, wr_ref, route_ref, cnt_ref, zero_ref, run):
    @pl.when(pl.program_id(0) == 0)
    def _():
        run[...] = jnp.zeros_like(run)

    n = _norm_mod(h_ref[...], g_ref[...], sh_ref[...], sc_ref[...])
    logits = _dot_split(n, wr_ref[...])
    lane = lax.broadcasted_iota(jnp.int32, logits.shape, 1)
    logits = jnp.where(lane < N_EXPERTS, logits, -jnp.inf)
    m1 = jnp.max(logits, axis=1, keepdims=True)
    i1 = jnp.min(jnp.where(logits == m1, lane, LANES), axis=1, keepdims=True)
    rest = jnp.where(lane == i1, -jnp.inf, logits)
    m2 = jnp.max(rest, axis=1, keepdims=True)
    i2 = jnp.min(jnp.where(rest == m2, lane, LANES), axis=1, keepdims=True)
    e2 = jnp.exp(m2 - m1)
    w1 = 1.0 / (1.0 + e2)
    w2 = e2 / (1.0 + e2)

    hit1 = lane == i1
    hit2 = lane == i2
    onehot = jnp.where(hit1 | hit2, 1.0, 0.0)
    r_i = lax.broadcasted_iota(jnp.int32, (TM, TM), 0)
    c_i = lax.broadcasted_iota(jnp.int32, (TM, TM), 1)
    lower = jnp.where(c_i < r_i, 1.0, 0.0).astype(BF16)
    prefix = jnp.dot(lower, onehot.astype(BF16), preferred_element_type=F32) + run[0:1, :]
    rank1 = jnp.sum(jnp.where(hit1, prefix, 0.0), axis=1, keepdims=True)
    rank2 = jnp.sum(jnp.where(hit2, prefix, 0.0), axis=1, keepdims=True)
    run[...] = run[...] + jnp.sum(onehot, axis=0, keepdims=True)
    cnt_ref[...] = run[...]

    rec = jnp.zeros(logits.shape, F32)
    for k, v in ((ROUTE_E1, i1.astype(F32)), (ROUTE_E2, i2.astype(F32)), (ROUTE_W1, w1), (ROUTE_W2, w2),
                 (ROUTE_R1, rank1), (ROUTE_R2, rank2)):
        rec = jnp.where(lane == k, v, rec)
    route_ref[...] = rec
    zero_ref[...] = jnp.zeros_like(zero_ref)


def _router(h, g2, mod3, w_r, layer, sorted_rows, *, tiles_per_batch, n_batch):
    T = h.shape[0]
    n_steps = T // TM
    zero_rows = -(-sorted_rows // (n_steps * SUBLANES)) * SUBLANES

    def grp(i):
        return jnp.minimum(i // tiles_per_batch, n_batch)

    def modspec(k):
        return pl.BlockSpec((None, 1, D_MODEL), lambda i: (grp(i), 0, k))

    return pl.pallas_call(
        _router_kernel,
        grid=(T // TM,),
        in_specs=[pl.BlockSpec((TM, D_MODEL), lambda i: (i, 0)),
                  pl.BlockSpec((None, 1, D_MODEL), lambda i: (layer, 0, 0)),
                  modspec(3), modspec(4),
                  pl.BlockSpec((D_MODEL, LANES), lambda i: (0, 0))],
        out_specs=(pl.BlockSpec((TM, LANES), lambda i: (i, 0)),
                   pl.BlockSpec((SUBLANES, LANES), lambda i: (0, 0)),
                   pl.BlockSpec((zero_rows, D_MODEL), lambda i: (i, 0))),
        out_shape=(jax.ShapeDtypeStruct((T, LANES), F32), jax.ShapeDtypeStruct((SUBLANES, LANES), F32),
                   jax.ShapeDtypeStruct((n_steps * zero_rows, D_MODEL), F32)),
        scratch_shapes=[pltpu.VMEM((SUBLANES, LANES), F32)],
        compiler_params=_cparams(1),
        name="moe_router",
    )(h, g2, mod3, mod3, w_r)


def _row_copy(src_ref, src_row, dst_ref, dst_row, sem):
    return pltpu.make_async_copy(src_ref.at[pl.ds(src_row, 1)], dst_ref.at[pl.ds(dst_row, 1)], sem)


def _dispatch_kernel(dest_ref, h_ref, g_ref, sh_ref, sc_ref, xs_in_ref, xs_ref, n_scr, sem):
    del xs_in_ref
    i = pl.program_id(0)
    base = i * (2 * TM)
    cur = i % 2
    n_scr[cur] = _norm_mod(h_ref[...], g_ref[...], sh_ref[...], sc_ref[...])

    def start(r, carry):
        for s in range(2):
            _row_copy(n_scr.at[cur], r, xs_ref, dest_ref[base + 2 * r + s], sem.at[cur]).start(priority=s)
        return carry

    lax.fori_loop(0, TM, start, 0, unroll=ROW_DMA_UNROLL)

    def wait_all(buf):
        for s in range(2):
            pltpu.make_async_copy(n_scr.at[buf], xs_ref.at[pl.ds(0, TM)], sem.at[buf]).wait()

    @pl.when(i > 0)
    def _():
        wait_all(1 - cur)

    @pl.when(i == pl.num_programs(0) - 1)
    def _():
        wait_all(cur)


def _dispatch(dest, h, g2, mod3, xs_zero, layer, *, tiles_per_batch, n_batch):
    T = h.shape[0]

    def grp(i):
        return jnp.minimum(i // tiles_per_batch, n_batch)

    def modspec(k):
        return pl.BlockSpec((None, 1, D_MODEL), lambda i, d: (grp(i), 0, k))

    grid_spec = pltpu.PrefetchScalarGridSpec(
        num_scalar_prefetch=1,
        grid=(T // TM,),
        in_specs=[pl.BlockSpec((TM, D_MODEL), lambda i, d: (i, 0)),
                  pl.BlockSpec((None, 1, D_MODEL), lambda i, d: (layer, 0, 0)),
                  modspec(3), modspec(4),
                  pl.BlockSpec(memory_space=pl.ANY)],
        out_specs=pl.BlockSpec(memory_space=pl.ANY),
        scratch_shapes=[pltpu.VMEM((2, TM, D_MODEL), F32), pltpu.SemaphoreType.DMA((2,))],
    )
    return pl.pallas_call(
        _dispatch_kernel,
        grid_spec=grid_spec,
        out_shape=jax.ShapeDtypeStruct(xs_zero.shape, xs_zero.dtype),
        input_output_aliases={5: 0},
        compiler_params=_cparams(1),
        name="moe_dispatch",
    )(dest, h, g2, mod3, mod3, xs_zero)


def _expert_kernel(te_ref, nu_ref, rows_ref, x_ref, wg_ref, wu_ref, wd_ref, acc, xb):
    del te_ref, nu_ref
    k = pl.program_id(0)
    f = pl.program_id(1)

    @pl.when(f == 0)
    def _():
        xb[...] = x_ref[...].astype(BF16)
        acc[...] = jnp.zeros_like(acc)

    wg = wg_ref[...].astype(BF16)
    wu = wu_ref[...].astype(BF16)
    wd = wd_ref[...].astype(BF16)
    acc[:TG_SUB, :] += _swiglu_partial(xb[:TG_SUB, :], wg, wu, wd)
    for sub in range(1, TG // TG_SUB):
        rows = slice(sub * TG_SUB, (sub + 1) * TG_SUB)

        @pl.when(rows_ref[k] > sub * TG_SUB)
        def _():
            acc[rows, :] += _swiglu_partial(xb[rows, :], wg, wu, wd)


def _experts(tile_expert, n_used, tile_rows, xs, wg, wu, wd, layer, n_tiles):
    P = n_tiles * TG
    assert xs.shape[0] >= P
    n_f = MOE_F_CHUNKS
    tf = D_FF_EXPERT // n_f
    assert tf % LANES == 0

    def fsel(k, f, nu):
        return jnp.where(k < nu[0], f, n_f - 1)

    grid_spec = pltpu.PrefetchScalarGridSpec(
        num_scalar_prefetch=3,
        grid=(P // TG, n_f),
        in_specs=[pl.BlockSpec((TG, D_MODEL), lambda k, f, te, nu, tr: (k, 0)),
                  pl.BlockSpec((None, None, D_MODEL, tf), lambda k, f, te, nu, tr: (layer, te[k], 0, fsel(k, f, nu))),
                  pl.BlockSpec((None, None, D_MODEL, tf), lambda k, f, te, nu, tr: (layer, te[k], 0, fsel(k, f, nu))),
                  pl.BlockSpec((None, None, tf, D_MODEL), lambda k, f, te, nu, tr: (layer, te[k], fsel(k, f, nu), 0))],
        out_specs=pl.BlockSpec((TG, D_MODEL), lambda k, f, te, nu, tr: (k, 0)),
        scratch_shapes=[pltpu.VMEM((TG, D_MODEL), BF16)],
    )
    return pl.pallas_call(
        _expert_kernel,
        grid_spec=grid_spec,
        out_shape=jax.ShapeDtypeStruct((P, D_MODEL), F32),
        compiler_params=_cparams(2),
        name="moe_experts",
    )(tile_expert, n_used, tile_rows, xs, wg, wu, wd)


def _combine_kernel(dest_ref, h_ref, g2_ref, route_ref, fg_ref, y_ref, o_ref, ybuf, sem, *, final):
    i = pl.program_id(0)
    n_steps = pl.num_programs(0)

    def gather(tile, buf):
        base = tile * (2 * TM)

        def start(r, carry):
            for s in range(2):
                _row_copy(y_ref, dest_ref[base + 2 * r + s], ybuf.at[buf], s * TM + r,
                          sem.at[buf]).start(priority=s)
            return carry

        lax.fori_loop(0, TM, start, 0, unroll=ROW_DMA_UNROLL)

    @pl.when(i == 0)
    def _():
        gather(0, 0)

    @pl.when(i + 1 < n_steps)
    def _():
        gather(i + 1, (i + 1) % 2)

    cur = i % 2
    for s in range(2):
        pltpu.make_async_copy(y_ref.at[pl.ds(0, TM)], ybuf.at[cur, pl.ds(s * TM, TM)], sem.at[cur]).wait()
    route = route_ref[...]
    f = (route[:, ROUTE_W1:ROUTE_W1 + 1] * ybuf[cur, 0:TM, :]
         + route[:, ROUTE_W2:ROUTE_W2 + 1] * ybuf[cur, TM:, :])
    out = h_ref[...] + g2_ref[...] * f
    if final:
        ms = jnp.mean(out * out, axis=-1, keepdims=True)
        out = out * lax.rsqrt(ms + EPS) * fg_ref[...]
    o_ref[...] = out


def _combine(dest, h, mod3, route, final_g, y, *, final, tiles_per_batch, n_batch):
    n_tok_tiles = h.shape[0] // TM - (1 if final else 0)

    def grp(i):
        return jnp.minimum(i // tiles_per_batch, n_batch)

    grid_spec = pltpu.PrefetchScalarGridSpec(
        num_scalar_prefetch=1,
        grid=(n_tok_tiles,),
        in_specs=[pl.BlockSpec((TM, D_MODEL), lambda i, d: (i, 0)),
                  pl.BlockSpec((None, 1, D_MODEL), lambda i, d: (grp(i), 0, 5)),
                  pl.BlockSpec((TM, LANES), lambda i, d: (i, 0)),
                  pl.BlockSpec((1, D_MODEL), lambda i, d: (0, 0)),
                  pl.BlockSpec(memory_space=pl.ANY)],
        out_specs=pl.BlockSpec((TM, D_MODEL), lambda i, d: (i, 0)),
        scratch_shapes=[pltpu.VMEM((2, 2 * TM, D_MODEL), F32), pltpu.SemaphoreType.DMA((2,))],
    )
    return pl.pallas_call(
        functools.partial(_combine_kernel, final=final),
        grid_spec=grid_spec,
        out_shape=jax.ShapeDtypeStruct((n_tok_tiles * TM, D_MODEL), F32),
        compiler_params=_cparams(1),
        name="moe_combine",
    )(dest, h, mod3, route, final_g, y)


def _moe_layer(h, g2, mod3, w_r, wg, wu, wd, final_g, layer, moe_layer, *, final, **geo):
    T = h.shape[0]
    n_tiles = (2 * T) // TG + N_EXPERTS
    route, cnt, xs_zero = _router(h, g2, mod3, w_r, layer, n_tiles * TG, **geo)

    counts = cnt[0, :N_EXPERTS].astype(jnp.int32)
    padded = ((counts + TG - 1) // TG) * TG
    ends = jnp.cumsum(padded)
    offs = ends - padded
    e12 = route[:, ROUTE_E1:ROUTE_E2 + 1].astype(jnp.int32)
    r12 = route[:, ROUTE_R1:ROUTE_R2 + 1].astype(jnp.int32)
    onehot = e12[:, :, None] == jnp.arange(N_EXPERTS)[None, None, :]
    dest = (jnp.sum(jnp.where(onehot, offs[None, None, :], 0), axis=-1) + r12).reshape(2 * T)
    n_used = (ends[-1] // TG).reshape(1)
    tiles = jnp.arange(n_tiles)
    te_raw = jnp.sum(tiles[:, None] >= (ends // TG)[None, :], axis=1)
    tile_expert = jnp.minimum(te_raw, N_EXPERTS - 1).astype(jnp.int32)
    sel = tile_expert[:, None] == jnp.arange(N_EXPERTS)[None, :]
    cnt_k = jnp.sum(jnp.where(sel, counts[None, :], 0), axis=1)
    off_k = jnp.sum(jnp.where(sel, offs[None, :], 0), axis=1)
    tile_rows = jnp.where(te_raw < N_EXPERTS, jnp.clip(cnt_k - (tiles * TG - off_k), 0, TG), 0)

    xs = _dispatch(dest, h, g2, mod3, xs_zero, layer, **geo)
    y = _experts(tile_expert, n_used, tile_rows.astype(jnp.int32), xs, wg, wu, wd, moe_layer, n_tiles)
    return _combine(dest, h, mod3, route, final_g, y, final=final, **geo)


def _rope_tables(seq):
    assert seq % GRID_W == 0
    n_rows = seq // GRID_W
    inv = ROPE_THETA ** (-jnp.arange(ROPE_FREQS, dtype=F32) / ROPE_FREQS)
    ang_r = jnp.arange(n_rows, dtype=F32)[:, None] * inv
    ang_c = jnp.arange(GRID_W, dtype=F32)[:, None] * inv
    cos_r, sin_r = (jnp.repeat(f(ang_r), GRID_W, axis=0) for f in (jnp.cos, jnp.sin))
    cos_c, sin_c = (jnp.tile(f(ang_c), (n_rows, 1)) for f in (jnp.cos, jnp.sin))
    cos = jnp.concatenate([cos_r, cos_r, cos_c, cos_c], axis=1)
    sin = jnp.concatenate([-sin_r, sin_r, -sin_c, sin_c], axis=1)
    cos = jnp.concatenate([cos, jnp.ones((TM, HEAD_DIM), F32)], axis=0)
    sin = jnp.concatenate([sin, jnp.zeros((TM, HEAD_DIM), F32)], axis=0)
    return cos, sin


def kernel(x, c, ctx, c_ctx, w_mod, b_mod, norm1_g, norm2_g, w_in, attn_sink, conv_w, conv_b, gate_a_w, gate_a_b, gate_x_w, gate_x_b, lru_lambda, w_o_attn, w_o_lru, w_out, ff_w_gate, ff_w_up, ff_w_down, router_w, exp_w_gate, exp_w_up, exp_w_down, final_g):
    n_batch, seq, _ = x.shape
    ctx_len = ctx.shape[1]
    assert n_batch * ctx_len == TM and seq % TM == 0 and n_batch + 1 <= MOD_ROWS
    n_lat = n_batch * seq
    tiles_per_batch = seq // TM
    geo = dict(tiles_per_batch=tiles_per_batch, n_batch=n_batch)
    shp = dict(n_batch=n_batch, seq=seq, ctx_len=ctx_len)

    cpad = jnp.zeros((MOD_ROWS, D_MODEL), F32).at[:n_batch].set(c).at[n_batch].set(c_ctx)
    mod = _modulation(cpad, w_mod, b_mod)
    cos_t, sin_t = _rope_tables(seq)
    perm = _time_permutation(TM)
    stream = (x.reshape(n_lat, D_MODEL), ctx.reshape(n_batch * ctx_len, D_MODEL), 0)

    g1 = norm1_g.reshape(DEPTH, 1, D_MODEL)
    g2 = norm2_g.reshape(DEPTH, 1, D_MODEL)
    w_in_b = w_in.astype(BF16)
    woa_b, wol_b, wout_b = w_o_attn.astype(BF16), w_o_lru.astype(BF16), w_out.astype(BF16)
    ffg_b, ffu_b, ffd_b = ff_w_gate.astype(BF16), ff_w_up.astype(BF16), ff_w_down.astype(BF16)

    for l in range(DEPTH):
        mod3 = mod[l].reshape(MOD_ROWS, 1, 6 * D_MODEL)
        if l > 0:
            stream = (h, h, n_lat // TM)
        q, kv, xr, gy, sga, sgr = _in_proj(stream, g1, mod3, w_in_b, l, cos_t, sin_t, perm,
                                           n_lat_tiles=n_lat // TM, **geo)
        wcat = jnp.concatenate([gate_a_w[l], gate_x_w[l]], axis=-1).astype(BF16)
        gate_b = jnp.stack([gate_a_b[l], gate_x_b[l]], axis=1)
        hf, hb, attn = _token_mixer(xr, conv_w[l], conv_b[l].reshape(1, D_RNN), wcat, gate_b, lru_lambda[l],
                                    attn_sink[l], q, kv, **shp)
        h = _merge(stream, attn, hf, hb, gy, sga, sgr, mod3, woa_b, wol_b, wout_b, l, **geo)
        i = l // 2
        if l % 2 == 0:
            h = _ffn(h, g2, mod3, ffg_b, ffu_b, ffd_b, l, i, **geo)
        else:
            w_r = jnp.zeros((D_MODEL, LANES), F32).at[:, :N_EXPERTS].set(router_w[i])
            h = _moe_layer(h, g2, mod3, w_r, exp_w_gate, exp_w_up, exp_w_down,
                           final_g.reshape(1, D_MODEL), l, i, final=(l == DEPTH - 1), **geo)

    assert DEPTH % 2 == 0 and h.shape[0] == n_lat
    return h.reshape(n_batch, seq, D_MODEL)
```

```python
import functools

import jax
import jax.numpy as jnp
from jax import lax
from jax.experimental import pallas as pl
from jax.experimental.pallas import tpu as pltpu

F32 = jnp.float32
BF16 = jnp.bfloat16

D_MODEL = 1024
DEPTH = 4
GRID_W = 64
N_HEADS = 8
N_KV_HEADS = 2
HEAD_DIM = 128
N_GROUPS = N_HEADS // N_KV_HEADS
ATTN_BLOCK = 128
ROPE_THETA = 10000.0
ROPE_FREQS = HEAD_DIM // 4
D_RNN = 1024
N_RNN_BLOCKS = 8
RNN_BLOCK = D_RNN // N_RNN_BLOCKS
LRU_C = 8.0
D_FF = 2816
N_EXPERTS = 8
D_FF_EXPERT = 3584
EPS = 1e-6
NEG_INF = -1e30
Q_COLS = N_HEADS * HEAD_DIM
KV_COLS = N_KV_HEADS * HEAD_DIM
IN_COLS = Q_COLS + 2 * KV_COLS + 2 * D_RNN + 2 * D_MODEL
LOG2E = 1.4426950408889634
ATTN_SCALE = HEAD_DIM ** -0.5 * LOG2E

LANES = 128
SUBLANES = 8
MXU_WIDTH = 256
TM = 512
TN_IN = 512
TT = 256
SUBSEQ = TT // SUBLANES
TG = 1024
TG_SUB = 512
MOE_F_CHUNKS = 7
ROW_DMA_UNROLL = 8
MOD_ROWS = 8
VMEM_LIMIT = 56 * 1024 * 1024


def _cparams(n_axes):
    return pltpu.CompilerParams(dimension_semantics=("arbitrary",) * n_axes,
                                vmem_limit_bytes=VMEM_LIMIT)


def _sigmoid(z):
    return 0.5 * jnp.tanh(0.5 * z) + 0.5


def _norm_mod(h, g, shift, scale):
    ms = jnp.mean(h * h, axis=-1, keepdims=True)
    y = h * lax.rsqrt(ms + EPS) * g
    return y * (1.0 + scale) + shift


def _mod_kernel(c_ref, w_ref, b_ref, o_ref):
    cv = c_ref[...]
    s = cv * _sigmoid(cv)
    o_ref[...] = _dot_split(s, w_ref[...]) + b_ref[...]


def _modulation(cpad, w_mod, b_mod):
    nchunk = 6
    return pl.pallas_call(
        _mod_kernel,
        grid=(DEPTH, nchunk),
        in_specs=[
            pl.BlockSpec((MOD_ROWS, D_MODEL), lambda l, n: (0, 0)),
            pl.BlockSpec((None, D_MODEL, D_MODEL), lambda l, n: (l, 0, n)),
            pl.BlockSpec((None, 1, D_MODEL), lambda l, n: (l, 0, n)),
        ],
        out_specs=pl.BlockSpec((None, MOD_ROWS, D_MODEL), lambda l, n: (l, 0, n)),
        out_shape=jax.ShapeDtypeStruct((DEPTH, MOD_ROWS, 6 * D_MODEL), F32),
        compiler_params=_cparams(2),
        name="modulation",
    )(cpad, w_mod, b_mod.reshape(DEPTH, 1, 6 * D_MODEL))


def _rope(xh, cos, sin_signed, first_half):
    sw = jnp.where(first_half, pltpu.roll(xh, HEAD_DIM - ROPE_FREQS, 1), pltpu.roll(xh, ROPE_FREQS, 1))
    return xh * cos + sw * sin_signed


def _stream_tile(h_lat_ref, h_ctx_ref):
    is_ctx_tile = pl.program_id(0) == pl.num_programs(0) - 1
    return jnp.where(is_ctx_tile, h_ctx_ref[...], h_lat_ref[...])


def _stream_specs(stream, n_lat_tiles):
    _, _, ctx_block = stream
    return [pl.BlockSpec((TM, D_MODEL), lambda i: (jnp.minimum(i, n_lat_tiles - 1), 0)),
            pl.BlockSpec((TM, D_MODEL), lambda i: (ctx_block, 0))]


def _in_proj_kernel(h_lat_ref, h_ctx_ref, g_ref, sh_ref, sc_ref, w_ref, cos_ref, sin_ref, perm_ref,
                    q_ref, kv_ref, x_ref, gy_ref, sga_ref, sgr_ref):
    n = _norm_mod(_stream_tile(h_lat_ref, h_ctx_ref), g_ref[...], sh_ref[...], sc_ref[...]).astype(BF16)
    cos = cos_ref[...]
    sin = sin_ref[...]
    lane = lax.broadcasted_iota(jnp.int32, (TM, LANES), 1)
    first_half = (lane & ROPE_FREQS) == 0

    def proj(chunk):
        return jnp.dot(n, w_ref[:, chunk * TN_IN:(chunk + 1) * TN_IN], preferred_element_type=F32)

    def rope_store(acc, n_heads, scale, ref, col0):
        for hh in range(n_heads):
            y = _rope(acc[:, hh * HEAD_DIM:(hh + 1) * HEAD_DIM], cos, sin, first_half)
            if scale != 1.0:
                y = y * scale
            ref[:, col0 + hh * HEAD_DIM:col0 + (hh + 1) * HEAD_DIM] = y.astype(BF16)

    for c in range(2):
        rope_store(proj(c), 4, ATTN_SCALE, q_ref, c * TN_IN)
    acc = proj(2)
    rope_store(acc, 2, 1.0, kv_ref, 0)
    kv_ref[:, KV_COLS:] = acc[:, KV_COLS:].astype(BF16)
    n_perm = jnp.dot(perm_ref[...], n, preferred_element_type=F32).astype(BF16)
    for c in range(2):
        cols = slice(c * TN_IN, (c + 1) * TN_IN)
        x_ref[:, cols] = jnp.dot(n_perm, w_ref[:, (3 + c) * TN_IN:(4 + c) * TN_IN],
                                 preferred_element_type=F32)
        gy_ref[:, cols] = jax.nn.gelu(proj(5 + c)).astype(BF16)
        sga_ref[:, cols] = _sigmoid(proj(7 + c)).astype(BF16)
        sgr_ref[:, cols] = _sigmoid(proj(9 + c)).astype(BF16)


def _in_proj(stream, g, mod3, w_in, layer, cos_t, sin_t, perm, *, n_lat_tiles, tiles_per_batch, n_batch):
    T = (n_lat_tiles + 1) * TM

    def grp(i):
        return jnp.minimum(i // tiles_per_batch, n_batch)

    def pos_tile(i):
        return jnp.where(i < n_lat_tiles, i % tiles_per_batch, tiles_per_batch)

    def tok(width):
        return pl.BlockSpec((TM, width), lambda i: (i, 0))

    out_shape = (
        jax.ShapeDtypeStruct((T, Q_COLS), BF16),
        jax.ShapeDtypeStruct((T, 2 * KV_COLS), BF16),
        jax.ShapeDtypeStruct((T, D_RNN), F32),
        jax.ShapeDtypeStruct((T, D_RNN), BF16),
        jax.ShapeDtypeStruct((T, D_MODEL), BF16),
        jax.ShapeDtypeStruct((T, D_MODEL), BF16),
    )
    return pl.pallas_call(
        _in_proj_kernel,
        grid=(T // TM,),
        in_specs=_stream_specs(stream, n_lat_tiles) + [
            pl.BlockSpec((None, 1, D_MODEL), lambda i: (layer, 0, 0)),
            pl.BlockSpec((None, 1, D_MODEL), lambda i: (grp(i), 0, 0)),
            pl.BlockSpec((None, 1, D_MODEL), lambda i: (grp(i), 0, 1)),
            pl.BlockSpec((None, D_MODEL, IN_COLS), lambda i: (layer, 0, 0)),
            pl.BlockSpec((TM, HEAD_DIM), lambda i: (pos_tile(i), 0)),
            pl.BlockSpec((TM, HEAD_DIM), lambda i: (pos_tile(i), 0)),
            pl.BlockSpec((TM, TM), lambda i: (0, 0)),
        ],
        out_specs=(tok(Q_COLS), tok(2 * KV_COLS), tok(D_RNN), tok(D_RNN), tok(D_MODEL), tok(D_MODEL)),
        out_shape=out_shape,
        compiler_params=_cparams(1),
        name="in_proj",
    )(stream[0], stream[1], g, mod3, mod3, w_in, cos_t, sin_t, perm)


def _stack_heads(qt):
    return jnp.concatenate([qt[:, g * HEAD_DIM:(g + 1) * HEAD_DIM] for g in range(N_GROUPS)], axis=0)


def _sink_col(sink_ref, kvh, rows):
    return jnp.concatenate(
        [jnp.full((rows, 1), sink_ref[kvh * N_GROUPS + g] * LOG2E, F32) for g in range(N_GROUPS)], axis=0)


def _ones_column(n_keys):
    lane = lax.broadcasted_iota(jnp.int32, (n_keys, HEAD_DIM), 1)
    return jnp.where(lane == 0, 1.0, 0.0).astype(BF16)


def _softplus(z):
    return jnp.maximum(z, 0.0) + jnp.log1p(jnp.exp(-jnp.abs(z)))


def _scan8(a, b, row, reverse):
    for s in (1, 2, 4):
        if reverse:
            keep = row < SUBLANES - s
            shift = SUBLANES - s
        else:
            keep = row >= s
            shift = s
        a_sh = jnp.where(keep, pltpu.roll(a, shift, 0), 1.0)
        b_sh = jnp.where(keep, pltpu.roll(b, shift, 0), 0.0)
        b = a * b_sh + b
        a = a * a_sh
    return a, b


def _attention_stages(sink_ref, q_ref, kv_ref, kvc_ref, o_ref, tile, n_blocks, is_ctx):
    rows = ATTN_BLOCK * N_GROUPS
    qi = lax.broadcasted_iota(jnp.int32, (rows, ATTN_BLOCK), 0) & (ATTN_BLOCK - 1)
    kj = lax.broadcasted_iota(jnp.int32, (rows, ATTN_BLOCK), 1)
    tri_prev = kj >= qi
    tri_next = kj <= qi
    q_per_tile = TT // ATTN_BLOCK
    gw = N_GROUPS * HEAD_DIM
    st_cols = 3 * ATTN_BLOCK + kvc_ref.shape[0]
    stages = []
    for kvh in range(N_KV_HEADS):
        for qb in range(q_per_tile):
            kcol = slice(kvh * HEAD_DIM, (kvh + 1) * HEAD_DIM)
            vcol = slice(KV_COLS + kvh * HEAD_DIM, KV_COLS + (kvh + 1) * HEAD_DIM)
            qrows = slice(qb * ATTN_BLOCK, (qb + 1) * ATTN_BLOCK)
            n = tile * q_per_tile + qb
            starts = [pl.multiple_of(jnp.maximum(n - 1, 0) * ATTN_BLOCK, ATTN_BLOCK),
                      pl.multiple_of(n * ATTN_BLOCK, ATTN_BLOCK),
                      pl.multiple_of(jnp.minimum(n + 1, n_blocks - 1) * ATTN_BLOCK, ATTN_BLOCK)]
            st = {}

            def scores(st=st, kvh=kvh, kcol=kcol, qrows=qrows, n=n, starts=starts):
                qs = _stack_heads(q_ref[qrows, kvh * gw:(kvh + 1) * gw])
                kall = jnp.concatenate(
                    [kv_ref[pl.ds(r, ATTN_BLOCK), kcol] for r in starts] + [kvc_ref[:, kcol]], axis=0)
                s = lax.dot_general(qs, kall, (((1,), (1,)), ((), ())), preferred_element_type=F32)
                pen_band = jnp.where(is_ctx, NEG_INF, 0.0)
                pen_prev = jnp.where(n > 0, pen_band, NEG_INF)
                pen_next = jnp.where(n < n_blocks - 1, pen_band, NEG_INF)
                sp = jnp.where(tri_prev, s[:, :ATTN_BLOCK] + pen_prev, NEG_INF)
                sc = s[:, ATTN_BLOCK:2 * ATTN_BLOCK] + pen_band
                sn = jnp.where(tri_next, s[:, 2 * ATTN_BLOCK:3 * ATTN_BLOCK] + pen_next, NEG_INF)
                st["s"] = jnp.concatenate([sp, sc, sn, s[:, 3 * ATTN_BLOCK:]], axis=1)

            def row_max(st=st, kvh=kvh):
                sink = _sink_col(sink_ref, kvh, ATTN_BLOCK)
                st["m"] = jnp.maximum(jnp.max(st["s"], axis=1, keepdims=True), sink)
                st["sink_term"] = jnp.exp2(sink - st["m"])
                st["p"] = []

            def numerators(c, st=st):
                cols = slice(c * ATTN_BLOCK, (c + 1) * ATTN_BLOCK)
                st["p"].append(jnp.exp2(st["s"][:, cols] - st["m"]).astype(BF16))

            def weighted_values(st=st, kvh=kvh, vcol=vcol, qrows=qrows, starts=starts):
                vall = jnp.concatenate(
                    [kv_ref[pl.ds(r, ATTN_BLOCK), vcol] for r in starts] + [kvc_ref[:, vcol]], axis=0)
                v_aug = jnp.concatenate([vall, _ones_column(vall.shape[0])], axis=1)
                p = jnp.concatenate(st.pop("p"), axis=1)
                oa = jnp.dot(p, v_aug, preferred_element_type=F32)
                o = oa[:, :HEAD_DIM] / (oa[:, HEAD_DIM:HEAD_DIM + 1] + st.pop("sink_term"))
                for g in range(N_GROUPS):
                    col0 = kvh * gw + g * HEAD_DIM
                    o_ref[qrows, col0:col0 + HEAD_DIM] = o[g * ATTN_BLOCK:(g + 1) * ATTN_BLOCK].astype(BF16)
                st.clear()

            n_chunks = st_cols // ATTN_BLOCK
            stages += ([scores, row_max] + [functools.partial(numerators, c) for c in range(n_chunks)]
                       + [weighted_values])
    return stages


def _mixer_kernel(xf_ref, xfp_ref, xfn_ref, xb_ref, xbp_ref, xbn_ref, cw_ref, cb_ref, wcat_ref,
                  gb_ref, lam_ref, unperm_ref, sink_ref, q_ref, kv_ref, kvc_ref,
                  hf_ref, hb_ref, attn_ref, a_s, b_s, h_bf, carry, *, n_blocks):
    j = pl.program_id(1)
    last_j = pl.num_programs(1) - 1

    @pl.when(j == 0)
    def _():
        carry[...] = jnp.zeros_like(carry)

    seg_start = (j <= 1, (j == 0) | (j == last_j))
    seg_end = ((j == 0) | (j == last_j), j <= 1)
    mains = (xf_ref, xb_ref)
    prevs = (xfp_ref, xbp_ref)
    nexts = (xfn_ref, xbn_ref)
    sub = lax.broadcasted_iota(jnp.int32, (SUBLANES, D_RNN), 0)
    S8 = SUBLANES

    def conv(d):
        x0 = mains[d][...]
        halo = prevs[d][...]
        t_m1 = jnp.where(seg_start[d], 0.0, halo[2 * S8 - 1:2 * S8, :])
        t_m2 = jnp.where(seg_start[d], 0.0, halo[S8 - 1:S8, :])
        t_p1 = jnp.where(seg_end[d], 0.0, nexts[d][0:1, :])
        g_m1 = jnp.where(sub == 0, t_m1, pltpu.roll(x0[TT - S8:, :], 1, 0))
        g_m2 = jnp.where(sub == 0, t_m2, pltpu.roll(x0[TT - 2 * S8:TT - S8, :], 1, 0))
        g_p1 = jnp.where(sub == S8 - 1, t_p1, pltpu.roll(x0[:S8, :], S8 - 1, 0))
        xc = cb_ref[...] + jnp.concatenate([g_m2, g_m1, x0[:TT - 2 * S8, :]], axis=0) * cw_ref[0:1, :]
        xc = xc + jnp.concatenate([g_m1, x0[:TT - S8, :]], axis=0) * cw_ref[1:2, :]
        xc = xc + x0 * cw_ref[2:3, :]
        return xc + jnp.concatenate([x0[S8:, :], g_p1], axis=0) * cw_ref[3:4, :]

    def gates(d, blk, xc, c_d):
        sl = slice(blk * RNN_BLOCK, (blk + 1) * RNN_BLOCK)
        xcb = xc[:, sl]
        z = jnp.dot(xcb.astype(BF16), wcat_ref[d, blk], preferred_element_type=F32)
        r = _sigmoid(z[:, :RNN_BLOCK] + gb_ref[d, 0:1, sl])
        ig = _sigmoid(z[:, RNN_BLOCK:] + gb_ref[d, 1:2, sl])
        a = jnp.exp2(r * c_d[:, sl])
        v = 1.0 - a * a
        root = jnp.where(v > 0.0, v * lax.rsqrt(v), 0.0)
        a_s[d, :, sl] = a
        b_s[d, :, sl] = root * (ig * xcb)

    attn_stages = _attention_stages(sink_ref, q_ref, kv_ref, kvc_ref, attn_ref, jnp.maximum(j - 1, 0),
                                    n_blocks, j == 0)
    n_gate_blocks = 2 * N_RNN_BLOCKS
    emitted = 0
    for d in range(2):
        xc = conv(d)
        c_d = (-LRU_C * LOG2E) * _softplus(-lam_ref[d:d + 1, :])
        for blk in range(N_RNN_BLOCKS):
            gates(d, blk, xc, c_d)
            done = d * N_RNN_BLOCKS + blk + 1
            while emitted * n_gate_blocks < done * len(attn_stages):
                attn_stages[emitted]()
                emitted += 1
    assert emitted == len(attn_stages)

    def local(g, c):
        hf, pf, hb, pb = c
        rf = pl.multiple_of(g * S8, S8)
        rb = pl.multiple_of((SUBSEQ - 1 - g) * S8, S8)
        af = a_s[0, pl.ds(rf, S8), :]
        ab = a_s[1, pl.ds(rb, S8), :]
        hf = af * hf + b_s[0, pl.ds(rf, S8), :]
        hb = ab * hb + b_s[1, pl.ds(rb, S8), :]
        pf = af * pf
        pb = ab * pb
        b_s[0, pl.ds(rf, S8), :] = hf
        b_s[1, pl.ds(rb, S8), :] = hb
        a_s[0, pl.ds(rf, S8), :] = pf
        a_s[1, pl.ds(rb, S8), :] = pb
        return hf, pf, hb, pb

    zero = jnp.zeros((S8, D_RNN), F32)
    one = jnp.ones((S8, D_RNN), F32)
    hf, pf, hb, pb = lax.fori_loop(0, SUBSEQ, local, (zero, one, zero, one), unroll=4)

    af, bf = _scan8(pf, hf, sub, False)
    endf = af * carry[0] + bf
    h_in = [jnp.where(sub == 0, carry[0], pltpu.roll(endf, 1, 0))]
    carry[0] = jnp.broadcast_to(endf[S8 - 1:S8, :], (S8, D_RNN))
    ab, bb = _scan8(pb, hb, sub, True)
    endb = ab * carry[1] + bb
    h_in.append(jnp.where(sub == S8 - 1, carry[1], pltpu.roll(endb, S8 - 1, 0)))
    carry[1] = jnp.broadcast_to(endb[0:1, :], (S8, D_RNN))

    outs = (hf_ref, hb_ref)
    for d in range(2):
        h_in2 = jnp.concatenate([h_in[d], h_in[d]], axis=0)

        def fix(k, c, d=d, h_in2=h_in2):
            rows = pl.ds(pl.multiple_of(k * 2 * S8, 2 * S8), 2 * S8)
            h_bf[d, rows, :] = (b_s[d, rows, :] + a_s[d, rows, :] * h_in2).astype(BF16)
            return c

        lax.fori_loop(0, SUBSEQ // 2, fix, 0, unroll=2)
        outs[d][...] = jnp.dot(unperm_ref[...], h_bf[d], preferred_element_type=F32).astype(BF16)


def _time_permutation(n_rows):
    p = jnp.arange(n_rows)
    src = (p // TT) * TT + (p % SUBLANES) * SUBSEQ + (p % TT) // SUBLANES
    return (src[:, None] == jnp.arange(n_rows)[None, :]).astype(BF16)


def _token_mixer(x, conv_w, conv_b, wcat, gate_b, lam, sink, q, kv, *, n_batch, seq, ctx_len):
    T = x.shape[0]
    assert ctx_len == TT and seq % TT == 0
    tps = seq // TT
    n_lat_t = n_batch * tps
    halo_rows = 2 * SUBLANES
    per_halo = TT // halo_rows
    per8 = TT // SUBLANES
    last8 = T // SUBLANES - 1
    unperm = _time_permutation(TT).T

    def ftile(b, j):
        return jnp.where(j == 0, n_lat_t + b, b * tps + j - 1)

    def btile(b, j):
        return jnp.where(j == 0, n_lat_t + b, b * tps + tps - j)

    def main(tile):
        return pl.BlockSpec((TT, D_RNN), lambda b, j: (tile(b, j), 0))

    def prev(tile):
        return pl.BlockSpec((halo_rows, D_RNN), lambda b, j: (jnp.maximum(tile(b, j) * per_halo - 1, 0), 0))

    def nxt(tile):
        return pl.BlockSpec((SUBLANES, D_RNN),
                            lambda b, j: (jnp.minimum((tile(b, j) + 1) * per8, last8), 0))

    def const(shape):
        return pl.BlockSpec(shape, lambda b, j: (0,) * len(shape))

    q_spec = pl.BlockSpec((TT, Q_COLS), lambda b, j: (ftile(b, j), 0))
    return pl.pallas_call(
        functools.partial(_mixer_kernel, n_blocks=seq // ATTN_BLOCK),
        grid=(n_batch, tps + 1),
        in_specs=[main(ftile), prev(ftile), nxt(ftile), main(btile), prev(btile), nxt(btile),
                  const(conv_w.shape), const(conv_b.shape), const(wcat.shape), const(gate_b.shape),
                  const(lam.shape), const(unperm.shape),
                  pl.BlockSpec(memory_space=pltpu.SMEM),
                  q_spec,
                  pl.BlockSpec((seq, 2 * KV_COLS), lambda b, j: (b, 0)),
                  pl.BlockSpec((ctx_len, 2 * KV_COLS), lambda b, j: (n_lat_t + b, 0))],
        out_specs=(main(ftile), main(btile), q_spec),
        out_shape=(jax.ShapeDtypeStruct((T, D_RNN), BF16), jax.ShapeDtypeStruct((T, D_RNN), BF16),
                   jax.ShapeDtypeStruct((T, Q_COLS), BF16)),
        scratch_shapes=[pltpu.VMEM((2, TT, D_RNN), F32),
                        pltpu.VMEM((2, TT, D_RNN), F32),
                        pltpu.VMEM((2, TT, D_RNN), BF16),
                        pltpu.VMEM((2, SUBLANES, D_RNN), F32)],
        compiler_params=_cparams(2),
        name="token_mixer",
    )(x, x, x, x, x, x, conv_w, conv_b, wcat, gate_b, lam, unperm, sink, q, kv, kv)


def _merge_kernel(h_lat_ref, h_ctx_ref, attn_ref, hf_ref, hb_ref, gy_ref, sga_ref, sgr_ref,
                  g1_ref, woa_ref, wol_ref, wout_ref, o_ref):
    rec = ((hf_ref[...].astype(F32) + hb_ref[...].astype(F32)) * gy_ref[...].astype(F32)).astype(BF16)
    ta = jnp.dot(attn_ref[...], woa_ref[...], preferred_element_type=F32)
    tl = jnp.dot(rec, wol_ref[...], preferred_element_type=F32)
    m = sga_ref[...].astype(F32) * ta + sgr_ref[...].astype(F32) * tl
    y = jnp.dot(m.astype(BF16), wout_ref[...], preferred_element_type=F32)
    o_ref[...] = _stream_tile(h_lat_ref, h_ctx_ref) + g1_ref[...] * y


def _merge(stream, attn, hf, hb, gy, sga, sgr, mod3, woa, wol, wout, layer, *, tiles_per_batch, n_batch):
    T = attn.shape[0]
    n_lat_tiles = T // TM - 1

    def grp(i):
        return jnp.minimum(i // tiles_per_batch, n_batch)

    tok = pl.BlockSpec((TM, D_MODEL), lambda i: (i, 0))
    wsp = pl.BlockSpec((None, D_MODEL, D_MODEL), lambda i: (layer, 0, 0))
    return pl.pallas_call(
        _merge_kernel,
        grid=(T // TM,),
        in_specs=_stream_specs(stream, n_lat_tiles) + [
                  tok, tok, tok, tok, tok, tok,
                  pl.BlockSpec((None, 1, D_MODEL), lambda i: (grp(i), 0, 2)),
                  wsp, wsp, wsp],
        out_specs=tok,
        out_shape=jax.ShapeDtypeStruct((T, D_MODEL), F32),
        compiler_params=_cparams(1),
        name="merge",
    )(stream[0], stream[1], attn, hf, hb, gy, sga, sgr, mod3, woa, wol, wout)


def _swiglu_partial(n, wg, wu, wd):
    gt = jnp.dot(n, wg, preferred_element_type=F32)
    ut = jnp.dot(n, wu, preferred_element_type=F32)
    act = (gt * _sigmoid(gt) * ut).astype(BF16)
    return jnp.dot(act, wd, preferred_element_type=F32)


FFN_F_CHUNKS = 11


def _ffn_kernel(h_ref, g_ref, sh_ref, sc_ref, g2_ref, wg_ref, wu_ref, wd_ref, o_ref):
    h = h_ref[...]
    n = _norm_mod(h, g_ref[...], sh_ref[...], sc_ref[...]).astype(BF16)
    tf = D_FF // FFN_F_CHUNKS
    acc = None
    for c in range(FFN_F_CHUNKS):
        cols = slice(c * tf, (c + 1) * tf)
        part = _swiglu_partial(n, wg_ref[:, cols], wu_ref[:, cols], wd_ref[cols, :])
        acc = part if acc is None else acc + part
    o_ref[...] = h + g2_ref[...] * acc


def _ffn(h, g, mod3, wg, wu, wd, layer, ff_layer, *, tiles_per_batch, n_batch):
    T = h.shape[0]
    assert (D_FF // FFN_F_CHUNKS) % LANES == 0

    def grp(i):
        return jnp.minimum(i // tiles_per_batch, n_batch)

    def modspec(k):
        return pl.BlockSpec((None, 1, D_MODEL), lambda i: (grp(i), 0, k))

    tok = pl.BlockSpec((TM, D_MODEL), lambda i: (i, 0))
    return pl.pallas_call(
        _ffn_kernel,
        grid=(T // TM,),
        in_specs=[tok, pl.BlockSpec((None, 1, D_MODEL), lambda i: (layer, 0, 0)),
                  modspec(3), modspec(4), modspec(5),
                  pl.BlockSpec((None, D_MODEL, D_FF), lambda i: (ff_layer, 0, 0)),
                  pl.BlockSpec((None, D_MODEL, D_FF), lambda i: (ff_layer, 0, 0)),
                  pl.BlockSpec((None, D_FF, D_MODEL), lambda i: (ff_layer, 0, 0))],
        out_specs=tok,
        out_shape=jax.ShapeDtypeStruct((T, D_MODEL), F32),
        compiler_params=_cparams(1),
        name="dense_ffn",
    )(h, g, mod3, mod3, mod3, wg, wu, wd)


ROUTE_E1, ROUTE_E2, ROUTE_W1, ROUTE_W2, ROUTE_R1, ROUTE_R2 = range(6)


def _dot_split(a, b):
    a_hi = a.astype(BF16)
    a_lo = (a - a_hi.astype(F32)).astype(BF16)
    b_hi = b.astype(BF16)
    b_lo = (b - b_hi.astype(F32)).astype(BF16)

    def mm(x, y):
        return jnp.dot(x, y, preferred_element_type=F32)

    return mm(a_hi, b_hi) + (mm(a_hi, b_lo) + mm(a_lo, b_hi))


def _router_kernel(h_ref, g_ref, sh_ref, sc_ref, wr_ref, route_ref, cnt_ref, zero_ref, run):
    @pl.when(pl.program_id(0) == 0)
    def _():
        run[...] = jnp.zeros_like(run)

    n = _norm_mod(h_ref[...], g_ref[...], sh_ref[...], sc_ref[...])
    logits = _dot_split(n, wr_ref[...])
    lane = lax.broadcasted_iota(jnp.int32, logits.shape, 1)
    logits = jnp.where(lane < N_EXPERTS, logits, -jnp.inf)
    m1 = jnp.max(logits, axis=1, keepdims=True)
    i1 = jnp.min(jnp.where(logits == m1, lane, LANES), axis=1, keepdims=True)
    rest = jnp.where(lane == i1, -jnp.inf, logits)
    m2 = jnp.max(rest, axis=1, keepdims=True)
    i2 = jnp.min(jnp.where(rest == m2, lane, LANES), axis=1, keepdims=True)
    e2 = jnp.exp(m2 - m1)
    w1 = 1.0 / (1.0 + e2)
    w2 = e2 / (1.0 + e2)

    hit1 = lane == i1
    hit2 = lane == i2
    onehot = jnp.where(hit1 | hit2, 1.0, 0.0)
    r_i = lax.broadcasted_iota(jnp.int32, (TM, TM), 0)
    c_i = lax.broadcasted_iota(jnp.int32, (TM, TM), 1)
    lower = jnp.where(c_i < r_i, 1.0, 0.0).astype(BF16)
    prefix = jnp.dot(lower, onehot.astype(BF16), preferred_element_type=F32) + run[0:1, :]
    rank1 = jnp.sum(jnp.where(hit1, prefix, 0.0), axis=1, keepdims=True)
    rank2 = jnp.sum(jnp.where(hit2, prefix, 0.0), axis=1, keepdims=True)
    run[...] = run[...] + jnp.sum(onehot, axis=0, keepdims=True)
    cnt_ref[...] = run[...]

    rec = jnp.zeros(logits.shape, F32)
    for k, v in ((ROUTE_E1, i1.astype(F32)), (ROUTE_E2, i2.astype(F32)), (ROUTE_W1, w1), (ROUTE_W2, w2),
                 (ROUTE_R1, rank1), (ROUTE_R2, rank2)):
        rec = jnp.where(lane == k, v, rec)
    route_ref[...] = rec
    zero_ref[...] = jnp.zeros_like(zero_ref)


def _router(h, g2, mod3, w_r, layer, sorted_rows, *, tiles_per_batch, n_batch):
    T = h.shape[0]
    n_steps = T // TM
    zero_rows = -(-sorted_rows // (n_steps * SUBLANES)) * SUBLANES

    def grp(i):
        return jnp.minimum(i // tiles_per_batch, n_batch)

    def modspec(k):
        return pl.BlockSpec((None, 1, D_MODEL), lambda i: (grp(i), 0, k))

    return pl.pallas_call(
        _router_kernel,
        grid=(T // TM,),
        in_specs=[pl.BlockSpec((TM, D_MODEL), lambda i: (i, 0)),
                  pl.BlockSpec((None, 1, D_MODEL), lambda i: (layer, 0, 0)),
                  modspec(3), modspec(4),
                  pl.BlockSpec((D_MODEL, LANES), lambda i: (0, 0))],
        out_specs=(pl.BlockSpec((TM, LANES), lambda i: (i, 0)),
                   pl.BlockSpec((SUBLANES, LANES), lambda i: (0, 0)),
                   pl.BlockSpec((zero_rows, D_MODEL), lambda i: (i, 0))),
        out_shape=(jax.ShapeDtypeStruct((T, LANES), F32), jax.ShapeDtypeStruct((SUBLANES, LANES), F32),
                   jax.ShapeDtypeStruct((n_steps * zero_rows, D_MODEL), F32)),
        scratch_shapes=[pltpu.VMEM((SUBLANES, LANES), F32)],
        compiler_params=_cparams(1),
        name="moe_router",
    )(h, g2, mod3, mod3, w_r)


def _row_copy(src_ref, src_row, dst_ref, dst_row, sem):
    return pltpu.make_async_copy(src_ref.at[pl.ds(src_row, 1)], dst_ref.at[pl.ds(dst_row, 1)], sem)


def _dispatch_kernel(dest_ref, h_ref, g_ref, sh_ref, sc_ref, xs_in_ref, xs_ref, n_scr, sem):
    del xs_in_ref
    i = pl.program_id(0)
    base = i * (2 * TM)
    cur = i % 2
    n_scr[cur] = _norm_mod(h_ref[...], g_ref[...], sh_ref[...], sc_ref[...])

    def start(r, carry):
        for s in range(2):
            _row_copy(n_scr.at[cur], r, xs_ref, dest_ref[base + 2 * r + s], sem.at[cur]).start(priority=s)
        return carry

    lax.fori_loop(0, TM, start, 0, unroll=ROW_DMA_UNROLL)

    def wait_all(buf):
        for s in range(2):
            pltpu.make_async_copy(n_scr.at[buf], xs_ref.at[pl.ds(0, TM)], sem.at[buf]).wait()

    @pl.when(i > 0)
    def _():
        wait_all(1 - cur)

    @pl.when(i == pl.num_programs(0) - 1)
    def _():
        wait_all(cur)


def _dispatch(dest, h, g2, mod3, xs_zero, layer, *, tiles_per_batch, n_batch):
    T = h.shape[0]

    def grp(i):
        return jnp.minimum(i // tiles_per_batch, n_batch)

    def modspec(k):
        return pl.BlockSpec((None, 1, D_MODEL), lambda i, d: (grp(i), 0, k))

    grid_spec = pltpu.PrefetchScalarGridSpec(
        num_scalar_prefetch=1,
        grid=(T // TM,),
        in_specs=[pl.BlockSpec((TM, D_MODEL), lambda i, d: (i, 0)),
                  pl.BlockSpec((None, 1, D_MODEL), lambda i, d: (layer, 0, 0)),
                  modspec(3), modspec(4),
                  pl.BlockSpec(memory_space=pl.ANY)],
        out_specs=pl.BlockSpec(memory_space=pl.ANY),
        scratch_shapes=[pltpu.VMEM((2, TM, D_MODEL), F32), pltpu.SemaphoreType.DMA((2,))],
    )
    return pl.pallas_call(
        _dispatch_kernel,
        grid_spec=grid_spec,
        out_shape=jax.ShapeDtypeStruct(xs_zero.shape, xs_zero.dtype),
        input_output_aliases={5: 0},
        compiler_params=_cparams(1),
        name="moe_dispatch",
    )(dest, h, g2, mod3, mod3, xs_zero)


def _expert_kernel(te_ref, nu_ref, rows_ref, x_ref, wg_ref, wu_ref, wd_ref, acc, xb):
    del te_ref, nu_ref
    k = pl.program_id(0)
    f = pl.program_id(1)

    @pl.when(f == 0)
    def _():
        xb[...] = x_ref[...].astype(BF16)
        acc[...] = jnp.zeros_like(acc)

    tf = wg_ref.shape[1]
    x0 = xb[:TG_SUB, :]
    part = None
    weights = []
    for c0 in range(0, tf, MXU_WIDTH):
        cols = slice(c0, c0 + MXU_WIDTH)
        w3 = (wg_ref[:, cols].astype(BF16), wu_ref[:, cols].astype(BF16), wd_ref[cols, :].astype(BF16))
        weights.append(w3)
        p = _swiglu_partial(x0, *w3)
        part = p if part is None else part + p
    acc[:TG_SUB, :] += part
    for sub in range(1, TG // TG_SUB):
        rows = slice(sub * TG_SUB, (sub + 1) * TG_SUB)

        @pl.when(rows_ref[k] > sub * TG_SUB)
        def _():
            part = None
            for w3 in weights:
                p = _swiglu_partial(xb[rows, :], *w3)
                part = p if part is None else part + p
            acc[rows, :] += part


def _experts(tile_expert, n_used, tile_rows, xs, wg, wu, wd, layer, n_tiles):
    P = n_tiles * TG
    assert xs.shape[0] >= P
    n_f = MOE_F_CHUNKS
    tf = D_FF_EXPERT // n_f
    assert tf % LANES == 0

    def fsel(k, f, nu):
        return jnp.where(k < nu[0], f, n_f - 1)

    grid_spec = pltpu.PrefetchScalarGridSpec(
        num_scalar_prefetch=3,
        grid=(P // TG, n_f),
        in_specs=[pl.BlockSpec((TG, D_MODEL), lambda k, f, te, nu, tr: (k, 0)),
                  pl.BlockSpec((None, None, D_MODEL, tf), lambda k, f, te, nu, tr: (layer, te[k], 0, fsel(k, f, nu))),
                  pl.BlockSpec((None, None, D_MODEL, tf), lambda k, f, te, nu, tr: (layer, te[k], 0, fsel(k, f, nu))),
                  pl.BlockSpec((None, None, tf, D_MODEL), lambda k, f, te, nu, tr: (layer, te[k], fsel(k, f, nu), 0))],
        out_specs=pl.BlockSpec((TG, D_MODEL), lambda k, f, te, nu, tr: (k, 0)),
        scratch_shapes=[pltpu.VMEM((TG, D_MODEL), BF16)],
    )
    return pl.pallas_call(
        _expert_kernel,
        grid_spec=grid_spec,
        out_shape=jax.ShapeDtypeStruct((P, D_MODEL), F32),
        compiler_params=_cparams(2),
        name="moe_experts",
    )(tile_expert, n_used, tile_rows, xs, wg, wu, wd)


def _combine_kernel(dest_ref, h_ref, g2_ref, route_ref, fg_ref, y_ref, o_ref, ybuf, sem, *, final):
    i = pl.program_id(0)
    n_steps = pl.num_programs(0)

    def gather(tile, buf):
        base = tile * (2 * TM)

        def start(r, carry):
            for s in range(2):
                _row_copy(y_ref, dest_ref[base + 2 * r + s], ybuf.at[buf], s * TM + r,
                          sem.at[buf]).start(priority=s)
            return carry

        lax.fori_loop(0, TM, start, 0, unroll=ROW_DMA_UNROLL)

    @pl.when(i == 0)
    def _():
        gather(0, 0)

    @pl.when(i + 1 < n_steps)
    def _():
        gather(i + 1, (i + 1) % 2)

    cur = i % 2
    for s in range(2):
        pltpu.make_async_copy(y_ref.at[pl.ds(0, TM)], ybuf.at[cur, pl.ds(s * TM, TM)], sem.at[cur]).wait()
    route = route_ref[...]
    f = (route[:, ROUTE_W1:ROUTE_W1 + 1] * ybuf[cur, 0:TM, :]
         + route[:, ROUTE_W2:ROUTE_W2 + 1] * ybuf[cur, TM:, :])
    out = h_ref[...] + g2_ref[...] * f
    if final:
        ms = jnp.mean(out * out, axis=-1, keepdims=True)
        out = out * lax.rsqrt(ms + EPS) * fg_ref[...]
    o_ref[...] = out


def _combine(dest, h, mod3, route, final_g, y, *, final, tiles_per_batch, n_batch):
    n_tok_tiles = h.shape[0] // TM - (1 if final else 0)

    def grp(i):
        return jnp.minimum(i // tiles_per_batch, n_batch)

    grid_spec = pltpu.PrefetchScalarGridSpec(
        num_scalar_prefetch=1,
        grid=(n_tok_tiles,),
        in_specs=[pl.BlockSpec((TM, D_MODEL), lambda i, d: (i, 0)),
                  pl.BlockSpec((None, 1, D_MODEL), lambda i, d: (grp(i), 0, 5)),
                  pl.BlockSpec((TM, LANES), lambda i, d: (i, 0)),
                  pl.BlockSpec((1, D_MODEL), lambda i, d: (0, 0)),
                  pl.BlockSpec(memory_space=pl.ANY)],
        out_specs=pl.BlockSpec((TM, D_MODEL), lambda i, d: (i, 0)),
        scratch_shapes=[pltpu.VMEM((2, 2 * TM, D_MODEL), F32), pltpu.SemaphoreType.DMA((2,))],
    )
    return pl.pallas_call(
        functools.partial(_combine_kernel, final=final),
        grid_spec=grid_spec,
        out_shape=jax.ShapeDtypeStruct((n_tok_tiles * TM, D_MODEL), F32),
        compiler_params=_cparams(1),
        name="moe_combine",
    )(dest, h, mod3, route, final_g, y)


def _moe_layer(h, g2, mod3, w_r, wg, wu, wd, final_g, layer, moe_layer, *, final, **geo):
    T = h.shape[0]
    n_tiles = (2 * T) // TG + N_EXPERTS
    route, cnt, xs_zero = _router(h, g2, mod3, w_r, layer, n_tiles * TG, **geo)

    counts = cnt[0, :N_EXPERTS].astype(jnp.int32)
    padded = ((counts + TG - 1) // TG) * TG
    ends = jnp.cumsum(padded)
    offs = ends - padded
    e12 = route[:, ROUTE_E1:ROUTE_E2 + 1].astype(jnp.int32)
    r12 = route[:, ROUTE_R1:ROUTE_R2 + 1].astype(jnp.int32)
    onehot = e12[:, :, None] == jnp.arange(N_EXPERTS)[None, None, :]
    dest = (jnp.sum(jnp.where(onehot, offs[None, None, :], 0), axis=-1) + r12).reshape(2 * T)
    n_used = (ends[-1] // TG).reshape(1)
    tiles = jnp.arange(n_tiles)
    te_raw = jnp.sum(tiles[:, None] >= (ends // TG)[None, :], axis=1)
    tile_expert = jnp.minimum(te_raw, N_EXPERTS - 1).astype(jnp.int32)
    sel = tile_expert[:, None] == jnp.arange(N_EXPERTS)[None, :]
    cnt_k = jnp.sum(jnp.where(sel, counts[None, :], 0), axis=1)
    off_k = jnp.sum(jnp.where(sel, offs[None, :], 0), axis=1)
    tile_rows = jnp.where(te_raw < N_EXPERTS, jnp.clip(cnt_k - (tiles * TG - off_k), 0, TG), 0)

    xs = _dispatch(dest, h, g2, mod3, xs_zero, layer, **geo)
    y = _experts(tile_expert, n_used, tile_rows.astype(jnp.int32), xs, wg, wu, wd, moe_layer, n_tiles)
    return _combine(dest, h, mod3, route, final_g, y, final=final, **geo)


def _rope_tables(seq):
    assert seq % GRID_W == 0
    n_rows = seq // GRID_W
    inv = ROPE_THETA ** (-jnp.arange(ROPE_FREQS, dtype=F32) / ROPE_FREQS)
    ang_r = jnp.arange(n_rows, dtype=F32)[:, None] * inv
    ang_c = jnp.arange(GRID_W, dtype=F32)[:, None] * inv
    cos_r, sin_r = (jnp.repeat(f(ang_r), GRID_W, axis=0) for f in (jnp.cos, jnp.sin))
    cos_c, sin_c = (jnp.tile(f(ang_c), (n_rows, 1)) for f in (jnp.cos, jnp.sin))
    cos = jnp.concatenate([cos_r, cos_r, cos_c, cos_c], axis=1)
    sin = jnp.concatenate([-sin_r, sin_r, -sin_c, sin_c], axis=1)
    cos = jnp.concatenate([cos, jnp.ones((TM, HEAD_DIM), F32)], axis=0)
    sin = jnp.concatenate([sin, jnp.zeros((TM, HEAD_DIM), F32)], axis=0)
    return cos, sin


def kernel(x, c, ctx, c_ctx, w_mod, b_mod, norm1_g, norm2_g, w_in, attn_sink, conv_w, conv_b, gate_a_w, gate_a_b, gate_x_w, gate_x_b, lru_lambda, w_o_attn, w_o_lru, w_out, ff_w_gate, ff_w_up, ff_w_down, router_w, exp_w_gate, exp_w_up, exp_w_down, final_g):
    n_batch, seq, _ = x.shape
    ctx_len = ctx.shape[1]
    assert n_batch * ctx_len == TM and seq % TM == 0 and n_batch + 1 <= MOD_ROWS
    n_lat = n_batch * seq
    tiles_per_batch = seq // TM
    geo = dict(tiles_per_batch=tiles_per_batch, n_batch=n_batch)
    shp = dict(n_batch=n_batch, seq=seq, ctx_len=ctx_len)

    cpad = jnp.zeros((MOD_ROWS, D_MODEL), F32).at[:n_batch].set(c).at[n_batch].set(c_ctx)
    mod = _modulation(cpad, w_mod, b_mod)
    cos_t, sin_t = _rope_tables(seq)
    perm = _time_permutation(TM)
    stream = (x.reshape(n_lat, D_MODEL), ctx.reshape(n_batch * ctx_len, D_MODEL), 0)

    g1 = norm1_g.reshape(DEPTH, 1, D_MODEL)
    g2 = norm2_g.reshape(DEPTH, 1, D_MODEL)
    w_in_b = w_in.astype(BF16)
    woa_b, wol_b, wout_b = w_o_attn.astype(BF16), w_o_lru.astype(BF16), w_out.astype(BF16)
    ffg_b, ffu_b, ffd_b = ff_w_gate.astype(BF16), ff_w_up.astype(BF16), ff_w_down.astype(BF16)

    for l in range(DEPTH):
        mod3 = mod[l].reshape(MOD_ROWS, 1, 6 * D_MODEL)
        if l > 0:
            stream = (h, h, n_lat // TM)
        q, kv, xr, gy, sga, sgr = _in_proj(stream, g1, mod3, w_in_b, l, cos_t, sin_t, perm,
                                           n_lat_tiles=n_lat // TM, **geo)
        wcat = jnp.concatenate([gate_a_w[l], gate_x_w[l]], axis=-1).astype(BF16)
        gate_b = jnp.stack([gate_a_b[l], gate_x_b[l]], axis=1)
        hf, hb, attn = _token_mixer(xr, conv_w[l], conv_b[l].reshape(1, D_RNN), wcat, gate_b, lru_lambda[l],
                                    attn_sink[l], q, kv, **shp)
        h = _merge(stream, attn, hf, hb, gy, sga, sgr, mod3, woa_b, wol_b, wout_b, l, **geo)
        i = l // 2
        if l % 2 == 0:
            h = _ffn(h, g2, mod3, ffg_b, ffu_b, ffd_b, l, i, **geo)
        else:
            w_r = jnp.zeros((D_MODEL, LANES), F32).at[:, :N_EXPERTS].set(router_w[i])
            h = _moe_layer(h, g2, mod3, w_r, exp_w_gate, exp_w_up, exp_w_down,
                           final_g.reshape(1, D_MODEL), l, i, final=(l == DEPTH - 1), **geo)

    assert DEPTH % 2 == 0 and h.shape[0] == n_lat
    return h.reshape(n_batch, seq, D_MODEL)
```

```python
import functools

import jax
import jax.numpy as jnp
from jax import lax
from jax.experimental import pallas as pl
from jax.experimental.pallas import tpu as pltpu

F32 = jnp.float32
BF16 = jnp.bfloat16

D_MODEL = 1024
DEPTH = 4
GRID_W = 64
N_HEADS = 8
N_KV_HEADS = 2
HEAD_DIM = 128
N_GROUPS = N_HEADS // N_KV_HEADS
ATTN_BLOCK = 128
ROPE_THETA = 10000.0
ROPE_FREQS = HEAD_DIM // 4
D_RNN = 1024
N_RNN_BLOCKS = 8
RNN_BLOCK = D_RNN // N_RNN_BLOCKS
LRU_C = 8.0
D_FF = 2816
N_EXPERTS = 8
D_FF_EXPERT = 3584
EPS = 1e-6
NEG_INF = -1e30
Q_COLS = N_HEADS * HEAD_DIM
KV_COLS = N_KV_HEADS * HEAD_DIM
IN_COLS = Q_COLS + 2 * KV_COLS + 2 * D_RNN + 2 * D_MODEL
LOG2E = 1.4426950408889634
ATTN_SCALE = HEAD_DIM ** -0.5 * LOG2E

LANES = 128
SUBLANES = 8
MXU_WIDTH = 256
TM = 512
TN_IN = 512
TT = 256
SUBSEQ = TT // SUBLANES
TG = 1024
TG_SUB = 512
MOE_F_CHUNKS = 7
ROW_DMA_UNROLL = 8
MOD_ROWS = 8
VMEM_LIMIT = 56 * 1024 * 1024


def _cparams(n_axes):
    return pltpu.CompilerParams(dimension_semantics=("arbitrary",) * n_axes,
                                vmem_limit_bytes=VMEM_LIMIT)


def _sigmoid(z):
    return 0.5 * jnp.tanh(0.5 * z) + 0.5


def _norm_mod(h, g, shift, scale):
    ms = jnp.mean(h * h, axis=-1, keepdims=True)
    y = h * lax.rsqrt(ms + EPS) * g
    return y * (1.0 + scale) + shift


def _mod_kernel(c_ref, w_ref, b_ref, o_ref):
    cv = c_ref[...]
    s = cv * _sigmoid(cv)
    o_ref[...] = _dot_split(s, w_ref[...]) + b_ref[...]


def _modulation(cpad, w_mod, b_mod):
    nchunk = 6
    return pl.pallas_call(
        _mod_kernel,
        grid=(DEPTH, nchunk),
        in_specs=[
            pl.BlockSpec((MOD_ROWS, D_MODEL), lambda l, n: (0, 0)),
            pl.BlockSpec((None, D_MODEL, D_MODEL), lambda l, n: (l, 0, n)),
            pl.BlockSpec((None, 1, D_MODEL), lambda l, n: (l, 0, n)),
        ],
        out_specs=pl.BlockSpec((None, MOD_ROWS, D_MODEL), lambda l, n: (l, 0, n)),
        out_shape=jax.ShapeDtypeStruct((DEPTH, MOD_ROWS, 6 * D_MODEL), F32),
        compiler_params=_cparams(2),
        name="modulation",
    )(cpad, w_mod, b_mod.reshape(DEPTH, 1, 6 * D_MODEL))


def _rope(xh, cos, sin_signed, first_half):
    sw = jnp.where(first_half, pltpu.roll(xh, HEAD_DIM - ROPE_FREQS, 1), pltpu.roll(xh, ROPE_FREQS, 1))
    return xh * cos + sw * sin_signed


def _stream_tile(h_lat_ref, h_ctx_ref):
    is_ctx_tile = pl.program_id(0) == pl.num_programs(0) - 1
    return jnp.where(is_ctx_tile, h_ctx_ref[...], h_lat_ref[...])


def _stream_specs(stream, n_lat_tiles):
    _, _, ctx_block = stream
    return [pl.BlockSpec((TM, D_MODEL), lambda i: (jnp.minimum(i, n_lat_tiles - 1), 0)),
            pl.BlockSpec((TM, D_MODEL), lambda i: (ctx_block, 0))]


def _in_proj_kernel(h_lat_ref, h_ctx_ref, g_ref, sh_ref, sc_ref, w_ref, cos_ref, sin_ref, perm_ref,
                    q_ref, kv_ref, x_ref, gy_ref, sga_ref, sgr_ref):
    n = _norm_mod(_stream_tile(h_lat_ref, h_ctx_ref), g_ref[...], sh_ref[...], sc_ref[...]).astype(BF16)
    cos = cos_ref[...]
    sin = sin_ref[...]
    lane = lax.broadcasted_iota(jnp.int32, (TM, LANES), 1)
    first_half = (lane & ROPE_FREQS) == 0

    def proj(chunk):
        return jnp.dot(n, w_ref[:, chunk * TN_IN:(chunk + 1) * TN_IN], preferred_element_type=F32)

    def rope_store(acc, n_heads, scale, ref, col0):
        for hh in range(n_heads):
            y = _rope(acc[:, hh * HEAD_DIM:(hh + 1) * HEAD_DIM], cos, sin, first_half)
            if scale != 1.0:
                y = y * scale
            ref[:, col0 + hh * HEAD_DIM:col0 + (hh + 1) * HEAD_DIM] = y.astype(BF16)

    for c in range(2):
        rope_store(proj(c), 4, ATTN_SCALE, q_ref, c * TN_IN)
    acc = proj(2)
    rope_store(acc, 2, 1.0, kv_ref, 0)
    kv_ref[:, KV_COLS:] = acc[:, KV_COLS:].astype(BF16)
    n_perm = jnp.dot(perm_ref[...], n, preferred_element_type=F32).astype(BF16)
    for c in range(2):
        cols = slice(c * TN_IN, (c + 1) * TN_IN)
        x_ref[:, cols] = jnp.dot(n_perm, w_ref[:, (3 + c) * TN_IN:(4 + c) * TN_IN],
                                 preferred_element_type=F32)
        gy_ref[:, cols] = jax.nn.gelu(proj(5 + c)).astype(BF16)
        sga_ref[:, cols] = _sigmoid(proj(7 + c)).astype(BF16)
        sgr_ref[:, cols] = _sigmoid(proj(9 + c)).astype(BF16)


def _in_proj(stream, g, mod3, w_in, layer, cos_t, sin_t, perm, *, n_lat_tiles, tiles_per_batch, n_batch):
    T = (n_lat_tiles + 1) * TM

    def grp(i):
        return jnp.minimum(i // tiles_per_batch, n_batch)

    def pos_tile(i):
        return jnp.where(i < n_lat_tiles, i % tiles_per_batch, tiles_per_batch)

    def tok(width):
        return pl.BlockSpec((TM, width), lambda i: (i, 0))

    out_shape = (
        jax.ShapeDtypeStruct((T, Q_COLS), BF16),
        jax.ShapeDtypeStruct((T, 2 * KV_COLS), BF16),
        jax.ShapeDtypeStruct((T, D_RNN), F32),
        jax.ShapeDtypeStruct((T, D_RNN), BF16),
        jax.ShapeDtypeStruct((T, D_MODEL), BF16),
        jax.ShapeDtypeStruct((T, D_MODEL), BF16),
    )
    return pl.pallas_call(
        _in_proj_kernel,
        grid=(T // TM,),
        in_specs=_stream_specs(stream, n_lat_tiles) + [
            pl.BlockSpec((None, 1, D_MODEL), lambda i: (layer, 0, 0)),
            pl.BlockSpec((None, 1, D_MODEL), lambda i: (grp(i), 0, 0)),
            pl.BlockSpec((None, 1, D_MODEL), lambda i: (grp(i), 0, 1)),
            pl.BlockSpec((None, D_MODEL, IN_COLS), lambda i: (layer, 0, 0)),
            pl.BlockSpec((TM, HEAD_DIM), lambda i: (pos_tile(i), 0)),
            pl.BlockSpec((TM, HEAD_DIM), lambda i: (pos_tile(i), 0)),
            pl.BlockSpec((TM, TM), lambda i: (0, 0)),
        ],
        out_specs=(tok(Q_COLS), tok(2 * KV_COLS), tok(D_RNN), tok(D_RNN), tok(D_MODEL), tok(D_MODEL)),
        out_shape=out_shape,
        compiler_params=_cparams(1),
        name="in_proj",
    )(stream[0], stream[1], g, mod3, mod3, w_in, cos_t, sin_t, perm)


def _stack_heads(qt):
    return jnp.concatenate([qt[:, g * HEAD_DIM:(g + 1) * HEAD_DIM] for g in range(N_GROUPS)], axis=0)


def _sink_col(sink_ref, kvh, rows):
    return jnp.concatenate(
        [jnp.full((rows, 1), sink_ref[kvh * N_GROUPS + g] * LOG2E, F32) for g in range(N_GROUPS)], axis=0)


def _ones_column(n_keys):
    lane = lax.broadcasted_iota(jnp.int32, (n_keys, HEAD_DIM), 1)
    return jnp.where(lane == 0, 1.0, 0.0).astype(BF16)


def _softplus(z):
    return jnp.maximum(z, 0.0) + jnp.log1p(jnp.exp(-jnp.abs(z)))


def _scan8(a, b, row, reverse):
    for s in (1, 2, 4):
        if reverse:
            keep = row < SUBLANES - s
            shift = SUBLANES - s
        else:
            keep = row >= s
            shift = s
        a_sh = jnp.where(keep, pltpu.roll(a, shift, 0), 1.0)
        b_sh = jnp.where(keep, pltpu.roll(b, shift, 0), 0.0)
        b = a * b_sh + b
        a = a * a_sh
    return a, b


def _attention_stages(sink_ref, q_ref, kv_ref, kvc_ref, o_ref, tile, n_blocks, is_ctx):
    rows = ATTN_BLOCK * N_GROUPS
    qi = lax.broadcasted_iota(jnp.int32, (rows, ATTN_BLOCK), 0) & (ATTN_BLOCK - 1)
    kj = lax.broadcasted_iota(jnp.int32, (rows, ATTN_BLOCK), 1)
    tri_prev = kj >= qi
    tri_next = kj <= qi
    q_per_tile = TT // ATTN_BLOCK
    gw = N_GROUPS * HEAD_DIM
    st_cols = 3 * ATTN_BLOCK + kvc_ref.shape[0]
    stages = []
    for kvh in range(N_KV_HEADS):
        for qb in range(q_per_tile):
            kcol = slice(kvh * HEAD_DIM, (kvh + 1) * HEAD_DIM)
            vcol = slice(KV_COLS + kvh * HEAD_DIM, KV_COLS + (kvh + 1) * HEAD_DIM)
            qrows = slice(qb * ATTN_BLOCK, (qb + 1) * ATTN_BLOCK)
            n = tile * q_per_tile + qb
            starts = [pl.multiple_of(jnp.maximum(n - 1, 0) * ATTN_BLOCK, ATTN_BLOCK),
                      pl.multiple_of(n * ATTN_BLOCK, ATTN_BLOCK),
                      pl.multiple_of(jnp.minimum(n + 1, n_blocks - 1) * ATTN_BLOCK, ATTN_BLOCK)]
            st = {}

            def scores(st=st, kvh=kvh, kcol=kcol, qrows=qrows, n=n, starts=starts):
                qs = _stack_heads(q_ref[qrows, kvh * gw:(kvh + 1) * gw])
                kall = jnp.concatenate(
                    [kv_ref[pl.ds(r, ATTN_BLOCK), kcol] for r in starts] + [kvc_ref[:, kcol]], axis=0)
                s = lax.dot_general(qs, kall, (((1,), (1,)), ((), ())), preferred_element_type=F32)
                pen_band = jnp.where(is_ctx, NEG_INF, 0.0)
                pen_prev = jnp.where(n > 0, pen_band, NEG_INF)
                pen_next = jnp.where(n < n_blocks - 1, pen_band, NEG_INF)
                sp = jnp.where(tri_prev, s[:, :ATTN_BLOCK] + pen_prev, NEG_INF)
                sc = s[:, ATTN_BLOCK:2 * ATTN_BLOCK] + pen_band
                sn = jnp.where(tri_next, s[:, 2 * ATTN_BLOCK:3 * ATTN_BLOCK] + pen_next, NEG_INF)
                st["s"] = jnp.concatenate([sp, sc, sn, s[:, 3 * ATTN_BLOCK:]], axis=1)

            def row_max(st=st, kvh=kvh):
                sink = _sink_col(sink_ref, kvh, ATTN_BLOCK)
                st["m"] = jnp.maximum(jnp.max(st["s"], axis=1, keepdims=True), sink)
                st["sink_term"] = jnp.exp2(sink - st["m"])
                st["p"] = []

            def numerators(c, st=st):
                cols = slice(c * ATTN_BLOCK, (c + 1) * ATTN_BLOCK)
                st["p"].append(jnp.exp2(st["s"][:, cols] - st["m"]).astype(BF16))

            def weighted_values(st=st, kvh=kvh, vcol=vcol, qrows=qrows, starts=starts):
                vall = jnp.concatenate(
                    [kv_ref[pl.ds(r, ATTN_BLOCK), vcol] for r in starts] + [kvc_ref[:, vcol]], axis=0)
                v_aug = jnp.concatenate([vall, _ones_column(vall.shape[0])], axis=1)
                p = jnp.concatenate(st.pop("p"), axis=1)
                oa = jnp.dot(p, v_aug, preferred_element_type=F32)
                o = oa[:, :HEAD_DIM] / (oa[:, HEAD_DIM:HEAD_DIM + 1] + st.pop("sink_term"))
                for g in range(N_GROUPS):
                    col0 = kvh * gw + g * HEAD_DIM
                    o_ref[qrows, col0:col0 + HEAD_DIM] = o[g * ATTN_BLOCK:(g + 1) * ATTN_BLOCK].astype(BF16)
                st.clear()

            n_chunks = st_cols // ATTN_BLOCK
            stages += ([scores, row_max] + [functools.partial(numerators, c) for c in range(n_chunks)]
                       + [weighted_values])
    return stages


def _mixer_kernel(xf_ref, xfp_ref, xfn_ref, xb_ref, xbp_ref, xbn_ref, cw_ref, cb_ref, wcat_ref,
                  gb_ref, lam_ref, unperm_ref, sink_ref, q_ref, kv_ref, kvc_ref,
                  hf_ref, hb_ref, attn_ref, a_s, b_s, h_bf, carry, *, n_blocks):
    j = pl.program_id(1)
    last_j = pl.num_programs(1) - 1

    @pl.when(j == 0)
    def _():
        carry[...] = jnp.zeros_like(carry)

    seg_start = (j <= 1, (j == 0) | (j == last_j))
    seg_end = ((j == 0) | (j == last_j), j <= 1)
    mains = (xf_ref, xb_ref)
    prevs = (xfp_ref, xbp_ref)
    nexts = (xfn_ref, xbn_ref)
    sub = lax.broadcasted_iota(jnp.int32, (SUBLANES, D_RNN), 0)
    S8 = SUBLANES

    def conv(d):
        x0 = mains[d][...]
        halo = prevs[d][...]
        t_m1 = jnp.where(seg_start[d], 0.0, halo[2 * S8 - 1:2 * S8, :])
        t_m2 = jnp.where(seg_start[d], 0.0, halo[S8 - 1:S8, :])
        t_p1 = jnp.where(seg_end[d], 0.0, nexts[d][0:1, :])
        g_m1 = jnp.where(sub == 0, t_m1, pltpu.roll(x0[TT - S8:, :], 1, 0))
        g_m2 = jnp.where(sub == 0, t_m2, pltpu.roll(x0[TT - 2 * S8:TT - S8, :], 1, 0))
        g_p1 = jnp.where(sub == S8 - 1, t_p1, pltpu.roll(x0[:S8, :], S8 - 1, 0))
        xc = cb_ref[...] + jnp.concatenate([g_m2, g_m1, x0[:TT - 2 * S8, :]], axis=0) * cw_ref[0:1, :]
        xc = xc + jnp.concatenate([g_m1, x0[:TT - S8, :]], axis=0) * cw_ref[1:2, :]
        xc = xc + x0 * cw_ref[2:3, :]
        return xc + jnp.concatenate([x0[S8:, :], g_p1], axis=0) * cw_ref[3:4, :]

    def gates(d, blk, xc, c_d):
        sl = slice(blk * RNN_BLOCK, (blk + 1) * RNN_BLOCK)
        xcb = xc[:, sl]
        z = jnp.dot(xcb.astype(BF16), wcat_ref[d, blk], preferred_element_type=F32)
        r = _sigmoid(z[:, :RNN_BLOCK] + gb_ref[d, 0:1, sl])
        ig = _sigmoid(z[:, RNN_BLOCK:] + gb_ref[d, 1:2, sl])
        a = jnp.exp2(r * c_d[:, sl])
        v = 1.0 - a * a
        root = jnp.where(v > 0.0, v * lax.rsqrt(v), 0.0)
        a_s[d, :, sl] = a
        b_s[d, :, sl] = root * (ig * xcb)

    attn_stages = _attention_stages(sink_ref, q_ref, kv_ref, kvc_ref, attn_ref, jnp.maximum(j - 1, 0),
                                    n_blocks, j == 0)
    n_gate_blocks = 2 * N_RNN_BLOCKS
    emitted = 0
    for d in range(2):
        xc = conv(d)
        c_d = (-LRU_C * LOG2E) * _softplus(-lam_ref[d:d + 1, :])
        for blk in range(N_RNN_BLOCKS):
            gates(d, blk, xc, c_d)
            done = d * N_RNN_BLOCKS + blk + 1
            while emitted * n_gate_blocks < done * len(attn_stages):
                attn_stages[emitted]()
                emitted += 1
    assert emitted == len(attn_stages)

    def local(g, c):
        hf, pf, hb, pb = c
        rf = pl.multiple_of(g * S8, S8)
        rb = pl.multiple_of((SUBSEQ - 1 - g) * S8, S8)
        af = a_s[0, pl.ds(rf, S8), :]
        ab = a_s[1, pl.ds(rb, S8), :]
        hf = af * hf + b_s[0, pl.ds(rf, S8), :]
        hb = ab * hb + b_s[1, pl.ds(rb, S8), :]
        pf = af * pf
        pb = ab * pb
        b_s[0, pl.ds(rf, S8), :] = hf
        b_s[1, pl.ds(rb, S8), :] = hb
        a_s[0, pl.ds(rf, S8), :] = pf
        a_s[1, pl.ds(rb, S8), :] = pb
        return hf, pf, hb, pb

    zero = jnp.zeros((S8, D_RNN), F32)
    one = jnp.ones((S8, D_RNN), F32)
    hf, pf, hb, pb = lax.fori_loop(0, SUBSEQ, local, (zero, one, zero, one), unroll=4)

    af, bf = _scan8(pf, hf, sub, False)
    endf = af * carry[0] + bf
    h_in = [jnp.where(sub == 0, carry[0], pltpu.roll(endf, 1, 0))]
    carry[0] = jnp.broadcast_to(endf[S8 - 1:S8, :], (S8, D_RNN))
    ab, bb = _scan8(pb, hb, sub, True)
    endb = ab * carry[1] + bb
    h_in.append(jnp.where(sub == S8 - 1, carry[1], pltpu.roll(endb, S8 - 1, 0)))
    carry[1] = jnp.broadcast_to(endb[0:1, :], (S8, D_RNN))

    outs = (hf_ref, hb_ref)
    for d in range(2):
        h_in2 = jnp.concatenate([h_in[d], h_in[d]], axis=0)

        def fix(k, c, d=d, h_in2=h_in2):
            rows = pl.ds(pl.multiple_of(k * 2 * S8, 2 * S8), 2 * S8)
            h_bf[d, rows, :] = (b_s[d, rows, :] + a_s[d, rows, :] * h_in2).astype(BF16)
            return c

        lax.fori_loop(0, SUBSEQ // 2, fix, 0, unroll=2)
        outs[d][...] = jnp.dot(unperm_ref[...], h_bf[d], preferred_element_type=F32).astype(BF16)


def _time_permutation(n_rows):
    p = jnp.arange(n_rows)
    src = (p // TT) * TT + (p % SUBLANES) * SUBSEQ + (p % TT) // SUBLANES
    return (src[:, None] == jnp.arange(n_rows)[None, :]).astype(BF16)


def _token_mixer(x, conv_w, conv_b, wcat, gate_b, lam, sink, q, kv, *, n_batch, seq, ctx_len):
    T = x.shape[0]
    assert ctx_len == TT and seq % TT == 0
    tps = seq // TT
    n_lat_t = n_batch * tps
    halo_rows = 2 * SUBLANES
    per_halo = TT // halo_rows
    per8 = TT // SUBLANES
    last8 = T // SUBLANES - 1
    unperm = _time_permutation(TT).T

    def ftile(b, j):
        return jnp.where(j == 0, n_lat_t + b, b * tps + j - 1)

    def btile(b, j):
        return jnp.where(j == 0, n_lat_t + b, b * tps + tps - j)

    def main(tile):
        return pl.BlockSpec((TT, D_RNN), lambda b, j: (tile(b, j), 0))

    def prev(tile):
        return pl.BlockSpec((halo_rows, D_RNN), lambda b, j: (jnp.maximum(tile(b, j) * per_halo - 1, 0), 0))

    def nxt(tile):
        return pl.BlockSpec((SUBLANES, D_RNN),
                            lambda b, j: (jnp.minimum((tile(b, j) + 1) * per8, last8), 0))

    def const(shape):
        return pl.BlockSpec(shape, lambda b, j: (0,) * len(shape))

    q_spec = pl.BlockSpec((TT, Q_COLS), lambda b, j: (ftile(b, j), 0))
    return pl.pallas_call(
        functools.partial(_mixer_kernel, n_blocks=seq // ATTN_BLOCK),
        grid=(n_batch, tps + 1),
        in_specs=[main(ftile), prev(ftile), nxt(ftile), main(btile), prev(btile), nxt(btile),
                  const(conv_w.shape), const(conv_b.shape), const(wcat.shape), const(gate_b.shape),
                  const(lam.shape), const(unperm.shape),
                  pl.BlockSpec(memory_space=pltpu.SMEM),
                  q_spec,
                  pl.BlockSpec((seq, 2 * KV_COLS), lambda b, j: (b, 0)),
                  pl.BlockSpec((ctx_len, 2 * KV_COLS), lambda b, j: (n_lat_t + b, 0))],
        out_specs=(main(ftile), main(btile), q_spec),
        out_shape=(jax.ShapeDtypeStruct((T, D_RNN), BF16), jax.ShapeDtypeStruct((T, D_RNN), BF16),
                   jax.ShapeDtypeStruct((T, Q_COLS), BF16)),
        scratch_shapes=[pltpu.VMEM((2, TT, D_RNN), F32),
                        pltpu.VMEM((2, TT, D_RNN), F32),
                        pltpu.VMEM((2, TT, D_RNN), BF16),
                        pltpu.VMEM((2, SUBLANES, D_RNN), F32)],
        compiler_params=_cparams(2),
        name="token_mixer",
    )(x, x, x, x, x, x, conv_w, conv_b, wcat, gate_b, lam, unperm, sink, q, kv, kv)


def _merge_kernel(h_lat_ref, h_ctx_ref, attn_ref, hf_ref, hb_ref, gy_ref, sga_ref, sgr_ref,
                  g1_ref, woa_ref, wol_ref, wout_ref, o_ref):
    rec = ((hf_ref[...].astype(F32) + hb_ref[...].astype(F32)) * gy_ref[...].astype(F32)).astype(BF16)
    ta = jnp.dot(attn_ref[...], woa_ref[...], preferred_element_type=F32)
    tl = jnp.dot(rec, wol_ref[...], preferred_element_type=F32)
    m = sga_ref[...].astype(F32) * ta + sgr_ref[...].astype(F32) * tl
    y = jnp.dot(m.astype(BF16), wout_ref[...], preferred_element_type=F32)
    o_ref[...] = _stream_tile(h_lat_ref, h_ctx_ref) + g1_ref[...] * y


def _merge(stream, attn, hf, hb, gy, sga, sgr, mod3, woa, wol, wout, layer, *, tiles_per_batch, n_batch):
    T = attn.shape[0]
    n_lat_tiles = T // TM - 1

    def grp(i):
        return jnp.minimum(i // tiles_per_batch, n_batch)

    tok = pl.BlockSpec((TM, D_MODEL), lambda i: (i, 0))
    wsp = pl.BlockSpec((None, D_MODEL, D_MODEL), lambda i: (layer, 0, 0))
    return pl.pallas_call(
        _merge_kernel,
        grid=(T // TM,),
        in_specs=_stream_specs(stream, n_lat_tiles) + [
                  tok, tok, tok, tok, tok, tok,
                  pl.BlockSpec((None, 1, D_MODEL), lambda i: (grp(i), 0, 2)),
                  wsp, wsp, wsp],
        out_specs=tok,
        out_shape=jax.ShapeDtypeStruct((T, D_MODEL), F32),
        compiler_params=_cparams(1),
        name="merge",
    )(stream[0], stream[1], attn, hf, hb, gy, sga, sgr, mod3, woa, wol, wout)


def _swiglu_partial(n, wg, wu, wd):
    gt = jnp.dot(n, wg, preferred_element_type=F32)
    ut = jnp.dot(n, wu, preferred_element_type=F32)
    act = (gt * _sigmoid(gt) * ut).astype(BF16)
    return jnp.dot(act, wd, preferred_element_type=F32)


FFN_F_CHUNKS = 11


def _ffn_kernel(h_ref, g_ref, sh_ref, sc_ref, g2_ref, wg_ref, wu_ref, wd_ref, o_ref):
    h = h_ref[...]
    n = _norm_mod(h, g_ref[...], sh_ref[...], sc_ref[...]).astype(BF16)
    tf = D_FF // FFN_F_CHUNKS
    acc = None
    for c in range(FFN_F_CHUNKS):
        cols = slice(c * tf, (c + 1) * tf)
        part = _swiglu_partial(n, wg_ref[:, cols], wu_ref[:, cols], wd_ref[cols, :])
        acc = part if acc is None else acc + part
    o_ref[...] = h + g2_ref[...] * acc


def _ffn(h, g, mod3, wg, wu, wd, layer, ff_layer, *, tiles_per_batch, n_batch):
    T = h.shape[0]
    assert (D_FF // FFN_F_CHUNKS) % LANES == 0

    def grp(i):
        return jnp.minimum(i // tiles_per_batch, n_batch)

    def modspec(k):
        return pl.BlockSpec((None, 1, D_MODEL), lambda i: (grp(i), 0, k))

    tok = pl.BlockSpec((TM, D_MODEL), lambda i: (i, 0))
    return pl.pallas_call(
        _ffn_kernel,
        grid=(T // TM,),
        in_specs=[tok, pl.BlockSpec((None, 1, D_MODEL), lambda i: (layer, 0, 0)),
                  modspec(3), modspec(4), modspec(5),
                  pl.BlockSpec((None, D_MODEL, D_FF), lambda i: (ff_layer, 0, 0)),
                  pl.BlockSpec((None, D_MODEL, D_FF), lambda i: (ff_layer, 0, 0)),
                  pl.BlockSpec((None, D_FF, D_MODEL), lambda i: (ff_layer, 0, 0))],
        out_specs=tok,
        out_shape=jax.ShapeDtypeStruct((T, D_MODEL), F32),
        compiler_params=_cparams(1),
        name="dense_ffn",
    )(h, g, mod3, mod3, mod3, wg, wu, wd)


ROUTE_E1, ROUTE_E2, ROUTE_W1, ROUTE_W2, ROUTE_R1, ROUTE_R2 = range(6)


def _dot_split(a, b):
    a_hi = a.astype(BF16)
    a_lo = (a - a_hi.astype(F32)).astype(BF16)
    b_hi = b.astype(BF16)
    b_lo = (b - b_hi.astype(F32)).astype(BF16)

    def mm(x, y):
        return jnp.dot(x, y, preferred_element_type=F32)

    return mm(a_hi, b_hi) + (mm(a_hi, b_lo) + mm(a_lo, b_hi))


def _router_kernel(h_ref, g_ref, sh_ref, sc_ref, wr_ref, route_ref, cnt_ref, zero_ref, run):
    @pl.when(pl.program_id(0) == 0)
    def _():
        run[...] = jnp.zeros_like(run)

    n = _norm_mod(h_ref[...], g_ref[...], sh_ref[...], sc_ref[...])
    logits = _dot_split(n, wr_ref[...])
    lane = lax.broadcasted_iota(jnp.int32, logits.shape, 1)
    logits = jnp.where(lane < N_EXPERTS, logits, -jnp.inf)
    m1 = jnp.max(logits, axis=1, keepdims=True)
    i1 = jnp.min(jnp.where(logits == m1, lane, LANES), axis=1, keepdims=True)
    rest = jnp.where(lane == i1, -jnp.inf, logits)
    m2 = jnp.max(rest, axis=1, keepdims=True)
    i2 = jnp.min(jnp.where(rest == m2, lane, LANES), axis=1, keepdims=True)
    e2 = jnp.exp(m2 - m1)
    w1 = 1.0 / (1.0 + e2)
    w2 = e2 / (1.0 + e2)

    hit1 = lane == i1
    hit2 = lane == i2
    onehot = jnp.where(hit1 | hit2, 1.0, 0.0)
    r_i = lax.broadcasted_iota(jnp.int32, (TM, TM), 0)
    c_i = lax.broadcasted_iota(jnp.int32, (TM, TM), 1)
    lower = jnp.where(c_i < r_i, 1.0, 0.0).astype(BF16)
    prefix = jnp.dot(lower, onehot.astype(BF16), preferred_element_type=F32) + run[0:1, :]
    rank1 = jnp.sum(jnp.where(hit1, prefix, 0.0), axis=1, keepdims=True)
    rank2 = jnp.sum(jnp.where(hit2, prefix, 0.0), axis=1, keepdims=True)
    run[...] = run[...] + jnp.sum(onehot, axis=0, keepdims=True)
    cnt_ref[...] = run[...]

    rec = jnp.zeros(logits.shape, F32)
    for k, v in ((ROUTE_E1, i1.astype(F32)), (ROUTE_E2, i2.astype(F32)), (ROUTE_W1, w1), (ROUTE_W2, w2),
                 (ROUTE_R1, rank1), (ROUTE_R2, rank2)):
        rec = jnp.where(lane == k, v, rec)
    route_ref[...] = rec
    zero_ref[...] = jnp.zeros_like(zero_ref)


def _router(h, g2, mod3, w_r, layer, sorted_rows, *, tiles_per_batch, n_batch):
    T = h.shape[0]
    n_steps = T // TM
    zero_rows = -(-sorted_rows // (n_steps * SUBLANES)) * SUBLANES

    def grp(i):
        return jnp.minimum(i // tiles_per_batch, n_batch)

    def modspec(k):
        return pl.BlockSpec((None, 1, D_MODEL), lambda i: (grp(i), 0, k))

    return pl.pallas_call(
        _router_kernel,
        grid=(T // TM,),
        in_specs=[pl.BlockSpec((TM, D_MODEL), lambda i: (i, 0)),
                  pl.BlockSpec((None, 1, D_MODEL), lambda i: (layer, 0, 0)),
                  modspec(3), modspec(4),
                  pl.BlockSpec((D_MODEL, LANES), lambda i: (0, 0))],
        out_specs=(pl.BlockSpec((TM, LANES), lambda i: (i, 0)),
                   pl.BlockSpec((SUBLANES, LANES), lambda i: (0, 0)),
                   pl.BlockSpec((zero_rows, D_MODEL), lambda i: (i, 0))),
        out_shape=(jax.ShapeDtypeStruct((T, LANES), F32), jax.ShapeDtypeStruct((SUBLANES, LANES), F32),
                   jax.ShapeDtypeStruct((n_steps * zero_rows, D_MODEL), F32)),
        scratch_shapes=[pltpu.VMEM((SUBLANES, LANES), F32)],
        compiler_params=_cparams(1),
        name="moe_router",
    )(h, g2, mod3, mod3, w_r)


def _row_copy(src_ref, src_row, dst_ref, dst_row, sem):
    return pltpu.make_async_copy(src_ref.at[pl.ds(src_row, 1)], dst_ref.at[pl.ds(dst_row, 1)], sem)


def _dispatch_kernel(dest_ref, h_ref, g_ref, sh_ref, sc_ref, xs_in_ref, xs_ref, n_scr, sem):
    del xs_in_ref
    i = pl.program_id(0)
    base = i * (2 * TM)
    cur = i % 2
    n_scr[cur] = _norm_mod(h_ref[...], g_ref[...], sh_ref[...], sc_ref[...])

    def start(r, carry):
        for s in range(2):
            _row_copy(n_scr.at[cur], r, xs_ref, dest_ref[base + 2 * r + s], sem.at[cur]).start(priority=s)
        return carry

    lax.fori_loop(0, TM, start, 0, unroll=ROW_DMA_UNROLL)

    def wait_all(buf):
        for s in range(2):
            pltpu.make_async_copy(n_scr.at[buf], xs_ref.at[pl.ds(0, TM)], sem.at[buf]).wait()

    @pl.when(i > 0)
    def _():
        wait_all(1 - cur)

    @pl.when(i == pl.num_programs(0) - 1)
    def _():
        wait_all(cur)


def _dispatch(dest, h, g2, mod3, xs_zero, layer, *, tiles_per_batch, n_batch):
    T = h.shape[0]

    def grp(i):
        return jnp.minimum(i // tiles_per_batch, n_batch)

    def modspec(k):
        return pl.BlockSpec((None, 1, D_MODEL), lambda i, d: (grp(i), 0, k))

    grid_spec = pltpu.PrefetchScalarGridSpec(
        num_scalar_prefetch=1,
        grid=(T // TM,),
        in_specs=[pl.BlockSpec((TM, D_MODEL), lambda i, d: (i, 0)),
                  pl.BlockSpec((None, 1, D_MODEL), lambda i, d: (layer, 0, 0)),
                  modspec(3), modspec(4),
                  pl.BlockSpec(memory_space=pl.ANY)],
        out_specs=pl.BlockSpec(memory_space=pl.ANY),
        scratch_shapes=[pltpu.VMEM((2, TM, D_MODEL), F32), pltpu.SemaphoreType.DMA((2,))],
    )
    return pl.pallas_call(
        _dispatch_kernel,
        grid_spec=grid_spec,
        out_shape=jax.ShapeDtypeStruct(xs_zero.shape, xs_zero.dtype),
        input_output_aliases={5: 0},
        compiler_params=_cparams(1),
        name="moe_dispatch",
    )(dest, h, g2, mod3, mod3, xs_zero)


def _expert_kernel(te_ref, nu_ref, rows_ref, x_ref, wg_ref, wu_ref, wd_ref, acc, xb):
    del te_ref, nu_ref
    k = pl.program_id(0)
    f = pl.program_id(1)

    @pl.when(f == 0)
    def _():
        xb[...] = x_ref[...].astype(BF16)
        acc[...] = jnp.zeros_like(acc)

    tf = wg_ref.shape[1]
    x0 = xb[:TG_SUB, :]
    part = None
    weights = []
    for c0 in range(0, tf, MXU_WIDTH):
        cols = slice(c0, c0 + MXU_WIDTH)
        w3 = (wg_ref[:, cols].astype(BF16), wu_ref[:, cols].astype(BF16), wd_ref[cols, :].astype(BF16))
        weights.append(w3)
        p = _swiglu_partial(x0, *w3)
        part = p if part is None else part + p
    acc[:TG_SUB, :] += part
    for sub in range(1, TG // TG_SUB):
        rows = slice(sub * TG_SUB, (sub + 1) * TG_SUB)

        @pl.when(rows_ref[k] > sub * TG_SUB)
        def _():
            part = None
            for w3 in weights:
                p = _swiglu_partial(xb[rows, :], *w3)
                part = p if part is None else part + p
            acc[rows, :] += part


def _experts(tile_expert, n_used, tile_rows, xs, wg, wu, wd, layer, n_tiles):
    P = n_tiles * TG
    assert xs.shape[0] >= P
    n_f = MOE_F_CHUNKS
    tf = D_FF_EXPERT // n_f
    assert tf % LANES == 0

    def fsel(k, f, nu):
        return jnp.where(k < nu[0], f, n_f - 1)

    grid_spec = pltpu.PrefetchScalarGridSpec(
        num_scalar_prefetch=3,
        grid=(P // TG, n_f),
        in_specs=[pl.BlockSpec((TG, D_MODEL), lambda k, f, te, nu, tr: (k, 0)),
                  pl.BlockSpec((None, None, D_MODEL, tf), lambda k, f, te, nu, tr: (layer, te[k], 0, fsel(k, f, nu))),
                  pl.BlockSpec((None, None, D_MODEL, tf), lambda k, f, te, nu, tr: (layer, te[k], 0, fsel(k, f, nu))),
                  pl.BlockSpec((None, None, tf, D_MODEL), lambda k, f, te, nu, tr: (layer, te[k], fsel(k, f, nu), 0))],
        out_specs=pl.BlockSpec((TG, D_MODEL), lambda k, f, te, nu, tr: (k, 0)),
        scratch_shapes=[pltpu.VMEM((TG, D_MODEL), BF16)],
    )
    return pl.pallas_call(
        _expert_kernel,
        grid_spec=grid_spec,
        out_shape=jax.ShapeDtypeStruct((P, D_MODEL), F32),
        compiler_params=_cparams(2),
        name="moe_experts",
    )(tile_expert, n_used, tile_rows, xs, wg, wu, wd)


def _combine_kernel(dest_ref, h_ref, g2_ref, route_ref, fg_ref, y_ref, o_ref, ybuf, sem, *, final):
    i = pl.program_id(0)
    n_steps = pl.num_programs(0)

    def gather(tile, buf):
        base = tile * (2 * TM)

        def start(r, carry):
            for s in range(2):
                _row_copy(y_ref, dest_ref[base + 2 * r + s], ybuf.at[buf], s * TM + r,
                          sem.at[buf]).start(priority=s)
            return carry

        lax.fori_loop(0, TM, start, 0, unroll=ROW_DMA_UNROLL)

    @pl.when(i == 0)
    def _():
        gather(0, 0)

    @pl.when(i + 1 < n_steps)
    def _():
        gather(i + 1, (i + 1) % 2)

    cur = i % 2
    for s in range(2):
        pltpu.make_async_copy(y_ref.at[pl.ds(0, TM)], ybuf.at[cur, pl.ds(s * TM, TM)], sem.at[cur]).wait()
    route = route_ref[...]
    f = (route[:, ROUTE_W1:ROUTE_W1 + 1] * ybuf[cur, 0:TM, :]
         + route[:, ROUTE_W2:ROUTE_W2 + 1] * ybuf[cur, TM:, :])
    out = h_ref[...] + g2_ref[...] * f
    if final:
        ms = jnp.mean(out * out, axis=-1, keepdims=True)
        out = out * lax.rsqrt(ms + EPS) * fg_ref[...]
    o_ref[...] = out


def _combine(dest, h, mod3, route, final_g, y, *, final, tiles_per_batch, n_batch):
    n_tok_tiles = h.shape[0] // TM - (1 if final else 0)

    def grp(i):
        return jnp.minimum(i // tiles_per_batch, n_batch)

    grid_spec = pltpu.PrefetchScalarGridSpec(
        num_scalar_prefetch=1,
        grid=(n_tok_tiles,),
        in_specs=[pl.BlockSpec((TM, D_MODEL), lambda i, d: (i, 0)),
                  pl.BlockSpec((None, 1, D_MODEL), lambda i, d: (grp(i), 0, 5)),
                  pl.BlockSpec((TM, LANES), lambda i, d: (i, 0)),
                  pl.BlockSpec((1, D_MODEL), lambda i, d: (0, 0)),
                  pl.BlockSpec(memory_space=pl.ANY)],
        out_specs=pl.BlockSpec((TM, D_MODEL), lambda i, d: (i, 0)),
        scratch_shapes=[pltpu.VMEM((2, 2 * TM, D_MODEL), F32), pltpu.SemaphoreType.DMA((2,))],
    )
    return pl.pallas_call(
        functools.partial(_combine_kernel, final=final),
        grid_spec=grid_spec,
        out_shape=jax.ShapeDtypeStruct((n_tok_tiles * TM, D_MODEL), F32),
        compiler_params=_cparams(1),
        name="moe_combine",
    )(dest, h, mod3, route, final_g, y)


def _moe_layer(h, g2, mod3, w_r, wg, wu, wd, final_g, layer, moe_layer, *, final, **geo):
    T = h.shape[0]
    n_tiles = (2 * T) // TG + N_EXPERTS
    route, cnt, xs_zero = _router(h, g2, mod3, w_r, layer, n_tiles * TG, **geo)

    counts = cnt[0, :N_EXPERTS].astype(jnp.int32)
    padded = ((counts + TG - 1) // TG) * TG
    ends = jnp.cumsum(padded)
    offs = ends - padded
    def slot_dest(e_lane, r_lane):
        e = route[:, e_lane].astype(jnp.int32)
        off = sum(jnp.where(e == k, offs[k], 0) for k in range(N_EXPERTS))
        return off + route[:, r_lane].astype(jnp.int32)

    dest = jnp.stack([slot_dest(ROUTE_E1, ROUTE_R1), slot_dest(ROUTE_E2, ROUTE_R2)], axis=1).reshape(2 * T)
    n_used = (ends[-1] // TG).reshape(1)
    tiles = jnp.arange(n_tiles)
    te_raw = jnp.sum(tiles[:, None] >= (ends // TG)[None, :], axis=1)
    tile_expert = jnp.minimum(te_raw, N_EXPERTS - 1).astype(jnp.int32)
    sel = tile_expert[:, None] == jnp.arange(N_EXPERTS)[None, :]
    cnt_k = jnp.sum(jnp.where(sel, counts[None, :], 0), axis=1)
    off_k = jnp.sum(jnp.where(sel, offs[None, :], 0), axis=1)
    tile_rows = jnp.where(te_raw < N_EXPERTS, jnp.clip(cnt_k - (tiles * TG - off_k), 0, TG), 0)

    xs = _dispatch(dest, h, g2, mod3, xs_zero, layer, **geo)
    y = _experts(tile_expert, n_used, tile_rows.astype(jnp.int32), xs, wg, wu, wd, moe_layer, n_tiles)
    return _combine(dest, h, mod3, route, final_g, y, final=final, **geo)


def _rope_tables(seq):
    assert seq % GRID_W == 0
    n_rows = seq // GRID_W
    inv = ROPE_THETA ** (-jnp.arange(ROPE_FREQS, dtype=F32) / ROPE_FREQS)
    ang_r = jnp.arange(n_rows, dtype=F32)[:, None] * inv
    ang_c = jnp.arange(GRID_W, dtype=F32)[:, None] * inv
    cos_r, sin_r = (jnp.repeat(f(ang_r), GRID_W, axis=0) for f in (jnp.cos, jnp.sin))
    cos_c, sin_c = (jnp.tile(f(ang_c), (n_rows, 1)) for f in (jnp.cos, jnp.sin))
    cos = jnp.concatenate([cos_r, cos_r, cos_c, cos_c], axis=1)
    sin = jnp.concatenate([-sin_r, sin_r, -sin_c, sin_c], axis=1)
    cos = jnp.concatenate([cos, jnp.ones((TM, HEAD_DIM), F32)], axis=0)
    sin = jnp.concatenate([sin, jnp.zeros((TM, HEAD_DIM), F32)], axis=0)
    return cos, sin


def kernel(x, c, ctx, c_ctx, w_mod, b_mod, norm1_g, norm2_g, w_in, attn_sink, conv_w, conv_b, gate_a_w, gate_a_b, gate_x_w, gate_x_b, lru_lambda, w_o_attn, w_o_lru, w_out, ff_w_gate, ff_w_up, ff_w_down, router_w, exp_w_gate, exp_w_up, exp_w_down, final_g):
    n_batch, seq, _ = x.shape
    ctx_len = ctx.shape[1]
    assert n_batch * ctx_len == TM and seq % TM == 0 and n_batch + 1 <= MOD_ROWS
    n_lat = n_batch * seq
    tiles_per_batch = seq // TM
    geo = dict(tiles_per_batch=tiles_per_batch, n_batch=n_batch)
    shp = dict(n_batch=n_batch, seq=seq, ctx_len=ctx_len)

    cpad = jnp.zeros((MOD_ROWS, D_MODEL), F32).at[:n_batch].set(c).at[n_batch].set(c_ctx)
    mod = _modulation(cpad, w_mod, b_mod)
    cos_t, sin_t = _rope_tables(seq)
    perm = _time_permutation(TM)
    stream = (x.reshape(n_lat, D_MODEL), ctx.reshape(n_batch * ctx_len, D_MODEL), 0)

    g1 = norm1_g.reshape(DEPTH, 1, D_MODEL)
    g2 = norm2_g.reshape(DEPTH, 1, D_MODEL)
    w_in_b = w_in.astype(BF16)
    woa_b, wol_b, wout_b = w_o_attn.astype(BF16), w_o_lru.astype(BF16), w_out.astype(BF16)
    ffg_b, ffu_b, ffd_b = ff_w_gate.astype(BF16), ff_w_up.astype(BF16), ff_w_down.astype(BF16)

    for l in range(DEPTH):
        mod3 = mod[l].reshape(MOD_ROWS, 1, 6 * D_MODEL)
        if l > 0:
            stream = (h, h, n_lat // TM)
        q, kv, xr, gy, sga, sgr = _in_proj(stream, g1, mod3, w_in_b, l, cos_t, sin_t, perm,
                                           n_lat_tiles=n_lat // TM, **geo)
        wcat = jnp.concatenate([gate_a_w[l], gate_x_w[l]], axis=-1).astype(BF16)
        gate_b = jnp.stack([gate_a_b[l], gate_x_b[l]], axis=1)
        hf, hb, attn = _token_mixer(xr, conv_w[l], conv_b[l].reshape(1, D_RNN), wcat, gate_b, lru_lambda[l],
                                    attn_sink[l], q, kv, **shp)
        h = _merge(stream, attn, hf, hb, gy, sga, sgr, mod3, woa_b, wol_b, wout_b, l, **geo)
        i = l // 2
        if l % 2 == 0:
            h = _ffn(h, g2, mod3, ffg_b, ffu_b, ffd_b, l, i, **geo)
        else:
            w_r = jnp.zeros((D_MODEL, LANES), F32).at[:, :N_EXPERTS].set(router_w[i])
            h = _moe_layer(h, g2, mod3, w_r, exp_w_gate, exp_w_up, exp_w_down,
                           final_g.reshape(1, D_MODEL), l, i, final=(l == DEPTH - 1), **geo)

    assert DEPTH % 2 == 0 and h.shape[0] == n_lat
    return h.reshape(n_batch, seq, D_MODEL)
```

```python
import functools

import jax
import jax.numpy as jnp
from jax import lax
from jax.experimental import pallas as pl
from jax.experimental.pallas import tpu as pltpu

F32 = jnp.float32
BF16 = jnp.bfloat16

D_MODEL = 1024
DEPTH = 4
GRID_W = 64
N_HEADS = 8
N_KV_HEADS = 2
HEAD_DIM = 128
N_GROUPS = N_HEADS // N_KV_HEADS
ATTN_BLOCK = 128
ROPE_THETA = 10000.0
ROPE_FREQS = HEAD_DIM // 4
D_RNN = 1024
N_RNN_BLOCKS = 8
RNN_BLOCK = D_RNN // N_RNN_BLOCKS
LRU_C = 8.0
D_FF = 2816
N_EXPERTS = 8
D_FF_EXPERT = 3584
EPS = 1e-6
NEG_INF = -1e30
Q_COLS = N_HEADS * HEAD_DIM
KV_COLS = N_KV_HEADS * HEAD_DIM
IN_COLS = Q_COLS + 2 * KV_COLS + 2 * D_RNN + 2 * D_MODEL
LOG2E = 1.4426950408889634
ATTN_SCALE = HEAD_DIM ** -0.5 * LOG2E

LANES = 128
SUBLANES = 8
MXU_WIDTH = 256
TM = 512
TN_IN = 512
TT = 256
SUBSEQ = TT // SUBLANES
TG = 2048
TG_SUB = 512
MOE_F_CHUNKS = 7
ROW_DMA_UNROLL = 8
MOD_ROWS = 8
VMEM_LIMIT = 56 * 1024 * 1024


def _cparams(n_axes):
    return pltpu.CompilerParams(dimension_semantics=("arbitrary",) * n_axes,
                                vmem_limit_bytes=VMEM_LIMIT)


def _sigmoid(z):
    return 0.5 * jnp.tanh(0.5 * z) + 0.5


def _norm_mod(h, g, shift, scale):
    ms = jnp.mean(h * h, axis=-1, keepdims=True)
    y = h * lax.rsqrt(ms + EPS) * g
    return y * (1.0 + scale) + shift


def _mod_kernel(c_ref, w_ref, b_ref, o_ref):
    cv = c_ref[...]
    s = cv * _sigmoid(cv)
    o_ref[...] = _dot_split(s, w_ref[...]) + b_ref[...]


def _modulation(cpad, w_mod, b_mod):
    nchunk = 6
    return pl.pallas_call(
        _mod_kernel,
        grid=(DEPTH, nchunk),
        in_specs=[
            pl.BlockSpec((MOD_ROWS, D_MODEL), lambda l, n: (0, 0)),
            pl.BlockSpec((None, D_MODEL, D_MODEL), lambda l, n: (l, 0, n)),
            pl.BlockSpec((None, 1, D_MODEL), lambda l, n: (l, 0, n)),
        ],
        out_specs=pl.BlockSpec((None, MOD_ROWS, D_MODEL), lambda l, n: (l, 0, n)),
        out_shape=jax.ShapeDtypeStruct((DEPTH, MOD_ROWS, 6 * D_MODEL), F32),
        compiler_params=_cparams(2),
        name="modulation",
    )(cpad, w_mod, b_mod.reshape(DEPTH, 1, 6 * D_MODEL))


def _rope(xh, cos, sin_signed, first_half):
    sw = jnp.where(first_half, pltpu.roll(xh, HEAD_DIM - ROPE_FREQS, 1), pltpu.roll(xh, ROPE_FREQS, 1))
    return xh * cos + sw * sin_signed


def _stream_tile(h_lat_ref, h_ctx_ref):
    is_ctx_tile = pl.program_id(0) == pl.num_programs(0) - 1
    return jnp.where(is_ctx_tile, h_ctx_ref[...], h_lat_ref[...])


def _stream_specs(stream, n_lat_tiles):
    _, _, ctx_block = stream
    return [pl.BlockSpec((TM, D_MODEL), lambda i: (jnp.minimum(i, n_lat_tiles - 1), 0)),
            pl.BlockSpec((TM, D_MODEL), lambda i: (ctx_block, 0))]


def _in_proj_kernel(h_lat_ref, h_ctx_ref, g_ref, sh_ref, sc_ref, w_ref, cos_ref, sin_ref, perm_ref,
                    q_ref, kv_ref, x_ref, gy_ref, sga_ref, sgr_ref):
    n = _norm_mod(_stream_tile(h_lat_ref, h_ctx_ref), g_ref[...], sh_ref[...], sc_ref[...]).astype(BF16)
    cos = cos_ref[...]
    sin = sin_ref[...]
    lane = lax.broadcasted_iota(jnp.int32, (TM, LANES), 1)
    first_half = (lane & ROPE_FREQS) == 0

    def proj(chunk):
        return jnp.dot(n, w_ref[:, chunk * TN_IN:(chunk + 1) * TN_IN], preferred_element_type=F32)

    def rope_store(acc, n_heads, scale, ref, col0):
        for hh in range(n_heads):
            y = _rope(acc[:, hh * HEAD_DIM:(hh + 1) * HEAD_DIM], cos, sin, first_half)
            if scale != 1.0:
                y = y * scale
            ref[:, col0 + hh * HEAD_DIM:col0 + (hh + 1) * HEAD_DIM] = y.astype(BF16)

    for c in range(2):
        rope_store(proj(c), 4, ATTN_SCALE, q_ref, c * TN_IN)
    acc = proj(2)
    rope_store(acc, 2, 1.0, kv_ref, 0)
    kv_ref[:, KV_COLS:] = acc[:, KV_COLS:].astype(BF16)
    n_perm = jnp.dot(perm_ref[...], n, preferred_element_type=F32).astype(BF16)
    for c in range(2):
        cols = slice(c * TN_IN, (c + 1) * TN_IN)
        x_ref[:, cols] = jnp.dot(n_perm, w_ref[:, (3 + c) * TN_IN:(4 + c) * TN_IN],
                                 preferred_element_type=F32)
        gy_ref[:, cols] = jax.nn.gelu(proj(5 + c)).astype(BF16)
        sga_ref[:, cols] = _sigmoid(proj(7 + c)).astype(BF16)
        sgr_ref[:, cols] = _sigmoid(proj(9 + c)).astype(BF16)


def _in_proj(stream, g, mod3, w_in, layer, cos_t, sin_t, perm, *, n_lat_tiles, tiles_per_batch, n_batch):
    T = (n_lat_tiles + 1) * TM

    def grp(i):
        return jnp.minimum(i // tiles_per_batch, n_batch)

    def pos_tile(i):
        return jnp.where(i < n_lat_tiles, i % tiles_per_batch, tiles_per_batch)

    def tok(width):
        return pl.BlockSpec((TM, width), lambda i: (i, 0))

    out_shape = (
        jax.ShapeDtypeStruct((T, Q_COLS), BF16),
        jax.ShapeDtypeStruct((T, 2 * KV_COLS), BF16),
        jax.ShapeDtypeStruct((T, D_RNN), F32),
        jax.ShapeDtypeStruct((T, D_RNN), BF16),
        jax.ShapeDtypeStruct((T, D_MODEL), BF16),
        jax.ShapeDtypeStruct((T, D_MODEL), BF16),
    )
    return pl.pallas_call(
        _in_proj_kernel,
        grid=(T // TM,),
        in_specs=_stream_specs(stream, n_lat_tiles) + [
            pl.BlockSpec((None, 1, D_MODEL), lambda i: (layer, 0, 0)),
            pl.BlockSpec((None, 1, D_MODEL), lambda i: (grp(i), 0, 0)),
            pl.BlockSpec((None, 1, D_MODEL), lambda i: (grp(i), 0, 1)),
            pl.BlockSpec((None, D_MODEL, IN_COLS), lambda i: (layer, 0, 0)),
            pl.BlockSpec((TM, HEAD_DIM), lambda i: (pos_tile(i), 0)),
            pl.BlockSpec((TM, HEAD_DIM), lambda i: (pos_tile(i), 0)),
            pl.BlockSpec((TM, TM), lambda i: (0, 0)),
        ],
        out_specs=(tok(Q_COLS), tok(2 * KV_COLS), tok(D_RNN), tok(D_RNN), tok(D_MODEL), tok(D_MODEL)),
        out_shape=out_shape,
        compiler_params=_cparams(1),
        name="in_proj",
    )(stream[0], stream[1], g, mod3, mod3, w_in, cos_t, sin_t, perm)


def _stack_heads(qt):
    return jnp.concatenate([qt[:, g * HEAD_DIM:(g + 1) * HEAD_DIM] for g in range(N_GROUPS)], axis=0)


def _sink_col(sink_ref, kvh, rows):
    return jnp.concatenate(
        [jnp.full((rows, 1), sink_ref[kvh * N_GROUPS + g] * LOG2E, F32) for g in range(N_GROUPS)], axis=0)


def _ones_column(n_keys):
    lane = lax.broadcasted_iota(jnp.int32, (n_keys, HEAD_DIM), 1)
    return jnp.where(lane == 0, 1.0, 0.0).astype(BF16)


def _softplus(z):
    return jnp.maximum(z, 0.0) + jnp.log1p(jnp.exp(-jnp.abs(z)))


def _scan8(a, b, row, reverse):
    for s in (1, 2, 4):
        if reverse:
            keep = row < SUBLANES - s
            shift = SUBLANES - s
        else:
            keep = row >= s
            shift = s
        a_sh = jnp.where(keep, pltpu.roll(a, shift, 0), 1.0)
        b_sh = jnp.where(keep, pltpu.roll(b, shift, 0), 0.0)
        b = a * b_sh + b
        a = a * a_sh
    return a, b


def _attention_stages(sink_ref, q_ref, kv_ref, kvc_ref, o_ref, tile, n_blocks, is_ctx):
    rows = ATTN_BLOCK * N_GROUPS
    qi = lax.broadcasted_iota(jnp.int32, (rows, ATTN_BLOCK), 0) & (ATTN_BLOCK - 1)
    kj = lax.broadcasted_iota(jnp.int32, (rows, ATTN_BLOCK), 1)
    tri_prev = kj >= qi
    tri_next = kj <= qi
    q_per_tile = TT // ATTN_BLOCK
    gw = N_GROUPS * HEAD_DIM
    st_cols = 3 * ATTN_BLOCK + kvc_ref.shape[0]
    stages = []
    for kvh in range(N_KV_HEADS):
        for qb in range(q_per_tile):
            kcol = slice(kvh * HEAD_DIM, (kvh + 1) * HEAD_DIM)
            vcol = slice(KV_COLS + kvh * HEAD_DIM, KV_COLS + (kvh + 1) * HEAD_DIM)
            qrows = slice(qb * ATTN_BLOCK, (qb + 1) * ATTN_BLOCK)
            n = tile * q_per_tile + qb
            starts = [pl.multiple_of(jnp.maximum(n - 1, 0) * ATTN_BLOCK, ATTN_BLOCK),
                      pl.multiple_of(n * ATTN_BLOCK, ATTN_BLOCK),
                      pl.multiple_of(jnp.minimum(n + 1, n_blocks - 1) * ATTN_BLOCK, ATTN_BLOCK)]
            st = {}

            def scores(st=st, kvh=kvh, kcol=kcol, qrows=qrows, n=n, starts=starts):
                qs = _stack_heads(q_ref[qrows, kvh * gw:(kvh + 1) * gw])
                kall = jnp.concatenate(
                    [kv_ref[pl.ds(r, ATTN_BLOCK), kcol] for r in starts] + [kvc_ref[:, kcol]], axis=0)
                s = lax.dot_general(qs, kall, (((1,), (1,)), ((), ())), preferred_element_type=F32)
                pen_band = jnp.where(is_ctx, NEG_INF, 0.0)
                pen_prev = jnp.where(n > 0, pen_band, NEG_INF)
                pen_next = jnp.where(n < n_blocks - 1, pen_band, NEG_INF)
                sp = jnp.where(tri_prev, s[:, :ATTN_BLOCK] + pen_prev, NEG_INF)
                sc = s[:, ATTN_BLOCK:2 * ATTN_BLOCK] + pen_band
                sn = jnp.where(tri_next, s[:, 2 * ATTN_BLOCK:3 * ATTN_BLOCK] + pen_next, NEG_INF)
                st["s"] = jnp.concatenate([sp, sc, sn, s[:, 3 * ATTN_BLOCK:]], axis=1)

            def row_max(st=st, kvh=kvh):
                sink = _sink_col(sink_ref, kvh, ATTN_BLOCK)
                st["m"] = jnp.maximum(jnp.max(st["s"], axis=1, keepdims=True), sink)
                st["sink_term"] = jnp.exp2(sink - st["m"])
                st["p"] = []

            def numerators(c, st=st):
                cols = slice(c * ATTN_BLOCK, (c + 1) * ATTN_BLOCK)
                st["p"].append(jnp.exp2(st["s"][:, cols] - st["m"]).astype(BF16))

            def weighted_values(st=st, kvh=kvh, vcol=vcol, qrows=qrows, starts=starts):
                vall = jnp.concatenate(
                    [kv_ref[pl.ds(r, ATTN_BLOCK), vcol] for r in starts] + [kvc_ref[:, vcol]], axis=0)
                v_aug = jnp.concatenate([vall, _ones_column(vall.shape[0])], axis=1)
                p = jnp.concatenate(st.pop("p"), axis=1)
                oa = jnp.dot(p, v_aug, preferred_element_type=F32)
                o = oa[:, :HEAD_DIM] / (oa[:, HEAD_DIM:HEAD_DIM + 1] + st.pop("sink_term"))
                for g in range(N_GROUPS):
                    col0 = kvh * gw + g * HEAD_DIM
                    o_ref[qrows, col0:col0 + HEAD_DIM] = o[g * ATTN_BLOCK:(g + 1) * ATTN_BLOCK].astype(BF16)
                st.clear()

            n_chunks = st_cols // ATTN_BLOCK
            stages += ([scores, row_max] + [functools.partial(numerators, c) for c in range(n_chunks)]
                       + [weighted_values])
    return stages


def _mixer_kernel(xf_ref, xfp_ref, xfn_ref, xb_ref, xbp_ref, xbn_ref, cw_ref, cb_ref, wcat_ref,
                  gb_ref, lam_ref, unperm_ref, sink_ref, q_ref, kv_ref, kvc_ref,
                  hf_ref, hb_ref, attn_ref, a_s, b_s, h_bf, carry, *, n_blocks):
    j = pl.program_id(1)
    last_j = pl.num_programs(1) - 1

    @pl.when(j == 0)
    def _():
        carry[...] = jnp.zeros_like(carry)

    seg_start = (j <= 1, (j == 0) | (j == last_j))
    seg_end = ((j == 0) | (j == last_j), j <= 1)
    mains = (xf_ref, xb_ref)
    prevs = (xfp_ref, xbp_ref)
    nexts = (xfn_ref, xbn_ref)
    sub = lax.broadcasted_iota(jnp.int32, (SUBLANES, D_RNN), 0)
    S8 = SUBLANES

    def conv(d):
        x0 = mains[d][...]
        halo = prevs[d][...]
        t_m1 = jnp.where(seg_start[d], 0.0, halo[2 * S8 - 1:2 * S8, :])
        t_m2 = jnp.where(seg_start[d], 0.0, halo[S8 - 1:S8, :])
        t_p1 = jnp.where(seg_end[d], 0.0, nexts[d][0:1, :])
        g_m1 = jnp.where(sub == 0, t_m1, pltpu.roll(x0[TT - S8:, :], 1, 0))
        g_m2 = jnp.where(sub == 0, t_m2, pltpu.roll(x0[TT - 2 * S8:TT - S8, :], 1, 0))
        g_p1 = jnp.where(sub == S8 - 1, t_p1, pltpu.roll(x0[:S8, :], S8 - 1, 0))
        xc = cb_ref[...] + jnp.concatenate([g_m2, g_m1, x0[:TT - 2 * S8, :]], axis=0) * cw_ref[0:1, :]
        xc = xc + jnp.concatenate([g_m1, x0[:TT - S8, :]], axis=0) * cw_ref[1:2, :]
        xc = xc + x0 * cw_ref[2:3, :]
        return xc + jnp.concatenate([x0[S8:, :], g_p1], axis=0) * cw_ref[3:4, :]

    def gates(d, blk, xc, c_d):
        sl = slice(blk * RNN_BLOCK, (blk + 1) * RNN_BLOCK)
        xcb = xc[:, sl]
        z = jnp.dot(xcb.astype(BF16), wcat_ref[d, blk], preferred_element_type=F32)
        r = _sigmoid(z[:, :RNN_BLOCK] + gb_ref[d, 0:1, sl])
        ig = _sigmoid(z[:, RNN_BLOCK:] + gb_ref[d, 1:2, sl])
        a = jnp.exp2(r * c_d[:, sl])
        v = 1.0 - a * a
        root = jnp.where(v > 0.0, v * lax.rsqrt(v), 0.0)
        a_s[d, :, sl] = a
        b_s[d, :, sl] = root * (ig * xcb)

    attn_stages = _attention_stages(sink_ref, q_ref, kv_ref, kvc_ref, attn_ref, jnp.maximum(j - 1, 0),
                                    n_blocks, j == 0)
    n_gate_blocks = 2 * N_RNN_BLOCKS
    emitted = 0
    for d in range(2):
        xc = conv(d)
        c_d = (-LRU_C * LOG2E) * _softplus(-lam_ref[d:d + 1, :])
        for blk in range(N_RNN_BLOCKS):
            gates(d, blk, xc, c_d)
            done = d * N_RNN_BLOCKS + blk + 1
            while emitted * n_gate_blocks < done * len(attn_stages):
                attn_stages[emitted]()
                emitted += 1
    assert emitted == len(attn_stages)

    def local(g, c):
        hf, pf, hb, pb = c
        rf = pl.multiple_of(g * S8, S8)
        rb = pl.multiple_of((SUBSEQ - 1 - g) * S8, S8)
        af = a_s[0, pl.ds(rf, S8), :]
        ab = a_s[1, pl.ds(rb, S8), :]
        hf = af * hf + b_s[0, pl.ds(rf, S8), :]
        hb = ab * hb + b_s[1, pl.ds(rb, S8), :]
        pf = af * pf
        pb = ab * pb
        b_s[0, pl.ds(rf, S8), :] = hf
        b_s[1, pl.ds(rb, S8), :] = hb
        a_s[0, pl.ds(rf, S8), :] = pf
        a_s[1, pl.ds(rb, S8), :] = pb
        return hf, pf, hb, pb

    zero = jnp.zeros((S8, D_RNN), F32)
    one = jnp.ones((S8, D_RNN), F32)
    hf, pf, hb, pb = lax.fori_loop(0, SUBSEQ, local, (zero, one, zero, one), unroll=4)

    af, bf = _scan8(pf, hf, sub, False)
    endf = af * carry[0] + bf
    h_in = [jnp.where(sub == 0, carry[0], pltpu.roll(endf, 1, 0))]
    carry[0] = jnp.broadcast_to(endf[S8 - 1:S8, :], (S8, D_RNN))
    ab, bb = _scan8(pb, hb, sub, True)
    endb = ab * carry[1] + bb
    h_in.append(jnp.where(sub == S8 - 1, carry[1], pltpu.roll(endb, S8 - 1, 0)))
    carry[1] = jnp.broadcast_to(endb[0:1, :], (S8, D_RNN))

    outs = (hf_ref, hb_ref)
    for d in range(2):
        h_in2 = jnp.concatenate([h_in[d], h_in[d]], axis=0)

        def fix(k, c, d=d, h_in2=h_in2):
            rows = pl.ds(pl.multiple_of(k * 2 * S8, 2 * S8), 2 * S8)
            h_bf[d, rows, :] = (b_s[d, rows, :] + a_s[d, rows, :] * h_in2).astype(BF16)
            return c

        lax.fori_loop(0, SUBSEQ // 2, fix, 0, unroll=2)
        outs[d][...] = jnp.dot(unperm_ref[...], h_bf[d], preferred_element_type=F32).astype(BF16)


def _time_permutation(n_rows):
    p = jnp.arange(n_rows)
    src = (p // TT) * TT + (p % SUBLANES) * SUBSEQ + (p % TT) // SUBLANES
    return (src[:, None] == jnp.arange(n_rows)[None, :]).astype(BF16)


def _token_mixer(x, conv_w, conv_b, wcat, gate_b, lam, sink, q, kv, *, n_batch, seq, ctx_len):
    T = x.shape[0]
    assert ctx_len == TT and seq % TT == 0
    tps = seq // TT
    n_lat_t = n_batch * tps
    halo_rows = 2 * SUBLANES
    per_halo = TT // halo_rows
    per8 = TT // SUBLANES
    last8 = T // SUBLANES - 1
    unperm = _time_permutation(TT).T

    def ftile(b, j):
        return jnp.where(j == 0, n_lat_t + b, b * tps + j - 1)

    def btile(b, j):
        return jnp.where(j == 0, n_lat_t + b, b * tps + tps - j)

    def main(tile):
        return pl.BlockSpec((TT, D_RNN), lambda b, j: (tile(b, j), 0))

    def prev(tile):
        return pl.BlockSpec((halo_rows, D_RNN), lambda b, j: (jnp.maximum(tile(b, j) * per_halo - 1, 0), 0))

    def nxt(tile):
        return pl.BlockSpec((SUBLANES, D_RNN),
                            lambda b, j: (jnp.minimum((tile(b, j) + 1) * per8, last8), 0))

    def const(shape):
        return pl.BlockSpec(shape, lambda b, j: (0,) * len(shape))

    q_spec = pl.BlockSpec((TT, Q_COLS), lambda b, j: (ftile(b, j), 0))
    return pl.pallas_call(
        functools.partial(_mixer_kernel, n_blocks=seq // ATTN_BLOCK),
        grid=(n_batch, tps + 1),
        in_specs=[main(ftile), prev(ftile), nxt(ftile), main(btile), prev(btile), nxt(btile),
                  const(conv_w.shape), const(conv_b.shape), const(wcat.shape), const(gate_b.shape),
                  const(lam.shape), const(unperm.shape),
                  pl.BlockSpec(memory_space=pltpu.SMEM),
                  q_spec,
                  pl.BlockSpec((seq, 2 * KV_COLS), lambda b, j: (b, 0)),
                  pl.BlockSpec((ctx_len, 2 * KV_COLS), lambda b, j: (n_lat_t + b, 0))],
        out_specs=(main(ftile), main(btile), q_spec),
        out_shape=(jax.ShapeDtypeStruct((T, D_RNN), BF16), jax.ShapeDtypeStruct((T, D_RNN), BF16),
                   jax.ShapeDtypeStruct((T, Q_COLS), BF16)),
        scratch_shapes=[pltpu.VMEM((2, TT, D_RNN), F32),
                        pltpu.VMEM((2, TT, D_RNN), F32),
                        pltpu.VMEM((2, TT, D_RNN), BF16),
                        pltpu.VMEM((2, SUBLANES, D_RNN), F32)],
        compiler_params=_cparams(2),
        name="token_mixer",
    )(x, x, x, x, x, x, conv_w, conv_b, wcat, gate_b, lam, unperm, sink, q, kv, kv)


def _merge_kernel(h_lat_ref, h_ctx_ref, attn_ref, hf_ref, hb_ref, gy_ref, sga_ref, sgr_ref,
                  g1_ref, woa_ref, wol_ref, wout_ref, o_ref):
    rec = ((hf_ref[...].astype(F32) + hb_ref[...].astype(F32)) * gy_ref[...].astype(F32)).astype(BF16)
    ta = jnp.dot(attn_ref[...], woa_ref[...], preferred_element_type=F32)
    tl = jnp.dot(rec, wol_ref[...], preferred_element_type=F32)
    m = sga_ref[...].astype(F32) * ta + sgr_ref[...].astype(F32) * tl
    y = jnp.dot(m.astype(BF16), wout_ref[...], preferred_element_type=F32)
    o_ref[...] = _stream_tile(h_lat_ref, h_ctx_ref) + g1_ref[...] * y


def _merge(stream, attn, hf, hb, gy, sga, sgr, mod3, woa, wol, wout, layer, *, tiles_per_batch, n_batch):
    T = attn.shape[0]
    n_lat_tiles = T // TM - 1

    def grp(i):
        return jnp.minimum(i // tiles_per_batch, n_batch)

    tok = pl.BlockSpec((TM, D_MODEL), lambda i: (i, 0))
    wsp = pl.BlockSpec((None, D_MODEL, D_MODEL), lambda i: (layer, 0, 0))
    return pl.pallas_call(
        _merge_kernel,
        grid=(T // TM,),
        in_specs=_stream_specs(stream, n_lat_tiles) + [
                  tok, tok, tok, tok, tok, tok,
                  pl.BlockSpec((None, 1, D_MODEL), lambda i: (grp(i), 0, 2)),
                  wsp, wsp, wsp],
        out_specs=tok,
        out_shape=jax.ShapeDtypeStruct((T, D_MODEL), F32),
        compiler_params=_cparams(1),
        name="merge",
    )(stream[0], stream[1], attn, hf, hb, gy, sga, sgr, mod3, woa, wol, wout)


def _swiglu_partial(n, wg, wu, wd):
    gt = jnp.dot(n, wg, preferred_element_type=F32)
    ut = jnp.dot(n, wu, preferred_element_type=F32)
    act = (gt * _sigmoid(gt) * ut).astype(BF16)
    return jnp.dot(act, wd, preferred_element_type=F32)


FFN_F_CHUNKS = 11


def _ffn_kernel(h_ref, g_ref, sh_ref, sc_ref, g2_ref, wg_ref, wu_ref, wd_ref, o_ref):
    h = h_ref[...]
    n = _norm_mod(h, g_ref[...], sh_ref[...], sc_ref[...]).astype(BF16)
    tf = D_FF // FFN_F_CHUNKS
    acc = None
    for c in range(FFN_F_CHUNKS):
        cols = slice(c * tf, (c + 1) * tf)
        part = _swiglu_partial(n, wg_ref[:, cols], wu_ref[:, cols], wd_ref[cols, :])
        acc = part if acc is None else acc + part
    o_ref[...] = h + g2_ref[...] * acc


def _ffn(h, g, mod3, wg, wu, wd, layer, ff_layer, *, tiles_per_batch, n_batch):
    T = h.shape[0]
    assert (D_FF // FFN_F_CHUNKS) % LANES == 0

    def grp(i):
        return jnp.minimum(i // tiles_per_batch, n_batch)

    def modspec(k):
        return pl.BlockSpec((None, 1, D_MODEL), lambda i: (grp(i), 0, k))

    tok = pl.BlockSpec((TM, D_MODEL), lambda i: (i, 0))
    return pl.pallas_call(
        _ffn_kernel,
        grid=(T // TM,),
        in_specs=[tok, pl.BlockSpec((None, 1, D_MODEL), lambda i: (layer, 0, 0)),
                  modspec(3), modspec(4), modspec(5),
                  pl.BlockSpec((None, D_MODEL, D_FF), lambda i: (ff_layer, 0, 0)),
                  pl.BlockSpec((None, D_MODEL, D_FF), lambda i: (ff_layer, 0, 0)),
                  pl.BlockSpec((None, D_FF, D_MODEL), lambda i: (ff_layer, 0, 0))],
        out_specs=tok,
        out_shape=jax.ShapeDtypeStruct((T, D_MODEL), F32),
        compiler_params=_cparams(1),
        name="dense_ffn",
    )(h, g, mod3, mod3, mod3, wg, wu, wd)


ROUTE_E1, ROUTE_E2, ROUTE_W1, ROUTE_W2, ROUTE_R1, ROUTE_R2 = range(6)


def _dot_split(a, b):
    a_hi = a.astype(BF16)
    a_lo = (a - a_hi.astype(F32)).astype(BF16)
    b_hi = b.astype(BF16)
    b_lo = (b - b_hi.astype(F32)).astype(BF16)

    def mm(x, y):
        return jnp.dot(x, y, preferred_element_type=F32)

    return mm(a_hi, b_hi) + (mm(a_hi, b_lo) + mm(a_lo, b_hi))


def _router_kernel(h_ref, g_ref, sh_ref, sc_ref, wr_ref, route_ref, cnt_ref, zero_ref, run):
    @pl.when(pl.program_id(0) == 0)
    def _():
        run[...] = jnp.zeros_like(run)

    n = _norm_mod(h_ref[...], g_ref[...], sh_ref[...], sc_ref[...])
    logits = _dot_split(n, wr_ref[...])
    lane = lax.broadcasted_iota(jnp.int32, logits.shape, 1)
    logits = jnp.where(lane < N_EXPERTS, logits, -jnp.inf)
    m1 = jnp.max(logits, axis=1, keepdims=True)
    i1 = jnp.min(jnp.where(logits == m1, lane, LANES), axis=1, keepdims=True)
    rest = jnp.where(lane == i1, -jnp.inf, logits)
    m2 = jnp.max(rest, axis=1, keepdims=True)
    i2 = jnp.min(jnp.where(rest == m2, lane, LANES), axis=1, keepdims=True)
    e2 = jnp.exp(m2 - m1)
    w1 = 1.0 / (1.0 + e2)
    w2 = e2 / (1.0 + e2)

    hit1 = lane == i1
    hit2 = lane == i2
    onehot = jnp.where(hit1 | hit2, 1.0, 0.0)
    r_i = lax.broadcasted_iota(jnp.int32, (TM, TM), 0)
    c_i = lax.broadcasted_iota(jnp.int32, (TM, TM), 1)
    lower = jnp.where(c_i < r_i, 1.0, 0.0).astype(BF16)
    prefix = jnp.dot(lower, onehot.astype(BF16), preferred_element_type=F32) + run[0:1, :]
    rank1 = jnp.sum(jnp.where(hit1, prefix, 0.0), axis=1, keepdims=True)
    rank2 = jnp.sum(jnp.where(hit2, prefix, 0.0), axis=1, keepdims=True)
    run[...] = run[...] + jnp.sum(onehot, axis=0, keepdims=True)
    cnt_ref[...] = run[...]

    rec = jnp.zeros(logits.shape, F32)
    for k, v in ((ROUTE_E1, i1.astype(F32)), (ROUTE_E2, i2.astype(F32)), (ROUTE_W1, w1), (ROUTE_W2, w2),
                 (ROUTE_R1, rank1), (ROUTE_R2, rank2)):
        rec = jnp.where(lane == k, v, rec)
    route_ref[...] = rec
    zero_ref[...] = jnp.zeros_like(zero_ref)


def _router(h, g2, mod3, w_r, layer, sorted_rows, *, tiles_per_batch, n_batch):
    T = h.shape[0]
    n_steps = T // TM
    zero_rows = -(-sorted_rows // (n_steps * SUBLANES)) * SUBLANES

    def grp(i):
        return jnp.minimum(i // tiles_per_batch, n_batch)

    def modspec(k):
        return pl.BlockSpec((None, 1, D_MODEL), lambda i: (grp(i), 0, k))

    return pl.pallas_call(
        _router_kernel,
        grid=(T // TM,),
        in_specs=[pl.BlockSpec((TM, D_MODEL), lambda i: (i, 0)),
                  pl.BlockSpec((None, 1, D_MODEL), lambda i: (layer, 0, 0)),
                  modspec(3), modspec(4),
                  pl.BlockSpec((D_MODEL, LANES), lambda i: (0, 0))],
        out_specs=(pl.BlockSpec((TM, LANES), lambda i: (i, 0)),
                   pl.BlockSpec((SUBLANES, LANES), lambda i: (0, 0)),
                   pl.BlockSpec((zero_rows, D_MODEL), lambda i: (i, 0))),
        out_shape=(jax.ShapeDtypeStruct((T, LANES), F32), jax.ShapeDtypeStruct((SUBLANES, LANES), F32),
                   jax.ShapeDtypeStruct((n_steps * zero_rows, D_MODEL), F32)),
        scratch_shapes=[pltpu.VMEM((SUBLANES, LANES), F32)],
        compiler_params=_cparams(1),
        name="moe_router",
    )(h, g2, mod3, mod3, w_r)


def _row_copy(src_ref, src_row, dst_ref, dst_row, sem):
    return pltpu.make_async_copy(src_ref.at[pl.ds(src_row, 1)], dst_ref.at[pl.ds(dst_row, 1)], sem)


def _dispatch_kernel(dest_ref, h_ref, g_ref, sh_ref, sc_ref, xs_in_ref, xs_ref, n_scr, sem):
    del xs_in_ref
    i = pl.program_id(0)
    base = i * (2 * TM)
    cur = i % 2
    n_scr[cur] = _norm_mod(h_ref[...], g_ref[...], sh_ref[...], sc_ref[...])

    def start(r, carry):
        for s in range(2):
            _row_copy(n_scr.at[cur], r, xs_ref, dest_ref[base + 2 * r + s], sem.at[cur]).start(priority=s)
        return carry

    lax.fori_loop(0, TM, start, 0, unroll=ROW_DMA_UNROLL)

    def wait_all(buf):
        for s in range(2):
            pltpu.make_async_copy(n_scr.at[buf], xs_ref.at[pl.ds(0, TM)], sem.at[buf]).wait()

    @pl.when(i > 0)
    def _():
        wait_all(1 - cur)

    @pl.when(i == pl.num_programs(0) - 1)
    def _():
        wait_all(cur)


def _dispatch(dest, h, g2, mod3, xs_zero, layer, *, tiles_per_batch, n_batch):
    T = h.shape[0]

    def grp(i):
        return jnp.minimum(i // tiles_per_batch, n_batch)

    def modspec(k):
        return pl.BlockSpec((None, 1, D_MODEL), lambda i, d: (grp(i), 0, k))

    grid_spec = pltpu.PrefetchScalarGridSpec(
        num_scalar_prefetch=1,
        grid=(T // TM,),
        in_specs=[pl.BlockSpec((TM, D_MODEL), lambda i, d: (i, 0)),
                  pl.BlockSpec((None, 1, D_MODEL), lambda i, d: (layer, 0, 0)),
                  modspec(3), modspec(4),
                  pl.BlockSpec(memory_space=pl.ANY)],
        out_specs=pl.BlockSpec(memory_space=pl.ANY),
        scratch_shapes=[pltpu.VMEM((2, TM, D_MODEL), F32), pltpu.SemaphoreType.DMA((2,))],
    )
    return pl.pallas_call(
        _dispatch_kernel,
        grid_spec=grid_spec,
        out_shape=jax.ShapeDtypeStruct(xs_zero.shape, xs_zero.dtype),
        input_output_aliases={5: 0},
        compiler_params=_cparams(1),
        name="moe_dispatch",
    )(dest, h, g2, mod3, mod3, xs_zero)


def _expert_kernel(te_ref, nu_ref, rows_ref, x_ref, wg_ref, wu_ref, wd_ref, acc, xb):
    del te_ref, nu_ref
    k = pl.program_id(0)
    f = pl.program_id(1)

    @pl.when(f == 0)
    def _():
        xb[...] = x_ref[...].astype(BF16)
        acc[...] = jnp.zeros_like(acc)

    tf = wg_ref.shape[1]
    x0 = xb[:TG_SUB, :]
    part = None
    weights = []
    for c0 in range(0, tf, MXU_WIDTH):
        cols = slice(c0, c0 + MXU_WIDTH)
        w3 = (wg_ref[:, cols].astype(BF16), wu_ref[:, cols].astype(BF16), wd_ref[cols, :].astype(BF16))
        weights.append(w3)
        p = _swiglu_partial(x0, *w3)
        part = p if part is None else part + p
    acc[:TG_SUB, :] += part
    for sub in range(1, TG // TG_SUB):
        rows = slice(sub * TG_SUB, (sub + 1) * TG_SUB)

        @pl.when(rows_ref[k] > sub * TG_SUB)
        def _():
            part = None
            for w3 in weights:
                p = _swiglu_partial(xb[rows, :], *w3)
                part = p if part is None else part + p
            acc[rows, :] += part


def _experts(tile_expert, n_used, tile_rows, xs, wg, wu, wd, layer, n_tiles):
    P = n_tiles * TG
    assert xs.shape[0] >= P
    n_f = MOE_F_CHUNKS
    tf = D_FF_EXPERT // n_f
    assert tf % LANES == 0

    def fsel(k, f, nu):
        return jnp.where(k < nu[0], f, n_f - 1)

    grid_spec = pltpu.PrefetchScalarGridSpec(
        num_scalar_prefetch=3,
        grid=(P // TG, n_f),
        in_specs=[pl.BlockSpec((TG, D_MODEL), lambda k, f, te, nu, tr: (k, 0)),
                  pl.BlockSpec((None, None, D_MODEL, tf), lambda k, f, te, nu, tr: (layer, te[k], 0, fsel(k, f, nu))),
                  pl.BlockSpec((None, None, D_MODEL, tf), lambda k, f, te, nu, tr: (layer, te[k], 0, fsel(k, f, nu))),
                  pl.BlockSpec((None, None, tf, D_MODEL), lambda k, f, te, nu, tr: (layer, te[k], fsel(k, f, nu), 0))],
        out_specs=pl.BlockSpec((TG, D_MODEL), lambda k, f, te, nu, tr: (k, 0)),
        scratch_shapes=[pltpu.VMEM((TG, D_MODEL), BF16)],
    )
    return pl.pallas_call(
        _expert_kernel,
        grid_spec=grid_spec,
        out_shape=jax.ShapeDtypeStruct((P, D_MODEL), F32),
        compiler_params=_cparams(2),
        name="moe_experts",
    )(tile_expert, n_used, tile_rows, xs, wg, wu, wd)


def _combine_kernel(dest_ref, h_ref, g2_ref, route_ref, fg_ref, y_ref, o_ref, ybuf, sem, *, final):
    i = pl.program_id(0)
    n_steps = pl.num_programs(0)

    def gather(tile, buf):
        base = tile * (2 * TM)

        def start(r, carry):
            for s in range(2):
                _row_copy(y_ref, dest_ref[base + 2 * r + s], ybuf.at[buf], s * TM + r,
                          sem.at[buf]).start(priority=s)
            return carry

        lax.fori_loop(0, TM, start, 0, unroll=ROW_DMA_UNROLL)

    @pl.when(i == 0)
    def _():
        gather(0, 0)

    @pl.when(i + 1 < n_steps)
    def _():
        gather(i + 1, (i + 1) % 2)

    cur = i % 2
    for s in range(2):
        pltpu.make_async_copy(y_ref.at[pl.ds(0, TM)], ybuf.at[cur, pl.ds(s * TM, TM)], sem.at[cur]).wait()
    route = route_ref[...]
    f = (route[:, ROUTE_W1:ROUTE_W1 + 1] * ybuf[cur, 0:TM, :]
         + route[:, ROUTE_W2:ROUTE_W2 + 1] * ybuf[cur, TM:, :])
    out = h_ref[...] + g2_ref[...] * f
    if final:
        ms = jnp.mean(out * out, axis=-1, keepdims=True)
        out = out * lax.rsqrt(ms + EPS) * fg_ref[...]
    o_ref[...] = out


def _combine(dest, h, mod3, route, final_g, y, *, final, tiles_per_batch, n_batch):
    n_tok_tiles = h.shape[0] // TM - (1 if final else 0)

    def grp(i):
        return jnp.minimum(i // tiles_per_batch, n_batch)

    grid_spec = pltpu.PrefetchScalarGridSpec(
        num_scalar_prefetch=1,
        grid=(n_tok_tiles,),
        in_specs=[pl.BlockSpec((TM, D_MODEL), lambda i, d: (i, 0)),
                  pl.BlockSpec((None, 1, D_MODEL), lambda i, d: (grp(i), 0, 5)),
                  pl.BlockSpec((TM, LANES), lambda i, d: (i, 0)),
                  pl.BlockSpec((1, D_MODEL), lambda i, d: (0, 0)),
                  pl.BlockSpec(memory_space=pl.ANY)],
        out_specs=pl.BlockSpec((TM, D_MODEL), lambda i, d: (i, 0)),
        scratch_shapes=[pltpu.VMEM((2, 2 * TM, D_MODEL), F32), pltpu.SemaphoreType.DMA((2,))],
    )
    return pl.pallas_call(
        functools.partial(_combine_kernel, final=final),
        grid_spec=grid_spec,
        out_shape=jax.ShapeDtypeStruct((n_tok_tiles * TM, D_MODEL), F32),
        compiler_params=_cparams(1),
        name="moe_combine",
    )(dest, h, mod3, route, final_g, y)


def _moe_layer(h, g2, mod3, w_r, wg, wu, wd, final_g, layer, moe_layer, *, final, **geo):
    T = h.shape[0]
    n_tiles = (2 * T) // TG + N_EXPERTS
    route, cnt, xs_zero = _router(h, g2, mod3, w_r, layer, n_tiles * TG, **geo)

    counts = cnt[0, :N_EXPERTS].astype(jnp.int32)
    padded = ((counts + TG - 1) // TG) * TG
    ends = jnp.cumsum(padded)
    offs = ends - padded
    e12 = route[:, ROUTE_E1:ROUTE_E2 + 1].astype(jnp.int32)
    r12 = route[:, ROUTE_R1:ROUTE_R2 + 1].astype(jnp.int32)
    onehot = e12[:, :, None] == jnp.arange(N_EXPERTS)[None, None, :]
    dest = (jnp.sum(jnp.where(onehot, offs[None, None, :], 0), axis=-1) + r12).reshape(2 * T)
    n_used = (ends[-1] // TG).reshape(1)
    tiles = jnp.arange(n_tiles)
    te_raw = jnp.sum(tiles[:, None] >= (ends // TG)[None, :], axis=1)
    tile_expert = jnp.minimum(te_raw, N_EXPERTS - 1).astype(jnp.int32)
    sel = tile_expert[:, None] == jnp.arange(N_EXPERTS)[None, :]
    cnt_k = jnp.sum(jnp.where(sel, counts[None, :], 0), axis=1)
    off_k = jnp.sum(jnp.where(sel, offs[None, :], 0), axis=1)
    tile_rows = jnp.where(te_raw < N_EXPERTS, jnp.clip(cnt_k - (tiles * TG - off_k), 0, TG), 0)

    xs = _dispatch(dest, h, g2, mod3, xs_zero, layer, **geo)
    y = _experts(tile_expert, n_used, tile_rows.astype(jnp.int32), xs, wg, wu, wd, moe_layer, n_tiles)
    return _combine(dest, h, mod3, route, final_g, y, final=final, **geo)


def _rope_tables(seq):
    assert seq % GRID_W == 0
    n_rows = seq // GRID_W
    inv = ROPE_THETA ** (-jnp.arange(ROPE_FREQS, dtype=F32) / ROPE_FREQS)
    ang_r = jnp.arange(n_rows, dtype=F32)[:, None] * inv
    ang_c = jnp.arange(GRID_W, dtype=F32)[:, None] * inv
    cos_r, sin_r = (jnp.repeat(f(ang_r), GRID_W, axis=0) for f in (jnp.cos, jnp.sin))
    cos_c, sin_c = (jnp.tile(f(ang_c), (n_rows, 1)) for f in (jnp.cos, jnp.sin))
    cos = jnp.concatenate([cos_r, cos_r, cos_c, cos_c], axis=1)
    sin = jnp.concatenate([-sin_r, sin_r, -sin_c, sin_c], axis=1)
    cos = jnp.concatenate([cos, jnp.ones((TM, HEAD_DIM), F32)], axis=0)
    sin = jnp.concatenate([sin, jnp.zeros((TM, HEAD_DIM), F32)], axis=0)
    return cos, sin


def kernel(x, c, ctx, c_ctx, w_mod, b_mod, norm1_g, norm2_g, w_in, attn_sink, conv_w, conv_b, gate_a_w, gate_a_b, gate_x_w, gate_x_b, lru_lambda, w_o_attn, w_o_lru, w_out, ff_w_gate, ff_w_up, ff_w_down, router_w, exp_w_gate, exp_w_up, exp_w_down, final_g):
    n_batch, seq, _ = x.shape
    ctx_len = ctx.shape[1]
    assert n_batch * ctx_len == TM and seq % TM == 0 and n_batch + 1 <= MOD_ROWS
    n_lat = n_batch * seq
    tiles_per_batch = seq // TM
    geo = dict(tiles_per_batch=tiles_per_batch, n_batch=n_batch)
    shp = dict(n_batch=n_batch, seq=seq, ctx_len=ctx_len)

    cpad = jnp.zeros((MOD_ROWS, D_MODEL), F32).at[:n_batch].set(c).at[n_batch].set(c_ctx)
    mod = _modulation(cpad, w_mod, b_mod)
    cos_t, sin_t = _rope_tables(seq)
    perm = _time_permutation(TM)
    stream = (x.reshape(n_lat, D_MODEL), ctx.reshape(n_batch * ctx_len, D_MODEL), 0)

    g1 = norm1_g.reshape(DEPTH, 1, D_MODEL)
    g2 = norm2_g.reshape(DEPTH, 1, D_MODEL)
    w_in_b = w_in.astype(BF16)
    woa_b, wol_b, wout_b = w_o_attn.astype(BF16), w_o_lru.astype(BF16), w_out.astype(BF16)
    ffg_b, ffu_b, ffd_b = ff_w_gate.astype(BF16), ff_w_up.astype(BF16), ff_w_down.astype(BF16)

    for l in range(DEPTH):
        mod3 = mod[l].reshape(MOD_ROWS, 1, 6 * D_MODEL)
        if l > 0:
            stream = (h, h, n_lat // TM)
        q, kv, xr, gy, sga, sgr = _in_proj(stream, g1, mod3, w_in_b, l, cos_t, sin_t, perm,
                                           n_lat_tiles=n_lat // TM, **geo)
        wcat = jnp.concatenate([gate_a_w[l], gate_x_w[l]], axis=-1).astype(BF16)
        gate_b = jnp.stack([gate_a_b[l], gate_x_b[l]], axis=1)
        hf, hb, attn = _token_mixer(xr, conv_w[l], conv_b[l].reshape(1, D_RNN), wcat, gate_b, lru_lambda[l],
                                    attn_sink[l], q, kv, **shp)
        h = _merge(stream, attn, hf, hb, gy, sga, sgr, mod3, woa_b, wol_b, wout_b, l, **geo)
        i = l // 2
        if l % 2 == 0:
            h = _ffn(h, g2, mod3, ffg_b, ffu_b, ffd_b, l, i, **geo)
        else:
            w_r = jnp.zeros((D_MODEL, LANES), F32).at[:, :N_EXPERTS].set(router_w[i])
            h = _moe_layer(h, g2, mod3, w_r, exp_w_gate, exp_w_up, exp_w_down,
                           final_g.reshape(1, D_MODEL), l, i, final=(l == DEPTH - 1), **geo)

    assert DEPTH % 2 == 0 and h.shape[0] == n_lat
    return h.reshape(n_batch, seq, D_MODEL)
```

```python
import functools

import jax
import jax.numpy as jnp
from jax import lax
from jax.experimental import pallas as pl
from jax.experimental.pallas import tpu as pltpu

F32 = jnp.float32
BF16 = jnp.bfloat16

D_MODEL = 1024
DEPTH = 4
GRID_W = 64
N_HEADS = 8
N_KV_HEADS = 2
HEAD_DIM = 128
N_GROUPS = N_HEADS // N_KV_HEADS
ATTN_BLOCK = 128
ROPE_THETA = 10000.0
ROPE_FREQS = HEAD_DIM // 4
D_RNN = 1024
N_RNN_BLOCKS = 8
RNN_BLOCK = D_RNN // N_RNN_BLOCKS
LRU_C = 8.0
D_FF = 2816
N_EXPERTS = 8
D_FF_EXPERT = 3584
EPS = 1e-6
NEG_INF = -1e30
Q_COLS = N_HEADS * HEAD_DIM
KV_COLS = N_KV_HEADS * HEAD_DIM
IN_COLS = Q_COLS + 2 * KV_COLS + 2 * D_RNN + 2 * D_MODEL
LOG2E = 1.4426950408889634
ATTN_SCALE = HEAD_DIM ** -0.5 * LOG2E

LANES = 128
SUBLANES = 8
MXU_WIDTH = 256
TM = 512
TN_IN = 512
TT = 256
SUBSEQ = TT // SUBLANES
TG = 2048
TG_SUB = 512
MOE_F_CHUNKS = 7
ROW_DMA_UNROLL = 8
MOD_ROWS = 8
VMEM_LIMIT = 56 * 1024 * 1024


def _cparams(n_axes):
    return pltpu.CompilerParams(dimension_semantics=("arbitrary",) * n_axes,
                                vmem_limit_bytes=VMEM_LIMIT)


def _sigmoid(z):
    return 0.5 * jnp.tanh(0.5 * z) + 0.5


def _norm_mod(h, g, shift, scale):
    ms = jnp.mean(h * h, axis=-1, keepdims=True)
    y = h * lax.rsqrt(ms + EPS) * g
    return y * (1.0 + scale) + shift


def _mod_kernel(c_ref, w_ref, b_ref, o_ref):
    cv = c_ref[...]
    s = cv * _sigmoid(cv)
    o_ref[...] = _dot_split(s, w_ref[...]) + b_ref[...]


def _modulation(cpad, w_mod, b_mod):
    nchunk = 6
    return pl.pallas_call(
        _mod_kernel,
        grid=(DEPTH, nchunk),
        in_specs=[
            pl.BlockSpec((MOD_ROWS, D_MODEL), lambda l, n: (0, 0)),
            pl.BlockSpec((None, D_MODEL, D_MODEL), lambda l, n: (l, 0, n)),
            pl.BlockSpec((None, 1, D_MODEL), lambda l, n: (l, 0, n)),
        ],
        out_specs=pl.BlockSpec((None, MOD_ROWS, D_MODEL), lambda l, n: (l, 0, n)),
        out_shape=jax.ShapeDtypeStruct((DEPTH, MOD_ROWS, 6 * D_MODEL), F32),
        compiler_params=_cparams(2),
        name="modulation",
    )(cpad, w_mod, b_mod.reshape(DEPTH, 1, 6 * D_MODEL))


def _rope(xh, cos, sin_signed, first_half):
    sw = jnp.where(first_half, pltpu.roll(xh, HEAD_DIM - ROPE_FREQS, 1), pltpu.roll(xh, ROPE_FREQS, 1))
    return xh * cos + sw * sin_signed


def _stream_tile(h_lat_ref, h_ctx_ref):
    is_ctx_tile = pl.program_id(0) == pl.num_programs(0) - 1
    return jnp.where(is_ctx_tile, h_ctx_ref[...], h_lat_ref[...])


def _stream_specs(stream, n_lat_tiles):
    _, _, ctx_block = stream
    return [pl.BlockSpec((TM, D_MODEL), lambda i: (jnp.minimum(i, n_lat_tiles - 1), 0)),
            pl.BlockSpec((TM, D_MODEL), lambda i: (ctx_block, 0))]


def _in_proj_kernel(h_lat_ref, h_ctx_ref, g_ref, sh_ref, sc_ref, w_ref, cos_ref, sin_ref, perm_ref,
                    q_ref, kv_ref, x_ref, gy_ref, sga_ref, sgr_ref):
    n = _norm_mod(_stream_tile(h_lat_ref, h_ctx_ref), g_ref[...], sh_ref[...], sc_ref[...]).astype(BF16)
    cos = cos_ref[...]
    sin = sin_ref[...]
    lane = lax.broadcasted_iota(jnp.int32, (TM, LANES), 1)
    first_half = (lane & ROPE_FREQS) == 0

    def proj(chunk):
        return jnp.dot(n, w_ref[:, chunk * TN_IN:(chunk + 1) * TN_IN], preferred_element_type=F32)

    def rope_store(acc, n_heads, scale, ref, col0):
        for hh in range(n_heads):
            y = _rope(acc[:, hh * HEAD_DIM:(hh + 1) * HEAD_DIM], cos, sin, first_half)
            if scale != 1.0:
                y = y * scale
            ref[:, col0 + hh * HEAD_DIM:col0 + (hh + 1) * HEAD_DIM] = y.astype(BF16)

    for c in range(2):
        rope_store(proj(c), 4, ATTN_SCALE, q_ref, c * TN_IN)
    acc = proj(2)
    rope_store(acc, 2, 1.0, kv_ref, 0)
    kv_ref[:, KV_COLS:] = acc[:, KV_COLS:].astype(BF16)
    n_perm = jnp.dot(perm_ref[...], n, preferred_element_type=F32).astype(BF16)
    for c in range(2):
        cols = slice(c * TN_IN, (c + 1) * TN_IN)
        x_ref[:, cols] = jnp.dot(n_perm, w_ref[:, (3 + c) * TN_IN:(4 + c) * TN_IN],
                                 preferred_element_type=F32)
        gy_ref[:, cols] = jax.nn.gelu(proj(5 + c)).astype(BF16)
        sga_ref[:, cols] = _sigmoid(proj(7 + c)).astype(BF16)
        sgr_ref[:, cols] = _sigmoid(proj(9 + c)).astype(BF16)


def _in_proj(stream, g, mod3, w_in, layer, cos_t, sin_t, perm, *, n_lat_tiles, tiles_per_batch, n_batch):
    T = (n_lat_tiles + 1) * TM

    def grp(i):
        return jnp.minimum(i // tiles_per_batch, n_batch)

    def pos_tile(i):
        return jnp.where(i < n_lat_tiles, i % tiles_per_batch, tiles_per_batch)

    def tok(width):
        return pl.BlockSpec((TM, width), lambda i: (i, 0))

    out_shape = (
        jax.ShapeDtypeStruct((T, Q_COLS), BF16),
        jax.ShapeDtypeStruct((T, 2 * KV_COLS), BF16),
        jax.ShapeDtypeStruct((T, D_RNN), F32),
        jax.ShapeDtypeStruct((T, D_RNN), BF16),
        jax.ShapeDtypeStruct((T, D_MODEL), BF16),
        jax.ShapeDtypeStruct((T, D_MODEL), BF16),
    )
    return pl.pallas_call(
        _in_proj_kernel,
        grid=(T // TM,),
        in_specs=_stream_specs(stream, n_lat_tiles) + [
            pl.BlockSpec((None, 1, D_MODEL), lambda i: (layer, 0, 0)),
            pl.BlockSpec((None, 1, D_MODEL), lambda i: (grp(i), 0, 0)),
            pl.BlockSpec((None, 1, D_MODEL), lambda i: (grp(i), 0, 1)),
            pl.BlockSpec((None, D_MODEL, IN_COLS), lambda i: (layer, 0, 0)),
            pl.BlockSpec((TM, HEAD_DIM), lambda i: (pos_tile(i), 0)),
            pl.BlockSpec((TM, HEAD_DIM), lambda i: (pos_tile(i), 0)),
            pl.BlockSpec((TM, TM), lambda i: (0, 0)),
        ],
        out_specs=(tok(Q_COLS), tok(2 * KV_COLS), tok(D_RNN), tok(D_RNN), tok(D_MODEL), tok(D_MODEL)),
        out_shape=out_shape,
        compiler_params=_cparams(1),
        name="in_proj",
    )(stream[0], stream[1], g, mod3, mod3, w_in, cos_t, sin_t, perm)


def _stack_heads(qt):
    return jnp.concatenate([qt[:, g * HEAD_DIM:(g + 1) * HEAD_DIM] for g in range(N_GROUPS)], axis=0)


def _sink_col(sink_ref, kvh, rows):
    return jnp.concatenate(
        [jnp.full((rows, 1), sink_ref[kvh * N_GROUPS + g] * LOG2E, F32) for g in range(N_GROUPS)], axis=0)


def _ones_column(n_keys):
    lane = lax.broadcasted_iota(jnp.int32, (n_keys, HEAD_DIM), 1)
    return jnp.where(lane == 0, 1.0, 0.0).astype(BF16)


def _softplus(z):
    return jnp.maximum(z, 0.0) + jnp.log1p(jnp.exp(-jnp.abs(z)))


def _scan8(a, b, row, reverse):
    for s in (1, 2, 4):
        if reverse:
            keep = row < SUBLANES - s
            shift = SUBLANES - s
        else:
            keep = row >= s
            shift = s
        a_sh = jnp.where(keep, pltpu.roll(a, shift, 0), 1.0)
        b_sh = jnp.where(keep, pltpu.roll(b, shift, 0), 0.0)
        b = a * b_sh + b
        a = a * a_sh
    return a, b


def _attention_stages(sink_ref, q_ref, kv_ref, kvc_ref, o_ref, tile, n_blocks, is_ctx):
    rows = ATTN_BLOCK * N_GROUPS
    qi = lax.broadcasted_iota(jnp.int32, (rows, ATTN_BLOCK), 0) & (ATTN_BLOCK - 1)
    kj = lax.broadcasted_iota(jnp.int32, (rows, ATTN_BLOCK), 1)
    tri_prev = kj >= qi
    tri_next = kj <= qi
    q_per_tile = TT // ATTN_BLOCK
    gw = N_GROUPS * HEAD_DIM
    st_cols = 3 * ATTN_BLOCK + kvc_ref.shape[0]
    stages = []
    for kvh in range(N_KV_HEADS):
        for qb in range(q_per_tile):
            kcol = slice(kvh * HEAD_DIM, (kvh + 1) * HEAD_DIM)
            vcol = slice(KV_COLS + kvh * HEAD_DIM, KV_COLS + (kvh + 1) * HEAD_DIM)
            qrows = slice(qb * ATTN_BLOCK, (qb + 1) * ATTN_BLOCK)
            n = tile * q_per_tile + qb
            starts = [pl.multiple_of(jnp.maximum(n - 1, 0) * ATTN_BLOCK, ATTN_BLOCK),
                      pl.multiple_of(n * ATTN_BLOCK, ATTN_BLOCK),
                      pl.multiple_of(jnp.minimum(n + 1, n_blocks - 1) * ATTN_BLOCK, ATTN_BLOCK)]
            st = {}

            def scores(st=st, kvh=kvh, kcol=kcol, qrows=qrows, n=n, starts=starts):
                qs = _stack_heads(q_ref[qrows, kvh * gw:(kvh + 1) * gw])
                kall = jnp.concatenate(
                    [kv_ref[pl.ds(r, ATTN_BLOCK), kcol] for r in starts] + [kvc_ref[:, kcol]], axis=0)
                s = lax.dot_general(qs, kall, (((1,), (1,)), ((), ())), preferred_element_type=F32)
                pen_band = jnp.where(is_ctx, NEG_INF, 0.0)
                pen_prev = jnp.where(n > 0, pen_band, NEG_INF)
                pen_next = jnp.where(n < n_blocks - 1, pen_band, NEG_INF)
                sp = jnp.where(tri_prev, s[:, :ATTN_BLOCK] + pen_prev, NEG_INF)
                sc = s[:, ATTN_BLOCK:2 * ATTN_BLOCK] + pen_band
                sn = jnp.where(tri_next, s[:, 2 * ATTN_BLOCK:3 * ATTN_BLOCK] + pen_next, NEG_INF)
                st["s"] = jnp.concatenate([sp, sc, sn, s[:, 3 * ATTN_BLOCK:]], axis=1)

            def row_max(st=st, kvh=kvh):
                sink = _sink_col(sink_ref, kvh, ATTN_BLOCK)
                st["m"] = jnp.maximum(jnp.max(st["s"], axis=1, keepdims=True), sink)
                st["sink_term"] = jnp.exp2(sink - st["m"])
                st["p"] = []

            def numerators(c, st=st):
                cols = slice(c * ATTN_BLOCK, (c + 1) * ATTN_BLOCK)
                st["p"].append(jnp.exp2(st["s"][:, cols] - st["m"]).astype(BF16))

            def weighted_values(st=st, kvh=kvh, vcol=vcol, qrows=qrows, starts=starts):
                vall = jnp.concatenate(
                    [kv_ref[pl.ds(r, ATTN_BLOCK), vcol] for r in starts] + [kvc_ref[:, vcol]], axis=0)
                v_aug = jnp.concatenate([vall, _ones_column(vall.shape[0])], axis=1)
                p = jnp.concatenate(st.pop("p"), axis=1)
                oa = jnp.dot(p, v_aug, preferred_element_type=F32)
                o = oa[:, :HEAD_DIM] / (oa[:, HEAD_DIM:HEAD_DIM + 1] + st.pop("sink_term"))
                for g in range(N_GROUPS):
                    col0 = kvh * gw + g * HEAD_DIM
                    o_ref[qrows, col0:col0 + HEAD_DIM] = o[g * ATTN_BLOCK:(g + 1) * ATTN_BLOCK].astype(BF16)
                st.clear()

            n_chunks = st_cols // ATTN_BLOCK
            stages += ([scores, row_max] + [functools.partial(numerators, c) for c in range(n_chunks)]
                       + [weighted_values])
    return stages


def _mixer_kernel(xf_ref, xfp_ref, xfn_ref, xb_ref, xbp_ref, xbn_ref, cw_ref, cb_ref, wcat_ref,
                  gb_ref, lam_ref, unperm_ref, sink_ref, q_ref, kv_ref, kvc_ref,
                  hf_ref, hb_ref, attn_ref, a_s, b_s, h_bf, carry, *, n_blocks):
    j = pl.program_id(1)
    last_j = pl.num_programs(1) - 1

    @pl.when(j == 0)
    def _():
        carry[...] = jnp.zeros_like(carry)

    seg_start = (j <= 1, (j == 0) | (j == last_j))
    seg_end = ((j == 0) | (j == last_j), j <= 1)
    mains = (xf_ref, xb_ref)
    prevs = (xfp_ref, xbp_ref)
    nexts = (xfn_ref, xbn_ref)
    sub = lax.broadcasted_iota(jnp.int32, (SUBLANES, D_RNN), 0)
    S8 = SUBLANES

    def conv(d):
        x0 = mains[d][...]
        halo = prevs[d][...]
        t_m1 = jnp.where(seg_start[d], 0.0, halo[2 * S8 - 1:2 * S8, :])
        t_m2 = jnp.where(seg_start[d], 0.0, halo[S8 - 1:S8, :])
        t_p1 = jnp.where(seg_end[d], 0.0, nexts[d][0:1, :])
        g_m1 = jnp.where(sub == 0, t_m1, pltpu.roll(x0[TT - S8:, :], 1, 0))
        g_m2 = jnp.where(sub == 0, t_m2, pltpu.roll(x0[TT - 2 * S8:TT - S8, :], 1, 0))
        g_p1 = jnp.where(sub == S8 - 1, t_p1, pltpu.roll(x0[:S8, :], S8 - 1, 0))
        xc = cb_ref[...] + jnp.concatenate([g_m2, g_m1, x0[:TT - 2 * S8, :]], axis=0) * cw_ref[0:1, :]
        xc = xc + jnp.concatenate([g_m1, x0[:TT - S8, :]], axis=0) * cw_ref[1:2, :]
        xc = xc + x0 * cw_ref[2:3, :]
        return xc + jnp.concatenate([x0[S8:, :], g_p1], axis=0) * cw_ref[3:4, :]

    def gates(d, blk, xc, c_d):
        sl = slice(blk * RNN_BLOCK, (blk + 1) * RNN_BLOCK)
        xcb = xc[:, sl]
        z = jnp.dot(xcb.astype(BF16), wcat_ref[d, blk], preferred_element_type=F32)
        r = _sigmoid(z[:, :RNN_BLOCK] + gb_ref[d, 0:1, sl])
        ig = _sigmoid(z[:, RNN_BLOCK:] + gb_ref[d, 1:2, sl])
        a = jnp.exp2(r * c_d[:, sl])
        v = 1.0 - a * a
        root = jnp.where(v > 0.0, v * lax.rsqrt(v), 0.0)
        a_s[d, :, sl] = a
        b_s[d, :, sl] = root * (ig * xcb)

    outs = (hf_ref, hb_ref)

    def gate_stages(d):
        st = {}

        def block(blk):
            if blk == 0:
                st["xc"] = conv(d)
                st["c"] = (-LRU_C * LOG2E) * _softplus(-lam_ref[d:d + 1, :])
            gates(d, blk, st["xc"], st["c"])

        return [functools.partial(block, blk) for blk in range(N_RNN_BLOCKS)]

    def scan_stages(d):
        st = {"h": jnp.zeros((S8, D_RNN), F32), "p": jnp.ones((S8, D_RNN), F32)}

        def step(g):
            r = (g if d == 0 else SUBSEQ - 1 - g) * S8
            a = a_s[d, r:r + S8, :]
            st["h"] = a * st["h"] + b_s[d, r:r + S8, :]
            st["p"] = a * st["p"]
            b_s[d, r:r + S8, :] = st["h"]
            a_s[d, r:r + S8, :] = st["p"]

        def chain():
            a8, b8 = _scan8(st.pop("p"), st.pop("h"), sub, d == 1)
            end = a8 * carry[d] + b8
            if d == 0:
                h_in = jnp.where(sub == 0, carry[0], pltpu.roll(end, 1, 0))
                carry[0] = jnp.broadcast_to(end[S8 - 1:S8, :], (S8, D_RNN))
            else:
                h_in = jnp.where(sub == S8 - 1, carry[1], pltpu.roll(end, S8 - 1, 0))
                carry[1] = jnp.broadcast_to(end[0:1, :], (S8, D_RNN))
            st["h_in2"] = jnp.concatenate([h_in, h_in], axis=0)

        def fix(k):
            rows = slice(k * 2 * S8, (k + 1) * 2 * S8)
            h_bf[d, rows, :] = (b_s[d, rows, :] + a_s[d, rows, :] * st["h_in2"]).astype(BF16)

        def store():
            outs[d][...] = jnp.dot(unperm_ref[...], h_bf[d], preferred_element_type=F32).astype(BF16)

        return ([functools.partial(step, g) for g in range(SUBSEQ)] + [chain]
                + [functools.partial(fix, k) for k in range(SUBSEQ // 2)] + [store])

    def emit(primary, *others):
        sent = [0] * len(others)
        for i, stage in enumerate(primary, 1):
            stage()
            for o, stages in enumerate(others):
                while sent[o] * len(primary) < i * len(stages):
                    stages[sent[o]]()
                    sent[o] += 1

    attn_stages = _attention_stages(sink_ref, q_ref, kv_ref, kvc_ref, attn_ref, jnp.maximum(j - 1, 0),
                                    n_blocks, j == 0)
    half = len(attn_stages) // 2
    emit(gate_stages(0), attn_stages[:half])
    emit(gate_stages(1), attn_stages[half:], scan_stages(0))
    emit(scan_stages(1))


def _time_permutation(n_rows):
    p = jnp.arange(n_rows)
    src = (p // TT) * TT + (p % SUBLANES) * SUBSEQ + (p % TT) // SUBLANES
    return (src[:, None] == jnp.arange(n_rows)[None, :]).astype(BF16)


def _token_mixer(x, conv_w, conv_b, wcat, gate_b, lam, sink, q, kv, *, n_batch, seq, ctx_len):
    T = x.shape[0]
    assert ctx_len == TT and seq % TT == 0
    tps = seq // TT
    n_lat_t = n_batch * tps
    halo_rows = 2 * SUBLANES
    per_halo = TT // halo_rows
    per8 = TT // SUBLANES
    last8 = T // SUBLANES - 1
    unperm = _time_permutation(TT).T

    def ftile(b, j):
        return jnp.where(j == 0, n_lat_t + b, b * tps + j - 1)

    def btile(b, j):
        return jnp.where(j == 0, n_lat_t + b, b * tps + tps - j)

    def main(tile):
        return pl.BlockSpec((TT, D_RNN), lambda b, j: (tile(b, j), 0))

    def prev(tile):
        return pl.BlockSpec((halo_rows, D_RNN), lambda b, j: (jnp.maximum(tile(b, j) * per_halo - 1, 0), 0))

    def nxt(tile):
        return pl.BlockSpec((SUBLANES, D_RNN),
                            lambda b, j: (jnp.minimum((tile(b, j) + 1) * per8, last8), 0))

    def const(shape):
        return pl.BlockSpec(shape, lambda b, j: (0,) * len(shape))

    q_spec = pl.BlockSpec((TT, Q_COLS), lambda b, j: (ftile(b, j), 0))
    return pl.pallas_call(
        functools.partial(_mixer_kernel, n_blocks=seq // ATTN_BLOCK),
        grid=(n_batch, tps + 1),
        in_specs=[main(ftile), prev(ftile), nxt(ftile), main(btile), prev(btile), nxt(btile),
                  const(conv_w.shape), const(conv_b.shape), const(wcat.shape), const(gate_b.shape),
                  const(lam.shape), const(unperm.shape),
                  pl.BlockSpec(memory_space=pltpu.SMEM),
                  q_spec,
                  pl.BlockSpec((seq, 2 * KV_COLS), lambda b, j: (b, 0)),
                  pl.BlockSpec((ctx_len, 2 * KV_COLS), lambda b, j: (n_lat_t + b, 0))],
        out_specs=(main(ftile), main(btile), q_spec),
        out_shape=(jax.ShapeDtypeStruct((T, D_RNN), BF16), jax.ShapeDtypeStruct((T, D_RNN), BF16),
                   jax.ShapeDtypeStruct((T, Q_COLS), BF16)),
        scratch_shapes=[pltpu.VMEM((2, TT, D_RNN), F32),
                        pltpu.VMEM((2, TT, D_RNN), F32),
                        pltpu.VMEM((2, TT, D_RNN), BF16),
                        pltpu.VMEM((2, SUBLANES, D_RNN), F32)],
        compiler_params=_cparams(2),
        name="token_mixer",
    )(x, x, x, x, x, x, conv_w, conv_b, wcat, gate_b, lam, unperm, sink, q, kv, kv)


def _merge_kernel(h_lat_ref, h_ctx_ref, attn_ref, hf_ref, hb_ref, gy_ref, sga_ref, sgr_ref,
                  g1_ref, woa_ref, wol_ref, wout_ref, o_ref):
    rec = ((hf_ref[...].astype(F32) + hb_ref[...].astype(F32)) * gy_ref[...].astype(F32)).astype(BF16)
    ta = jnp.dot(attn_ref[...], woa_ref[...], preferred_element_type=F32)
    tl = jnp.dot(rec, wol_ref[...], preferred_element_type=F32)
    m = sga_ref[...].astype(F32) * ta + sgr_ref[...].astype(F32) * tl
    y = jnp.dot(m.astype(BF16), wout_ref[...], preferred_element_type=F32)
    o_ref[...] = _stream_tile(h_lat_ref, h_ctx_ref) + g1_ref[...] * y


def _merge(stream, attn, hf, hb, gy, sga, sgr, mod3, woa, wol, wout, layer, *, tiles_per_batch, n_batch):
    T = attn.shape[0]
    n_lat_tiles = T // TM - 1

    def grp(i):
        return jnp.minimum(i // tiles_per_batch, n_batch)

    tok = pl.BlockSpec((TM, D_MODEL), lambda i: (i, 0))
    wsp = pl.BlockSpec((None, D_MODEL, D_MODEL), lambda i: (layer, 0, 0))
    return pl.pallas_call(
        _merge_kernel,
        grid=(T // TM,),
        in_specs=_stream_specs(stream, n_lat_tiles) + [
                  tok, tok, tok, tok, tok, tok,
                  pl.BlockSpec((None, 1, D_MODEL), lambda i: (grp(i), 0, 2)),
                  wsp, wsp, wsp],
        out_specs=tok,
        out_shape=jax.ShapeDtypeStruct((T, D_MODEL), F32),
        compiler_params=_cparams(1),
        name="merge",
    )(stream[0], stream[1], attn, hf, hb, gy, sga, sgr, mod3, woa, wol, wout)


def _swiglu_partial(n, wg, wu, wd):
    gt = jnp.dot(n, wg, preferred_element_type=F32)
    ut = jnp.dot(n, wu, preferred_element_type=F32)
    act = (gt * _sigmoid(gt) * ut).astype(BF16)
    return jnp.dot(act, wd, preferred_element_type=F32)


FFN_F_CHUNKS = 11


def _ffn_kernel(h_ref, g_ref, sh_ref, sc_ref, g2_ref, wg_ref, wu_ref, wd_ref, o_ref):
    h = h_ref[...]
    n = _norm_mod(h, g_ref[...], sh_ref[...], sc_ref[...]).astype(BF16)
    tf = D_FF // FFN_F_CHUNKS
    acc = None
    for c in range(FFN_F_CHUNKS):
        cols = slice(c * tf, (c + 1) * tf)
        part = _swiglu_partial(n, wg_ref[:, cols], wu_ref[:, cols], wd_ref[cols, :])
        acc = part if acc is None else acc + part
    o_ref[...] = h + g2_ref[...] * acc


def _ffn(h, g, mod3, wg, wu, wd, layer, ff_layer, *, tiles_per_batch, n_batch):
    T = h.shape[0]
    assert (D_FF // FFN_F_CHUNKS) % LANES == 0

    def grp(i):
        return jnp.minimum(i // tiles_per_batch, n_batch)

    def modspec(k):
        return pl.BlockSpec((None, 1, D_MODEL), lambda i: (grp(i), 0, k))

    tok = pl.BlockSpec((TM, D_MODEL), lambda i: (i, 0))
    return pl.pallas_call(
        _ffn_kernel,
        grid=(T // TM,),
        in_specs=[tok, pl.BlockSpec((None, 1, D_MODEL), lambda i: (layer, 0, 0)),
                  modspec(3), modspec(4), modspec(5),
                  pl.BlockSpec((None, D_MODEL, D_FF), lambda i: (ff_layer, 0, 0)),
                  pl.BlockSpec((None, D_MODEL, D_FF), lambda i: (ff_layer, 0, 0)),
                  pl.BlockSpec((None, D_FF, D_MODEL), lambda i: (ff_layer, 0, 0))],
        out_specs=tok,
        out_shape=jax.ShapeDtypeStruct((T, D_MODEL), F32),
        compiler_params=_cparams(1),
        name="dense_ffn",
    )(h, g, mod3, mod3, mod3, wg, wu, wd)


ROUTE_E1, ROUTE_E2, ROUTE_W1, ROUTE_W2, ROUTE_R1, ROUTE_R2 = range(6)


def _dot_split(a, b):
    a_hi = a.astype(BF16)
    a_lo = (a - a_hi.astype(F32)).astype(BF16)
    b_hi = b.astype(BF16)
    b_lo = (b - b_hi.astype(F32)).astype(BF16)

    def mm(x, y):
        return jnp.dot(x, y, preferred_element_type=F32)

    return mm(a_hi, b_hi) + (mm(a_hi, b_lo) + mm(a_lo, b_hi))


def _router_kernel(h_ref, g_ref, sh_ref, sc_ref, wr_ref, route_ref, cnt_ref, zero_ref, run):
    @pl.when(pl.program_id(0) == 0)
    def _():
        run[...] = jnp.zeros_like(run)

    n = _norm_mod(h_ref[...], g_ref[...], sh_ref[...], sc_ref[...])
    logits = _dot_split(n, wr_ref[...])
    lane = lax.broadcasted_iota(jnp.int32, logits.shape, 1)
    logits = jnp.where(lane < N_EXPERTS, logits, -jnp.inf)
    m1 = jnp.max(logits, axis=1, keepdims=True)
    i1 = jnp.min(jnp.where(logits == m1, lane, LANES), axis=1, keepdims=True)
    rest = jnp.where(lane == i1, -jnp.inf, logits)
    m2 = jnp.max(rest, axis=1, keepdims=True)
    i2 = jnp.min(jnp.where(rest == m2, lane, LANES), axis=1, keepdims=True)
    e2 = jnp.exp(m2 - m1)
    w1 = 1.0 / (1.0 + e2)
    w2 = e2 / (1.0 + e2)

    hit1 = lane == i1
    hit2 = lane == i2
    onehot = jnp.where(hit1 | hit2, 1.0, 0.0)
    r_i = lax.broadcasted_iota(jnp.int32, (TM, TM), 0)
    c_i = lax.broadcasted_iota(jnp.int32, (TM, TM), 1)
    lower = jnp.where(c_i < r_i, 1.0, 0.0).astype(BF16)
    prefix = jnp.dot(lower, onehot.astype(BF16), preferred_element_type=F32) + run[0:1, :]
    rank1 = jnp.sum(jnp.where(hit1, prefix, 0.0), axis=1, keepdims=True)
    rank2 = jnp.sum(jnp.where(hit2, prefix, 0.0), axis=1, keepdims=True)
    run[...] = run[...] + jnp.sum(onehot, axis=0, keepdims=True)
    cnt_ref[...] = run[...]

    rec = jnp.zeros(logits.shape, F32)
    for k, v in ((ROUTE_E1, i1.astype(F32)), (ROUTE_E2, i2.astype(F32)), (ROUTE_W1, w1), (ROUTE_W2, w2),
                 (ROUTE_R1, rank1), (ROUTE_R2, rank2)):
        rec = jnp.where(lane == k, v, rec)
    route_ref[...] = rec
    zero_ref[...] = jnp.zeros_like(zero_ref)


def _router(h, g2, mod3, w_r, layer, sorted_rows, *, tiles_per_batch, n_batch):
    T = h.shape[0]
    n_steps = T // TM
    zero_rows = -(-sorted_rows // (n_steps * SUBLANES)) * SUBLANES

    def grp(i):
        return jnp.minimum(i // tiles_per_batch, n_batch)

    def modspec(k):
        return pl.BlockSpec((None, 1, D_MODEL), lambda i: (grp(i), 0, k))

    return pl.pallas_call(
        _router_kernel,
        grid=(T // TM,),
        in_specs=[pl.BlockSpec((TM, D_MODEL), lambda i: (i, 0)),
                  pl.BlockSpec((None, 1, D_MODEL), lambda i: (layer, 0, 0)),
                  modspec(3), modspec(4),
                  pl.BlockSpec((D_MODEL, LANES), lambda i: (0, 0))],
        out_specs=(pl.BlockSpec((TM, LANES), lambda i: (i, 0)),
                   pl.BlockSpec((SUBLANES, LANES), lambda i: (0, 0)),
                   pl.BlockSpec((zero_rows, D_MODEL), lambda i: (i, 0))),
        out_shape=(jax.ShapeDtypeStruct((T, LANES), F32), jax.ShapeDtypeStruct((SUBLANES, LANES), F32),
                   jax.ShapeDtypeStruct((n_steps * zero_rows, D_MODEL), F32)),
        scratch_shapes=[pltpu.VMEM((SUBLANES, LANES), F32)],
        compiler_params=_cparams(1),
        name="moe_router",
    )(h, g2, mod3, mod3, w_r)


def _row_copy(src_ref, src_row, dst_ref, dst_row, sem):
    return pltpu.make_async_copy(src_ref.at[pl.ds(src_row, 1)], dst_ref.at[pl.ds(dst_row, 1)], sem)


def _dispatch_kernel(dest_ref, h_ref, g_ref, sh_ref, sc_ref, xs_in_ref, xs_ref, n_scr, sem):
    del xs_in_ref
    i = pl.program_id(0)
    base = i * (2 * TM)
    cur = i % 2
    n_scr[cur] = _norm_mod(h_ref[...], g_ref[...], sh_ref[...], sc_ref[...])

    def start(r, carry):
        for s in range(2):
            _row_copy(n_scr.at[cur], r, xs_ref, dest_ref[base + 2 * r + s], sem.at[cur]).start(priority=s)
        return carry

    lax.fori_loop(0, TM, start, 0, unroll=ROW_DMA_UNROLL)

    def wait_all(buf):
        for s in range(2):
            pltpu.make_async_copy(n_scr.at[buf], xs_ref.at[pl.ds(0, TM)], sem.at[buf]).wait()

    @pl.when(i > 0)
    def _():
        wait_all(1 - cur)

    @pl.when(i == pl.num_programs(0) - 1)
    def _():
        wait_all(cur)


def _dispatch(dest, h, g2, mod3, xs_zero, layer, *, tiles_per_batch, n_batch):
    T = h.shape[0]

    def grp(i):
        return jnp.minimum(i // tiles_per_batch, n_batch)

    def modspec(k):
        return pl.BlockSpec((None, 1, D_MODEL), lambda i, d: (grp(i), 0, k))

    grid_spec = pltpu.PrefetchScalarGridSpec(
        num_scalar_prefetch=1,
        grid=(T // TM,),
        in_specs=[pl.BlockSpec((TM, D_MODEL), lambda i, d: (i, 0)),
                  pl.BlockSpec((None, 1, D_MODEL), lambda i, d: (layer, 0, 0)),
                  modspec(3), modspec(4),
                  pl.BlockSpec(memory_space=pl.ANY)],
        out_specs=pl.BlockSpec(memory_space=pl.ANY),
        scratch_shapes=[pltpu.VMEM((2, TM, D_MODEL), F32), pltpu.SemaphoreType.DMA((2,))],
    )
    return pl.pallas_call(
        _dispatch_kernel,
        grid_spec=grid_spec,
        out_shape=jax.ShapeDtypeStruct(xs_zero.shape, xs_zero.dtype),
        input_output_aliases={5: 0},
        compiler_params=_cparams(1),
        name="moe_dispatch",
    )(dest, h, g2, mod3, mod3, xs_zero)


def _expert_kernel(te_ref, nu_ref, rows_ref, x_ref, wg_ref, wu_ref, wd_ref, acc, xb):
    del te_ref, nu_ref
    k = pl.program_id(0)
    f = pl.program_id(1)

    @pl.when(f == 0)
    def _():
        xb[...] = x_ref[...].astype(BF16)
        acc[...] = jnp.zeros_like(acc)

    tf = wg_ref.shape[1]
    x0 = xb[:TG_SUB, :]
    part = None
    weights = []
    for c0 in range(0, tf, MXU_WIDTH):
        cols = slice(c0, c0 + MXU_WIDTH)
        w3 = (wg_ref[:, cols].astype(BF16), wu_ref[:, cols].astype(BF16), wd_ref[cols, :].astype(BF16))
        weights.append(w3)
        p = _swiglu_partial(x0, *w3)
        part = p if part is None else part + p
    acc[:TG_SUB, :] += part
    for sub in range(1, TG // TG_SUB):
        rows = slice(sub * TG_SUB, (sub + 1) * TG_SUB)

        @pl.when(rows_ref[k] > sub * TG_SUB)
        def _():
            part = None
            for w3 in weights:
                p = _swiglu_partial(xb[rows, :], *w3)
                part = p if part is None else part + p
            acc[rows, :] += part


def _experts(tile_expert, n_used, tile_rows, xs, wg, wu, wd, layer, n_tiles):
    P = n_tiles * TG
    assert xs.shape[0] >= P
    n_f = MOE_F_CHUNKS
    tf = D_FF_EXPERT // n_f
    assert tf % LANES == 0

    def fsel(k, f, nu):
        return jnp.where(k < nu[0], f, n_f - 1)

    grid_spec = pltpu.PrefetchScalarGridSpec(
        num_scalar_prefetch=3,
        grid=(P // TG, n_f),
        in_specs=[pl.BlockSpec((TG, D_MODEL), lambda k, f, te, nu, tr: (k, 0)),
                  pl.BlockSpec((None, None, D_MODEL, tf), lambda k, f, te, nu, tr: (layer, te[k], 0, fsel(k, f, nu))),
                  pl.BlockSpec((None, None, D_MODEL, tf), lambda k, f, te, nu, tr: (layer, te[k], 0, fsel(k, f, nu))),
                  pl.BlockSpec((None, None, tf, D_MODEL), lambda k, f, te, nu, tr: (layer, te[k], fsel(k, f, nu), 0))],
        out_specs=pl.BlockSpec((TG, D_MODEL), lambda k, f, te, nu, tr: (k, 0)),
        scratch_shapes=[pltpu.VMEM((TG, D_MODEL), BF16)],
    )
    return pl.pallas_call(
        _expert_kernel,
        grid_spec=grid_spec,
        out_shape=jax.ShapeDtypeStruct((P, D_MODEL), F32),
        compiler_params=_cparams(2),
        name="moe_experts",
    )(tile_expert, n_used, tile_rows, xs, wg, wu, wd)


def _combine_kernel(dest_ref, h_ref, g2_ref, route_ref, fg_ref, y_ref, o_ref, ybuf, sem, *, final):
    i = pl.program_id(0)
    n_steps = pl.num_programs(0)

    def gather(tile, buf):
        base = tile * (2 * TM)

        def start(r, carry):
            for s in range(2):
                _row_copy(y_ref, dest_ref[base + 2 * r + s], ybuf.at[buf], s * TM + r,
                          sem.at[buf]).start(priority=s)
            return carry

        lax.fori_loop(0, TM, start, 0, unroll=ROW_DMA_UNROLL)

    @pl.when(i == 0)
    def _():
        gather(0, 0)

    @pl.when(i + 1 < n_steps)
    def _():
        gather(i + 1, (i + 1) % 2)

    cur = i % 2
    for s in range(2):
        pltpu.make_async_copy(y_ref.at[pl.ds(0, TM)], ybuf.at[cur, pl.ds(s * TM, TM)], sem.at[cur]).wait()
    route = route_ref[...]
    f = (route[:, ROUTE_W1:ROUTE_W1 + 1] * ybuf[cur, 0:TM, :]
         + route[:, ROUTE_W2:ROUTE_W2 + 1] * ybuf[cur, TM:, :])
    out = h_ref[...] + g2_ref[...] * f
    if final:
        ms = jnp.mean(out * out, axis=-1, keepdims=True)
        out = out * lax.rsqrt(ms + EPS) * fg_ref[...]
    o_ref[...] = out


def _combine(dest, h, mod3, route, final_g, y, *, final, tiles_per_batch, n_batch):
    n_tok_tiles = h.shape[0] // TM - (1 if final else 0)

    def grp(i):
        return jnp.minimum(i // tiles_per_batch, n_batch)

    grid_spec = pltpu.PrefetchScalarGridSpec(
        num_scalar_prefetch=1,
        grid=(n_tok_tiles,),
        in_specs=[pl.BlockSpec((TM, D_MODEL), lambda i, d: (i, 0)),
                  pl.BlockSpec((None, 1, D_MODEL), lambda i, d: (grp(i), 0, 5)),
                  pl.BlockSpec((TM, LANES), lambda i, d: (i, 0)),
                  pl.BlockSpec((1, D_MODEL), lambda i, d: (0, 0)),
                  pl.BlockSpec(memory_space=pl.ANY)],
        out_specs=pl.BlockSpec((TM, D_MODEL), lambda i, d: (i, 0)),
        scratch_shapes=[pltpu.VMEM((2, 2 * TM, D_MODEL), F32), pltpu.SemaphoreType.DMA((2,))],
    )
    return pl.pallas_call(
        functools.partial(_combine_kernel, final=final),
        grid_spec=grid_spec,
        out_shape=jax.ShapeDtypeStruct((n_tok_tiles * TM, D_MODEL), F32),
        compiler_params=_cparams(1),
        name="moe_combine",
    )(dest, h, mod3, route, final_g, y)


def _moe_layer(h, g2, mod3, w_r, wg, wu, wd, final_g, layer, moe_layer, *, final, **geo):
    T = h.shape[0]
    n_tiles = (2 * T) // TG + N_EXPERTS
    route, cnt, xs_zero = _router(h, g2, mod3, w_r, layer, n_tiles * TG, **geo)

    counts = cnt[0, :N_EXPERTS].astype(jnp.int32)
    padded = ((counts + TG - 1) // TG) * TG
    ends = jnp.cumsum(padded)
    offs = ends - padded
    e12 = route[:, ROUTE_E1:ROUTE_E2 + 1].astype(jnp.int32)
    r12 = route[:, ROUTE_R1:ROUTE_R2 + 1].astype(jnp.int32)
    onehot = e12[:, :, None] == jnp.arange(N_EXPERTS)[None, None, :]
    dest = (jnp.sum(jnp.where(onehot, offs[None, None, :], 0), axis=-1) + r12).reshape(2 * T)
    n_used = (ends[-1] // TG).reshape(1)
    tiles = jnp.arange(n_tiles)
    te_raw = jnp.sum(tiles[:, None] >= (ends // TG)[None, :], axis=1)
    tile_expert = jnp.minimum(te_raw, N_EXPERTS - 1).astype(jnp.int32)
    sel = tile_expert[:, None] == jnp.arange(N_EXPERTS)[None, :]
    cnt_k = jnp.sum(jnp.where(sel, counts[None, :], 0), axis=1)
    off_k = jnp.sum(jnp.where(sel, offs[None, :], 0), axis=1)
    tile_rows = jnp.where(te_raw < N_EXPERTS, jnp.clip(cnt_k - (tiles * TG - off_k), 0, TG), 0)

    xs = _dispatch(dest, h, g2, mod3, xs_zero, layer, **geo)
    y = _experts(tile_expert, n_used, tile_rows.astype(jnp.int32), xs, wg, wu, wd, moe_layer, n_tiles)
    return _combine(dest, h, mod3, route, final_g, y, final=final, **geo)


def _rope_tables(seq):
    assert seq % GRID_W == 0
    n_rows = seq // GRID_W
    inv = ROPE_THETA ** (-jnp.arange(ROPE_FREQS, dtype=F32) / ROPE_FREQS)
    ang_r = jnp.arange(n_rows, dtype=F32)[:, None] * inv
    ang_c = jnp.arange(GRID_W, dtype=F32)[:, None] * inv
    cos_r, sin_r = (jnp.repeat(f(ang_r), GRID_W, axis=0) for f in (jnp.cos, jnp.sin))
    cos_c, sin_c = (jnp.tile(f(ang_c), (n_rows, 1)) for f in (jnp.cos, jnp.sin))
    cos = jnp.concatenate([cos_r, cos_r, cos_c, cos_c], axis=1)
    sin = jnp.concatenate([-sin_r, sin_r, -sin_c, sin_c], axis=1)
    cos = jnp.concatenate([cos, jnp.ones((TM, HEAD_DIM), F32)], axis=0)
    sin = jnp.concatenate([sin, jnp.zeros((TM, HEAD_DIM), F32)], axis=0)
    return cos, sin


def kernel(x, c, ctx, c_ctx, w_mod, b_mod, norm1_g, norm2_g, w_in, attn_sink, conv_w, conv_b, gate_a_w, gate_a_b, gate_x_w, gate_x_b, lru_lambda, w_o_attn, w_o_lru, w_out, ff_w_gate, ff_w_up, ff_w_down, router_w, exp_w_gate, exp_w_up, exp_w_down, final_g):
    n_batch, seq, _ = x.shape
    ctx_len = ctx.shape[1]
    assert n_batch * ctx_len == TM and seq % TM == 0 and n_batch + 1 <= MOD_ROWS
    n_lat = n_batch * seq
    tiles_per_batch = seq // TM
    geo = dict(tiles_per_batch=tiles_per_batch, n_batch=n_batch)
    shp = dict(n_batch=n_batch, seq=seq, ctx_len=ctx_len)

    cpad = jnp.zeros((MOD_ROWS, D_MODEL), F32).at[:n_batch].set(c).at[n_batch].set(c_ctx)
    mod = _modulation(cpad, w_mod, b_mod)
    cos_t, sin_t = _rope_tables(seq)
    perm = _time_permutation(TM)
    stream = (x.reshape(n_lat, D_MODEL), ctx.reshape(n_batch * ctx_len, D_MODEL), 0)

    g1 = norm1_g.reshape(DEPTH, 1, D_MODEL)
    g2 = norm2_g.reshape(DEPTH, 1, D_MODEL)
    w_in_b = w_in.astype(BF16)
    woa_b, wol_b, wout_b = w_o_attn.astype(BF16), w_o_lru.astype(BF16), w_out.astype(BF16)
    ffg_b, ffu_b, ffd_b = ff_w_gate.astype(BF16), ff_w_up.astype(BF16), ff_w_down.astype(BF16)

    for l in range(DEPTH):
        mod3 = mod[l].reshape(MOD_ROWS, 1, 6 * D_MODEL)
        if l > 0:
            stream = (h, h, n_lat // TM)
        q, kv, xr, gy, sga, sgr = _in_proj(stream, g1, mod3, w_in_b, l, cos_t, sin_t, perm,
                                           n_lat_tiles=n_lat // TM, **geo)
        wcat = jnp.concatenate([gate_a_w[l], gate_x_w[l]], axis=-1).astype(BF16)
        gate_b = jnp.stack([gate_a_b[l], gate_x_b[l]], axis=1)
        hf, hb, attn = _token_mixer(xr, conv_w[l], conv_b[l].reshape(1, D_RNN), wcat, gate_b, lru_lambda[l],
                                    attn_sink[l], q, kv, **shp)
        h = _merge(stream, attn, hf, hb, gy, sga, sgr, mod3, woa_b, wol_b, wout_b, l, **geo)
        i = l // 2
        if l % 2 == 0:
            h = _ffn(h, g2, mod3, ffg_b, ffu_b, ffd_b, l, i, **geo)
        else:
            w_r = jnp.zeros((D_MODEL, LANES), F32).at[:, :N_EXPERTS].set(router_w[i])
            h = _moe_layer(h, g2, mod3, w_r, exp_w_gate, exp_w_up, exp_w_down,
                           final_g.reshape(1, D_MODEL), l, i, final=(l == DEPTH - 1), **geo)

    assert DEPTH % 2 == 0 and h.shape[0] == n_lat
    return h.reshape(n_batch, seq, D_MODEL)
```

```python
import functools

import jax
import jax.numpy as jnp
from jax import lax
from jax.experimental import pallas as pl
from jax.experimental.pallas import tpu as pltpu

F32 = jnp.float32
BF16 = jnp.bfloat16

D_MODEL = 1024
DEPTH = 4
GRID_W = 64
N_HEADS = 8
N_KV_HEADS = 2
HEAD_DIM = 128
N_GROUPS = N_HEADS // N_KV_HEADS
ATTN_BLOCK = 128
ROPE_THETA = 10000.0
ROPE_FREQS = HEAD_DIM // 4
D_RNN = 1024
N_RNN_BLOCKS = 8
RNN_BLOCK = D_RNN // N_RNN_BLOCKS
LRU_C = 8.0
D_FF = 2816
N_EXPERTS = 8
D_FF_EXPERT = 3584
EPS = 1e-6
NEG_INF = -1e30
Q_COLS = N_HEADS * HEAD_DIM
KV_COLS = N_KV_HEADS * HEAD_DIM
IN_COLS = Q_COLS + 2 * KV_COLS + 2 * D_RNN + 2 * D_MODEL
LOG2E = 1.4426950408889634
ATTN_SCALE = HEAD_DIM ** -0.5 * LOG2E

LANES = 128
SUBLANES = 8
MXU_WIDTH = 256
TM = 512
TN_IN = 512
TT = 256
SUBSEQ = TT // SUBLANES
TG = 2048
TG_SUB = 512
MOE_F_CHUNKS = 7
ROW_DMA_UNROLL = 8
MOD_ROWS = 8
VMEM_LIMIT = 56 * 1024 * 1024


def _cparams(n_axes):
    return pltpu.CompilerParams(dimension_semantics=("arbitrary",) * n_axes,
                                vmem_limit_bytes=VMEM_LIMIT)


def _sigmoid(z):
    return 0.5 * jnp.tanh(0.5 * z) + 0.5


def _norm_mod(h, g, shift, scale):
    ms = jnp.mean(h * h, axis=-1, keepdims=True)
    y = h * lax.rsqrt(ms + EPS) * g
    return y * (1.0 + scale) + shift


def _mod_kernel(c_ref, w_ref, b_ref, o_ref):
    cv = c_ref[...]
    s = cv * _sigmoid(cv)
    o_ref[...] = _dot_split(s, w_ref[...]) + b_ref[...]


def _modulation(cpad, w_mod, b_mod):
    nchunk = 6
    return pl.pallas_call(
        _mod_kernel,
        grid=(DEPTH, nchunk),
        in_specs=[
            pl.BlockSpec((MOD_ROWS, D_MODEL), lambda l, n: (0, 0)),
            pl.BlockSpec((None, D_MODEL, D_MODEL), lambda l, n: (l, 0, n)),
            pl.BlockSpec((None, 1, D_MODEL), lambda l, n: (l, 0, n)),
        ],
        out_specs=pl.BlockSpec((None, MOD_ROWS, D_MODEL), lambda l, n: (l, 0, n)),
        out_shape=jax.ShapeDtypeStruct((DEPTH, MOD_ROWS, 6 * D_MODEL), F32),
        compiler_params=_cparams(2),
        name="modulation",
    )(cpad, w_mod, b_mod.reshape(DEPTH, 1, 6 * D_MODEL))


def _rope(xh, cos, sin_signed, first_half):
    sw = jnp.where(first_half, pltpu.roll(xh, HEAD_DIM - ROPE_FREQS, 1), pltpu.roll(xh, ROPE_FREQS, 1))
    return xh * cos + sw * sin_signed


def _stream_tile(h_lat_ref, h_ctx_ref):
    is_ctx_tile = pl.program_id(0) == pl.num_programs(0) - 1
    return jnp.where(is_ctx_tile, h_ctx_ref[...], h_lat_ref[...])


def _stream_specs(stream, n_lat_tiles):
    _, _, ctx_block = stream
    return [pl.BlockSpec((TM, D_MODEL), lambda i: (jnp.minimum(i, n_lat_tiles - 1), 0)),
            pl.BlockSpec((TM, D_MODEL), lambda i: (ctx_block, 0))]


def _in_proj_kernel(h_lat_ref, h_ctx_ref, g_ref, sh_ref, sc_ref, w_ref, cos_ref, sin_ref, perm_ref,
                    q_ref, kv_ref, x_ref, gy_ref, sga_ref, sgr_ref):
    n = _norm_mod(_stream_tile(h_lat_ref, h_ctx_ref), g_ref[...], sh_ref[...], sc_ref[...]).astype(BF16)
    cos = cos_ref[...]
    sin = sin_ref[...]
    lane = lax.broadcasted_iota(jnp.int32, (TM, LANES), 1)
    first_half = (lane & ROPE_FREQS) == 0

    def proj(chunk):
        return jnp.dot(n, w_ref[:, chunk * TN_IN:(chunk + 1) * TN_IN], preferred_element_type=F32)

    def rope_store(acc, n_heads, scale, ref, col0):
        for hh in range(n_heads):
            y = _rope(acc[:, hh * HEAD_DIM:(hh + 1) * HEAD_DIM], cos, sin, first_half)
            if scale != 1.0:
                y = y * scale
            ref[:, col0 + hh * HEAD_DIM:col0 + (hh + 1) * HEAD_DIM] = y.astype(BF16)

    for c in range(2):
        rope_store(proj(c), 4, ATTN_SCALE, q_ref, c * TN_IN)
    acc = proj(2)
    rope_store(acc, 2, 1.0, kv_ref, 0)
    kv_ref[:, KV_COLS:] = acc[:, KV_COLS:].astype(BF16)
    n_perm = jnp.dot(perm_ref[...], n, preferred_element_type=F32).astype(BF16)
    for c in range(2):
        cols = slice(c * TN_IN, (c + 1) * TN_IN)
        x_ref[:, cols] = jnp.dot(n_perm, w_ref[:, (3 + c) * TN_IN:(4 + c) * TN_IN],
                                 preferred_element_type=F32)
        gy_ref[:, cols] = jax.nn.gelu(proj(5 + c)).astype(BF16)
        sga_ref[:, cols] = _sigmoid(proj(7 + c)).astype(BF16)
        sgr_ref[:, cols] = _sigmoid(proj(9 + c)).astype(BF16)


def _in_proj(stream, g, mod3, w_in, layer, cos_t, sin_t, perm, *, n_lat_tiles, tiles_per_batch, n_batch):
    T = (n_lat_tiles + 1) * TM

    def grp(i):
        return jnp.minimum(i // tiles_per_batch, n_batch)

    def pos_tile(i):
        return jnp.where(i < n_lat_tiles, i % tiles_per_batch, tiles_per_batch)

    def tok(width):
        return pl.BlockSpec((TM, width), lambda i: (i, 0))

    out_shape = (
        jax.ShapeDtypeStruct((T, Q_COLS), BF16),
        jax.ShapeDtypeStruct((T, 2 * KV_COLS), BF16),
        jax.ShapeDtypeStruct((T, D_RNN), F32),
        jax.ShapeDtypeStruct((T, D_RNN), BF16),
        jax.ShapeDtypeStruct((T, D_MODEL), BF16),
        jax.ShapeDtypeStruct((T, D_MODEL), BF16),
    )
    return pl.pallas_call(
        _in_proj_kernel,
        grid=(T // TM,),
        in_specs=_stream_specs(stream, n_lat_tiles) + [
            pl.BlockSpec((None, 1, D_MODEL), lambda i: (layer, 0, 0)),
            pl.BlockSpec((None, 1, D_MODEL), lambda i: (grp(i), 0, 0)),
            pl.BlockSpec((None, 1, D_MODEL), lambda i: (grp(i), 0, 1)),
            pl.BlockSpec((None, D_MODEL, IN_COLS), lambda i: (layer, 0, 0)),
            pl.BlockSpec((TM, HEAD_DIM), lambda i: (pos_tile(i), 0)),
            pl.BlockSpec((TM, HEAD_DIM), lambda i: (pos_tile(i), 0)),
            pl.BlockSpec((TM, TM), lambda i: (0, 0)),
        ],
        out_specs=(tok(Q_COLS), tok(2 * KV_COLS), tok(D_RNN), tok(D_RNN), tok(D_MODEL), tok(D_MODEL)),
        out_shape=out_shape,
        compiler_params=_cparams(1),
        name="in_proj",
    )(stream[0], stream[1], g, mod3, mod3, w_in, cos_t, sin_t, perm)


def _stack_heads(qt):
    return jnp.concatenate([qt[:, g * HEAD_DIM:(g + 1) * HEAD_DIM] for g in range(N_GROUPS)], axis=0)


def _sink_col(sink_ref, kvh, rows):
    return jnp.concatenate(
        [jnp.full((rows, 1), sink_ref[kvh * N_GROUPS + g] * LOG2E, F32) for g in range(N_GROUPS)], axis=0)


def _ones_column(n_keys):
    lane = lax.broadcasted_iota(jnp.int32, (n_keys, HEAD_DIM), 1)
    return jnp.where(lane == 0, 1.0, 0.0).astype(BF16)


def _softplus(z):
    return jnp.maximum(z, 0.0) + jnp.log1p(jnp.exp(-jnp.abs(z)))


def _scan8(a, b, row, reverse):
    for s in (1, 2, 4):
        if reverse:
            keep = row < SUBLANES - s
            shift = SUBLANES - s
        else:
            keep = row >= s
            shift = s
        a_sh = jnp.where(keep, pltpu.roll(a, shift, 0), 1.0)
        b_sh = jnp.where(keep, pltpu.roll(b, shift, 0), 0.0)
        b = a * b_sh + b
        a = a * a_sh
    return a, b


def _attention_stages(sink_ref, q_ref, kv_ref, kvc_ref, o_ref, tile, n_blocks, is_ctx):
    rows = ATTN_BLOCK * N_GROUPS
    qi = lax.broadcasted_iota(jnp.int32, (rows, ATTN_BLOCK), 0) & (ATTN_BLOCK - 1)
    kj = lax.broadcasted_iota(jnp.int32, (rows, ATTN_BLOCK), 1)
    tri_prev = kj >= qi
    tri_next = kj <= qi
    q_per_tile = TT // ATTN_BLOCK
    gw = N_GROUPS * HEAD_DIM
    st_cols = 3 * ATTN_BLOCK + kvc_ref.shape[0]
    stages = []
    for kvh in range(N_KV_HEADS):
        for qb in range(q_per_tile):
            kcol = slice(kvh * HEAD_DIM, (kvh + 1) * HEAD_DIM)
            vcol = slice(KV_COLS + kvh * HEAD_DIM, KV_COLS + (kvh + 1) * HEAD_DIM)
            qrows = slice(qb * ATTN_BLOCK, (qb + 1) * ATTN_BLOCK)
            n = tile * q_per_tile + qb
            starts = [pl.multiple_of(jnp.maximum(n - 1, 0) * ATTN_BLOCK, ATTN_BLOCK),
                      pl.multiple_of(n * ATTN_BLOCK, ATTN_BLOCK),
                      pl.multiple_of(jnp.minimum(n + 1, n_blocks - 1) * ATTN_BLOCK, ATTN_BLOCK)]
            st = {}

            def scores(st=st, kvh=kvh, kcol=kcol, qrows=qrows, n=n, starts=starts):
                qs = _stack_heads(q_ref[qrows, kvh * gw:(kvh + 1) * gw])
                kall = jnp.concatenate(
                    [kv_ref[pl.ds(r, ATTN_BLOCK), kcol] for r in starts] + [kvc_ref[:, kcol]], axis=0)
                s = lax.dot_general(qs, kall, (((1,), (1,)), ((), ())), preferred_element_type=F32)
                pen_band = jnp.where(is_ctx, NEG_INF, 0.0)
                pen_prev = jnp.where(n > 0, pen_band, NEG_INF)
                pen_next = jnp.where(n < n_blocks - 1, pen_band, NEG_INF)
                sp = jnp.where(tri_prev, s[:, :ATTN_BLOCK] + pen_prev, NEG_INF)
                sc = s[:, ATTN_BLOCK:2 * ATTN_BLOCK] + pen_band
                sn = jnp.where(tri_next, s[:, 2 * ATTN_BLOCK:3 * ATTN_BLOCK] + pen_next, NEG_INF)
                st["s"] = jnp.concatenate([sp, sc, sn, s[:, 3 * ATTN_BLOCK:]], axis=1)

            def row_max(st=st, kvh=kvh):
                sink = _sink_col(sink_ref, kvh, ATTN_BLOCK)
                st["m"] = jnp.maximum(jnp.max(st["s"], axis=1, keepdims=True), sink)
                st["sink_term"] = jnp.exp2(sink - st["m"])
                st["p"] = []

            def numerators(c, st=st):
                cols = slice(c * ATTN_BLOCK, (c + 1) * ATTN_BLOCK)
                st["p"].append(jnp.exp2(st["s"][:, cols] - st["m"]).astype(BF16))

            def weighted_values(st=st, kvh=kvh, vcol=vcol, qrows=qrows, starts=starts):
                vall = jnp.concatenate(
                    [kv_ref[pl.ds(r, ATTN_BLOCK), vcol] for r in starts] + [kvc_ref[:, vcol]], axis=0)
                v_aug = jnp.concatenate([vall, _ones_column(vall.shape[0])], axis=1)
                p = jnp.concatenate(st.pop("p"), axis=1)
                oa = jnp.dot(p, v_aug, preferred_element_type=F32)
                o = oa[:, :HEAD_DIM] / (oa[:, HEAD_DIM:HEAD_DIM + 1] + st.pop("sink_term"))
                for g in range(N_GROUPS):
                    col0 = kvh * gw + g * HEAD_DIM
                    o_ref[qrows, col0:col0 + HEAD_DIM] = o[g * ATTN_BLOCK:(g + 1) * ATTN_BLOCK].astype(BF16)
                st.clear()

            n_chunks = st_cols // ATTN_BLOCK
            stages += ([scores, row_max] + [functools.partial(numerators, c) for c in range(n_chunks)]
                       + [weighted_values])
    return stages


def _mixer_kernel(xf_ref, xfp_ref, xfn_ref, xb_ref, xbp_ref, xbn_ref, cw_ref, cb_ref, wcat_ref,
                  gb_ref, lam_ref, unperm_ref, sink_ref, q_ref, kv_ref, kvc_ref,
                  hf_ref, hb_ref, attn_ref, a_s, b_s, h_bf, carry, *, n_blocks):
    j = pl.program_id(1)
    last_j = pl.num_programs(1) - 1

    @pl.when(j == 0)
    def _():
        carry[...] = jnp.zeros_like(carry)

    seg_start = (j <= 1, (j == 0) | (j == last_j))
    seg_end = ((j == 0) | (j == last_j), j <= 1)
    mains = (xf_ref, xb_ref)
    prevs = (xfp_ref, xbp_ref)
    nexts = (xfn_ref, xbn_ref)
    sub = lax.broadcasted_iota(jnp.int32, (SUBLANES, D_RNN), 0)
    S8 = SUBLANES

    sub_b = lax.broadcasted_iota(jnp.int32, (SUBLANES, RNN_BLOCK), 0)

    def conv(d, sl):
        x0 = mains[d][:, sl]
        halo = prevs[d][:, sl]
        nxt = nexts[d][:, sl]
        cw = cw_ref[:, sl]
        t_m1 = jnp.where(seg_start[d], 0.0, halo[2 * S8 - 1:2 * S8, :])
        t_m2 = jnp.where(seg_start[d], 0.0, halo[S8 - 1:S8, :])
        t_p1 = jnp.where(seg_end[d], 0.0, nxt[0:1, :])
        g_m1 = jnp.where(sub_b == 0, t_m1, pltpu.roll(x0[TT - S8:, :], 1, 0))
        g_m2 = jnp.where(sub_b == 0, t_m2, pltpu.roll(x0[TT - 2 * S8:TT - S8, :], 1, 0))
        g_p1 = jnp.where(sub_b == S8 - 1, t_p1, pltpu.roll(x0[:S8, :], S8 - 1, 0))
        xc = cb_ref[:, sl] + jnp.concatenate([g_m2, g_m1, x0[:TT - 2 * S8, :]], axis=0) * cw[0:1, :]
        xc = xc + jnp.concatenate([g_m1, x0[:TT - S8, :]], axis=0) * cw[1:2, :]
        xc = xc + x0 * cw[2:3, :]
        return xc + jnp.concatenate([x0[S8:, :], g_p1], axis=0) * cw[3:4, :]

    def gates(d, blk, c_d):
        sl = slice(blk * RNN_BLOCK, (blk + 1) * RNN_BLOCK)
        xcb = conv(d, sl)
        z = jnp.dot(xcb.astype(BF16), wcat_ref[d, blk], preferred_element_type=F32)
        r = _sigmoid(z[:, :RNN_BLOCK] + gb_ref[d, 0:1, sl])
        ig = _sigmoid(z[:, RNN_BLOCK:] + gb_ref[d, 1:2, sl])
        a = jnp.exp2(r * c_d[:, sl])
        v = 1.0 - a * a
        root = jnp.where(v > 0.0, v * lax.rsqrt(v), 0.0)
        a_s[d, :, sl] = a
        b_s[d, :, sl] = root * (ig * xcb)

    outs = (hf_ref, hb_ref)

    def gate_stages(d):
        st = {}

        def block(blk):
            if blk == 0:
                st["c"] = (-LRU_C * LOG2E) * _softplus(-lam_ref[d:d + 1, :])
            gates(d, blk, st["c"])

        return [functools.partial(block, blk) for blk in range(N_RNN_BLOCKS)]

    def scan_stages(d):
        st = {"h": jnp.zeros((S8, D_RNN), F32), "p": jnp.ones((S8, D_RNN), F32)}

        def step(g):
            r = (g if d == 0 else SUBSEQ - 1 - g) * S8
            a = a_s[d, r:r + S8, :]
            st["h"] = a * st["h"] + b_s[d, r:r + S8, :]
            st["p"] = a * st["p"]
            b_s[d, r:r + S8, :] = st["h"]
            a_s[d, r:r + S8, :] = st["p"]

        def chain():
            a8, b8 = _scan8(st.pop("p"), st.pop("h"), sub, d == 1)
            end = a8 * carry[d] + b8
            if d == 0:
                h_in = jnp.where(sub == 0, carry[0], pltpu.roll(end, 1, 0))
                carry[0] = jnp.broadcast_to(end[S8 - 1:S8, :], (S8, D_RNN))
            else:
                h_in = jnp.where(sub == S8 - 1, carry[1], pltpu.roll(end, S8 - 1, 0))
                carry[1] = jnp.broadcast_to(end[0:1, :], (S8, D_RNN))
            st["h_in2"] = jnp.concatenate([h_in, h_in], axis=0)

        def fix(k):
            rows = slice(k * 2 * S8, (k + 1) * 2 * S8)
            h_bf[d, rows, :] = (b_s[d, rows, :] + a_s[d, rows, :] * st["h_in2"]).astype(BF16)

        def store():
            outs[d][...] = jnp.dot(unperm_ref[...], h_bf[d], preferred_element_type=F32).astype(BF16)

        return ([functools.partial(step, g) for g in range(SUBSEQ)] + [chain]
                + [functools.partial(fix, k) for k in range(SUBSEQ // 2)] + [store])

    def emit(primary, *others):
        sent = [0] * len(others)
        for i, stage in enumerate(primary, 1):
            stage()
            for o, stages in enumerate(others):
                while sent[o] * len(primary) < i * len(stages):
                    stages[sent[o]]()
                    sent[o] += 1

    attn_stages = _attention_stages(sink_ref, q_ref, kv_ref, kvc_ref, attn_ref, jnp.maximum(j - 1, 0),
                                    n_blocks, j == 0)
    half = len(attn_stages) // 2
    emit(gate_stages(0), attn_stages[:half])
    emit(gate_stages(1), attn_stages[half:], scan_stages(0))
    emit(scan_stages(1))


def _time_permutation(n_rows):
    p = jnp.arange(n_rows)
    src = (p // TT) * TT + (p % SUBLANES) * SUBSEQ + (p % TT) // SUBLANES
    return (src[:, None] == jnp.arange(n_rows)[None, :]).astype(BF16)


def _token_mixer(x, conv_w, conv_b, wcat, gate_b, lam, sink, q, kv, *, n_batch, seq, ctx_len):
    T = x.shape[0]
    assert ctx_len == TT and seq % TT == 0
    tps = seq // TT
    n_lat_t = n_batch * tps
    halo_rows = 2 * SUBLANES
    per_halo = TT // halo_rows
    per8 = TT // SUBLANES
    last8 = T // SUBLANES - 1
    unperm = _time_permutation(TT).T

    def ftile(b, j):
        return jnp.where(j == 0, n_lat_t + b, b * tps + j - 1)

    def btile(b, j):
        return jnp.where(j == 0, n_lat_t + b, b * tps + tps - j)

    def main(tile):
        return pl.BlockSpec((TT, D_RNN), lambda b, j: (tile(b, j), 0))

    def prev(tile):
        return pl.BlockSpec((halo_rows, D_RNN), lambda b, j: (jnp.maximum(tile(b, j) * per_halo - 1, 0), 0))

    def nxt(tile):
        return pl.BlockSpec((SUBLANES, D_RNN),
                            lambda b, j: (jnp.minimum((tile(b, j) + 1) * per8, last8), 0))

    def const(shape):
        return pl.BlockSpec(shape, lambda b, j: (0,) * len(shape))

    q_spec = pl.BlockSpec((TT, Q_COLS), lambda b, j: (ftile(b, j), 0))
    return pl.pallas_call(
        functools.partial(_mixer_kernel, n_blocks=seq // ATTN_BLOCK),
        grid=(n_batch, tps + 1),
        in_specs=[main(ftile), prev(ftile), nxt(ftile), main(btile), prev(btile), nxt(btile),
                  const(conv_w.shape), const(conv_b.shape), const(wcat.shape), const(gate_b.shape),
                  const(lam.shape), const(unperm.shape),
                  pl.BlockSpec(memory_space=pltpu.SMEM),
                  q_spec,
                  pl.BlockSpec((seq, 2 * KV_COLS), lambda b, j: (b, 0)),
                  pl.BlockSpec((ctx_len, 2 * KV_COLS), lambda b, j: (n_lat_t + b, 0))],
        out_specs=(main(ftile), main(btile), q_spec),
        out_shape=(jax.ShapeDtypeStruct((T, D_RNN), BF16), jax.ShapeDtypeStruct((T, D_RNN), BF16),
                   jax.ShapeDtypeStruct((T, Q_COLS), BF16)),
        scratch_shapes=[pltpu.VMEM((2, TT, D_RNN), F32),
                        pltpu.VMEM((2, TT, D_RNN), F32),
                        pltpu.VMEM((2, TT, D_RNN), BF16),
                        pltpu.VMEM((2, SUBLANES, D_RNN), F32)],
        compiler_params=_cparams(2),
        name="token_mixer",
    )(x, x, x, x, x, x, conv_w, conv_b, wcat, gate_b, lam, unperm, sink, q, kv, kv)


def _merge_kernel(h_lat_ref, h_ctx_ref, attn_ref, hf_ref, hb_ref, gy_ref, sga_ref, sgr_ref,
                  g1_ref, woa_ref, wol_ref, wout_ref, o_ref):
    rec = ((hf_ref[...].astype(F32) + hb_ref[...].astype(F32)) * gy_ref[...].astype(F32)).astype(BF16)
    ta = jnp.dot(attn_ref[...], woa_ref[...], preferred_element_type=F32)
    tl = jnp.dot(rec, wol_ref[...], preferred_element_type=F32)
    m = sga_ref[...].astype(F32) * ta + sgr_ref[...].astype(F32) * tl
    y = jnp.dot(m.astype(BF16), wout_ref[...], preferred_element_type=F32)
    o_ref[...] = _stream_tile(h_lat_ref, h_ctx_ref) + g1_ref[...] * y


def _merge(stream, attn, hf, hb, gy, sga, sgr, mod3, woa, wol, wout, layer, *, tiles_per_batch, n_batch):
    T = attn.shape[0]
    n_lat_tiles = T // TM - 1

    def grp(i):
        return jnp.minimum(i // tiles_per_batch, n_batch)

    tok = pl.BlockSpec((TM, D_MODEL), lambda i: (i, 0))
    wsp = pl.BlockSpec((None, D_MODEL, D_MODEL), lambda i: (layer, 0, 0))
    return pl.pallas_call(
        _merge_kernel,
        grid=(T // TM,),
        in_specs=_stream_specs(stream, n_lat_tiles) + [
                  tok, tok, tok, tok, tok, tok,
                  pl.BlockSpec((None, 1, D_MODEL), lambda i: (grp(i), 0, 2)),
                  wsp, wsp, wsp],
        out_specs=tok,
        out_shape=jax.ShapeDtypeStruct((T, D_MODEL), F32),
        compiler_params=_cparams(1),
        name="merge",
    )(stream[0], stream[1], attn, hf, hb, gy, sga, sgr, mod3, woa, wol, wout)


def _swiglu_partial(n, wg, wu, wd):
    gt = jnp.dot(n, wg, preferred_element_type=F32)
    ut = jnp.dot(n, wu, preferred_element_type=F32)
    act = (gt * _sigmoid(gt) * ut).astype(BF16)
    return jnp.dot(act, wd, preferred_element_type=F32)


FFN_F_CHUNKS = 11


def _ffn_kernel(h_ref, g_ref, sh_ref, sc_ref, g2_ref, wg_ref, wu_ref, wd_ref, o_ref):
    h = h_ref[...]
    n = _norm_mod(h, g_ref[...], sh_ref[...], sc_ref[...]).astype(BF16)
    tf = D_FF // FFN_F_CHUNKS
    acc = None
    for c in range(FFN_F_CHUNKS):
        cols = slice(c * tf, (c + 1) * tf)
        part = _swiglu_partial(n, wg_ref[:, cols], wu_ref[:, cols], wd_ref[cols, :])
        acc = part if acc is None else acc + part
    o_ref[...] = h + g2_ref[...] * acc


def _ffn(h, g, mod3, wg, wu, wd, layer, ff_layer, *, tiles_per_batch, n_batch):
    T = h.shape[0]
    assert (D_FF // FFN_F_CHUNKS) % LANES == 0

    def grp(i):
        return jnp.minimum(i // tiles_per_batch, n_batch)

    def modspec(k):
        return pl.BlockSpec((None, 1, D_MODEL), lambda i: (grp(i), 0, k))

    tok = pl.BlockSpec((TM, D_MODEL), lambda i: (i, 0))
    return pl.pallas_call(
        _ffn_kernel,
        grid=(T // TM,),
        in_specs=[tok, pl.BlockSpec((None, 1, D_MODEL), lambda i: (layer, 0, 0)),
                  modspec(3), modspec(4), modspec(5),
                  pl.BlockSpec((None, D_MODEL, D_FF), lambda i: (ff_layer, 0, 0)),
                  pl.BlockSpec((None, D_MODEL, D_FF), lambda i: (ff_layer, 0, 0)),
                  pl.BlockSpec((None, D_FF, D_MODEL), lambda i: (ff_layer, 0, 0))],
        out_specs=tok,
        out_shape=jax.ShapeDtypeStruct((T, D_MODEL), F32),
        compiler_params=_cparams(1),
        name="dense_ffn",
    )(h, g, mod3, mod3, mod3, wg, wu, wd)


ROUTE_E1, ROUTE_E2, ROUTE_W1, ROUTE_W2, ROUTE_R1, ROUTE_R2 = range(6)


def _dot_split(a, b):
    a_hi = a.astype(BF16)
    a_lo = (a - a_hi.astype(F32)).astype(BF16)
    b_hi = b.astype(BF16)
    b_lo = (b - b_hi.astype(F32)).astype(BF16)

    def mm(x, y):
        return jnp.dot(x, y, preferred_element_type=F32)

    return mm(a_hi, b_hi) + (mm(a_hi, b_lo) + mm(a_lo, b_hi))


def _router_kernel(h_ref, g_ref, sh_ref, sc_ref, wr_ref, route_ref, cnt_ref, zero_ref, run):
    @pl.when(pl.program_id(0) == 0)
    def _():
        run[...] = jnp.zeros_like(run)

    n = _norm_mod(h_ref[...], g_ref[...], sh_ref[...], sc_ref[...])
    logits = _dot_split(n, wr_ref[...])
    lane = lax.broadcasted_iota(jnp.int32, logits.shape, 1)
    logits = jnp.where(lane < N_EXPERTS, logits, -jnp.inf)
    m1 = jnp.max(logits, axis=1, keepdims=True)
    i1 = jnp.min(jnp.where(logits == m1, lane, LANES), axis=1, keepdims=True)
    rest = jnp.where(lane == i1, -jnp.inf, logits)
    m2 = jnp.max(rest, axis=1, keepdims=True)
    i2 = jnp.min(jnp.where(rest == m2, lane, LANES), axis=1, keepdims=True)
    e2 = jnp.exp(m2 - m1)
    w1 = 1.0 / (1.0 + e2)
    w2 = e2 / (1.0 + e2)

    hit1 = lane == i1
    hit2 = lane == i2
    onehot = jnp.where(hit1 | hit2, 1.0, 0.0)
    r_i = lax.broadcasted_iota(jnp.int32, (TM, TM), 0)
    c_i = lax.broadcasted_iota(jnp.int32, (TM, TM), 1)
    lower = jnp.where(c_i < r_i, 1.0, 0.0).astype(BF16)
    prefix = jnp.dot(lower, onehot.astype(BF16), preferred_element_type=F32) + run[0:1, :]
    rank1 = jnp.sum(jnp.where(hit1, prefix, 0.0), axis=1, keepdims=True)
    rank2 = jnp.sum(jnp.where(hit2, prefix, 0.0), axis=1, keepdims=True)
    run[...] = run[...] + jnp.sum(onehot, axis=0, keepdims=True)
    cnt_ref[...] = run[...]

    rec = jnp.zeros(logits.shape, F32)
    for k, v in ((ROUTE_E1, i1.astype(F32)), (ROUTE_E2, i2.astype(F32)), (ROUTE_W1, w1), (ROUTE_W2, w2),
                 (ROUTE_R1, rank1), (ROUTE_R2, rank2)):
        rec = jnp.where(lane == k, v, rec)
    route_ref[...] = rec
    zero_ref[...] = jnp.zeros_like(zero_ref)


def _router(h, g2, mod3, w_r, layer, sorted_rows, *, tiles_per_batch, n_batch):
    T = h.shape[0]
    n_steps = T // TM
    zero_rows = -(-sorted_rows // (n_steps * SUBLANES)) * SUBLANES

    def grp(i):
        return jnp.minimum(i // tiles_per_batch, n_batch)

    def modspec(k):
        return pl.BlockSpec((None, 1, D_MODEL), lambda i: (grp(i), 0, k))

    return pl.pallas_call(
        _router_kernel,
        grid=(T // TM,),
        in_specs=[pl.BlockSpec((TM, D_MODEL), lambda i: (i, 0)),
                  pl.BlockSpec((None, 1, D_MODEL), lambda i: (layer, 0, 0)),
                  modspec(3), modspec(4),
                  pl.BlockSpec((D_MODEL, LANES), lambda i: (0, 0))],
        out_specs=(pl.BlockSpec((TM, LANES), lambda i: (i, 0)),
                   pl.BlockSpec((SUBLANES, LANES), lambda i: (0, 0)),
                   pl.BlockSpec((zero_rows, D_MODEL), lambda i: (i, 0))),
        out_shape=(jax.ShapeDtypeStruct((T, LANES), F32), jax.ShapeDtypeStruct((SUBLANES, LANES), F32),
                   jax.ShapeDtypeStruct((n_steps * zero_rows, D_MODEL), F32)),
        scratch_shapes=[pltpu.VMEM((SUBLANES, LANES), F32)],
        compiler_params=_cparams(1),
        name="moe_router",
    )(h, g2, mod3, mod3, w_r)


def _row_copy(src_ref, src_row, dst_ref, dst_row, sem):
    return pltpu.make_async_copy(src_ref.at[pl.ds(src_row, 1)], dst_ref.at[pl.ds(dst_row, 1)], sem)


def _dispatch_kernel(dest_ref, h_ref, g_ref, sh_ref, sc_ref, xs_in_ref, xs_ref, n_scr, sem):
    del xs_in_ref
    i = pl.program_id(0)
    base = i * (2 * TM)
    cur = i % 2
    n_scr[cur] = _norm_mod(h_ref[...], g_ref[...], sh_ref[...], sc_ref[...])

    def start(r, carry):
        for s in range(2):
            _row_copy(n_scr.at[cur], r, xs_ref, dest_ref[base + 2 * r + s], sem.at[cur]).start(priority=s)
        return carry

    lax.fori_loop(0, TM, start, 0, unroll=ROW_DMA_UNROLL)

    def wait_all(buf):
        for s in range(2):
            pltpu.make_async_copy(n_scr.at[buf], xs_ref.at[pl.ds(0, TM)], sem.at[buf]).wait()

    @pl.when(i > 0)
    def _():
        wait_all(1 - cur)

    @pl.when(i == pl.num_programs(0) - 1)
    def _():
        wait_all(cur)


def _dispatch(dest, h, g2, mod3, xs_zero, layer, *, tiles_per_batch, n_batch):
    T = h.shape[0]

    def grp(i):
        return jnp.minimum(i // tiles_per_batch, n_batch)

    def modspec(k):
        return pl.BlockSpec((None, 1, D_MODEL), lambda i, d: (grp(i), 0, k))

    grid_spec = pltpu.PrefetchScalarGridSpec(
        num_scalar_prefetch=1,
        grid=(T // TM,),
        in_specs=[pl.BlockSpec((TM, D_MODEL), lambda i, d: (i, 0)),
                  pl.BlockSpec((None, 1, D_MODEL), lambda i, d: (layer, 0, 0)),
                  modspec(3), modspec(4),
                  pl.BlockSpec(memory_space=pl.ANY)],
        out_specs=pl.BlockSpec(memory_space=pl.ANY),
        scratch_shapes=[pltpu.VMEM((2, TM, D_MODEL), F32), pltpu.SemaphoreType.DMA((2,))],
    )
    return pl.pallas_call(
        _dispatch_kernel,
        grid_spec=grid_spec,
        out_shape=jax.ShapeDtypeStruct(xs_zero.shape, xs_zero.dtype),
        input_output_aliases={5: 0},
        compiler_params=_cparams(1),
        name="moe_dispatch",
    )(dest, h, g2, mod3, mod3, xs_zero)


def _expert_kernel(te_ref, nu_ref, rows_ref, x_ref, wg_ref, wu_ref, wd_ref, acc, xb):
    del te_ref, nu_ref
    k = pl.program_id(0)
    f = pl.program_id(1)

    @pl.when(f == 0)
    def _():
        xb[...] = x_ref[...].astype(BF16)
        acc[...] = jnp.zeros_like(acc)

    tf = wg_ref.shape[1]
    x0 = xb[:TG_SUB, :]
    part = None
    weights = []
    for c0 in range(0, tf, MXU_WIDTH):
        cols = slice(c0, c0 + MXU_WIDTH)
        w3 = (wg_ref[:, cols].astype(BF16), wu_ref[:, cols].astype(BF16), wd_ref[cols, :].astype(BF16))
        weights.append(w3)
        p = _swiglu_partial(x0, *w3)
        part = p if part is None else part + p
    acc[:TG_SUB, :] += part
    for sub in range(1, TG // TG_SUB):
        rows = slice(sub * TG_SUB, (sub + 1) * TG_SUB)

        @pl.when(rows_ref[k] > sub * TG_SUB)
        def _():
            part = None
            for w3 in weights:
                p = _swiglu_partial(xb[rows, :], *w3)
                part = p if part is None else part + p
            acc[rows, :] += part


def _experts(tile_expert, n_used, tile_rows, xs, wg, wu, wd, layer, n_tiles):
    P = n_tiles * TG
    assert xs.shape[0] >= P
    n_f = MOE_F_CHUNKS
    tf = D_FF_EXPERT // n_f
    assert tf % LANES == 0

    def fsel(k, f, nu):
        return jnp.where(k < nu[0], f, n_f - 1)

    grid_spec = pltpu.PrefetchScalarGridSpec(
        num_scalar_prefetch=3,
        grid=(P // TG, n_f),
        in_specs=[pl.BlockSpec((TG, D_MODEL), lambda k, f, te, nu, tr: (k, 0)),
                  pl.BlockSpec((None, None, D_MODEL, tf), lambda k, f, te, nu, tr: (layer, te[k], 0, fsel(k, f, nu))),
                  pl.BlockSpec((None, None, D_MODEL, tf), lambda k, f, te, nu, tr: (layer, te[k], 0, fsel(k, f, nu))),
                  pl.BlockSpec((None, None, tf, D_MODEL), lambda k, f, te, nu, tr: (layer, te[k], fsel(k, f, nu), 0))],
        out_specs=pl.BlockSpec((TG, D_MODEL), lambda k, f, te, nu, tr: (k, 0)),
        scratch_shapes=[pltpu.VMEM((TG, D_MODEL), BF16)],
    )
    return pl.pallas_call(
        _expert_kernel,
        grid_spec=grid_spec,
        out_shape=jax.ShapeDtypeStruct((P, D_MODEL), F32),
        compiler_params=_cparams(2),
        name="moe_experts",
    )(tile_expert, n_used, tile_rows, xs, wg, wu, wd)


def _combine_kernel(dest_ref, h_ref, g2_ref, route_ref, fg_ref, y_ref, o_ref, ybuf, sem, *, final):
    i = pl.program_id(0)
    n_steps = pl.num_programs(0)

    def gather(tile, buf):
        base = tile * (2 * TM)

        def start(r, carry):
            for s in range(2):
                _row_copy(y_ref, dest_ref[base + 2 * r + s], ybuf.at[buf], s * TM + r,
                          sem.at[buf]).start(priority=s)
            return carry

        lax.fori_loop(0, TM, start, 0, unroll=ROW_DMA_UNROLL)

    @pl.when(i == 0)
    def _():
        gather(0, 0)

    @pl.when(i + 1 < n_steps)
    def _():
        gather(i + 1, (i + 1) % 2)

    cur = i % 2
    for s in range(2):
        pltpu.make_async_copy(y_ref.at[pl.ds(0, TM)], ybuf.at[cur, pl.ds(s * TM, TM)], sem.at[cur]).wait()
    route = route_ref[...]
    f = (route[:, ROUTE_W1:ROUTE_W1 + 1] * ybuf[cur, 0:TM, :]
         + route[:, ROUTE_W2:ROUTE_W2 + 1] * ybuf[cur, TM:, :])
    out = h_ref[...] + g2_ref[...] * f
    if final:
        ms = jnp.mean(out * out, axis=-1, keepdims=True)
        out = out * lax.rsqrt(ms + EPS) * fg_ref[...]
    o_ref[...] = out


def _combine(dest, h, mod3, route, final_g, y, *, final, tiles_per_batch, n_batch):
    n_tok_tiles = h.shape[0] // TM - (1 if final else 0)

    def grp(i):
        return jnp.minimum(i // tiles_per_batch, n_batch)

    grid_spec = pltpu.PrefetchScalarGridSpec(
        num_scalar_prefetch=1,
        grid=(n_tok_tiles,),
        in_specs=[pl.BlockSpec((TM, D_MODEL), lambda i, d: (i, 0)),
                  pl.BlockSpec((None, 1, D_MODEL), lambda i, d: (grp(i), 0, 5)),
                  pl.BlockSpec((TM, LANES), lambda i, d: (i, 0)),
                  pl.BlockSpec((1, D_MODEL), lambda i, d: (0, 0)),
                  pl.BlockSpec(memory_space=pl.ANY)],
        out_specs=pl.BlockSpec((TM, D_MODEL), lambda i, d: (i, 0)),
        scratch_shapes=[pltpu.VMEM((2, 2 * TM, D_MODEL), F32), pltpu.SemaphoreType.DMA((2,))],
    )
    return pl.pallas_call(
        functools.partial(_combine_kernel, final=final),
        grid_spec=grid_spec,
        out_shape=jax.ShapeDtypeStruct((n_tok_tiles * TM, D_MODEL), F32),
        compiler_params=_cparams(1),
        name="moe_combine",
    )(dest, h, mod3, route, final_g, y)


def _moe_layer(h, g2, mod3, w_r, wg, wu, wd, final_g, layer, moe_layer, *, final, **geo):
    T = h.shape[0]
    n_tiles = (2 * T) // TG + N_EXPERTS
    route, cnt, xs_zero = _router(h, g2, mod3, w_r, layer, n_tiles * TG, **geo)

    counts = cnt[0, :N_EXPERTS].astype(jnp.int32)
    padded = ((counts + TG - 1) // TG) * TG
    ends = jnp.cumsum(padded)
    offs = ends - padded
    e12 = route[:, ROUTE_E1:ROUTE_E2 + 1].astype(jnp.int32)
    r12 = route[:, ROUTE_R1:ROUTE_R2 + 1].astype(jnp.int32)
    onehot = e12[:, :, None] == jnp.arange(N_EXPERTS)[None, None, :]
    dest = (jnp.sum(jnp.where(onehot, offs[None, None, :], 0), axis=-1) + r12).reshape(2 * T)
    n_used = (ends[-1] // TG).reshape(1)
    tiles = jnp.arange(n_tiles)
    te_raw = jnp.sum(tiles[:, None] >= (ends // TG)[None, :], axis=1)
    tile_expert = jnp.minimum(te_raw, N_EXPERTS - 1).astype(jnp.int32)
    sel = tile_expert[:, None] == jnp.arange(N_EXPERTS)[None, :]
    cnt_k = jnp.sum(jnp.where(sel, counts[None, :], 0), axis=1)
    off_k = jnp.sum(jnp.where(sel, offs[None, :], 0), axis=1)
    tile_rows = jnp.where(te_raw < N_EXPERTS, jnp.clip(cnt_k - (tiles * TG - off_k), 0, TG), 0)

    xs = _dispatch(dest, h, g2, mod3, xs_zero, layer, **geo)
    y = _experts(tile_expert, n_used, tile_rows.astype(jnp.int32), xs, wg, wu, wd, moe_layer, n_tiles)
    return _combine(dest, h, mod3, route, final_g, y, final=final, **geo)


def _rope_tables(seq):
    assert seq % GRID_W == 0
    n_rows = seq // GRID_W
    inv = ROPE_THETA ** (-jnp.arange(ROPE_FREQS, dtype=F32) / ROPE_FREQS)
    ang_r = jnp.arange(n_rows, dtype=F32)[:, None] * inv
    ang_c = jnp.arange(GRID_W, dtype=F32)[:, None] * inv
    cos_r, sin_r = (jnp.repeat(f(ang_r), GRID_W, axis=0) for f in (jnp.cos, jnp.sin))
    cos_c, sin_c = (jnp.tile(f(ang_c), (n_rows, 1)) for f in (jnp.cos, jnp.sin))
    cos = jnp.concatenate([cos_r, cos_r, cos_c, cos_c], axis=1)
    sin = jnp.concatenate([-sin_r, sin_r, -sin_c, sin_c], axis=1)
    cos = jnp.concatenate([cos, jnp.ones((TM, HEAD_DIM), F32)], axis=0)
    sin = jnp.concatenate([sin, jnp.zeros((TM, HEAD_DIM), F32)], axis=0)
    return cos, sin


def kernel(x, c, ctx, c_ctx, w_mod, b_mod, norm1_g, norm2_g, w_in, attn_sink, conv_w, conv_b, gate_a_w, gate_a_b, gate_x_w, gate_x_b, lru_lambda, w_o_attn, w_o_lru, w_out, ff_w_gate, ff_w_up, ff_w_down, router_w, exp_w_gate, exp_w_up, exp_w_down, final_g):
    n_batch, seq, _ = x.shape
    ctx_len = ctx.shape[1]
    assert n_batch * ctx_len == TM and seq % TM == 0 and n_batch + 1 <= MOD_ROWS
    n_lat = n_batch * seq
    tiles_per_batch = seq // TM
    geo = dict(tiles_per_batch=tiles_per_batch, n_batch=n_batch)
    shp = dict(n_batch=n_batch, seq=seq, ctx_len=ctx_len)

    cpad = jnp.zeros((MOD_ROWS, D_MODEL), F32).at[:n_batch].set(c).at[n_batch].set(c_ctx)
    mod = _modulation(cpad, w_mod, b_mod)
    cos_t, sin_t = _rope_tables(seq)
    perm = _time_permutation(TM)
    stream = (x.reshape(n_lat, D_MODEL), ctx.reshape(n_batch * ctx_len, D_MODEL), 0)

    g1 = norm1_g.reshape(DEPTH, 1, D_MODEL)
    g2 = norm2_g.reshape(DEPTH, 1, D_MODEL)
    w_in_b = w_in.astype(BF16)
    woa_b, wol_b, wout_b = w_o_attn.astype(BF16), w_o_lru.astype(BF16), w_out.astype(BF16)
    ffg_b, ffu_b, ffd_b = ff_w_gate.astype(BF16), ff_w_up.astype(BF16), ff_w_down.astype(BF16)

    for l in range(DEPTH):
        mod3 = mod[l].reshape(MOD_ROWS, 1, 6 * D_MODEL)
        if l > 0:
            stream = (h, h, n_lat // TM)
        q, kv, xr, gy, sga, sgr = _in_proj(stream, g1, mod3, w_in_b, l, cos_t, sin_t, perm,
                                           n_lat_tiles=n_lat // TM, **geo)
        wcat = jnp.concatenate([gate_a_w[l], gate_x_w[l]], axis=-1).astype(BF16)
        gate_b = jnp.stack([gate_a_b[l], gate_x_b[l]], axis=1)
        hf, hb, attn = _token_mixer(xr, conv_w[l], conv_b[l].reshape(1, D_RNN), wcat, gate_b, lru_lambda[l],
                                    attn_sink[l], q, kv, **shp)
        h = _merge(stream, attn, hf, hb, gy, sga, sgr, mod3, woa_b, wol_b, wout_b, l, **geo)
        i = l // 2
        if l % 2 == 0:
            h = _ffn(h, g2, mod3, ffg_b, ffu_b, ffd_b, l, i, **geo)
        else:
            w_r = jnp.zeros((D_MODEL, LANES), F32).at[:, :N_EXPERTS].set(router_w[i])
            h = _moe_layer(h, g2, mod3, w_r, exp_w_gate, exp_w_up, exp_w_down,
                           final_g.reshape(1, D_MODEL), l, i, final=(l == DEPTH - 1), **geo)

    assert DEPTH % 2 == 0 and h.shape[0] == n_lat
    return h.reshape(n_batch, seq, D_MODEL)
```
